```python
import jax, jax.numpy as jnp
from jax import lax
import numpy as np

D_MODEL = 2048
BATCH = 8
SEQ = 8192
DEPTH = 1

CHUNK = 64
GMLP_BLOCK = 128
A_WIDTH = D_MODEL // 2
A_GROUPS = 8
A_HEAD = A_WIDTH // A_GROUPS
B_WIDTH = D_MODEL // 2
CONV_WIDTH = 31
D_FF = ((8 * D_MODEL // 3 + 255) // 256) * 256
LN_EPS = 1e-5
ALPHA = (2.0 * DEPTH) ** 0.25
BETA = (8.0 * DEPTH) ** -0.25
IN_COLS = 2 * A_WIDTH + 2 * B_WIDTH + 2 * D_MODEL

kernel_name = "hybrid_gmlp_conformer_conv_macaron_deepnorm"


def layer_norm(x, g, b):
    xf = x.astype(jnp.float32)
    mu = jnp.mean(xf, axis=-1, keepdims=True)
    var = jnp.mean(jnp.square(xf - mu), axis=-1, keepdims=True)
    y = (xf - mu) * lax.rsqrt(var + LN_EPS)
    return (y * g.astype(jnp.float32) + b.astype(jnp.float32)).astype(x.dtype)


def swiglu(x, w_gu, w_down):
    gate, up = jnp.split(x @ w_gu, 2, axis=-1)
    return (jax.nn.silu(gate) * up) @ w_down


def block_causal_mask(n):
    c = jnp.arange(n) // CHUNK
    return c[None, :] <= c[:, None]


def spatial_gating(u, v, ln_g, ln_b, w_s, b_s):
    bsz, seq, _ = v.shape
    v = layer_norm(v, ln_g, ln_b)
    nblk = seq // GMLP_BLOCK
    vb = v.reshape(bsz, nblk, GMLP_BLOCK, A_GROUPS, A_HEAD)
    w = jnp.where(block_causal_mask(GMLP_BLOCK)[None], w_s, jnp.zeros((), w_s.dtype))
    s = jnp.einsum('hij,bnjhd->bnihd', w, vb) + jnp.transpose(b_s)[None, None, :, :, None]
    return u * s.reshape(bsz, seq, A_WIDTH)


def conv_module(h, w_dw, b_dw, ln_g, ln_b):
    a, g = jnp.split(h, 2, axis=-1)
    z = a * jax.nn.sigmoid(g)
    z = lax.conv_general_dilated(
        z, w_dw[:, None, :], window_strides=(1,),
        padding=[(CONV_WIDTH - 1, 0)],
        dimension_numbers=('NWC', 'WIO', 'NWC'),
        feature_group_count=B_WIDTH) + b_dw
    z = layer_norm(z, ln_g, ln_b)
    return jax.nn.silu(z)


def _fwd_setup_inputs(seed: int = 0) -> dict:
    key = jax.random.key(seed)
    ks = jax.random.split(key, 32)
    n = jax.random.normal
    L = DEPTH
    def gain(k, d):
        return 1.0 + 0.01 * n(k, (L, d), jnp.float32)
    def bias(k, shape):
        return 0.01 * n(k, (L,) + shape, jnp.float32)
    return {
        "x": n(ks[0], (BATCH, SEQ, D_MODEL), jnp.float32),
        "ffn1_w_gu": n(ks[1], (L, D_MODEL, 2 * D_FF), jnp.float32) * D_MODEL ** -0.5,
        "ffn1_w_down": n(ks[2], (L, D_FF, D_MODEL), jnp.float32) * (D_FF ** -0.5 * BETA),
        "ln1_g": gain(ks[3], D_MODEL),
        "ln1_b": bias(ks[4], (D_MODEL,)),
        "w_in": n(ks[5], (L, D_MODEL, IN_COLS), jnp.float32) * D_MODEL ** -0.5,
        "b_in": bias(ks[6], (IN_COLS,)),
        "sgu_ln_g": gain(ks[7], A_WIDTH),
        "sgu_ln_b": bias(ks[8], (A_WIDTH,)),
        "sgu_w_s": n(ks[9], (L, A_GROUPS, GMLP_BLOCK, GMLP_BLOCK), jnp.float32) * (0.5 * GMLP_BLOCK ** -0.5),
        "sgu_b_s": 1.0 + 0.01 * n(ks[10], (L, A_GROUPS, GMLP_BLOCK), jnp.float32),
        "w_a_proj": n(ks[11], (L, A_WIDTH, D_MODEL), jnp.float32) * (A_WIDTH ** -0.5 * BETA),
        "conv_w_dw": n(ks[12], (L, CONV_WIDTH, B_WIDTH), jnp.float32) * CONV_WIDTH ** -0.5,
        "conv_b_dw": bias(ks[13], (B_WIDTH,)),
        "conv_ln_g": gain(ks[14], B_WIDTH),
        "conv_ln_b": bias(ks[15], (B_WIDTH,)),
        "w_b_proj": n(ks[16], (L, B_WIDTH, D_MODEL), jnp.float32) * (B_WIDTH ** -0.5 * BETA),
        "w_out": n(ks[17], (L, D_MODEL, D_MODEL), jnp.float32) * (D_MODEL ** -0.5 * BETA),
        "ln2_g": gain(ks[18], D_MODEL),
        "ln2_b": bias(ks[19], (D_MODEL,)),
        "ffn2_w_gu": n(ks[20], (L, D_MODEL, 2 * D_FF), jnp.float32) * D_MODEL ** -0.5,
        "ffn2_w_down": n(ks[21], (L, D_FF, D_MODEL), jnp.float32) * (D_FF ** -0.5 * BETA),
        "ln3_g": gain(ks[22], D_MODEL),
        "ln3_b": bias(ks[23], (D_MODEL,)),
    }


def _fwd_reference(x, ffn1_w_gu, ffn1_w_down, ln1_g, ln1_b, w_in, b_in, sgu_ln_g, sgu_ln_b,
              sgu_w_s, sgu_b_s, w_a_proj, conv_w_dw, conv_b_dw, conv_ln_g, conv_ln_b,
              w_b_proj, w_out, ln2_g, ln2_b, ffn2_w_gu, ffn2_w_down, ln3_g, ln3_b):
    for l in range(DEPTH):
        x = layer_norm(ALPHA * x + 0.5 * swiglu(x, ffn1_w_gu[l], ffn1_w_down[l]), ln1_g[l], ln1_b[l])

        proj = x @ w_in[l] + b_in[l]
        u_a, v_a, h_b, gate_logits = jnp.split(
            proj, [A_WIDTH, 2 * A_WIDTH, 2 * A_WIDTH + 2 * B_WIDTH], axis=-1)

        y_a = spatial_gating(jax.nn.gelu(u_a), jax.nn.gelu(v_a), sgu_ln_g[l], sgu_ln_b[l],
                             sgu_w_s[l], sgu_b_s[l]) @ w_a_proj[l]
        y_b = conv_module(h_b, conv_w_dw[l], conv_b_dw[l], conv_ln_g[l], conv_ln_b[l]) @ w_b_proj[l]

        g_a, g_b = jnp.split(jax.nn.sigmoid(gate_logits), 2, axis=-1)
        mix = (g_a * y_a + g_b * y_b) @ w_out[l]
        x = layer_norm(ALPHA * x + mix, ln2_g[l], ln2_b[l])

        x = layer_norm(ALPHA * x + 0.5 * swiglu(x, ffn2_w_gu[l], ffn2_w_down[l]), ln3_g[l], ln3_b[l])
    return x


import jax as _jax
import jax.numpy as _jnp

TWIN_FORMAT = 'train_step'
FWD_PARAMS = ['x', 'ffn1_w_gu', 'ffn1_w_down', 'ln1_g', 'ln1_b', 'w_in', 'b_in', 'sgu_ln_g', 'sgu_ln_b', 'sgu_w_s', 'sgu_b_s', 'w_a_proj', 'conv_w_dw', 'conv_b_dw', 'conv_ln_g', 'conv_ln_b', 'w_b_proj', 'w_out', 'ln2_g', 'ln2_b', 'ffn2_w_gu', 'ffn2_w_down', 'ln3_g', 'ln3_b']
TWIN_WEIGHTS = ['ffn1_w_gu', 'ffn1_w_down', 'ln1_g', 'ln1_b', 'w_in', 'b_in', 'sgu_ln_g', 'sgu_ln_b', 'sgu_w_s', 'sgu_b_s', 'w_a_proj', 'conv_w_dw', 'conv_b_dw', 'conv_ln_g', 'conv_ln_b', 'w_b_proj', 'w_out', 'ln2_g', 'ln2_b', 'ffn2_w_gu', 'ffn2_w_down', 'ln3_g', 'ln3_b']
TWIN_DIFF_INPUT = 'x'
TWIN_INPUTS = ['x', 'ffn1_w_gu', 'ffn1_w_down', 'ln1_g', 'ln1_b', 'w_in', 'b_in', 'sgu_ln_g', 'sgu_ln_b', 'sgu_w_s', 'sgu_b_s', 'w_a_proj', 'conv_w_dw', 'conv_b_dw', 'conv_ln_g', 'conv_ln_b', 'w_b_proj', 'w_out', 'ln2_g', 'ln2_b', 'ffn2_w_gu', 'ffn2_w_down', 'ln3_g', 'ln3_b', 'loss_target', 'm_ffn1_w_gu', 'm_ffn1_w_down', 'm_ln1_g', 'm_ln1_b', 'm_w_in', 'm_b_in', 'm_sgu_ln_g', 'm_sgu_ln_b', 'm_sgu_w_s', 'm_sgu_b_s', 'm_w_a_proj', 'm_conv_w_dw', 'm_conv_b_dw', 'm_conv_ln_g', 'm_conv_ln_b', 'm_w_b_proj', 'm_w_out', 'm_ln2_g', 'm_ln2_b', 'm_ffn2_w_gu', 'm_ffn2_w_down', 'm_ln3_g', 'm_ln3_b', 'v_ffn1_w_gu', 'v_ffn1_w_down', 'v_ln1_g', 'v_ln1_b', 'v_w_in', 'v_b_in', 'v_sgu_ln_g', 'v_sgu_ln_b', 'v_sgu_w_s', 'v_sgu_b_s', 'v_w_a_proj', 'v_conv_w_dw', 'v_conv_b_dw', 'v_conv_ln_g', 'v_conv_ln_b', 'v_w_b_proj', 'v_w_out', 'v_ln2_g', 'v_ln2_b', 'v_ffn2_w_gu', 'v_ffn2_w_down', 'v_ln3_g', 'v_ln3_b']
TWIN_OUTPUTS = ['loss', 'grad_x', 'grad_ffn1_w_gu', 'grad_ffn1_w_down', 'grad_ln1_g', 'grad_ln1_b', 'grad_w_in', 'grad_b_in', 'grad_sgu_ln_g', 'grad_sgu_ln_b', 'grad_sgu_w_s', 'grad_sgu_b_s', 'grad_w_a_proj', 'grad_conv_w_dw', 'grad_conv_b_dw', 'grad_conv_ln_g', 'grad_conv_ln_b', 'grad_w_b_proj', 'grad_w_out', 'grad_ln2_g', 'grad_ln2_b', 'grad_ffn2_w_gu', 'grad_ffn2_w_down', 'grad_ln3_g', 'grad_ln3_b', 'delta_ffn1_w_gu', 'delta_ffn1_w_down', 'delta_ln1_g', 'delta_ln1_b', 'delta_w_in', 'delta_b_in', 'delta_sgu_ln_g', 'delta_sgu_ln_b', 'delta_sgu_w_s', 'delta_sgu_b_s', 'delta_w_a_proj', 'delta_conv_w_dw', 'delta_conv_b_dw', 'delta_conv_ln_g', 'delta_conv_ln_b', 'delta_w_b_proj', 'delta_w_out', 'delta_ln2_g', 'delta_ln2_b', 'delta_ffn2_w_gu', 'delta_ffn2_w_down', 'delta_ln3_g', 'delta_ln3_b', 'new_m_ffn1_w_gu', 'new_m_ffn1_w_down', 'new_m_ln1_g', 'new_m_ln1_b', 'new_m_w_in', 'new_m_b_in', 'new_m_sgu_ln_g', 'new_m_sgu_ln_b', 'new_m_sgu_w_s', 'new_m_sgu_b_s', 'new_m_w_a_proj', 'new_m_conv_w_dw', 'new_m_conv_b_dw', 'new_m_conv_ln_g', 'new_m_conv_ln_b', 'new_m_w_b_proj', 'new_m_w_out', 'new_m_ln2_g', 'new_m_ln2_b', 'new_m_ffn2_w_gu', 'new_m_ffn2_w_down', 'new_m_ln3_g', 'new_m_ln3_b', 'new_v_ffn1_w_gu', 'new_v_ffn1_w_down', 'new_v_ln1_g', 'new_v_ln1_b', 'new_v_w_in', 'new_v_b_in', 'new_v_sgu_ln_g', 'new_v_sgu_ln_b', 'new_v_sgu_w_s', 'new_v_sgu_b_s', 'new_v_w_a_proj', 'new_v_conv_w_dw', 'new_v_conv_b_dw', 'new_v_conv_ln_g', 'new_v_conv_ln_b', 'new_v_w_b_proj', 'new_v_w_out', 'new_v_ln2_g', 'new_v_ln2_b', 'new_v_ffn2_w_gu', 'new_v_ffn2_w_down', 'new_v_ln3_g', 'new_v_ln3_b']
TWIN_LEAF_KINDS = {'loss': 'loss', 'grad_x': 'grad_x', 'grad_ffn1_w_gu': 'grad_w', 'grad_ffn1_w_down': 'grad_w', 'grad_ln1_g': 'grad_w', 'grad_ln1_b': 'grad_w', 'grad_w_in': 'grad_w', 'grad_b_in': 'grad_w', 'grad_sgu_ln_g': 'grad_w', 'grad_sgu_ln_b': 'grad_w', 'grad_sgu_w_s': 'grad_w', 'grad_sgu_b_s': 'grad_w', 'grad_w_a_proj': 'grad_w', 'grad_conv_w_dw': 'grad_w', 'grad_conv_b_dw': 'grad_w', 'grad_conv_ln_g': 'grad_w', 'grad_conv_ln_b': 'grad_w', 'grad_w_b_proj': 'grad_w', 'grad_w_out': 'grad_w', 'grad_ln2_g': 'grad_w', 'grad_ln2_b': 'grad_w', 'grad_ffn2_w_gu': 'grad_w', 'grad_ffn2_w_down': 'grad_w', 'grad_ln3_g': 'grad_w', 'grad_ln3_b': 'grad_w', 'delta_ffn1_w_gu': 'delta_w', 'delta_ffn1_w_down': 'delta_w', 'delta_ln1_g': 'delta_w', 'delta_ln1_b': 'delta_w', 'delta_w_in': 'delta_w', 'delta_b_in': 'delta_w', 'delta_sgu_ln_g': 'delta_w', 'delta_sgu_ln_b': 'delta_w', 'delta_sgu_w_s': 'delta_w', 'delta_sgu_b_s': 'delta_w', 'delta_w_a_proj': 'delta_w', 'delta_conv_w_dw': 'delta_w', 'delta_conv_b_dw': 'delta_w', 'delta_conv_ln_g': 'delta_w', 'delta_conv_ln_b': 'delta_w', 'delta_w_b_proj': 'delta_w', 'delta_w_out': 'delta_w', 'delta_ln2_g': 'delta_w', 'delta_ln2_b': 'delta_w', 'delta_ffn2_w_gu': 'delta_w', 'delta_ffn2_w_down': 'delta_w', 'delta_ln3_g': 'delta_w', 'delta_ln3_b': 'delta_w', 'new_m_ffn1_w_gu': 'new_m', 'new_m_ffn1_w_down': 'new_m', 'new_m_ln1_g': 'new_m', 'new_m_ln1_b': 'new_m', 'new_m_w_in': 'new_m', 'new_m_b_in': 'new_m', 'new_m_sgu_ln_g': 'new_m', 'new_m_sgu_ln_b': 'new_m', 'new_m_sgu_w_s': 'new_m', 'new_m_sgu_b_s': 'new_m', 'new_m_w_a_proj': 'new_m', 'new_m_conv_w_dw': 'new_m', 'new_m_conv_b_dw': 'new_m', 'new_m_conv_ln_g': 'new_m', 'new_m_conv_ln_b': 'new_m', 'new_m_w_b_proj': 'new_m', 'new_m_w_out': 'new_m', 'new_m_ln2_g': 'new_m', 'new_m_ln2_b': 'new_m', 'new_m_ffn2_w_gu': 'new_m', 'new_m_ffn2_w_down': 'new_m', 'new_m_ln3_g': 'new_m', 'new_m_ln3_b': 'new_m', 'new_v_ffn1_w_gu': 'new_v', 'new_v_ffn1_w_down': 'new_v', 'new_v_ln1_g': 'new_v', 'new_v_ln1_b': 'new_v', 'new_v_w_in': 'new_v', 'new_v_b_in': 'new_v', 'new_v_sgu_ln_g': 'new_v', 'new_v_sgu_ln_b': 'new_v', 'new_v_sgu_w_s': 'new_v', 'new_v_sgu_b_s': 'new_v', 'new_v_w_a_proj': 'new_v', 'new_v_conv_w_dw': 'new_v', 'new_v_conv_b_dw': 'new_v', 'new_v_conv_ln_g': 'new_v', 'new_v_conv_ln_b': 'new_v', 'new_v_w_b_proj': 'new_v', 'new_v_w_out': 'new_v', 'new_v_ln2_g': 'new_v', 'new_v_ln2_b': 'new_v', 'new_v_ffn2_w_gu': 'new_v', 'new_v_ffn2_w_down': 'new_v', 'new_v_ln3_g': 'new_v', 'new_v_ln3_b': 'new_v'}


def _forward(args):
    return _fwd_reference(*[args[k] for k in FWD_PARAMS])


def _output_shape():
    def fwd():
        inp = _fwd_setup_inputs(0)
        return _fwd_reference(*[inp[k] for k in FWD_PARAMS])
    out = _jax.eval_shape(fwd)
    return out.shape, out.dtype

N_MICROBATCH = 1
ADAM_LR = 0.001
ADAM_B1 = 0.9
ADAM_B2 = 0.999
ADAM_EPS = 1e-08
ADAM_WD = 0.01
ADAM_STEP = 10
PER_EXAMPLE_BATCH_AXIS = {'x': 0, 'loss_target': 0}
SHARED_INPUTS = []
_WEIGHT_DTYPES = {'ffn1_w_gu': _jnp.float32, 'ffn1_w_down': _jnp.float32, 'ln1_g': _jnp.float32, 'ln1_b': _jnp.float32, 'w_in': _jnp.float32, 'b_in': _jnp.float32, 'sgu_ln_g': _jnp.float32, 'sgu_ln_b': _jnp.float32, 'sgu_w_s': _jnp.float32, 'sgu_b_s': _jnp.float32, 'w_a_proj': _jnp.float32, 'conv_w_dw': _jnp.float32, 'conv_b_dw': _jnp.float32, 'conv_ln_g': _jnp.float32, 'conv_ln_b': _jnp.float32, 'w_b_proj': _jnp.float32, 'w_out': _jnp.float32, 'ln2_g': _jnp.float32, 'ln2_b': _jnp.float32, 'ffn2_w_gu': _jnp.float32, 'ffn2_w_down': _jnp.float32, 'ln3_g': _jnp.float32, 'ln3_b': _jnp.float32}
MOMENT_SCALE = {'ffn1_w_gu': 1.148800e-02, 'ffn1_w_down': 3.149593e-02, 'ln1_g': 4.480045e-01, 'ln1_b': 2.324487e-01, 'w_in': 1.114520e-02, 'b_in': 1.348731e-02, 'sgu_ln_g': 8.671669e-03, 'sgu_ln_b': 7.932984e-03, 'sgu_w_s': 1.616731e-02, 'sgu_b_s': 1.916526e-02, 'w_a_proj': 2.625596e-02, 'conv_w_dw': 1.775284e-02, 'conv_b_dw': 3.932761e-02, 'conv_ln_g': 2.332242e-02, 'conv_ln_b': 2.347851e-02, 'w_b_proj': 2.185887e-02, 'w_out': 3.505344e-02, 'ln2_g': 4.629158e-01, 'ln2_b': 2.341414e-01, 'ffn2_w_gu': 1.130115e-02, 'ffn2_w_down': 3.102105e-02, 'ln3_g': 3.197032e+01, 'ln3_b': 1.591462e+00}


def _to_microbatches(a, axis):
    t = _jnp.moveaxis(a, axis, 0)
    t = t.reshape((N_MICROBATCH, t.shape[0] // N_MICROBATCH) + t.shape[1:])
    return _jnp.moveaxis(t, 1, axis + 1)


def setup_inputs(seed: int = 0) -> dict:
    inp = _fwd_setup_inputs(seed)
    key = _jax.random.fold_in(_jax.random.key(seed), 7919)
    shape, _ = _output_shape()
    out = dict(inp)
    out["loss_target"] = _jax.random.normal(_jax.random.fold_in(key, 0), shape, _jnp.float32)
    for i, name in enumerate(TWIN_WEIGHTS):
        w = inp[name].astype(_jnp.float32)
        if MOMENT_SCALE is None:
            s = _jnp.sqrt(_jnp.mean(_jnp.square(w)) + 1e-30)
        else:
            s = MOMENT_SCALE[name]
        km, kv = _jax.random.split(_jax.random.fold_in(key, i + 1))
        out[name] = w
        out["m_" + name] = s * _jax.random.normal(km, w.shape, _jnp.float32)
        out["v_" + name] = (s * s) * _jax.random.uniform(kv, w.shape, _jnp.float32, 0.5, 1.5)
    if N_MICROBATCH > 1:
        for name, axis in PER_EXAMPLE_BATCH_AXIS.items():
            out[name] = _to_microbatches(out[name], axis)
    return {'x': out['x'], 'ffn1_w_gu': out['ffn1_w_gu'], 'ffn1_w_down': out['ffn1_w_down'], 'ln1_g': out['ln1_g'], 'ln1_b': out['ln1_b'], 'w_in': out['w_in'], 'b_in': out['b_in'], 'sgu_ln_g': out['sgu_ln_g'], 'sgu_ln_b': out['sgu_ln_b'], 'sgu_w_s': out['sgu_w_s'], 'sgu_b_s': out['sgu_b_s'], 'w_a_proj': out['w_a_proj'], 'conv_w_dw': out['conv_w_dw'], 'conv_b_dw': out['conv_b_dw'], 'conv_ln_g': out['conv_ln_g'], 'conv_ln_b': out['conv_ln_b'], 'w_b_proj': out['w_b_proj'], 'w_out': out['w_out'], 'ln2_g': out['ln2_g'], 'ln2_b': out['ln2_b'], 'ffn2_w_gu': out['ffn2_w_gu'], 'ffn2_w_down': out['ffn2_w_down'], 'ln3_g': out['ln3_g'], 'ln3_b': out['ln3_b'], 'loss_target': out['loss_target'], 'm_ffn1_w_gu': out['m_ffn1_w_gu'], 'm_ffn1_w_down': out['m_ffn1_w_down'], 'm_ln1_g': out['m_ln1_g'], 'm_ln1_b': out['m_ln1_b'], 'm_w_in': out['m_w_in'], 'm_b_in': out['m_b_in'], 'm_sgu_ln_g': out['m_sgu_ln_g'], 'm_sgu_ln_b': out['m_sgu_ln_b'], 'm_sgu_w_s': out['m_sgu_w_s'], 'm_sgu_b_s': out['m_sgu_b_s'], 'm_w_a_proj': out['m_w_a_proj'], 'm_conv_w_dw': out['m_conv_w_dw'], 'm_conv_b_dw': out['m_conv_b_dw'], 'm_conv_ln_g': out['m_conv_ln_g'], 'm_conv_ln_b': out['m_conv_ln_b'], 'm_w_b_proj': out['m_w_b_proj'], 'm_w_out': out['m_w_out'], 'm_ln2_g': out['m_ln2_g'], 'm_ln2_b': out['m_ln2_b'], 'm_ffn2_w_gu': out['m_ffn2_w_gu'], 'm_ffn2_w_down': out['m_ffn2_w_down'], 'm_ln3_g': out['m_ln3_g'], 'm_ln3_b': out['m_ln3_b'], 'v_ffn1_w_gu': out['v_ffn1_w_gu'], 'v_ffn1_w_down': out['v_ffn1_w_down'], 'v_ln1_g': out['v_ln1_g'], 'v_ln1_b': out['v_ln1_b'], 'v_w_in': out['v_w_in'], 'v_b_in': out['v_b_in'], 'v_sgu_ln_g': out['v_sgu_ln_g'], 'v_sgu_ln_b': out['v_sgu_ln_b'], 'v_sgu_w_s': out['v_sgu_w_s'], 'v_sgu_b_s': out['v_sgu_b_s'], 'v_w_a_proj': out['v_w_a_proj'], 'v_conv_w_dw': out['v_conv_w_dw'], 'v_conv_b_dw': out['v_conv_b_dw'], 'v_conv_ln_g': out['v_conv_ln_g'], 'v_conv_ln_b': out['v_conv_ln_b'], 'v_w_b_proj': out['v_w_b_proj'], 'v_w_out': out['v_w_out'], 'v_ln2_g': out['v_ln2_g'], 'v_ln2_b': out['v_ln2_b'], 'v_ffn2_w_gu': out['v_ffn2_w_gu'], 'v_ffn2_w_down': out['v_ffn2_w_down'], 'v_ln3_g': out['v_ln3_g'], 'v_ln3_b': out['v_ln3_b']}


def _loss(weights, diff, rest, loss_target):
    with _jax.named_scope("forward"):
        args = {**rest, TWIN_DIFF_INPUT: diff, **{k: w.astype(_WEIGHT_DTYPES[k]) for k, w in weights.items()}}
        y = _forward(args)
    with _jax.named_scope("loss_head"):
        err = _jnp.square(y.astype(_jnp.float32) - loss_target)
        return 0.5 * _jnp.sum(_jnp.mean(err, axis=-1)) if err.ndim else 0.5 * err


def _adamw(w, g, m, v):
    m = ADAM_B1 * m + (1.0 - ADAM_B1) * g
    v = ADAM_B2 * v + (1.0 - ADAM_B2) * _jnp.square(g)
    m_hat = m / (1.0 - ADAM_B1 ** ADAM_STEP)
    v_hat = v / (1.0 - ADAM_B2 ** ADAM_STEP)
    delta = -ADAM_LR * (m_hat / (_jnp.sqrt(v_hat) + ADAM_EPS) + ADAM_WD * w)
    return delta, m, v


def reference(x, ffn1_w_gu, ffn1_w_down, ln1_g, ln1_b, w_in, b_in, sgu_ln_g, sgu_ln_b, sgu_w_s, sgu_b_s, w_a_proj, conv_w_dw, conv_b_dw, conv_ln_g, conv_ln_b, w_b_proj, w_out, ln2_g, ln2_b, ffn2_w_gu, ffn2_w_down, ln3_g, ln3_b, loss_target, m_ffn1_w_gu, m_ffn1_w_down, m_ln1_g, m_ln1_b, m_w_in, m_b_in, m_sgu_ln_g, m_sgu_ln_b, m_sgu_w_s, m_sgu_b_s, m_w_a_proj, m_conv_w_dw, m_conv_b_dw, m_conv_ln_g, m_conv_ln_b, m_w_b_proj, m_w_out, m_ln2_g, m_ln2_b, m_ffn2_w_gu, m_ffn2_w_down, m_ln3_g, m_ln3_b, v_ffn1_w_gu, v_ffn1_w_down, v_ln1_g, v_ln1_b, v_w_in, v_b_in, v_sgu_ln_g, v_sgu_ln_b, v_sgu_w_s, v_sgu_b_s, v_w_a_proj, v_conv_w_dw, v_conv_b_dw, v_conv_ln_g, v_conv_ln_b, v_w_b_proj, v_w_out, v_ln2_g, v_ln2_b, v_ffn2_w_gu, v_ffn2_w_down, v_ln3_g, v_ln3_b):
    given = dict(x=x, ffn1_w_gu=ffn1_w_gu, ffn1_w_down=ffn1_w_down, ln1_g=ln1_g, ln1_b=ln1_b, w_in=w_in, b_in=b_in, sgu_ln_g=sgu_ln_g, sgu_ln_b=sgu_ln_b, sgu_w_s=sgu_w_s, sgu_b_s=sgu_b_s, w_a_proj=w_a_proj, conv_w_dw=conv_w_dw, conv_b_dw=conv_b_dw, conv_ln_g=conv_ln_g, conv_ln_b=conv_ln_b, w_b_proj=w_b_proj, w_out=w_out, ln2_g=ln2_g, ln2_b=ln2_b, ffn2_w_gu=ffn2_w_gu, ffn2_w_down=ffn2_w_down, ln3_g=ln3_g, ln3_b=ln3_b, loss_target=loss_target, m_ffn1_w_gu=m_ffn1_w_gu, m_ffn1_w_down=m_ffn1_w_down, m_ln1_g=m_ln1_g, m_ln1_b=m_ln1_b, m_w_in=m_w_in, m_b_in=m_b_in, m_sgu_ln_g=m_sgu_ln_g, m_sgu_ln_b=m_sgu_ln_b, m_sgu_w_s=m_sgu_w_s, m_sgu_b_s=m_sgu_b_s, m_w_a_proj=m_w_a_proj, m_conv_w_dw=m_conv_w_dw, m_conv_b_dw=m_conv_b_dw, m_conv_ln_g=m_conv_ln_g, m_conv_ln_b=m_conv_ln_b, m_w_b_proj=m_w_b_proj, m_w_out=m_w_out, m_ln2_g=m_ln2_g, m_ln2_b=m_ln2_b, m_ffn2_w_gu=m_ffn2_w_gu, m_ffn2_w_down=m_ffn2_w_down, m_ln3_g=m_ln3_g, m_ln3_b=m_ln3_b, v_ffn1_w_gu=v_ffn1_w_gu, v_ffn1_w_down=v_ffn1_w_down, v_ln1_g=v_ln1_g, v_ln1_b=v_ln1_b, v_w_in=v_w_in, v_b_in=v_b_in, v_sgu_ln_g=v_sgu_ln_g, v_sgu_ln_b=v_sgu_ln_b, v_sgu_w_s=v_sgu_w_s, v_sgu_b_s=v_sgu_b_s, v_w_a_proj=v_w_a_proj, v_conv_w_dw=v_conv_w_dw, v_conv_b_dw=v_conv_b_dw, v_conv_ln_g=v_conv_ln_g, v_conv_ln_b=v_conv_ln_b, v_w_b_proj=v_w_b_proj, v_w_out=v_w_out, v_ln2_g=v_ln2_g, v_ln2_b=v_ln2_b, v_ffn2_w_gu=v_ffn2_w_gu, v_ffn2_w_down=v_ffn2_w_down, v_ln3_g=v_ln3_g, v_ln3_b=v_ln3_b)
    weights = {n: given[n] for n in TWIN_WEIGHTS}
    shared = {n: given[n] for n in SHARED_INPUTS}
    per_example = {n: given[n] for n in ['x']}
    grad_fn = _jax.value_and_grad(_loss, argnums=(0, 1))

    def one_microbatch(ex, loss_target):
        ex = dict(ex)
        diff = ex.pop(TWIN_DIFF_INPUT)
        return grad_fn(weights, diff, {**shared, **ex}, loss_target)

    if N_MICROBATCH == 1:
        loss, (grad_w, grad_x) = one_microbatch(per_example, given["loss_target"])
    else:
        def body(carry, xs):
            loss_sum, grad_sum = carry
            l_k, (gw_k, gx_k) = one_microbatch(xs[0], xs[1])
            with _jax.named_scope("update"):
                return (loss_sum + l_k, _jax.tree.map(_jnp.add, grad_sum, gw_k)), gx_k

        init = (_jnp.zeros((), _jnp.float32), _jax.tree.map(_jnp.zeros_like, weights))
        (loss, grad_w), grad_x = _jax.lax.scan(body, init, (per_example, given["loss_target"]))
    with _jax.named_scope("update"):
        delta_w, new_m, new_v = {}, {}, {}
        for n in TWIN_WEIGHTS:
            delta_w[n], new_m[n], new_v[n] = _adamw(weights[n], grad_w[n], given["m_" + n], given["v_" + n])
    return (loss, grad_x, *[grad_w[n] for n in TWIN_WEIGHTS], *[delta_w[n] for n in TWIN_WEIGHTS],
            *[new_m[n] for n in TWIN_WEIGHTS], *[new_v[n] for n in TWIN_WEIGHTS])
```

```python
import functools
import math

import jax
import jax.numpy as jnp
from jax import lax
from jax.experimental import pallas as pl
from jax.experimental.pallas import tpu as pltpu

F32 = jnp.float32
BF16 = jnp.bfloat16

ALPHA = 2.0 ** 0.25
LN_EPS = 1e-5
CONV_WIDTH = 31
CONV_HALO = 32
CONV_ROWS = 64
CONV_WPAD = 32
CHUNK = 64
GMLP_BLOCK = 128
A_GROUPS = 8
N_DEV = 8
LANES = 128

ADAM_LR = 0.001
ADAM_B1 = 0.9
ADAM_B2 = 0.999
ADAM_EPS = 1e-08
ADAM_WD = 0.01
ADAM_STEP = 10
ADAM_C1 = 1.0 - ADAM_B1 ** ADAM_STEP
ADAM_C2 = 1.0 - ADAM_B2 ** ADAM_STEP

VMEM_LIMIT_BYTES = 56 * 2 ** 20
MESH = pl.DeviceIdType.MESH
ANY = pl.BlockSpec(memory_space=pl.ANY)

WEIGHTS = ['ffn1_w_gu', 'ffn1_w_down', 'ln1_g', 'ln1_b', 'w_in', 'b_in', 'sgu_ln_g', 'sgu_ln_b', 'sgu_w_s',
           'sgu_b_s', 'w_a_proj', 'conv_w_dw', 'conv_b_dw', 'conv_ln_g', 'conv_ln_b', 'w_b_proj', 'w_out',
           'ln2_g', 'ln2_b', 'ffn2_w_gu', 'ffn2_w_down', 'ln3_g', 'ln3_b']
BIG = {'ffn1_w_gu': 1, 'ffn1_w_down': 0, 'w_in': 1, 'w_a_proj': 1, 'w_b_proj': 1, 'w_out': 0,
       'ffn2_w_gu': 1, 'ffn2_w_down': 0}
SMALL = [n for n in WEIGHTS if n not in BIG and n != 'conv_w_dw']


def _tile(n, pref, mult=8):
    best = None
    for d in range(mult, min(n, pref) + 1, mult):
        if n % d == 0:
            best = d
    return n if best is None else best


def _params(*sem):
    return pltpu.CompilerParams(dimension_semantics=sem, vmem_limit_bytes=VMEM_LIMIT_BYTES)


def _dot(a, b):
    return jnp.dot(a, b, preferred_element_type=F32)


def _dot_nt(a, b):
    return lax.dot_general(a, b, (((1,), (1,)), ((), ())), preferred_element_type=F32)


def _dot_tn(a, b):
    return lax.dot_general(a, b, (((0,), (0,)), ((), ())), preferred_element_type=F32)


def _sig(x):
    return 1.0 / (1.0 + jnp.exp(-x))


_GELU_K = math.sqrt(2.0 / math.pi)
_GELU_C = 0.044715


def _gelu(x):
    t = jnp.tanh(_GELU_K * (x + _GELU_C * x * x * x))
    return 0.5 * x * (1.0 + t)


def _gelu_grad(x):
    x2 = x * x
    t = jnp.tanh(_GELU_K * (x + _GELU_C * x2 * x))
    return 0.5 * (1.0 + t) + 0.5 * x * (1.0 - t * t) * (_GELU_K * (1.0 + 3.0 * _GELU_C * x2))


def _ln_stats(r):
    mu = jnp.mean(r, axis=-1, keepdims=True)
    rc = r - mu
    var = jnp.mean(rc * rc, axis=-1, keepdims=True)
    rstd = lax.rsqrt(var + LN_EPS)
    return rc * rstd, rstd


def _ln_bwd(dy, xh, rstd, g):
    dxh = dy * g
    m1 = jnp.mean(dxh, axis=-1, keepdims=True)
    m2 = jnp.mean(dxh * xh, axis=-1, keepdims=True)
    return rstd * (dxh - m1 - xh * m2)


def _colsum(v):
    return jnp.sum(v, axis=0, keepdims=True)


def _chunk_mask(transposed):
    shift = CHUNK.bit_length() - 1
    r = lax.broadcasted_iota(jnp.int32, (GMLP_BLOCK, GMLP_BLOCK), 0) >> shift
    c = lax.broadcasted_iota(jnp.int32, (GMLP_BLOCK, GMLP_BLOCK), 1) >> shift
    return (r <= c) if transposed else (c <= r)


def _ffn_fwd(xh, lg, lb, wgu, wd, *, affine, name, final=None):
    T, D = xh.shape
    F = wd.shape[0]
    tm = _tile(T, 512)
    tn = _tile(F, 512, LANES)
    nj = F // tn
    is_final = final is not None

    def body(*refs):
        if is_final:
            (xh_ref, lg_ref, lb_ref, wg_ref, wu_ref, wd_ref, ng_ref, nb_ref, tgt_ref,
             gate_ref, up_ref, dr_ref, loss_ref, dng_ref, dnb_ref, xb_sc, acc_sc) = refs
        else:
            (xh_ref, lg_ref, lb_ref, wg_ref, wu_ref, wd_ref,
             gate_ref, up_ref, xho_ref, rstd_ref, xb_sc, acc_sc) = refs
        i = pl.program_id(0)
        j = pl.program_id(1)

        def xin():
            v = xh_ref[...]
            return v * lg_ref[...] + lb_ref[...] if affine else v

        @pl.when(j == 0)
        def _():
            xb_sc[...] = xin().astype(BF16)
            acc_sc[...] = jnp.zeros_like(acc_sc)

        xb = xb_sc[...]
        g = _dot(xb, wg_ref[...])
        u = _dot(xb, wu_ref[...])
        gate_ref[...] = g.astype(BF16)
        up_ref[...] = u.astype(BF16)
        h = g * _sig(g) * u
        acc_sc[...] += _dot(h.astype(BF16), wd_ref[...])

        @pl.when(j == nj - 1)
        def _():
            r = ALPHA * xin() + 0.5 * acc_sc[...]
            xho, rstd = _ln_stats(r)
            if not is_final:
                xho_ref[...] = xho
                rstd_ref[...] = jnp.broadcast_to(rstd, (tm, LANES))
            else:
                @pl.when(i == 0)
                def _():
                    loss_ref[...] = jnp.zeros_like(loss_ref)
                    dng_ref[...] = jnp.zeros_like(dng_ref)
                    dnb_ref[...] = jnp.zeros_like(dnb_ref)
                ng = ng_ref[...]
                e = xho * ng + nb_ref[...] - tgt_ref[...]
                part = _colsum(jnp.sum(e * e, axis=1, keepdims=True)) * (0.5 / D)
                loss_ref[...] += jnp.broadcast_to(part, loss_ref.shape)
                dy = e * (1.0 / D)
                dng_ref[...] += _colsum(dy * xho)
                dnb_ref[...] += _colsum(dy)
                dr_ref[...] = _ln_bwd(dy, xho, rstd, ng)

    row = pl.BlockSpec((tm, D), lambda i, j: (i, 0))
    vec = pl.BlockSpec((1, D), lambda i, j: (0, 0))
    hid = pl.BlockSpec((tm, tn), lambda i, j: (i, j))
    in_specs = [row, vec, vec,
                pl.BlockSpec((D, tn), lambda i, j: (0, j)),
                pl.BlockSpec((D, tn), lambda i, j: (0, j + nj)),
                pl.BlockSpec((tn, D), lambda i, j: (j, 0))]
    args = [xh, lg, lb, wgu, wgu, wd]
    out_shape = [jax.ShapeDtypeStruct((T, F), BF16), jax.ShapeDtypeStruct((T, F), BF16)]
    out_specs = [hid, hid]
    if is_final:
        in_specs += [vec, vec, row]
        args += list(final)
        out_shape += [jax.ShapeDtypeStruct((T, D), F32), jax.ShapeDtypeStruct((8, LANES), F32),
                      jax.ShapeDtypeStruct((1, D), F32), jax.ShapeDtypeStruct((1, D), F32)]
        out_specs += [row, pl.BlockSpec((8, LANES), lambda i, j: (0, 0)), vec, vec]
        sem = ("arbitrary", "arbitrary")
    else:
        out_shape += [jax.ShapeDtypeStruct((T, D), F32), jax.ShapeDtypeStruct((T, LANES), F32)]
        out_specs += [row, pl.BlockSpec((tm, LANES), lambda i, j: (i, 0))]
        sem = ("parallel", "arbitrary")
    return pl.pallas_call(
        body, name=name, grid=(T // tm, nj), in_specs=in_specs, out_specs=out_specs, out_shape=out_shape,
        scratch_shapes=[pltpu.VMEM((tm, D), BF16), pltpu.VMEM((tm, D), F32)],
        compiler_params=_params(*sem))(*args)


def _ffn_bwd(dr, gate, up, wgu, wd, *, name, prev=None):
    T, D = dr.shape
    F = wd.shape[0]
    tm = _tile(T, 512)
    tn = _tile(F, 512, LANES)
    nj = F // tn
    has_prev = prev is not None

    def body(*refs):
        if has_prev:
            (dr_ref, gate_ref, up_ref, wd_ref, wg_ref, wu_ref, xh_ref, rstd_ref, lg_ref,
             h_ref, dg_ref, du_ref, dprev_ref, dlg_ref, dlb_ref, df_sc, dx_sc) = refs
        else:
            (dr_ref, gate_ref, up_ref, wd_ref, wg_ref, wu_ref,
             h_ref, dg_ref, du_ref, dprev_ref, df_sc, dx_sc) = refs
        i = pl.program_id(0)
        j = pl.program_id(1)

        @pl.when(j == 0)
        def _():
            d = dr_ref[...]
            df_sc[...] = (0.5 * d).astype(BF16)
            dx_sc[...] = ALPHA * d

        g = gate_ref[...].astype(F32)
        u = up_ref[...].astype(F32)
        dh = _dot_nt(df_sc[...], wd_ref[...])
        s = _sig(g)
        sil = g * s
        h_ref[...] = (sil * u).astype(BF16)
        dg = (dh * u * (s * (1.0 + g * (1.0 - s)))).astype(BF16)
        du = (dh * sil).astype(BF16)
        dg_ref[...] = dg
        du_ref[...] = du
        dx_sc[...] += _dot_nt(dg, wg_ref[...]) + _dot_nt(du, wu_ref[...])

        @pl.when(j == nj - 1)
        def _():
            dxin = dx_sc[...]
            if not has_prev:
                dprev_ref[...] = dxin
            else:
                @pl.when(i == 0)
                def _():
                    dlg_ref[...] = jnp.zeros_like(dlg_ref)
                    dlb_ref[...] = jnp.zeros_like(dlb_ref)
                xh = xh_ref[...]
                dlg_ref[...] += _colsum(dxin * xh)
                dlb_ref[...] += _colsum(dxin)
                dprev_ref[...] = _ln_bwd(dxin, xh, rstd_ref[:, 0:1], lg_ref[...])

    row = pl.BlockSpec((tm, D), lambda i, j: (i, 0))
    vec = pl.BlockSpec((1, D), lambda i, j: (0, 0))
    hid = pl.BlockSpec((tm, tn), lambda i, j: (i, j))
    in_specs = [row, hid, hid,
                pl.BlockSpec((tn, D), lambda i, j: (j, 0)),
                pl.BlockSpec((D, tn), lambda i, j: (0, j)),
                pl.BlockSpec((D, tn), lambda i, j: (0, j + nj))]
    args = [dr, gate, up, wd, wgu, wgu]
    out_shape = [jax.ShapeDtypeStruct((T, F), BF16)] * 3 + [jax.ShapeDtypeStruct((T, D), F32)]
    out_specs = [hid, hid, hid, row]
    if has_prev:
        in_specs += [row, pl.BlockSpec((tm, LANES), lambda i, j: (i, 0)), vec]
        args += list(prev)
        out_shape += [jax.ShapeDtypeStruct((1, D), F32)] * 2
        out_specs += [vec, vec]
        sem = ("arbitrary", "arbitrary")
    else:
        sem = ("parallel", "arbitrary")
    return pl.pallas_call(
        body, name=name, grid=(T // tm, nj), in_specs=in_specs, out_specs=out_specs, out_shape=out_shape,
        scratch_shapes=[pltpu.VMEM((tm, D), BF16), pltpu.VMEM((tm, D), F32)],
        compiler_params=_params(*sem))(*args)


def _mm_tn(a, b, *, name, tm_pref, tn_pref, scale=1.0, a_affine=None, into=None, col_off=0, n_total=None):
    T, M = a.shape
    N = b.shape[1]
    n_total = N if n_total is None else n_total
    tM = _tile(M, tm_pref, LANES)
    tN = _tile(N, tn_pref, LANES)
    tk = _tile(T, 512)
    nt = T // tk
    assert col_off % tN == 0
    off_blocks = col_off // tN
    has_aff = a_affine is not None
    has_into = into is not None

    def body(*refs):
        refs = list(refs)
        a_ref = refs.pop(0)
        if has_aff:
            lg_ref = refs.pop(0)
            lb_ref = refs.pop(0)
        b_ref = refs.pop(0)
        if has_into:
            refs.pop(0)
        o_ref, acc_sc = refs
        t = pl.program_id(2)

        @pl.when(t == 0)
        def _():
            acc_sc[...] = jnp.zeros_like(acc_sc)

        av = a_ref[...]
        if has_aff:
            av = av * lg_ref[...] + lb_ref[...]
        acc_sc[...] += _dot_tn(av.astype(BF16), b_ref[...].astype(BF16))

        @pl.when(t == nt - 1)
        def _():
            o_ref[...] = (acc_sc[...] * scale).astype(BF16)

    in_specs = [pl.BlockSpec((tk, tM), lambda m, n, t: (t, m))]
    args = [a]
    if has_aff:
        in_specs += [pl.BlockSpec((1, tM), lambda m, n, t: (0, m))] * 2
        args += list(a_affine)
    in_specs.append(pl.BlockSpec((tk, tN), lambda m, n, t: (t, n)))
    args.append(b)
    aliases = {}
    if has_into:
        aliases = {len(args): 0}
        in_specs.append(ANY)
        args.append(into)
    return pl.pallas_call(
        body, name=name, grid=(M // tM, N // tN, nt), in_specs=in_specs,
        out_specs=pl.BlockSpec((tM, tN), lambda m, n, t: (m, n + off_blocks)),
        out_shape=jax.ShapeDtypeStruct((M, n_total), BF16),
        scratch_shapes=[pltpu.VMEM((tM, tN), F32)], input_output_aliases=aliases,
        compiler_params=_params("parallel", "parallel", "arbitrary"))(*args)


def _inproj_fwd(xh, lg, lb, w, bias, *, name):
    T, D = xh.shape
    N = w.shape[1]
    tm = _tile(T, 512)
    tn = _tile(N, 1024, LANES)

    def body(xh_ref, lg_ref, lb_ref, w_ref, b_ref, o_ref, xb_sc):
        @pl.when(pl.program_id(1) == 0)
        def _():
            xb_sc[...] = (xh_ref[...] * lg_ref[...] + lb_ref[...]).astype(BF16)
        o_ref[...] = _dot(xb_sc[...], w_ref[...]) + b_ref[...]

    return pl.pallas_call(
        body, name=name, grid=(T // tm, N // tn),
        in_specs=[pl.BlockSpec((tm, D), lambda i, j: (i, 0)),
                  pl.BlockSpec((1, D), lambda i, j: (0, 0)), pl.BlockSpec((1, D), lambda i, j: (0, 0)),
                  pl.BlockSpec((D, tn), lambda i, j: (0, j)), pl.BlockSpec((1, tn), lambda i, j: (0, j))],
        out_specs=pl.BlockSpec((tm, tn), lambda i, j: (i, j)),
        out_shape=jax.ShapeDtypeStruct((T, N), F32),
        scratch_shapes=[pltpu.VMEM((tm, D), BF16)],
        compiler_params=_params("parallel", "arbitrary"))(xh, lg, lb, w, bias)


def _inproj_bwd(dproj, w, dr_next, xh, rstd, lg, *, name):
    T, N = dproj.shape
    D = w.shape[0]
    tm = _tile(T, 512)
    tn = _tile(N, 1024, LANES)
    nj = N // tn

    def body(dp_ref, w_ref, drn_ref, xh_ref, rstd_ref, lg_ref, dprev_ref, dlg_ref, dlb_ref, dx_sc):
        i = pl.program_id(0)
        j = pl.program_id(1)

        @pl.when(j == 0)
        def _():
            dx_sc[...] = ALPHA * drn_ref[...]

        dx_sc[...] += _dot_nt(dp_ref[...], w_ref[...])

        @pl.when(j == nj - 1)
        def _():
            @pl.when(i == 0)
            def _():
                dlg_ref[...] = jnp.zeros_like(dlg_ref)
                dlb_ref[...] = jnp.zeros_like(dlb_ref)
            dx = dx_sc[...]
            x_hat = xh_ref[...]
            dlg_ref[...] += _colsum(dx * x_hat)
            dlb_ref[...] += _colsum(dx)
            dprev_ref[...] = _ln_bwd(dx, x_hat, rstd_ref[:, 0:1], lg_ref[...])

    row = pl.BlockSpec((tm, D), lambda i, j: (i, 0))
    vec = pl.BlockSpec((1, D), lambda i, j: (0, 0))
    return pl.pallas_call(
        body, name=name, grid=(T // tm, nj),
        in_specs=[pl.BlockSpec((tm, tn), lambda i, j: (i, j)), pl.BlockSpec((D, tn), lambda i, j: (0, j)),
                  row, row, pl.BlockSpec((tm, LANES), lambda i, j: (i, 0)), vec],
        out_specs=[row, vec, vec],
        out_shape=[jax.ShapeDtypeStruct((T, D), F32), jax.ShapeDtypeStruct((1, D), F32),
                   jax.ShapeDtypeStruct((1, D), F32)],
        scratch_shapes=[pltpu.VMEM((tm, D), F32)],
        compiler_params=_params("arbitrary", "arbitrary"))(dproj, w, dr_next, xh, rstd, lg)


def _sgu_fwd(proj, ln_g, ln_b, w_s, b_sb, *, name):
    T = proj.shape[0]
    A = proj.shape[1] // 8
    hd = A // A_GROUPS
    tm = _tile(T, 256, GMLP_BLOCK)

    def body(u_ref, v_ref, g_ref, b_ref, ws_ref, bs_ref, o_ref):
        gu = _gelu(u_ref[...])
        vh, _ = _ln_stats(_gelu(v_ref[...]))
        vn = (vh * g_ref[...] + b_ref[...]).astype(BF16)
        mask = _chunk_mask(False)
        for h in range(A_GROUPS):
            wm = jnp.where(mask, ws_ref[h], 0.0).astype(BF16)
            cols = slice(h * hd, (h + 1) * hd)
            for n in range(tm // GMLP_BLOCK):
                rows = slice(n * GMLP_BLOCK, (n + 1) * GMLP_BLOCK)
                s = _dot(wm, vn[rows, cols]) + bs_ref[h][:, :hd]
                o_ref[rows, cols] = (gu[rows, cols] * s).astype(BF16)

    vec = pl.BlockSpec((1, A), lambda i: (0, 0))
    full = pl.BlockSpec((A_GROUPS, GMLP_BLOCK, GMLP_BLOCK), lambda i: (0, 0, 0))
    return pl.pallas_call(
        body, name=name, grid=(T // tm,),
        in_specs=[pl.BlockSpec((tm, A), lambda i: (i, 0)), pl.BlockSpec((tm, A), lambda i: (i, 1)),
                  vec, vec, full, full],
        out_specs=pl.BlockSpec((tm, A), lambda i: (i, 0)),
        out_shape=jax.ShapeDtypeStruct((T, A), BF16),
        compiler_params=_params("parallel"))(proj, proj, ln_g, ln_b, w_s, b_sb)


def _sgu_bwd(proj, dsg, ln_g, ln_b, w_s, w_st, b_sb, dproj, *, name):
    T = proj.shape[0]
    A = proj.shape[1] // 8
    hd = A // A_GROUPS
    tm = _tile(T, 256, GMLP_BLOCK)
    nt = T // tm

    def body(u_ref, v_ref, dsg_ref, g_ref, b_ref, ws_ref, wst_ref, bs_ref, _alias,
             dp_ref, dbin_ref, dlg_ref, dlb_ref, dws_ref, dbs_ref, dvn_sc, dgu_sc, dbs_sc):
        i = pl.program_id(0)

        @pl.when(i == 0)
        def _():
            dbin_ref[...] = jnp.zeros_like(dbin_ref)
            dlg_ref[...] = jnp.zeros_like(dlg_ref)
            dlb_ref[...] = jnp.zeros_like(dlb_ref)
            dws_ref[...] = jnp.zeros_like(dws_ref)
            dbs_sc[...] = jnp.zeros_like(dbs_sc)

        u = u_ref[...]
        v = v_ref[...]
        gu = _gelu(u)
        vh, rstd = _ln_stats(_gelu(v))
        gain = g_ref[...]
        vn = (vh * gain + b_ref[...]).astype(BF16)
        dsg_v = dsg_ref[...]
        mask = _chunk_mask(False)
        mask_t = _chunk_mask(True)
        for h in range(A_GROUPS):
            wm = jnp.where(mask, ws_ref[h], 0.0).astype(BF16)
            wmt = jnp.where(mask_t, wst_ref[h], 0.0).astype(BF16)
            cols = slice(h * hd, (h + 1) * hd)
            for n in range(tm // GMLP_BLOCK):
                rows = slice(n * GMLP_BLOCK, (n + 1) * GMLP_BLOCK)
                vb = vn[rows, cols]
                s = _dot(wm, vb) + bs_ref[h][:, :hd]
                d_out = dsg_v[rows, cols]
                dgu_sc[rows, cols] = d_out * s
                ds = d_out * gu[rows, cols]
                ds_b = ds.astype(BF16)
                dws_ref[h] += _dot_nt(ds_b, vb)
                dbs_sc[h] += ds
                dvn_sc[rows, cols] = _dot(wmt, ds_b)
        dvn = dvn_sc[...]
        dlg_ref[...] += _colsum(dvn * vh)
        dlb_ref[...] += _colsum(dvn)
        dv = _ln_bwd(dvn, vh, rstd, gain) * _gelu_grad(v)
        du = dgu_sc[...] * _gelu_grad(u)
        dp_ref[:, 0:A] = du.astype(BF16)
        dp_ref[:, A:2 * A] = dv.astype(BF16)
        dbin_ref[:, 0:A] += _colsum(du)
        dbin_ref[:, A:2 * A] += _colsum(dv)

        @pl.when(i == nt - 1)
        def _():
            for h in range(A_GROUPS):
                dws_ref[h] = jnp.where(mask, dws_ref[h], 0.0)
                dbs_ref[h:h + 1, :] = _colsum(dbs_sc[h].T)

    vec = pl.BlockSpec((1, A), lambda i: (0, 0))
    full = pl.BlockSpec((A_GROUPS, GMLP_BLOCK, GMLP_BLOCK), lambda i: (0, 0, 0))
    tile = pl.BlockSpec((tm, A), lambda i: (i, 0))
    return pl.pallas_call(
        body, name=name, grid=(nt,),
        in_specs=[tile, pl.BlockSpec((tm, A), lambda i: (i, 1)), tile, vec, vec, full, full, full, ANY],
        out_specs=[pl.BlockSpec((tm, 2 * A), lambda i: (i, 0)), pl.BlockSpec((1, 2 * A), lambda i: (0, 0)),
                   vec, vec, full, pl.BlockSpec((A_GROUPS, GMLP_BLOCK), lambda i: (0, 0))],
        out_shape=[jax.ShapeDtypeStruct(dproj.shape, BF16), jax.ShapeDtypeStruct((1, 2 * A), F32),
                   jax.ShapeDtypeStruct((1, A), F32), jax.ShapeDtypeStruct((1, A), F32),
                   jax.ShapeDtypeStruct((A_GROUPS, GMLP_BLOCK, GMLP_BLOCK), F32),
                   jax.ShapeDtypeStruct((A_GROUPS, GMLP_BLOCK), F32)],
        scratch_shapes=[pltpu.VMEM((tm, A), F32), pltpu.VMEM((tm, A), F32),
                        pltpu.VMEM((A_GROUPS, GMLP_BLOCK, hd), F32)],
        input_output_aliases={8: 0},
        compiler_params=_params("arbitrary"))(proj, proj, dsg, ln_g, ln_b, w_s, w_st, b_sb, dproj)


def _conv_tiles(T, B):
    tm = _tile(T, 256, CONV_ROWS)
    lb = min(LANES, B)
    return tm, tm // CONV_HALO, lb


def _conv_fwd(proj, w_dw, b_dw, ln_g, ln_b, *, name):
    T = proj.shape[0]
    B = proj.shape[1] // 8
    tm, nh, lb = _conv_tiles(T, B)

    def body(ap_ref, gp_ref, a_ref, g_ref, w_ref, bdw_ref, lg_ref, lb_ref, c_ref, cv_ref, z_sc):
        i = pl.program_id(0)
        z_sc[0:CONV_HALO, :] = jnp.where(i > 0, ap_ref[...] * _sig(gp_ref[...]), 0.0)
        z_sc[CONV_HALO:CONV_HALO + tm, :] = a_ref[...] * _sig(g_ref[...])
        for cb in range(B // lb):
            ls = slice(cb * lb, (cb + 1) * lb)
            for rc in range(tm // CONV_ROWS):
                base = rc * CONV_ROWS + CONV_HALO - (CONV_WIDTH - 1)
                acc = jnp.zeros((CONV_ROWS, lb), F32)
                for k in range(CONV_WIDTH):
                    acc = acc + w_ref[k:k + 1, ls] * z_sc[base + k:base + k + CONV_ROWS, ls]
                c_ref[rc * CONV_ROWS:(rc + 1) * CONV_ROWS, ls] = acc + bdw_ref[:, ls]
        xh, _ = _ln_stats(c_ref[...])
        y = xh * lg_ref[...] + lb_ref[...]
        cv_ref[...] = (y * _sig(y)).astype(BF16)

    vec = pl.BlockSpec((1, B), lambda i: (0, 0))
    halo_a = pl.BlockSpec((CONV_HALO, B), lambda i: (jnp.maximum(i * nh - 1, 0), 2))
    halo_g = pl.BlockSpec((CONV_HALO, B), lambda i: (jnp.maximum(i * nh - 1, 0), 3))
    return pl.pallas_call(
        body, name=name, grid=(T // tm,),
        in_specs=[halo_a, halo_g, pl.BlockSpec((tm, B), lambda i: (i, 2)), pl.BlockSpec((tm, B), lambda i: (i, 3)),
                  pl.BlockSpec((CONV_WPAD, B), lambda i: (0, 0)), vec, vec, vec],
        out_specs=[pl.BlockSpec((tm, B), lambda i: (i, 0))] * 2,
        out_shape=[jax.ShapeDtypeStruct((T, B), F32), jax.ShapeDtypeStruct((T, B), BF16)],
        scratch_shapes=[pltpu.VMEM((CONV_HALO + tm, B), F32)],
        compiler_params=_params("parallel"))(proj, proj, proj, proj, w_dw, b_dw, ln_g, ln_b)


def _conv_bwd_ln(dcv, c, ln_g, ln_b, *, name):
    T, B = c.shape
    tm = _tile(T, 512)

    def body(dcv_ref, c_ref, lg_ref, lb_ref, dc_ref, dlg_ref, dlb_ref, dbdw_ref):
        @pl.when(pl.program_id(0) == 0)
        def _():
            dlg_ref[...] = jnp.zeros_like(dlg_ref)
            dlb_ref[...] = jnp.zeros_like(dlb_ref)
            dbdw_ref[...] = jnp.zeros_like(dbdw_ref)
        gain = lg_ref[...]
        xh, rstd = _ln_stats(c_ref[...])
        y = xh * gain + lb_ref[...]
        s = _sig(y)
        dy = dcv_ref[...] * (s * (1.0 + y * (1.0 - s)))
        dlg_ref[...] += _colsum(dy * xh)
        dlb_ref[...] += _colsum(dy)
        dc = _ln_bwd(dy, xh, rstd, gain)
        dc_ref[...] = dc
        dbdw_ref[...] += _colsum(dc)

    tile = pl.BlockSpec((tm, B), lambda i: (i, 0))
    vec = pl.BlockSpec((1, B), lambda i: (0, 0))
    return pl.pallas_call(
        body, name=name, grid=(T // tm,), in_specs=[tile, tile, vec, vec], out_specs=[tile, vec, vec, vec],
        out_shape=[jax.ShapeDtypeStruct((T, B), F32)] + [jax.ShapeDtypeStruct((1, B), F32)] * 3,
        compiler_params=_params("arbitrary"))(dcv, c, ln_g, ln_b)


def _conv_bwd(proj, dc, w_dw, dproj, *, name):
    T = proj.shape[0]
    B = proj.shape[1] // 8
    tm, nh, lb = _conv_tiles(T, B)
    nt = T // tm
    n_halo = T // CONV_HALO

    def body(ap_ref, gp_ref, a_ref, g_ref, dc_ref, dcn_ref, w_ref, _alias,
             dp_ref, dbin_ref, dw_ref, z_sc, dc_sc, dz_sc, dw_sc):
        i = pl.program_id(0)

        @pl.when(i == 0)
        def _():
            dbin_ref[...] = jnp.zeros_like(dbin_ref)
            dw_sc[...] = jnp.zeros_like(dw_sc)

        a = a_ref[...]
        s = _sig(g_ref[...])
        z_sc[0:CONV_HALO, :] = jnp.where(i > 0, ap_ref[...] * _sig(gp_ref[...]), 0.0)
        z_sc[CONV_HALO:CONV_HALO + tm, :] = a * s
        dc_sc[0:tm, :] = dc_ref[...]
        dc_sc[tm:tm + CONV_HALO, :] = jnp.where(i < nt - 1, dcn_ref[...], 0.0)
        for cb in range(B // lb):
            ls = slice(cb * lb, (cb + 1) * lb)
            for rc in range(tm // CONV_ROWS):
                r0 = rc * CONV_ROWS
                dcc = dc_sc[r0:r0 + CONV_ROWS, ls]
                acc = jnp.zeros((CONV_ROWS, lb), F32)
                for k in range(CONV_WIDTH):
                    up = r0 + (CONV_WIDTH - 1) - k
                    acc = acc + w_ref[k:k + 1, ls] * dc_sc[up:up + CONV_ROWS, ls]
                    dn = r0 + CONV_HALO - (CONV_WIDTH - 1) + k
                    prod = dcc * z_sc[dn:dn + CONV_ROWS, ls]
                    dw_sc[8 * k:8 * k + 8, ls] += jnp.sum(prod.reshape(CONV_ROWS // 8, 8, lb), axis=0)
                dz_sc[r0:r0 + CONV_ROWS, ls] = acc
        dz = dz_sc[...]
        da = dz * s
        dg = dz * a * s * (1.0 - s)
        dp_ref[:, 0:B] = da.astype(BF16)
        dp_ref[:, B:2 * B] = dg.astype(BF16)
        dbin_ref[:, 0:B] += _colsum(da)
        dbin_ref[:, B:2 * B] += _colsum(dg)

        @pl.when(i == nt - 1)
        def _():
            for k in range(CONV_WIDTH):
                dw_ref[k:k + 1, :] = _colsum(dw_sc[8 * k:8 * k + 8, :])

    halo_a = pl.BlockSpec((CONV_HALO, B), lambda i: (jnp.maximum(i * nh - 1, 0), 2))
    halo_g = pl.BlockSpec((CONV_HALO, B), lambda i: (jnp.maximum(i * nh - 1, 0), 3))
    halo_dc = pl.BlockSpec((CONV_HALO, B), lambda i: (jnp.minimum((i + 1) * nh, n_halo - 1), 0))
    return pl.pallas_call(
        body, name=name, grid=(nt,),
        in_specs=[halo_a, halo_g, pl.BlockSpec((tm, B), lambda i: (i, 2)), pl.BlockSpec((tm, B), lambda i: (i, 3)),
                  pl.BlockSpec((tm, B), lambda i: (i, 0)), halo_dc,
                  pl.BlockSpec((CONV_WPAD, B), lambda i: (0, 0)), ANY],
        out_specs=[pl.BlockSpec((tm, 2 * B), lambda i: (i, 1)), pl.BlockSpec((1, 2 * B), lambda i: (0, 0)),
                   pl.BlockSpec((CONV_WIDTH, B), lambda i: (0, 0))],
        out_shape=[jax.ShapeDtypeStruct(dproj.shape, BF16), jax.ShapeDtypeStruct((1, 2 * B), F32),
                   jax.ShapeDtypeStruct((CONV_WIDTH, B), F32)],
        scratch_shapes=[pltpu.VMEM((CONV_HALO + tm, B), F32), pltpu.VMEM((tm + CONV_HALO, B), F32),
                        pltpu.VMEM((tm, B), F32), pltpu.VMEM((8 * CONV_WIDTH, B), F32)],
        input_output_aliases={7: 0},
        compiler_params=_params("arbitrary"))(proj, proj, proj, proj, dc, dc, w_dw, dproj)


def _mix_fwd_gate(sg, cv, proj, wa, wb, *, name):
    T, A = sg.shape
    D = wa.shape[1]
    tm = _tile(T, 256)

    def body(sg_ref, cv_ref, la_ref, lb_ref, wa_ref, wb_ref, ya_ref, yb_ref, m_ref):
        ya = _dot(sg_ref[...], wa_ref[...])
        yb = _dot(cv_ref[...], wb_ref[...])
        ya_ref[...] = ya.astype(BF16)
        yb_ref[...] = yb.astype(BF16)
        m_ref[...] = (_sig(la_ref[...]) * ya + _sig(lb_ref[...]) * yb).astype(BF16)

    act = pl.BlockSpec((tm, A), lambda i: (i, 0))
    wide = pl.BlockSpec((tm, D), lambda i: (i, 0))
    wspec = pl.BlockSpec((A, D), lambda i: (0, 0))
    return pl.pallas_call(
        body, name=name, grid=(T // tm,),
        in_specs=[act, act, pl.BlockSpec((tm, D), lambda i: (i, 2)), pl.BlockSpec((tm, D), lambda i: (i, 3)),
                  wspec, wspec],
        out_specs=[wide] * 3, out_shape=[jax.ShapeDtypeStruct((T, D), BF16)] * 3,
        compiler_params=_params("parallel"))(sg, cv, proj, proj, wa, wb)


def _mix_fwd_out(m, wout, xh, lg, lb, *, name):
    T, D = xh.shape
    tm = _tile(T, 512)

    def body(m_ref, w_ref, xh_ref, lg_ref, lb_ref, xho_ref, rstd_ref):
        r = ALPHA * (xh_ref[...] * lg_ref[...] + lb_ref[...]) + _dot(m_ref[...], w_ref[...])
        xho, rstd = _ln_stats(r)
        xho_ref[...] = xho
        rstd_ref[...] = jnp.broadcast_to(rstd, (tm, LANES))

    row = pl.BlockSpec((tm, D), lambda i: (i, 0))
    vec = pl.BlockSpec((1, D), lambda i: (0, 0))
    return pl.pallas_call(
        body, name=name, grid=(T // tm,),
        in_specs=[row, pl.BlockSpec((D, D), lambda i: (0, 0)), row, vec, vec],
        out_specs=[row, pl.BlockSpec((tm, LANES), lambda i: (i, 0))],
        out_shape=[jax.ShapeDtypeStruct((T, D), F32), jax.ShapeDtypeStruct((T, LANES), F32)],
        compiler_params=_params("parallel"))(m, wout, xh, lg, lb)


def _mix_bwd_gate(dr, wout, proj, ya, yb, *, name):
    T, D = dr.shape
    N = proj.shape[1]
    tm = _tile(T, 256)

    def body(dr_ref, w_ref, la_ref, lb_ref, ya_ref, yb_ref, dya_ref, dyb_ref, dp_ref, dbin_ref):
        @pl.when(pl.program_id(0) == 0)
        def _():
            dbin_ref[...] = jnp.zeros_like(dbin_ref)
        dm = _dot_nt(dr_ref[...].astype(BF16), w_ref[...])
        sa = _sig(la_ref[...])
        sb = _sig(lb_ref[...])
        dya_ref[...] = (dm * sa).astype(BF16)
        dyb_ref[...] = (dm * sb).astype(BF16)
        dla = dm * ya_ref[...].astype(F32) * sa * (1.0 - sa)
        dlb = dm * yb_ref[...].astype(F32) * sb * (1.0 - sb)
        dp_ref[:, 0:D] = dla.astype(BF16)
        dp_ref[:, D:2 * D] = dlb.astype(BF16)
        dbin_ref[:, 0:D] += _colsum(dla)
        dbin_ref[:, D:2 * D] += _colsum(dlb)

    row = pl.BlockSpec((tm, D), lambda i: (i, 0))
    return pl.pallas_call(
        body, name=name, grid=(T // tm,),
        in_specs=[row, pl.BlockSpec((D, D), lambda i: (0, 0)), pl.BlockSpec((tm, D), lambda i: (i, 2)),
                  pl.BlockSpec((tm, D), lambda i: (i, 3)), row, row],
        out_specs=[row, row, pl.BlockSpec((tm, 2 * D), lambda i: (i, 1)), pl.BlockSpec((1, 2 * D), lambda i: (0, 0))],
        out_shape=[jax.ShapeDtypeStruct((T, D), BF16), jax.ShapeDtypeStruct((T, D), BF16),
                   jax.ShapeDtypeStruct((T, N), BF16), jax.ShapeDtypeStruct((1, 2 * D), F32)],
        compiler_params=_params("arbitrary"))(dr, wout, proj, proj, ya, yb)


def _mix_bwd_proj(dya, dyb, wa, wb, *, name):
    T, D = dya.shape
    A = wa.shape[0]
    tm = _tile(T, 512)

    def body(dya_ref, dyb_ref, wa_ref, wb_ref, dsg_ref, dcv_ref):
        dsg_ref[...] = _dot_nt(dya_ref[...], wa_ref[...])
        dcv_ref[...] = _dot_nt(dyb_ref[...], wb_ref[...])

    row = pl.BlockSpec((tm, D), lambda i: (i, 0))
    wspec = pl.BlockSpec((A, D), lambda i: (0, 0))
    act = pl.BlockSpec((tm, A), lambda i: (i, 0))
    return pl.pallas_call(
        body, name=name, grid=(T // tm,), in_specs=[row, row, wspec, wspec], out_specs=[act, act],
        out_shape=[jax.ShapeDtypeStruct((T, A), F32)] * 2,
        compiler_params=_params("parallel"))(dya, dyb, wa, wb)


def _mesh_pos():
    return lax.axis_index("x"), lax.axis_index("y"), lax.axis_index("c")


def _shard_view(ref, p, shape, axis):
    r, c = shape
    if axis == 0:
        return ref.at[pl.ds(pl.multiple_of(p * r, 16), r), :]
    return ref.at[:, pl.ds(pl.multiple_of(p * c, LANES), c)]


def _all_gather(shards, axes, *, name):
    n = len(shards)
    shapes = [s.shape for s in shards]

    def body(*refs):
        ins, outs = refs[:n], refs[n:2 * n]
        send_sems, recv_sems, local_sems = refs[2 * n:]
        x, y, c = _mesh_pos()
        me, sibling = (x, y, c), (x, y, 1 - c)
        chips = [(1 - x, y), (x, 1 - y), (1 - x, 1 - y)]

        def view(t, pos):
            px, py, pc = pos
            return _shard_view(outs[t], 4 * px + 2 * py + pc, shapes[t], axes[t])

        def copy(t, k, block, to, src=None):
            return pltpu.make_async_remote_copy(
                src_ref=view(t, block) if src is None else src, dst_ref=view(t, block),
                send_sem=send_sems.at[7 * t + k], recv_sem=recv_sems.at[7 * t + k],
                device_id=to, device_id_type=MESH)

        mine = [pltpu.make_async_copy(ins[t], view(t, me), local_sems.at[t]) for t in range(n)]
        for cp in mine:
            cp.start()
        started = []
        for t in range(n):
            first = [copy(t, 0, me, sibling, src=ins[t])]
            first += [copy(t, 1 + j, me, (*chip, c), src=ins[t]) for j, chip in enumerate(chips)]
            for cp in first:
                cp.start()
            started += first
        for t in range(n):
            for j, chip in enumerate(chips):
                copy(t, 1 + j, (*chip, c), me).wait_recv()
                fwd = copy(t, 4 + j, (*chip, c), sibling)
                fwd.start()
                started.append(fwd)
        for t in range(n):
            copy(t, 0, sibling, me).wait_recv()
            for j, chip in enumerate(chips):
                copy(t, 4 + j, (*chip, 1 - c), me).wait_recv()
        for cp in started:
            cp.wait_send()
        for cp in mine:
            cp.wait()

    out_shape = [jax.ShapeDtypeStruct((N_DEV * s.shape[0], s.shape[1]) if ax == 0
                                      else (s.shape[0], N_DEV * s.shape[1]), s.dtype)
                 for s, ax in zip(shards, axes)]
    return pl.pallas_call(
        body, name=name, in_specs=[ANY] * n, out_specs=[ANY] * n, out_shape=out_shape,
        scratch_shapes=[pltpu.SemaphoreType.DMA((7 * n,)), pltpu.SemaphoreType.DMA((7 * n,)),
                        pltpu.SemaphoreType.DMA((n,))],
        compiler_params=pltpu.CompilerParams(has_side_effects=True))(*shards)


def _rs_to_sibling(grads, shapes, axes, *, name):
    n = len(grads)

    def body(*refs):
        gs, outs = refs[:n], refs[n:2 * n]
        send_sems, recv_sems = refs[2 * n:]
        x, y, c = _mesh_pos()
        copies = []
        for t in range(n):
            for k in range(4):
                cp = pltpu.make_async_remote_copy(
                    src_ref=_shard_view(gs[t], 2 * k + (1 - c), shapes[t], axes[t]), dst_ref=outs[t].at[k],
                    send_sem=send_sems.at[4 * t + k], recv_sem=recv_sems.at[4 * t + k],
                    device_id=(x, y, 1 - c), device_id_type=MESH)
                cp.start()
                copies.append(cp)
        for cp in copies:
            cp.wait_recv()
        for cp in copies:
            cp.wait_send()

    return pl.pallas_call(
        body, name=name, in_specs=[ANY] * n, out_specs=[ANY] * n,
        out_shape=[jax.ShapeDtypeStruct((4,) + tuple(s), BF16) for s in shapes],
        scratch_shapes=[pltpu.SemaphoreType.DMA((4 * n,)), pltpu.SemaphoreType.DMA((4 * n,))],
        compiler_params=pltpu.CompilerParams(has_side_effects=True))(*grads)


def _rs_pair_sum(g, recv, cidx, shape, axis, *, name):
    r, c = shape
    tr = _tile(r, max(8, (1 << 20) // c), 16)
    nr = r // tr

    def body(c_ref, g_ref, rv_ref, o_ref):
        o_ref[...] = (g_ref[...].astype(F32) + rv_ref[...].astype(F32)).astype(BF16)

    if axis == 1:
        g_spec = pl.BlockSpec((tr, c), lambda k, i, s: (i, 2 * k + s[0]))
    else:
        g_spec = pl.BlockSpec((tr, c), lambda k, i, s: ((2 * k + s[0]) * nr + i, 0))
    blk = pl.BlockSpec((None, tr, c), lambda k, i, s: (k, i, 0))
    return pl.pallas_call(
        body, name=name,
        grid_spec=pltpu.PrefetchScalarGridSpec(num_scalar_prefetch=1, grid=(4, nr), in_specs=[g_spec, blk],
                                               out_specs=blk),
        out_shape=jax.ShapeDtypeStruct((4, r, c), BF16),
        compiler_params=_params("parallel", "parallel"))(cidx, g, recv)


def _rs_to_chips(parts, *, name):
    n = len(parts)

    def body(*refs):
        ps, outs = refs[:n], refs[n:2 * n]
        send_sems, recv_sems, local_sems = refs[2 * n:]
        x, y, c = _mesh_pos()
        my_chip = 2 * x + y
        peers = [(1 - x, y), (x, 1 - y), (1 - x, 1 - y)]
        local = [pltpu.make_async_copy(ps[t].at[my_chip], outs[t].at[my_chip], local_sems.at[t]) for t in range(n)]
        for cp in local:
            cp.start()
        sends = []
        for t in range(n):
            for j, (px, py) in enumerate(peers):
                cp = pltpu.make_async_remote_copy(
                    src_ref=ps[t].at[2 * px + py], dst_ref=outs[t].at[my_chip],
                    send_sem=send_sems.at[3 * t + j], recv_sem=recv_sems.at[3 * t + j],
                    device_id=(px, py, c), device_id_type=MESH)
                cp.start()
                sends.append(cp)
        for t in range(n):
            for j, (px, py) in enumerate(peers):
                pltpu.make_async_remote_copy(
                    src_ref=ps[t].at[2 * px + py], dst_ref=outs[t].at[2 * px + py],
                    send_sem=send_sems.at[3 * t + j], recv_sem=recv_sems.at[3 * t + j],
                    device_id=(x, y, c), device_id_type=MESH).wait_recv()
        for cp in sends:
            cp.wait_send()
        for cp in local:
            cp.wait()

    return pl.pallas_call(
        body, name=name, in_specs=[ANY] * n, out_specs=[ANY] * n,
        out_shape=[jax.ShapeDtypeStruct(p.shape, BF16) for p in parts],
        scratch_shapes=[pltpu.SemaphoreType.DMA((3 * n,)), pltpu.SemaphoreType.DMA((3 * n,)),
                        pltpu.SemaphoreType.DMA((n,))],
        compiler_params=pltpu.CompilerParams(has_side_effects=True))(*parts)


def _all_reduce_small(buf, *, name):
    R = buf.shape[0]

    def body(in_ref, out_ref, slots, send_sems, recv_sems, local_sem):
        x, y, c = _mesh_pos()
        me = 4 * x + 2 * y + c
        local = pltpu.make_async_copy(in_ref, slots.at[me], local_sem)
        local.start()
        flips = [(fx, fy, fc) for fx in (0, 1) for fy in (0, 1) for fc in (0, 1)][1:]
        peers = [(1 - x if fx else x, 1 - y if fy else y, 1 - c if fc else c) for fx, fy, fc in flips]
        sends = []
        for k, peer in enumerate(peers):
            cp = pltpu.make_async_remote_copy(src_ref=in_ref, dst_ref=slots.at[me], send_sem=send_sems.at[k],
                                              recv_sem=recv_sems.at[k], device_id=peer, device_id_type=MESH)
            cp.start()
            sends.append(cp)
        for k, (px, py, pc) in enumerate(peers):
            pltpu.make_async_remote_copy(src_ref=in_ref, dst_ref=slots.at[4 * px + 2 * py + pc],
                                         send_sem=send_sems.at[k], recv_sem=recv_sems.at[k],
                                         device_id=(x, y, c), device_id_type=MESH).wait_recv()
        for cp in sends:
            cp.wait_send()
        local.wait()
        acc = slots[0]
        for p in range(1, N_DEV):
            acc = acc + slots[p]
        out_ref[...] = acc

    vm = pl.BlockSpec(memory_space=pltpu.VMEM)
    return pl.pallas_call(
        body, name=name, in_specs=[vm], out_specs=vm, out_shape=jax.ShapeDtypeStruct(buf.shape, F32),
        scratch_shapes=[pltpu.VMEM((N_DEV, R, LANES), F32), pltpu.SemaphoreType.DMA((7,)),
                        pltpu.SemaphoreType.DMA((7,)), pltpu.SemaphoreType.DMA],
        compiler_params=pltpu.CompilerParams(has_side_effects=True, vmem_limit_bytes=VMEM_LIMIT_BYTES))(buf)


def _adam_math(g, w, m, v):
    m_new = ADAM_B1 * m + (1.0 - ADAM_B1) * g
    v_new = ADAM_B2 * v + (1.0 - ADAM_B2) * (g * g)
    m_hat = m_new / ADAM_C1
    v_hat = v_new / ADAM_C2
    delta = -ADAM_LR * (m_hat / (jnp.sqrt(v_hat) + ADAM_EPS) + ADAM_WD * w)
    return delta, m_new, v_new


def _adamw_sharded(q, w, m, v, *, name):
    r, c = w.shape
    tr = _tile(r, max(8, (1 << 18) // c), 16)

    def body(q_ref, w_ref, m_ref, v_ref, g_ref, d_ref, mo_ref, vo_ref):
        g = ((q_ref[0].astype(F32) + q_ref[1].astype(F32)) + q_ref[2].astype(F32)) + q_ref[3].astype(F32)
        g_ref[...] = g
        d_ref[...], mo_ref[...], vo_ref[...] = _adam_math(g, w_ref[...], m_ref[...], v_ref[...])

    blk = pl.BlockSpec((tr, c), lambda i: (i, 0))
    return pl.pallas_call(
        body, name=name, grid=(r // tr,),
        in_specs=[pl.BlockSpec((4, tr, c), lambda i: (0, i, 0)), blk, blk, blk], out_specs=[blk] * 4,
        out_shape=[jax.ShapeDtypeStruct((r, c), F32)] * 4,
        compiler_params=_params("parallel"))(q, w, m, v)


def _adamw_plain(g, w, m, v, *, name):
    r, c = w.shape
    tr = _tile(r, 512)

    def body(g_ref, w_ref, m_ref, v_ref, d_ref, mo_ref, vo_ref):
        d_ref[...], mo_ref[...], vo_ref[...] = _adam_math(g_ref[...], w_ref[...], m_ref[...], v_ref[...])

    blk = pl.BlockSpec((tr, c), lambda i: (i, 0))
    return pl.pallas_call(
        body, name=name, grid=(r // tr,), in_specs=[blk] * 4, out_specs=[blk] * 3,
        out_shape=[jax.ShapeDtypeStruct((r, c), F32)] * 3,
        compiler_params=_params("parallel"))(g, w, m, v)


def _pack_rows(arrays):
    return jnp.concatenate([a.reshape(-1, LANES) for a in arrays], axis=0)


def kernel(x, ffn1_w_gu, ffn1_w_down, ln1_g, ln1_b, w_in, b_in, sgu_ln_g, sgu_ln_b, sgu_w_s, sgu_b_s, w_a_proj, conv_w_dw, conv_b_dw, conv_ln_g, conv_ln_b, w_b_proj, w_out, ln2_g, ln2_b, ffn2_w_gu, ffn2_w_down, ln3_g, ln3_b, loss_target, m_ffn1_w_gu, m_ffn1_w_down, m_ln1_g, m_ln1_b, m_w_in, m_b_in, m_sgu_ln_g, m_sgu_ln_b, m_sgu_w_s, m_sgu_b_s, m_w_a_proj, m_conv_w_dw, m_conv_b_dw, m_conv_ln_g, m_conv_ln_b, m_w_b_proj, m_w_out, m_ln2_g, m_ln2_b, m_ffn2_w_gu, m_ffn2_w_down, m_ln3_g, m_ln3_b, v_ffn1_w_gu, v_ffn1_w_down, v_ln1_g, v_ln1_b, v_w_in, v_b_in, v_sgu_ln_g, v_sgu_ln_b, v_sgu_w_s, v_sgu_b_s, v_w_a_proj, v_conv_w_dw, v_conv_b_dw, v_conv_ln_g, v_conv_ln_b, v_w_b_proj, v_w_out, v_ln2_g, v_ln2_b, v_ffn2_w_gu, v_ffn2_w_down, v_ln3_g, v_ln3_b):
    given = dict(locals())
    w = {n: given[n][0] for n in WEIGHTS}
    mom = {n: given["m_" + n][0] for n in WEIGHTS}
    var = {n: given["v_" + n][0] for n in WEIGHTS}
    xt = x[0]
    target = loss_target[0]
    T, D = xt.shape
    A = w['w_a_proj'].shape[0]

    big_names = list(BIG)
    conv_w_pad = jnp.pad(w['conv_w_dw'], ((0, CONV_WPAD - CONV_WIDTH), (0, 0)))
    shards = [w[n].astype(BF16) for n in big_names] + [conv_w_pad]
    axes = [BIG[n] for n in big_names] + [1]
    gathered = _all_gather(shards, axes, name="all_gather_weights")
    full = dict(zip(big_names, gathered[:-1]))
    conv_w_full = gathered[-1]

    def row(v):
        return v.reshape(1, -1)

    ones = jnp.ones((1, D), F32)
    zeros = jnp.zeros((1, D), F32)
    w_s = w['sgu_w_s']
    w_st = jnp.swapaxes(w_s, 1, 2)
    b_sb = jnp.broadcast_to(w['sgu_b_s'][:, :, None], w_s.shape)

    gate1, up1, xh1, rstd1 = _ffn_fwd(xt, ones, zeros, full['ffn1_w_gu'], full['ffn1_w_down'],
                                      affine=False, name="ffn1_fwd")
    g1, b1 = row(w['ln1_g']), row(w['ln1_b'])
    proj = _inproj_fwd(xh1, g1, b1, full['w_in'], row(w['b_in']), name="inproj_fwd")
    sg = _sgu_fwd(proj, row(w['sgu_ln_g']), row(w['sgu_ln_b']), w_s, b_sb, name="sgu_fwd")
    conv_out, cv = _conv_fwd(proj, conv_w_full, row(w['conv_b_dw']), row(w['conv_ln_g']), row(w['conv_ln_b']),
                             name="conv_fwd")
    ya, yb, mixed = _mix_fwd_gate(sg, cv, proj, full['w_a_proj'], full['w_b_proj'], name="mix_fwd_gate")
    xh2, rstd2 = _mix_fwd_out(mixed, full['w_out'], xh1, g1, b1, name="mix_fwd_out")
    g2, b2 = row(w['ln2_g']), row(w['ln2_b'])
    gate2, up2, dr3, loss_part, d_ln3_g, d_ln3_b = _ffn_fwd(
        xh2, g2, b2, full['ffn2_w_gu'], full['ffn2_w_down'], affine=True, name="ffn2_fwd_loss",
        final=(row(w['ln3_g']), row(w['ln3_b']), target))

    F = full['ffn2_w_down'].shape[0]
    h2, dgate2, dup2, dr2, d_ln2_g, d_ln2_b = _ffn_bwd(dr3, gate2, up2, full['ffn2_w_gu'], full['ffn2_w_down'],
                                                      name="ffn2_bwd", prev=(xh2, rstd2, g2))
    G = {}
    G['ffn2_w_down'] = _mm_tn(h2, dr3, name="dw_ffn2_down", tm_pref=1408, tn_pref=2048, scale=0.5)
    gu = _mm_tn(xh2, dgate2, name="dw_ffn2_gate", tm_pref=2048, tn_pref=1408, a_affine=(g2, b2), n_total=2 * F)
    G['ffn2_w_gu'] = _mm_tn(xh2, dup2, name="dw_ffn2_up", tm_pref=2048, tn_pref=1408, a_affine=(g2, b2),
                            into=gu, col_off=F, n_total=2 * F)

    dya, dyb, dproj, dbin_gate = _mix_bwd_gate(dr2, full['w_out'], proj, ya, yb, name="mix_bwd_gate")
    G['w_out'] = _mm_tn(mixed, dr2, name="dw_out", tm_pref=2048, tn_pref=1024)
    G['w_a_proj'] = _mm_tn(sg, dya, name="dw_a_proj", tm_pref=1024, tn_pref=2048)
    G['w_b_proj'] = _mm_tn(cv, dyb, name="dw_b_proj", tm_pref=1024, tn_pref=2048)
    dsg, dcv = _mix_bwd_proj(dya, dyb, full['w_a_proj'], full['w_b_proj'], name="mix_bwd_proj")

    dproj, dbin_sgu, d_sgu_ln_g, d_sgu_ln_b, d_w_s, d_b_s = _sgu_bwd(
        proj, dsg, row(w['sgu_ln_g']), row(w['sgu_ln_b']), w_s, w_st, b_sb, dproj, name="sgu_bwd")
    dconv, d_conv_ln_g, d_conv_ln_b, d_conv_b = _conv_bwd_ln(dcv, conv_out, row(w['conv_ln_g']),
                                                            row(w['conv_ln_b']), name="conv_bwd_ln")
    dproj, dbin_conv, d_conv_w = _conv_bwd(proj, dconv, conv_w_full, dproj, name="conv_bwd")

    dr1, d_ln1_g, d_ln1_b = _inproj_bwd(dproj, full['w_in'], dr2, xh1, rstd1, g1, name="inproj_bwd")
    G['w_in'] = _mm_tn(xh1, dproj, name="dw_in", tm_pref=2048, tn_pref=1024, a_affine=(g1, b1))

    h1, dgate1, dup1, grad_x = _ffn_bwd(dr1, gate1, up1, full['ffn1_w_gu'], full['ffn1_w_down'], name="ffn1_bwd")
    G['ffn1_w_down'] = _mm_tn(h1, dr1, name="dw_ffn1_down", tm_pref=1408, tn_pref=2048, scale=0.5)
    gu = _mm_tn(xt, dgate1, name="dw_ffn1_gate", tm_pref=2048, tn_pref=1408, n_total=2 * F)
    G['ffn1_w_gu'] = _mm_tn(xt, dup1, name="dw_ffn1_up", tm_pref=2048, tn_pref=1408, into=gu, col_off=F,
                            n_total=2 * F)

    cidx = lax.axis_index("c").astype(jnp.int32).reshape(1)
    shapes = [w[n].shape for n in big_names]
    big_axes = [BIG[n] for n in big_names]
    from_sibling = _rs_to_sibling([G[n] for n in big_names], shapes, big_axes, name="rs_to_sibling")
    parts = [_rs_pair_sum(G[n], rv, cidx, shp, ax, name="rs_pair_sum_" + n)
             for n, rv, shp, ax in zip(big_names, from_sibling, shapes, big_axes)]
    chip_parts = _rs_to_chips(parts, name="rs_to_chips")
    grads, deltas, new_m, new_v = {}, {}, {}, {}
    for n, q in zip(big_names, chip_parts):
        grads[n], deltas[n], new_m[n], new_v[n] = _adamw_sharded(q, w[n], mom[n], var[n], name="adamw_" + n)

    B = conv_w_full.shape[1]
    small_g = {'ln1_g': d_ln1_g, 'ln1_b': d_ln1_b,
               'b_in': jnp.concatenate([dbin_sgu, dbin_conv, dbin_gate], axis=1),
               'sgu_ln_g': d_sgu_ln_g, 'sgu_ln_b': d_sgu_ln_b, 'sgu_w_s': d_w_s, 'sgu_b_s': d_b_s,
               'conv_b_dw': d_conv_b, 'conv_ln_g': d_conv_ln_g, 'conv_ln_b': d_conv_ln_b,
               'ln2_g': d_ln2_g, 'ln2_b': d_ln2_b, 'ln3_g': d_ln3_g, 'ln3_b': d_ln3_b}
    packed = _pack_rows([small_g[n] for n in SMALL] + [d_conv_w, loss_part])
    reduced = _all_reduce_small(packed, name="all_reduce_small")
    n_small_rows = sum(w[n].size for n in SMALL) // LANES
    conv_rows = CONV_WIDTH * B // LANES
    d_small, m_small, v_small = _adamw_plain(
        reduced[:n_small_rows], _pack_rows([w[n] for n in SMALL]), _pack_rows([mom[n] for n in SMALL]),
        _pack_rows([var[n] for n in SMALL]), name="adamw_small")
    off = 0
    for n in SMALL:
        rows = w[n].size // LANES
        grads[n] = reduced[off:off + rows].reshape(w[n].shape)
        deltas[n] = d_small[off:off + rows].reshape(w[n].shape)
        new_m[n] = m_small[off:off + rows].reshape(w[n].shape)
        new_v[n] = v_small[off:off + rows].reshape(w[n].shape)
        off += rows
    conv_g_full = reduced[off:off + conv_rows].reshape(CONV_WIDTH, B)
    bs = w['conv_w_dw'].shape[1]
    my_block = 4 * lax.axis_index("x") + 2 * lax.axis_index("y") + lax.axis_index("c")
    grads['conv_w_dw'] = lax.dynamic_slice(conv_g_full, (0, my_block * bs), (CONV_WIDTH, bs))
    deltas['conv_w_dw'], new_m['conv_w_dw'], new_v['conv_w_dw'] = _adamw_plain(
        grads['conv_w_dw'], w['conv_w_dw'], mom['conv_w_dw'], var['conv_w_dw'], name="adamw_conv_w")
    loss = reduced[off + conv_rows, 0]

    def lead(a):
        return a[None]

    return (loss, grad_x[None], *[lead(grads[n]) for n in WEIGHTS], *[lead(deltas[n]) for n in WEIGHTS],
            *[lead(new_m[n]) for n in WEIGHTS], *[lead(new_v[n]) for n in WEIGHTS])
```

```python
import functools
import math

import jax
import jax.numpy as jnp
from jax import lax
from jax.experimental import pallas as pl
from jax.experimental.pallas import tpu as pltpu

F32 = jnp.float32
BF16 = jnp.bfloat16

ALPHA = 2.0 ** 0.25
LN_EPS = 1e-5
CONV_WIDTH = 31
CONV_HALO = 32
CONV_ROWS = 64
CONV_WPAD = 32
CHUNK = 64
GMLP_BLOCK = 128
A_GROUPS = 8
N_DEV = 8
LANES = 128

ADAM_LR = 0.001
ADAM_B1 = 0.9
ADAM_B2 = 0.999
ADAM_EPS = 1e-08
ADAM_WD = 0.01
ADAM_STEP = 10
ADAM_C1 = 1.0 - ADAM_B1 ** ADAM_STEP
ADAM_C2 = 1.0 - ADAM_B2 ** ADAM_STEP

VMEM_LIMIT_BYTES = 56 * 2 ** 20
MESH = pl.DeviceIdType.MESH
ANY = pl.BlockSpec(memory_space=pl.ANY)

WEIGHTS = ['ffn1_w_gu', 'ffn1_w_down', 'ln1_g', 'ln1_b', 'w_in', 'b_in', 'sgu_ln_g', 'sgu_ln_b', 'sgu_w_s',
           'sgu_b_s', 'w_a_proj', 'conv_w_dw', 'conv_b_dw', 'conv_ln_g', 'conv_ln_b', 'w_b_proj', 'w_out',
           'ln2_g', 'ln2_b', 'ffn2_w_gu', 'ffn2_w_down', 'ln3_g', 'ln3_b']
BIG = {'ffn1_w_gu': 1, 'ffn1_w_down': 0, 'w_in': 1, 'w_a_proj': 1, 'w_b_proj': 1, 'w_out': 0,
       'ffn2_w_gu': 1, 'ffn2_w_down': 0}
SMALL = [n for n in WEIGHTS if n not in BIG and n != 'conv_w_dw']


def _tile(n, pref, mult=8):
    best = None
    for d in range(mult, min(n, pref) + 1, mult):
        if n % d == 0:
            best = d
    return n if best is None else best


def _params(*sem):
    return pltpu.CompilerParams(dimension_semantics=sem, vmem_limit_bytes=VMEM_LIMIT_BYTES)


def _dot(a, b):
    return jnp.dot(a, b, preferred_element_type=F32)


def _dot_nt(a, b):
    return lax.dot_general(a, b, (((1,), (1,)), ((), ())), preferred_element_type=F32)


def _dot_tn(a, b):
    return lax.dot_general(a, b, (((0,), (0,)), ((), ())), preferred_element_type=F32)


def _sig(x):
    return 1.0 / (1.0 + jnp.exp(-x))


_GELU_K = math.sqrt(2.0 / math.pi)
_GELU_C = 0.044715


def _gelu(x):
    t = jnp.tanh(_GELU_K * (x + _GELU_C * x * x * x))
    return 0.5 * x * (1.0 + t)


def _gelu_grad(x):
    x2 = x * x
    t = jnp.tanh(_GELU_K * (x + _GELU_C * x2 * x))
    return 0.5 * (1.0 + t) + 0.5 * x * (1.0 - t * t) * (_GELU_K * (1.0 + 3.0 * _GELU_C * x2))


def _ln_stats(r):
    mu = jnp.mean(r, axis=-1, keepdims=True)
    rc = r - mu
    var = jnp.mean(rc * rc, axis=-1, keepdims=True)
    rstd = lax.rsqrt(var + LN_EPS)
    return rc * rstd, rstd


def _ln_bwd(dy, xh, rstd, g):
    dxh = dy * g
    m1 = jnp.mean(dxh, axis=-1, keepdims=True)
    m2 = jnp.mean(dxh * xh, axis=-1, keepdims=True)
    return rstd * (dxh - m1 - xh * m2)


def _colsum(v):
    return jnp.sum(v, axis=0, keepdims=True)


def _chunk_mask(transposed):
    shift = CHUNK.bit_length() - 1
    r = lax.broadcasted_iota(jnp.int32, (GMLP_BLOCK, GMLP_BLOCK), 0) >> shift
    c = lax.broadcasted_iota(jnp.int32, (GMLP_BLOCK, GMLP_BLOCK), 1) >> shift
    return (r <= c) if transposed else (c <= r)


class _Comm:
    def __init__(self, inputs, out_shape, scratch, run):
        self.inputs, self.out_shape, self.scratch, self.run = list(inputs), list(out_shape), list(scratch), run
        self.parts = [len(self.out_shape)]

    def split(self, outs):
        res, o = [], 0
        for n in self.parts:
            res.append(list(outs[o:o + n]))
            o += n
        return res


def _join(*comms):
    comms = [c for c in comms if c is not None]
    if not comms:
        return None

    def run(ins, outs, sems, phase):
        i = o = s = 0
        for c in comms:
            c.run(ins[i:i + len(c.inputs)], outs[o:o + len(c.out_shape)], sems[s:s + len(c.scratch)], phase)
            i, o, s = i + len(c.inputs), o + len(c.out_shape), s + len(c.scratch)

    joined = _Comm(sum((c.inputs for c in comms), []), sum((c.out_shape for c in comms), []),
                   sum((c.scratch for c in comms), []), run)
    joined.parts = [len(c.out_shape) for c in comms]
    return joined


def _call(body, *, name, grid, in_specs, out_specs, out_shape, args, sem, scratch_shapes=(), aliases=None, carry=None):
    in_specs, out_specs, out_shape = list(in_specs), list(out_specs), list(out_shape)
    scratch_shapes = list(scratch_shapes)
    if carry is None:
        return pl.pallas_call(body, name=name, grid=grid, in_specs=in_specs, out_specs=out_specs,
                              out_shape=out_shape, scratch_shapes=scratch_shapes,
                              input_output_aliases=aliases or {}, compiler_params=_params(*sem))(*args)
    n_in, n_out, n_scr = len(args), len(out_shape), len(scratch_shapes)
    c_in, c_out = len(carry.inputs), len(carry.out_shape)

    def wrapped(*refs):
        ins, c_ins = refs[:n_in], refs[n_in:n_in + c_in]
        o0 = n_in + c_in
        outs, c_outs = refs[o0:o0 + n_out], refs[o0 + n_out:o0 + n_out + c_out]
        s0 = o0 + n_out + c_out
        scr, c_sems = refs[s0:s0 + n_scr], refs[s0 + n_scr:]
        ids = [pl.program_id(a) for a in range(len(grid))]
        first = functools.reduce(jnp.logical_and, [i == 0 for i in ids])
        last = functools.reduce(jnp.logical_and, [i == g - 1 for i, g in zip(ids, grid)])

        @pl.when(first)
        def _():
            carry.run(c_ins, c_outs, c_sems, "start")

        body(*ins, *outs, *scr)

        @pl.when(last)
        def _():
            carry.run(c_ins, c_outs, c_sems, "finish")

    res = pl.pallas_call(
        wrapped, name=name, grid=grid, in_specs=in_specs + [ANY] * c_in, out_specs=out_specs + [ANY] * c_out,
        out_shape=out_shape + carry.out_shape, scratch_shapes=scratch_shapes + carry.scratch,
        input_output_aliases=aliases or {},
        compiler_params=pltpu.CompilerParams(dimension_semantics=("arbitrary",) * len(grid),
                                             vmem_limit_bytes=VMEM_LIMIT_BYTES, has_side_effects=True),
    )(*args, *carry.inputs)
    return list(res[:n_out]), list(res[n_out:])


def _comm_call(comm, *, name):
    n_in, n_out = len(comm.inputs), len(comm.out_shape)

    def body(*refs):
        ins, outs, sems = refs[:n_in], refs[n_in:n_in + n_out], refs[n_in + n_out:]
        comm.run(ins, outs, sems, "start")
        comm.run(ins, outs, sems, "finish")

    return list(pl.pallas_call(
        body, name=name, in_specs=[ANY] * n_in, out_specs=[ANY] * n_out, out_shape=comm.out_shape,
        scratch_shapes=comm.scratch, compiler_params=pltpu.CompilerParams(has_side_effects=True))(*comm.inputs))


def _ffn_fwd(xh, lg, lb, wgu, wd, *, affine, name, final=None, carry=None):
    T, D = xh.shape
    F = wd.shape[0]
    tm = _tile(T, 512)
    tn = _tile(F, 512, LANES)
    nj = F // tn
    is_final = final is not None

    def body(*refs):
        if is_final:
            (xh_ref, lg_ref, lb_ref, wg_ref, wu_ref, wd_ref, ng_ref, nb_ref, tgt_ref,
             gate_ref, up_ref, dr_ref, loss_ref, dng_ref, dnb_ref, xb_sc, acc_sc) = refs
        else:
            (xh_ref, lg_ref, lb_ref, wg_ref, wu_ref, wd_ref,
             gate_ref, up_ref, xho_ref, rstd_ref, xb_sc, acc_sc) = refs
        i = pl.program_id(0)
        j = pl.program_id(1)

        def xin():
            v = xh_ref[...]
            return v * lg_ref[...] + lb_ref[...] if affine else v

        @pl.when(j == 0)
        def _():
            xb_sc[...] = xin().astype(BF16)
            acc_sc[...] = jnp.zeros_like(acc_sc)

        xb = xb_sc[...]
        g = _dot(xb, wg_ref[...])
        u = _dot(xb, wu_ref[...])
        gate_ref[...] = g.astype(BF16)
        up_ref[...] = u.astype(BF16)
        h = g * _sig(g) * u
        acc_sc[...] += _dot(h.astype(BF16), wd_ref[...])

        @pl.when(j == nj - 1)
        def _():
            r = ALPHA * xin() + 0.5 * acc_sc[...]
            xho, rstd = _ln_stats(r)
            if not is_final:
                xho_ref[...] = xho
                rstd_ref[...] = jnp.broadcast_to(rstd, (tm, LANES))
            else:
                @pl.when(i == 0)
                def _():
                    loss_ref[...] = jnp.zeros_like(loss_ref)
                    dng_ref[...] = jnp.zeros_like(dng_ref)
                    dnb_ref[...] = jnp.zeros_like(dnb_ref)
                ng = ng_ref[...]
                e = xho * ng + nb_ref[...] - tgt_ref[...]
                part = _colsum(jnp.sum(e * e, axis=1, keepdims=True)) * (0.5 / D)
                loss_ref[...] += jnp.broadcast_to(part, loss_ref.shape)
                dy = e * (1.0 / D)
                dng_ref[...] += _colsum(dy * xho)
                dnb_ref[...] += _colsum(dy)
                dr_ref[...] = _ln_bwd(dy, xho, rstd, ng)

    row = pl.BlockSpec((tm, D), lambda i, j: (i, 0))
    vec = pl.BlockSpec((1, D), lambda i, j: (0, 0))
    hid = pl.BlockSpec((tm, tn), lambda i, j: (i, j))
    in_specs = [row, vec, vec,
                pl.BlockSpec((D, tn), lambda i, j: (0, j)),
                pl.BlockSpec((D, tn), lambda i, j: (0, j + nj)),
                pl.BlockSpec((tn, D), lambda i, j: (j, 0))]
    args = [xh, lg, lb, wgu, wgu, wd]
    out_shape = [jax.ShapeDtypeStruct((T, F), BF16), jax.ShapeDtypeStruct((T, F), BF16)]
    out_specs = [hid, hid]
    if is_final:
        in_specs += [vec, vec, row]
        args += list(final)
        out_shape += [jax.ShapeDtypeStruct((T, D), F32), jax.ShapeDtypeStruct((8, LANES), F32),
                      jax.ShapeDtypeStruct((1, D), F32), jax.ShapeDtypeStruct((1, D), F32)]
        out_specs += [row, pl.BlockSpec((8, LANES), lambda i, j: (0, 0)), vec, vec]
        sem = ("arbitrary", "arbitrary")
    else:
        out_shape += [jax.ShapeDtypeStruct((T, D), F32), jax.ShapeDtypeStruct((T, LANES), F32)]
        out_specs += [row, pl.BlockSpec((tm, LANES), lambda i, j: (i, 0))]
        sem = ("parallel", "arbitrary")
    return _call(body, name=name, grid=(T // tm, nj), in_specs=in_specs, out_specs=out_specs, out_shape=out_shape,
                 scratch_shapes=[pltpu.VMEM((tm, D), BF16), pltpu.VMEM((tm, D), F32)], sem=sem, args=args,
                 carry=carry)


def _ffn_bwd(dr, gate, up, wgu, wd, *, name, prev=None, carry=None):
    T, D = dr.shape
    F = wd.shape[0]
    tm = _tile(T, 512)
    tn = _tile(F, 512, LANES)
    nj = F // tn
    has_prev = prev is not None

    def body(*refs):
        if has_prev:
            (dr_ref, gate_ref, up_ref, wd_ref, wg_ref, wu_ref, xh_ref, rstd_ref, lg_ref,
             h_ref, dg_ref, du_ref, dprev_ref, dlg_ref, dlb_ref, df_sc, dx_sc) = refs
        else:
            (dr_ref, gate_ref, up_ref, wd_ref, wg_ref, wu_ref,
             h_ref, dg_ref, du_ref, dprev_ref, df_sc, dx_sc) = refs
        i = pl.program_id(0)
        j = pl.program_id(1)

        @pl.when(j == 0)
        def _():
            d = dr_ref[...]
            df_sc[...] = (0.5 * d).astype(BF16)
            dx_sc[...] = ALPHA * d

        g = gate_ref[...].astype(F32)
        u = up_ref[...].astype(F32)
        dh = _dot_nt(df_sc[...], wd_ref[...])
        s = _sig(g)
        sil = g * s
        h_ref[...] = (sil * u).astype(BF16)
        dg = (dh * u * (s * (1.0 + g * (1.0 - s)))).astype(BF16)
        du = (dh * sil).astype(BF16)
        dg_ref[...] = dg
        du_ref[...] = du
        dx_sc[...] += _dot_nt(dg, wg_ref[...]) + _dot_nt(du, wu_ref[...])

        @pl.when(j == nj - 1)
        def _():
            dxin = dx_sc[...]
            if not has_prev:
                dprev_ref[...] = dxin
            else:
                @pl.when(i == 0)
                def _():
                    dlg_ref[...] = jnp.zeros_like(dlg_ref)
                    dlb_ref[...] = jnp.zeros_like(dlb_ref)
                xh = xh_ref[...]
                dlg_ref[...] += _colsum(dxin * xh)
                dlb_ref[...] += _colsum(dxin)
                dprev_ref[...] = _ln_bwd(dxin, xh, rstd_ref[:, 0:1], lg_ref[...])

    row = pl.BlockSpec((tm, D), lambda i, j: (i, 0))
    vec = pl.BlockSpec((1, D), lambda i, j: (0, 0))
    hid = pl.BlockSpec((tm, tn), lambda i, j: (i, j))
    in_specs = [row, hid, hid,
                pl.BlockSpec((tn, D), lambda i, j: (j, 0)),
                pl.BlockSpec((D, tn), lambda i, j: (0, j)),
                pl.BlockSpec((D, tn), lambda i, j: (0, j + nj))]
    args = [dr, gate, up, wd, wgu, wgu]
    out_shape = [jax.ShapeDtypeStruct((T, F), BF16)] * 3 + [jax.ShapeDtypeStruct((T, D), F32)]
    out_specs = [hid, hid, hid, row]
    if has_prev:
        in_specs += [row, pl.BlockSpec((tm, LANES), lambda i, j: (i, 0)), vec]
        args += list(prev)
        out_shape += [jax.ShapeDtypeStruct((1, D), F32)] * 2
        out_specs += [vec, vec]
        sem = ("arbitrary", "arbitrary")
    else:
        sem = ("parallel", "arbitrary")
    return _call(body, name=name, grid=(T // tm, nj), in_specs=in_specs, out_specs=out_specs, out_shape=out_shape,
                 scratch_shapes=[pltpu.VMEM((tm, D), BF16), pltpu.VMEM((tm, D), F32)], sem=sem, args=args,
                 carry=carry)


def _mm_tn(a, b, *, name, tm_pref, tn_pref, scale=1.0, a_affine=None, into=None, col_off=0, n_total=None,
           carry=None):
    T, M = a.shape
    N = b.shape[1]
    n_total = N if n_total is None else n_total
    tM = _tile(M, tm_pref, LANES)
    tN = _tile(N, tn_pref, LANES)
    tk = _tile(T, 512)
    nt = T // tk
    assert col_off % tN == 0
    off_blocks = col_off // tN
    has_aff = a_affine is not None
    has_into = into is not None

    def body(*refs):
        refs = list(refs)
        a_ref = refs.pop(0)
        if has_aff:
            lg_ref = refs.pop(0)
            lb_ref = refs.pop(0)
        b_ref = refs.pop(0)
        if has_into:
            refs.pop(0)
        o_ref, acc_sc = refs
        t = pl.program_id(2)

        @pl.when(t == 0)
        def _():
            acc_sc[...] = jnp.zeros_like(acc_sc)

        av = a_ref[...]
        if has_aff:
            av = av * lg_ref[...] + lb_ref[...]
        acc_sc[...] += _dot_tn(av.astype(BF16), b_ref[...].astype(BF16))

        @pl.when(t == nt - 1)
        def _():
            o_ref[...] = (acc_sc[...] * scale).astype(BF16)

    in_specs = [pl.BlockSpec((tk, tM), lambda m, n, t: (t, m))]
    args = [a]
    if has_aff:
        in_specs += [pl.BlockSpec((1, tM), lambda m, n, t: (0, m))] * 2
        args += list(a_affine)
    in_specs.append(pl.BlockSpec((tk, tN), lambda m, n, t: (t, n)))
    args.append(b)
    aliases = {}
    if has_into:
        aliases = {len(args): 0}
        in_specs.append(ANY)
        args.append(into)
    res = _call(body, name=name, grid=(M // tM, N // tN, nt), in_specs=in_specs,
                out_specs=[pl.BlockSpec((tM, tN), lambda m, n, t: (m, n + off_blocks))],
                out_shape=[jax.ShapeDtypeStruct((M, n_total), BF16)],
                scratch_shapes=[pltpu.VMEM((tM, tN), F32)], aliases=aliases,
                sem=("parallel", "parallel", "arbitrary"), args=args, carry=carry)
    return res[0] if carry is None else (res[0][0], res[1])


def _inproj_fwd(xh, lg, lb, w, bias, *, name):
    T, D = xh.shape
    N = w.shape[1]
    tm = _tile(T, 512)
    tn = _tile(N, 1024, LANES)

    def body(xh_ref, lg_ref, lb_ref, w_ref, b_ref, o_ref, xb_sc):
        @pl.when(pl.program_id(1) == 0)
        def _():
            xb_sc[...] = (xh_ref[...] * lg_ref[...] + lb_ref[...]).astype(BF16)
        o_ref[...] = _dot(xb_sc[...], w_ref[...]) + b_ref[...]

    return pl.pallas_call(
        body, name=name, grid=(T // tm, N // tn),
        in_specs=[pl.BlockSpec((tm, D), lambda i, j: (i, 0)),
                  pl.BlockSpec((1, D), lambda i, j: (0, 0)), pl.BlockSpec((1, D), lambda i, j: (0, 0)),
                  pl.BlockSpec((D, tn), lambda i, j: (0, j)), pl.BlockSpec((1, tn), lambda i, j: (0, j))],
        out_specs=pl.BlockSpec((tm, tn), lambda i, j: (i, j)),
        out_shape=jax.ShapeDtypeStruct((T, N), F32),
        scratch_shapes=[pltpu.VMEM((tm, D), BF16)],
        compiler_params=_params("parallel", "arbitrary"))(xh, lg, lb, w, bias)


def _inproj_bwd(dproj, w, dr_next, xh, rstd, lg, *, name, carry=None):
    T, N = dproj.shape
    D = w.shape[0]
    tm = _tile(T, 512)
    tn = _tile(N, 1024, LANES)
    nj = N // tn

    def body(dp_ref, w_ref, drn_ref, xh_ref, rstd_ref, lg_ref, dprev_ref, dlg_ref, dlb_ref, dx_sc):
        i = pl.program_id(0)
        j = pl.program_id(1)

        @pl.when(j == 0)
        def _():
            dx_sc[...] = ALPHA * drn_ref[...]

        dx_sc[...] += _dot_nt(dp_ref[...], w_ref[...])

        @pl.when(j == nj - 1)
        def _():
            @pl.when(i == 0)
            def _():
                dlg_ref[...] = jnp.zeros_like(dlg_ref)
                dlb_ref[...] = jnp.zeros_like(dlb_ref)
            dx = dx_sc[...]
            x_hat = xh_ref[...]
            dlg_ref[...] += _colsum(dx * x_hat)
            dlb_ref[...] += _colsum(dx)
            dprev_ref[...] = _ln_bwd(dx, x_hat, rstd_ref[:, 0:1], lg_ref[...])

    row = pl.BlockSpec((tm, D), lambda i, j: (i, 0))
    vec = pl.BlockSpec((1, D), lambda i, j: (0, 0))
    return _call(
        body, name=name, grid=(T // tm, nj),
        in_specs=[pl.BlockSpec((tm, tn), lambda i, j: (i, j)), pl.BlockSpec((D, tn), lambda i, j: (0, j)),
                  row, row, pl.BlockSpec((tm, LANES), lambda i, j: (i, 0)), vec],
        out_specs=[row, vec, vec],
        out_shape=[jax.ShapeDtypeStruct((T, D), F32), jax.ShapeDtypeStruct((1, D), F32),
                   jax.ShapeDtypeStruct((1, D), F32)],
        scratch_shapes=[pltpu.VMEM((tm, D), F32)], sem=("arbitrary", "arbitrary"),
        args=[dproj, w, dr_next, xh, rstd, lg], carry=carry)


def _sgu_fwd(proj, ln_g, ln_b, w_s, b_sb, *, name):
    T = proj.shape[0]
    A = proj.shape[1] // 8
    hd = A // A_GROUPS
    tm = _tile(T, 256, GMLP_BLOCK)

    def body(u_ref, v_ref, g_ref, b_ref, ws_ref, bs_ref, o_ref):
        gu = _gelu(u_ref[...])
        vh, _ = _ln_stats(_gelu(v_ref[...]))
        vn = (vh * g_ref[...] + b_ref[...]).astype(BF16)
        mask = _chunk_mask(False)
        for h in range(A_GROUPS):
            wm = jnp.where(mask, ws_ref[h], 0.0).astype(BF16)
            cols = slice(h * hd, (h + 1) * hd)
            for n in range(tm // GMLP_BLOCK):
                rows = slice(n * GMLP_BLOCK, (n + 1) * GMLP_BLOCK)
                s = _dot(wm, vn[rows, cols]) + bs_ref[h][:, :hd]
                o_ref[rows, cols] = (gu[rows, cols] * s).astype(BF16)

    vec = pl.BlockSpec((1, A), lambda i: (0, 0))
    full = pl.BlockSpec((A_GROUPS, GMLP_BLOCK, GMLP_BLOCK), lambda i: (0, 0, 0))
    return pl.pallas_call(
        body, name=name, grid=(T // tm,),
        in_specs=[pl.BlockSpec((tm, A), lambda i: (i, 0)), pl.BlockSpec((tm, A), lambda i: (i, 1)),
                  vec, vec, full, full],
        out_specs=pl.BlockSpec((tm, A), lambda i: (i, 0)),
        out_shape=jax.ShapeDtypeStruct((T, A), BF16),
        compiler_params=_params("parallel"))(proj, proj, ln_g, ln_b, w_s, b_sb)


def _sgu_bwd(proj, dsg, ln_g, ln_b, w_s, w_st, b_sb, dproj, *, name):
    T = proj.shape[0]
    A = proj.shape[1] // 8
    hd = A // A_GROUPS
    tm = _tile(T, 256, GMLP_BLOCK)
    nt = T // tm

    def body(u_ref, v_ref, dsg_ref, g_ref, b_ref, ws_ref, wst_ref, bs_ref, _alias,
             dp_ref, dbin_ref, dlg_ref, dlb_ref, dws_ref, dbs_ref, dvn_sc, dgu_sc, dbs_sc):
        i = pl.program_id(0)

        @pl.when(i == 0)
        def _():
            dbin_ref[...] = jnp.zeros_like(dbin_ref)
            dlg_ref[...] = jnp.zeros_like(dlg_ref)
            dlb_ref[...] = jnp.zeros_like(dlb_ref)
            dws_ref[...] = jnp.zeros_like(dws_ref)
            dbs_sc[...] = jnp.zeros_like(dbs_sc)

        u = u_ref[...]
        v = v_ref[...]
        gu = _gelu(u)
        vh, rstd = _ln_stats(_gelu(v))
        gain = g_ref[...]
        vn = (vh * gain + b_ref[...]).astype(BF16)
        dsg_v = dsg_ref[...]
        mask = _chunk_mask(False)
        mask_t = _chunk_mask(True)
        for h in range(A_GROUPS):
            wm = jnp.where(mask, ws_ref[h], 0.0).astype(BF16)
            wmt = jnp.where(mask_t, wst_ref[h], 0.0).astype(BF16)
            cols = slice(h * hd, (h + 1) * hd)
            for n in range(tm // GMLP_BLOCK):
                rows = slice(n * GMLP_BLOCK, (n + 1) * GMLP_BLOCK)
                vb = vn[rows, cols]
                s = _dot(wm, vb) + bs_ref[h][:, :hd]
                d_out = dsg_v[rows, cols]
                dgu_sc[rows, cols] = d_out * s
                ds = d_out * gu[rows, cols]
                ds_b = ds.astype(BF16)
                dws_ref[h] += _dot_nt(ds_b, vb)
                dbs_sc[h] += ds
                dvn_sc[rows, cols] = _dot(wmt, ds_b)
        dvn = dvn_sc[...]
        dlg_ref[...] += _colsum(dvn * vh)
        dlb_ref[...] += _colsum(dvn)
        dv = _ln_bwd(dvn, vh, rstd, gain) * _gelu_grad(v)
        du = dgu_sc[...] * _gelu_grad(u)
        dp_ref[:, 0:A] = du.astype(BF16)
        dp_ref[:, A:2 * A] = dv.astype(BF16)
        dbin_ref[:, 0:A] += _colsum(du)
        dbin_ref[:, A:2 * A] += _colsum(dv)

        @pl.when(i == nt - 1)
        def _():
            for h in range(A_GROUPS):
                dws_ref[h] = jnp.where(mask, dws_ref[h], 0.0)
                dbs_ref[h:h + 1, :] = _colsum(dbs_sc[h].T)

    vec = pl.BlockSpec((1, A), lambda i: (0, 0))
    full = pl.BlockSpec((A_GROUPS, GMLP_BLOCK, GMLP_BLOCK), lambda i: (0, 0, 0))
    tile = pl.BlockSpec((tm, A), lambda i: (i, 0))
    return pl.pallas_call(
        body, name=name, grid=(nt,),
        in_specs=[tile, pl.BlockSpec((tm, A), lambda i: (i, 1)), tile, vec, vec, full, full, full, ANY],
        out_specs=[pl.BlockSpec((tm, 2 * A), lambda i: (i, 0)), pl.BlockSpec((1, 2 * A), lambda i: (0, 0)),
                   vec, vec, full, pl.BlockSpec((A_GROUPS, GMLP_BLOCK), lambda i: (0, 0))],
        out_shape=[jax.ShapeDtypeStruct(dproj.shape, BF16), jax.ShapeDtypeStruct((1, 2 * A), F32),
                   jax.ShapeDtypeStruct((1, A), F32), jax.ShapeDtypeStruct((1, A), F32),
                   jax.ShapeDtypeStruct((A_GROUPS, GMLP_BLOCK, GMLP_BLOCK), F32),
                   jax.ShapeDtypeStruct((A_GROUPS, GMLP_BLOCK), F32)],
        scratch_shapes=[pltpu.VMEM((tm, A), F32), pltpu.VMEM((tm, A), F32),
                        pltpu.VMEM((A_GROUPS, GMLP_BLOCK, hd), F32)],
        input_output_aliases={8: 0},
        compiler_params=_params("arbitrary"))(proj, proj, dsg, ln_g, ln_b, w_s, w_st, b_sb, dproj)


def _conv_tiles(T, B):
    tm = _tile(T, 256, CONV_ROWS)
    lb = min(LANES, B)
    return tm, tm // CONV_HALO, lb


def _conv_fwd(proj, w_dw, b_dw, ln_g, ln_b, *, name):
    T = proj.shape[0]
    B = proj.shape[1] // 8
    tm, nh, lb = _conv_tiles(T, B)

    def body(ap_ref, gp_ref, a_ref, g_ref, w_ref, bdw_ref, lg_ref, lb_ref, c_ref, cv_ref, z_sc):
        i = pl.program_id(0)
        z_sc[0:CONV_HALO, :] = jnp.where(i > 0, ap_ref[...] * _sig(gp_ref[...]), 0.0)
        z_sc[CONV_HALO:CONV_HALO + tm, :] = a_ref[...] * _sig(g_ref[...])
        for cb in range(B // lb):
            ls = slice(cb * lb, (cb + 1) * lb)
            for rc in range(tm // CONV_ROWS):
                base = rc * CONV_ROWS + CONV_HALO - (CONV_WIDTH - 1)
                acc = jnp.zeros((CONV_ROWS, lb), F32)
                for k in range(CONV_WIDTH):
                    acc = acc + w_ref[k:k + 1, ls] * z_sc[base + k:base + k + CONV_ROWS, ls]
                c_ref[rc * CONV_ROWS:(rc + 1) * CONV_ROWS, ls] = acc + bdw_ref[:, ls]
        xh, _ = _ln_stats(c_ref[...])
        y = xh * lg_ref[...] + lb_ref[...]
        cv_ref[...] = (y * _sig(y)).astype(BF16)

    vec = pl.BlockSpec((1, B), lambda i: (0, 0))
    halo_a = pl.BlockSpec((CONV_HALO, B), lambda i: (jnp.maximum(i * nh - 1, 0), 2))
    halo_g = pl.BlockSpec((CONV_HALO, B), lambda i: (jnp.maximum(i * nh - 1, 0), 3))
    return pl.pallas_call(
        body, name=name, grid=(T // tm,),
        in_specs=[halo_a, halo_g, pl.BlockSpec((tm, B), lambda i: (i, 2)), pl.BlockSpec((tm, B), lambda i: (i, 3)),
                  pl.BlockSpec((CONV_WPAD, B), lambda i: (0, 0)), vec, vec, vec],
        out_specs=[pl.BlockSpec((tm, B), lambda i: (i, 0))] * 2,
        out_shape=[jax.ShapeDtypeStruct((T, B), F32), jax.ShapeDtypeStruct((T, B), BF16)],
        scratch_shapes=[pltpu.VMEM((CONV_HALO + tm, B), F32)],
        compiler_params=_params("parallel"))(proj, proj, proj, proj, w_dw, b_dw, ln_g, ln_b)


def _conv_bwd_ln(dcv, c, ln_g, ln_b, *, name):
    T, B = c.shape
    tm = _tile(T, 512)

    def body(dcv_ref, c_ref, lg_ref, lb_ref, dc_ref, dlg_ref, dlb_ref, dbdw_ref):
        @pl.when(pl.program_id(0) == 0)
        def _():
            dlg_ref[...] = jnp.zeros_like(dlg_ref)
            dlb_ref[...] = jnp.zeros_like(dlb_ref)
            dbdw_ref[...] = jnp.zeros_like(dbdw_ref)
        gain = lg_ref[...]
        xh, rstd = _ln_stats(c_ref[...])
        y = xh * gain + lb_ref[...]
        s = _sig(y)
        dy = dcv_ref[...] * (s * (1.0 + y * (1.0 - s)))
        dlg_ref[...] += _colsum(dy * xh)
        dlb_ref[...] += _colsum(dy)
        dc = _ln_bwd(dy, xh, rstd, gain)
        dc_ref[...] = dc
        dbdw_ref[...] += _colsum(dc)

    tile = pl.BlockSpec((tm, B), lambda i: (i, 0))
    vec = pl.BlockSpec((1, B), lambda i: (0, 0))
    return pl.pallas_call(
        body, name=name, grid=(T // tm,), in_specs=[tile, tile, vec, vec], out_specs=[tile, vec, vec, vec],
        out_shape=[jax.ShapeDtypeStruct((T, B), F32)] + [jax.ShapeDtypeStruct((1, B), F32)] * 3,
        compiler_params=_params("arbitrary"))(dcv, c, ln_g, ln_b)


def _conv_bwd(proj, dc, w_dw, dproj, *, name, carry=None):
    T = proj.shape[0]
    B = proj.shape[1] // 8
    tm, nh, lb = _conv_tiles(T, B)
    nt = T // tm
    n_halo = T // CONV_HALO

    def body(ap_ref, gp_ref, a_ref, g_ref, dc_ref, dcn_ref, w_ref, _alias,
             dp_ref, dbin_ref, dw_ref, z_sc, dc_sc, dz_sc, dw_sc):
        i = pl.program_id(0)

        @pl.when(i == 0)
        def _():
            dbin_ref[...] = jnp.zeros_like(dbin_ref)
            dw_sc[...] = jnp.zeros_like(dw_sc)

        a = a_ref[...]
        s = _sig(g_ref[...])
        z_sc[0:CONV_HALO, :] = jnp.where(i > 0, ap_ref[...] * _sig(gp_ref[...]), 0.0)
        z_sc[CONV_HALO:CONV_HALO + tm, :] = a * s
        dc_sc[0:tm, :] = dc_ref[...]
        dc_sc[tm:tm + CONV_HALO, :] = jnp.where(i < nt - 1, dcn_ref[...], 0.0)
        for cb in range(B // lb):
            ls = slice(cb * lb, (cb + 1) * lb)
            for rc in range(tm // CONV_ROWS):
                r0 = rc * CONV_ROWS
                dcc = dc_sc[r0:r0 + CONV_ROWS, ls]
                acc = jnp.zeros((CONV_ROWS, lb), F32)
                for k in range(CONV_WIDTH):
                    up = r0 + (CONV_WIDTH - 1) - k
                    acc = acc + w_ref[k:k + 1, ls] * dc_sc[up:up + CONV_ROWS, ls]
                    dn = r0 + CONV_HALO - (CONV_WIDTH - 1) + k
                    prod = dcc * z_sc[dn:dn + CONV_ROWS, ls]
                    dw_sc[8 * k:8 * k + 8, ls] += jnp.sum(prod.reshape(CONV_ROWS // 8, 8, lb), axis=0)
                dz_sc[r0:r0 + CONV_ROWS, ls] = acc
        dz = dz_sc[...]
        da = dz * s
        dg = dz * a * s * (1.0 - s)
        dp_ref[:, 0:B] = da.astype(BF16)
        dp_ref[:, B:2 * B] = dg.astype(BF16)
        dbin_ref[:, 0:B] += _colsum(da)
        dbin_ref[:, B:2 * B] += _colsum(dg)

        @pl.when(i == nt - 1)
        def _():
            for k in range(CONV_WIDTH):
                dw_ref[k:k + 1, :] = _colsum(dw_sc[8 * k:8 * k + 8, :])

    halo_a = pl.BlockSpec((CONV_HALO, B), lambda i: (jnp.maximum(i * nh - 1, 0), 2))
    halo_g = pl.BlockSpec((CONV_HALO, B), lambda i: (jnp.maximum(i * nh - 1, 0), 3))
    halo_dc = pl.BlockSpec((CONV_HALO, B), lambda i: (jnp.minimum((i + 1) * nh, n_halo - 1), 0))
    return _call(
        body, name=name, grid=(nt,),
        in_specs=[halo_a, halo_g, pl.BlockSpec((tm, B), lambda i: (i, 2)), pl.BlockSpec((tm, B), lambda i: (i, 3)),
                  pl.BlockSpec((tm, B), lambda i: (i, 0)), halo_dc,
                  pl.BlockSpec((CONV_WPAD, B), lambda i: (0, 0)), ANY],
        out_specs=[pl.BlockSpec((tm, 2 * B), lambda i: (i, 1)), pl.BlockSpec((1, 2 * B), lambda i: (0, 0)),
                   pl.BlockSpec((CONV_WIDTH, B), lambda i: (0, 0))],
        out_shape=[jax.ShapeDtypeStruct(dproj.shape, BF16), jax.ShapeDtypeStruct((1, 2 * B), F32),
                   jax.ShapeDtypeStruct((CONV_WIDTH, B), F32)],
        scratch_shapes=[pltpu.VMEM((CONV_HALO + tm, B), F32), pltpu.VMEM((tm + CONV_HALO, B), F32),
                        pltpu.VMEM((tm, B), F32), pltpu.VMEM((8 * CONV_WIDTH, B), F32)],
        aliases={7: 0}, sem=("arbitrary",), args=[proj, proj, proj, proj, dc, dc, w_dw, dproj], carry=carry)


def _mix_fwd_gate(sg, cv, proj, wa, wb, *, name):
    T, A = sg.shape
    D = wa.shape[1]
    tm = _tile(T, 256)

    def body(sg_ref, cv_ref, la_ref, lb_ref, wa_ref, wb_ref, ya_ref, yb_ref, m_ref):
        ya = _dot(sg_ref[...], wa_ref[...])
        yb = _dot(cv_ref[...], wb_ref[...])
        ya_ref[...] = ya.astype(BF16)
        yb_ref[...] = yb.astype(BF16)
        m_ref[...] = (_sig(la_ref[...]) * ya + _sig(lb_ref[...]) * yb).astype(BF16)

    act = pl.BlockSpec((tm, A), lambda i: (i, 0))
    wide = pl.BlockSpec((tm, D), lambda i: (i, 0))
    wspec = pl.BlockSpec((A, D), lambda i: (0, 0))
    return pl.pallas_call(
        body, name=name, grid=(T // tm,),
        in_specs=[act, act, pl.BlockSpec((tm, D), lambda i: (i, 2)), pl.BlockSpec((tm, D), lambda i: (i, 3)),
                  wspec, wspec],
        out_specs=[wide] * 3, out_shape=[jax.ShapeDtypeStruct((T, D), BF16)] * 3,
        compiler_params=_params("parallel"))(sg, cv, proj, proj, wa, wb)


def _mix_fwd_out(m, wout, xh, lg, lb, *, name):
    T, D = xh.shape
    tm = _tile(T, 512)

    def body(m_ref, w_ref, xh_ref, lg_ref, lb_ref, xho_ref, rstd_ref):
        r = ALPHA * (xh_ref[...] * lg_ref[...] + lb_ref[...]) + _dot(m_ref[...], w_ref[...])
        xho, rstd = _ln_stats(r)
        xho_ref[...] = xho
        rstd_ref[...] = jnp.broadcast_to(rstd, (tm, LANES))

    row = pl.BlockSpec((tm, D), lambda i: (i, 0))
    vec = pl.BlockSpec((1, D), lambda i: (0, 0))
    return pl.pallas_call(
        body, name=name, grid=(T // tm,),
        in_specs=[row, pl.BlockSpec((D, D), lambda i: (0, 0)), row, vec, vec],
        out_specs=[row, pl.BlockSpec((tm, LANES), lambda i: (i, 0))],
        out_shape=[jax.ShapeDtypeStruct((T, D), F32), jax.ShapeDtypeStruct((T, LANES), F32)],
        compiler_params=_params("parallel"))(m, wout, xh, lg, lb)


def _mix_bwd_gate(dr, wout, proj, ya, yb, *, name, carry=None):
    T, D = dr.shape
    N = proj.shape[1]
    tm = _tile(T, 256)

    def body(dr_ref, w_ref, la_ref, lb_ref, ya_ref, yb_ref, dya_ref, dyb_ref, dp_ref, dbin_ref):
        @pl.when(pl.program_id(0) == 0)
        def _():
            dbin_ref[...] = jnp.zeros_like(dbin_ref)
        dm = _dot_nt(dr_ref[...].astype(BF16), w_ref[...])
        sa = _sig(la_ref[...])
        sb = _sig(lb_ref[...])
        dya_ref[...] = (dm * sa).astype(BF16)
        dyb_ref[...] = (dm * sb).astype(BF16)
        dla = dm * ya_ref[...].astype(F32) * sa * (1.0 - sa)
        dlb = dm * yb_ref[...].astype(F32) * sb * (1.0 - sb)
        dp_ref[:, 0:D] = dla.astype(BF16)
        dp_ref[:, D:2 * D] = dlb.astype(BF16)
        dbin_ref[:, 0:D] += _colsum(dla)
        dbin_ref[:, D:2 * D] += _colsum(dlb)

    row = pl.BlockSpec((tm, D), lambda i: (i, 0))
    return _call(
        body, name=name, grid=(T // tm,),
        in_specs=[row, pl.BlockSpec((D, D), lambda i: (0, 0)), pl.BlockSpec((tm, D), lambda i: (i, 2)),
                  pl.BlockSpec((tm, D), lambda i: (i, 3)), row, row],
        out_specs=[row, row, pl.BlockSpec((tm, 2 * D), lambda i: (i, 1)), pl.BlockSpec((1, 2 * D), lambda i: (0, 0))],
        out_shape=[jax.ShapeDtypeStruct((T, D), BF16), jax.ShapeDtypeStruct((T, D), BF16),
                   jax.ShapeDtypeStruct((T, N), BF16), jax.ShapeDtypeStruct((1, 2 * D), F32)],
        sem=("arbitrary",), args=[dr, wout, proj, proj, ya, yb], carry=carry)


def _mix_bwd_proj(dya, dyb, wa, wb, *, name):
    T, D = dya.shape
    A = wa.shape[0]
    tm = _tile(T, 512)

    def body(dya_ref, dyb_ref, wa_ref, wb_ref, dsg_ref, dcv_ref):
        dsg_ref[...] = _dot_nt(dya_ref[...], wa_ref[...])
        dcv_ref[...] = _dot_nt(dyb_ref[...], wb_ref[...])

    row = pl.BlockSpec((tm, D), lambda i: (i, 0))
    wspec = pl.BlockSpec((A, D), lambda i: (0, 0))
    act = pl.BlockSpec((tm, A), lambda i: (i, 0))
    return pl.pallas_call(
        body, name=name, grid=(T // tm,), in_specs=[row, row, wspec, wspec], out_specs=[act, act],
        out_shape=[jax.ShapeDtypeStruct((T, A), F32)] * 2,
        compiler_params=_params("parallel"))(dya, dyb, wa, wb)


def _mesh_pos():
    return lax.axis_index("x"), lax.axis_index("y"), lax.axis_index("c")


def _shard_view(ref, p, shape, axis):
    r, c = shape
    if axis == 0:
        return ref.at[pl.ds(pl.multiple_of(p * r, 16), r), :]
    return ref.at[:, pl.ds(pl.multiple_of(p * c, LANES), c)]


def _all_gather(shards, axes):
    n = len(shards)
    shapes = [s.shape for s in shards]

    def run(ins, outs, sems, phase):
        send_sems, recv_sems, local_sems = sems
        x, y, c = _mesh_pos()
        me, sibling = (x, y, c), (x, y, 1 - c)
        chips = [(1 - x, y), (x, 1 - y), (1 - x, 1 - y)]

        def view(t, pos):
            px, py, pc = pos
            return _shard_view(outs[t], 4 * px + 2 * py + pc, shapes[t], axes[t])

        def copy(t, k, block, to, src=None):
            return pltpu.make_async_remote_copy(
                src_ref=view(t, block) if src is None else src, dst_ref=view(t, block),
                send_sem=send_sems.at[7 * t + k], recv_sem=recv_sems.at[7 * t + k],
                device_id=to, device_id_type=MESH)

        mine = [pltpu.make_async_copy(ins[t], view(t, me), local_sems.at[t]) for t in range(n)]
        first = []
        for t in range(n):
            first.append(copy(t, 0, me, sibling, src=ins[t]))
            first += [copy(t, 1 + j, me, (*chip, c), src=ins[t]) for j, chip in enumerate(chips)]
        if phase == "start":
            for cp in mine + first:
                cp.start()
            return
        passed = []
        for t in range(n):
            for j, chip in enumerate(chips):
                copy(t, 1 + j, (*chip, c), me).wait_recv()
                fwd = copy(t, 4 + j, (*chip, c), sibling)
                fwd.start()
                passed.append(fwd)
        for t in range(n):
            copy(t, 0, sibling, me).wait_recv()
            for j, chip in enumerate(chips):
                copy(t, 4 + j, (*chip, 1 - c), me).wait_recv()
        for cp in first + passed:
            cp.wait_send()
        for cp in mine:
            cp.wait()

    out_shape = [jax.ShapeDtypeStruct((N_DEV * s.shape[0], s.shape[1]) if ax == 0
                                      else (s.shape[0], N_DEV * s.shape[1]), s.dtype)
                 for s, ax in zip(shards, axes)]
    return _Comm(shards, out_shape, [pltpu.SemaphoreType.DMA((7 * n,)), pltpu.SemaphoreType.DMA((7 * n,)),
                                     pltpu.SemaphoreType.DMA((n,))], run)


def _rs_to_sibling(grads, shapes, axes):
    n = len(grads)

    def run(gs, outs, sems, phase):
        send_sems, recv_sems = sems
        x, y, c = _mesh_pos()
        copies = [pltpu.make_async_remote_copy(
            src_ref=_shard_view(gs[t], 2 * k + (1 - c), shapes[t], axes[t]), dst_ref=outs[t].at[k],
            send_sem=send_sems.at[4 * t + k], recv_sem=recv_sems.at[4 * t + k],
            device_id=(x, y, 1 - c), device_id_type=MESH) for t in range(n) for k in range(4)]
        if phase == "start":
            for cp in copies:
                cp.start()
            return
        for cp in copies:
            cp.wait_recv()
        for cp in copies:
            cp.wait_send()

    return _Comm(grads, [jax.ShapeDtypeStruct((4,) + tuple(s), BF16) for s in shapes],
                 [pltpu.SemaphoreType.DMA((4 * n,)), pltpu.SemaphoreType.DMA((4 * n,))], run)


def _rs_pair_sum(g, recv, cidx, shape, axis, *, name):
    r, c = shape
    tr = _tile(r, max(8, (1 << 20) // c), 16)
    nr = r // tr

    def body(c_ref, g_ref, rv_ref, o_ref):
        o_ref[...] = (g_ref[...].astype(F32) + rv_ref[...].astype(F32)).astype(BF16)

    if axis == 1:
        g_spec = pl.BlockSpec((tr, c), lambda k, i, s: (i, 2 * k + s[0]))
    else:
        g_spec = pl.BlockSpec((tr, c), lambda k, i, s: ((2 * k + s[0]) * nr + i, 0))
    blk = pl.BlockSpec((None, tr, c), lambda k, i, s: (k, i, 0))
    return pl.pallas_call(
        body, name=name,
        grid_spec=pltpu.PrefetchScalarGridSpec(num_scalar_prefetch=1, grid=(4, nr), in_specs=[g_spec, blk],
                                               out_specs=blk),
        out_shape=jax.ShapeDtypeStruct((4, r, c), BF16),
        compiler_params=_params("parallel", "parallel"))(cidx, g, recv)


def _rs_to_chips(parts):
    n = len(parts)

    def run(ps, outs, sems, phase):
        send_sems, recv_sems, local_sems = sems
        x, y, c = _mesh_pos()
        my_chip = 2 * x + y
        peers = [(1 - x, y), (x, 1 - y), (1 - x, 1 - y)]
        local = [pltpu.make_async_copy(ps[t].at[my_chip], outs[t].at[my_chip], local_sems.at[t]) for t in range(n)]
        sends = [pltpu.make_async_remote_copy(
            src_ref=ps[t].at[2 * px + py], dst_ref=outs[t].at[my_chip],
            send_sem=send_sems.at[3 * t + j], recv_sem=recv_sems.at[3 * t + j],
            device_id=(px, py, c), device_id_type=MESH) for t in range(n) for j, (px, py) in enumerate(peers)]
        if phase == "start":
            for cp in local + sends:
                cp.start()
            return
        for t in range(n):
            for j, (px, py) in enumerate(peers):
                pltpu.make_async_remote_copy(
                    src_ref=ps[t].at[2 * px + py], dst_ref=outs[t].at[2 * px + py],
                    send_sem=send_sems.at[3 * t + j], recv_sem=recv_sems.at[3 * t + j],
                    device_id=(x, y, c), device_id_type=MESH).wait_recv()
        for cp in sends:
            cp.wait_send()
        for cp in local:
            cp.wait()

    return _Comm(parts, [jax.ShapeDtypeStruct(p.shape, BF16) for p in parts],
                 [pltpu.SemaphoreType.DMA((3 * n,)), pltpu.SemaphoreType.DMA((3 * n,)),
                  pltpu.SemaphoreType.DMA((n,))], run)


def _all_reduce_small(buf, *, name):
    R = buf.shape[0]

    def body(in_ref, out_ref, slots, send_sems, recv_sems, local_sem):
        x, y, c = _mesh_pos()
        me = 4 * x + 2 * y + c
        local = pltpu.make_async_copy(in_ref, slots.at[me], local_sem)
        local.start()
        flips = [(fx, fy, fc) for fx in (0, 1) for fy in (0, 1) for fc in (0, 1)][1:]
        peers = [(1 - x if fx else x, 1 - y if fy else y, 1 - c if fc else c) for fx, fy, fc in flips]
        sends = []
        for k, peer in enumerate(peers):
            cp = pltpu.make_async_remote_copy(src_ref=in_ref, dst_ref=slots.at[me], send_sem=send_sems.at[k],
                                              recv_sem=recv_sems.at[k], device_id=peer, device_id_type=MESH)
            cp.start()
            sends.append(cp)
        for k, (px, py, pc) in enumerate(peers):
            pltpu.make_async_remote_copy(src_ref=in_ref, dst_ref=slots.at[4 * px + 2 * py + pc],
                                         send_sem=send_sems.at[k], recv_sem=recv_sems.at[k],
                                         device_id=(x, y, c), device_id_type=MESH).wait_recv()
        for cp in sends:
            cp.wait_send()
        local.wait()
        acc = slots[0]
        for p in range(1, N_DEV):
            acc = acc + slots[p]
        out_ref[...] = acc

    vm = pl.BlockSpec(memory_space=pltpu.VMEM)
    return pl.pallas_call(
        body, name=name, in_specs=[vm], out_specs=vm, out_shape=jax.ShapeDtypeStruct(buf.shape, F32),
        scratch_shapes=[pltpu.VMEM((N_DEV, R, LANES), F32), pltpu.SemaphoreType.DMA((7,)),
                        pltpu.SemaphoreType.DMA((7,)), pltpu.SemaphoreType.DMA],
        compiler_params=pltpu.CompilerParams(has_side_effects=True, vmem_limit_bytes=VMEM_LIMIT_BYTES))(buf)


def _adam_math(g, w, m, v):
    m_new = ADAM_B1 * m + (1.0 - ADAM_B1) * g
    v_new = ADAM_B2 * v + (1.0 - ADAM_B2) * (g * g)
    m_hat = m_new / ADAM_C1
    v_hat = v_new / ADAM_C2
    delta = -ADAM_LR * (m_hat / (jnp.sqrt(v_hat) + ADAM_EPS) + ADAM_WD * w)
    return delta, m_new, v_new


def _adamw_sharded(q, w, m, v, *, name):
    r, c = w.shape
    tr = _tile(r, max(8, (1 << 18) // c), 16)

    def body(q_ref, w_ref, m_ref, v_ref, g_ref, d_ref, mo_ref, vo_ref):
        g = ((q_ref[0].astype(F32) + q_ref[1].astype(F32)) + q_ref[2].astype(F32)) + q_ref[3].astype(F32)
        g_ref[...] = g
        d_ref[...], mo_ref[...], vo_ref[...] = _adam_math(g, w_ref[...], m_ref[...], v_ref[...])

    blk = pl.BlockSpec((tr, c), lambda i: (i, 0))
    return pl.pallas_call(
        body, name=name, grid=(r // tr,),
        in_specs=[pl.BlockSpec((4, tr, c), lambda i: (0, i, 0)), blk, blk, blk], out_specs=[blk] * 4,
        out_shape=[jax.ShapeDtypeStruct((r, c), F32)] * 4,
        compiler_params=_params("parallel"))(q, w, m, v)


def _adamw_plain(g, w, m, v, *, name):
    r, c = w.shape
    tr = _tile(r, 512)

    def body(g_ref, w_ref, m_ref, v_ref, d_ref, mo_ref, vo_ref):
        d_ref[...], mo_ref[...], vo_ref[...] = _adam_math(g_ref[...], w_ref[...], m_ref[...], v_ref[...])

    blk = pl.BlockSpec((tr, c), lambda i: (i, 0))
    return pl.pallas_call(
        body, name=name, grid=(r // tr,), in_specs=[blk] * 4, out_specs=[blk] * 3,
        out_shape=[jax.ShapeDtypeStruct((r, c), F32)] * 3,
        compiler_params=_params("parallel"))(g, w, m, v)


def _pack_rows(arrays):
    return jnp.concatenate([a.reshape(-1, LANES) for a in arrays], axis=0)


def kernel(x, ffn1_w_gu, ffn1_w_down, ln1_g, ln1_b, w_in, b_in, sgu_ln_g, sgu_ln_b, sgu_w_s, sgu_b_s, w_a_proj, conv_w_dw, conv_b_dw, conv_ln_g, conv_ln_b, w_b_proj, w_out, ln2_g, ln2_b, ffn2_w_gu, ffn2_w_down, ln3_g, ln3_b, loss_target, m_ffn1_w_gu, m_ffn1_w_down, m_ln1_g, m_ln1_b, m_w_in, m_b_in, m_sgu_ln_g, m_sgu_ln_b, m_sgu_w_s, m_sgu_b_s, m_w_a_proj, m_conv_w_dw, m_conv_b_dw, m_conv_ln_g, m_conv_ln_b, m_w_b_proj, m_w_out, m_ln2_g, m_ln2_b, m_ffn2_w_gu, m_ffn2_w_down, m_ln3_g, m_ln3_b, v_ffn1_w_gu, v_ffn1_w_down, v_ln1_g, v_ln1_b, v_w_in, v_b_in, v_sgu_ln_g, v_sgu_ln_b, v_sgu_w_s, v_sgu_b_s, v_w_a_proj, v_conv_w_dw, v_conv_b_dw, v_conv_ln_g, v_conv_ln_b, v_w_b_proj, v_w_out, v_ln2_g, v_ln2_b, v_ffn2_w_gu, v_ffn2_w_down, v_ln3_g, v_ln3_b):
    given = dict(locals())
    w = {n: given[n][0] for n in WEIGHTS}
    mom = {n: given["m_" + n][0] for n in WEIGHTS}
    var = {n: given["v_" + n][0] for n in WEIGHTS}
    xt = x[0]
    target = loss_target[0]
    T, D = xt.shape
    A = w['w_a_proj'].shape[0]

    big_names = list(BIG)
    early = ['ffn1_w_gu', 'ffn1_w_down']
    late = [n for n in big_names if n not in early]
    conv_w_pad = jnp.pad(w['conv_w_dw'], ((0, CONV_WPAD - CONV_WIDTH), (0, 0)))
    w_bf = {n: w[n].astype(BF16) for n in big_names}
    full = dict(zip(early, _comm_call(_all_gather([w_bf[n] for n in early], [BIG[n] for n in early]),
                                      name="all_gather_ffn1")))
    gather_late = _all_gather([w_bf[n] for n in late] + [conv_w_pad], [BIG[n] for n in late] + [1])

    def row(v):
        return v.reshape(1, -1)

    ones = jnp.ones((1, D), F32)
    zeros = jnp.zeros((1, D), F32)
    w_s = w['sgu_w_s']
    w_st = jnp.swapaxes(w_s, 1, 2)
    b_sb = jnp.broadcast_to(w['sgu_b_s'][:, :, None], w_s.shape)

    (gate1, up1, xh1, rstd1), gathered = _ffn_fwd(xt, ones, zeros, full['ffn1_w_gu'], full['ffn1_w_down'],
                                                  affine=False, name="ffn1_fwd", carry=gather_late)
    full.update(zip(late, gathered[:-1]))
    conv_w_full = gathered[-1]
    g1, b1 = row(w['ln1_g']), row(w['ln1_b'])
    proj = _inproj_fwd(xh1, g1, b1, full['w_in'], row(w['b_in']), name="inproj_fwd")
    sg = _sgu_fwd(proj, row(w['sgu_ln_g']), row(w['sgu_ln_b']), w_s, b_sb, name="sgu_fwd")
    conv_out, cv = _conv_fwd(proj, conv_w_full, row(w['conv_b_dw']), row(w['conv_ln_g']), row(w['conv_ln_b']),
                             name="conv_fwd")
    ya, yb, mixed = _mix_fwd_gate(sg, cv, proj, full['w_a_proj'], full['w_b_proj'], name="mix_fwd_gate")
    xh2, rstd2 = _mix_fwd_out(mixed, full['w_out'], xh1, g1, b1, name="mix_fwd_out")
    g2, b2 = row(w['ln2_g']), row(w['ln2_b'])
    gate2, up2, dr3, loss_part, d_ln3_g, d_ln3_b = _ffn_fwd(
        xh2, g2, b2, full['ffn2_w_gu'], full['ffn2_w_down'], affine=True, name="ffn2_fwd_loss",
        final=(row(w['ln3_g']), row(w['ln3_b']), target))

    F = full['ffn2_w_down'].shape[0]
    h2, dgate2, dup2, dr2, d_ln2_g, d_ln2_b = _ffn_bwd(dr3, gate2, up2, full['ffn2_w_gu'], full['ffn2_w_down'],
                                                      name="ffn2_bwd", prev=(xh2, rstd2, g2))
    G, P, Q = {}, {}, {}
    cidx = lax.axis_index("c").astype(jnp.int32).reshape(1)

    def to_sibling(names):
        return _rs_to_sibling([G[n] for n in names], [w[n].shape for n in names], [BIG[n] for n in names])

    def pair_sum(names, received):
        for n, rv in zip(names, received):
            P[n] = _rs_pair_sum(G[n], rv, cidx, w[n].shape, BIG[n], name="rs_pair_sum_" + n)

    def to_chips(names):
        return _rs_to_chips([P[n] for n in names])

    G['ffn2_w_down'] = _mm_tn(h2, dr3, name="dw_ffn2_down", tm_pref=1408, tn_pref=2048, scale=0.5)
    gu, rv = _mm_tn(xh2, dgate2, name="dw_ffn2_gate", tm_pref=2048, tn_pref=1408, a_affine=(g2, b2),
                    n_total=2 * F, carry=to_sibling(['ffn2_w_down']))
    pair_sum(['ffn2_w_down'], rv)
    G['ffn2_w_gu'], q = _mm_tn(xh2, dup2, name="dw_ffn2_up", tm_pref=2048, tn_pref=1408, a_affine=(g2, b2),
                               into=gu, col_off=F, n_total=2 * F, carry=to_chips(['ffn2_w_down']))
    Q['ffn2_w_down'] = q[0]

    (dya, dyb, dproj, dbin_gate), rv = _mix_bwd_gate(dr2, full['w_out'], proj, ya, yb, name="mix_bwd_gate",
                                                     carry=to_sibling(['ffn2_w_gu']))
    pair_sum(['ffn2_w_gu'], rv)
    dsg, dcv = _mix_bwd_proj(dya, dyb, full['w_a_proj'], full['w_b_proj'], name="mix_bwd_proj")
    dproj, dbin_sgu, d_sgu_ln_g, d_sgu_ln_b, d_w_s, d_b_s = _sgu_bwd(
        proj, dsg, row(w['sgu_ln_g']), row(w['sgu_ln_b']), w_s, w_st, b_sb, dproj, name="sgu_bwd")
    dconv, d_conv_ln_g, d_conv_ln_b, d_conv_b = _conv_bwd_ln(dcv, conv_out, row(w['conv_ln_g']),
                                                            row(w['conv_ln_b']), name="conv_bwd_ln")
    (dproj, dbin_conv, d_conv_w), q = _conv_bwd(proj, dconv, conv_w_full, dproj, name="conv_bwd",
                                                carry=to_chips(['ffn2_w_gu']))
    Q['ffn2_w_gu'] = q[0]

    mid = ['w_out', 'w_a_proj', 'w_b_proj']
    G['w_out'] = _mm_tn(mixed, dr2, name="dw_out", tm_pref=2048, tn_pref=1024)
    G['w_a_proj'] = _mm_tn(sg, dya, name="dw_a_proj", tm_pref=1024, tn_pref=2048)
    G['w_b_proj'] = _mm_tn(cv, dyb, name="dw_b_proj", tm_pref=1024, tn_pref=2048)
    G['w_in'], rv = _mm_tn(xh1, dproj, name="dw_in", tm_pref=2048, tn_pref=1024, a_affine=(g1, b1),
                           carry=to_sibling(mid))
    pair_sum(mid, rv)
    both = _join(to_chips(mid), to_sibling(['w_in']))
    (dr1, d_ln1_g, d_ln1_b), moved = _inproj_bwd(dproj, full['w_in'], dr2, xh1, rstd1, g1, name="inproj_bwd",
                                                 carry=both)
    q, rv = both.split(moved)
    Q.update(zip(mid, q))
    pair_sum(['w_in'], rv)

    h1, dgate1, dup1, grad_x = _ffn_bwd(dr1, gate1, up1, full['ffn1_w_gu'], full['ffn1_w_down'], name="ffn1_bwd")
    G['ffn1_w_down'], q = _mm_tn(h1, dr1, name="dw_ffn1_down", tm_pref=1408, tn_pref=2048, scale=0.5,
                                 carry=to_chips(['w_in']))
    Q['w_in'] = q[0]
    gu, rv = _mm_tn(xt, dgate1, name="dw_ffn1_gate", tm_pref=2048, tn_pref=1408, n_total=2 * F,
                    carry=to_sibling(['ffn1_w_down']))
    pair_sum(['ffn1_w_down'], rv)
    G['ffn1_w_gu'], q = _mm_tn(xt, dup1, name="dw_ffn1_up", tm_pref=2048, tn_pref=1408, into=gu, col_off=F,
                               n_total=2 * F, carry=to_chips(['ffn1_w_down']))
    Q['ffn1_w_down'] = q[0]
    pair_sum(['ffn1_w_gu'], _comm_call(to_sibling(['ffn1_w_gu']), name="rs_to_sibling_last"))
    Q['ffn1_w_gu'] = _comm_call(to_chips(['ffn1_w_gu']), name="rs_to_chips_last")[0]

    grads, deltas, new_m, new_v = {}, {}, {}, {}
    for n in big_names:
        grads[n], deltas[n], new_m[n], new_v[n] = _adamw_sharded(Q[n], w[n], mom[n], var[n], name="adamw_" + n)

    B = conv_w_full.shape[1]
    small_g = {'ln1_g': d_ln1_g, 'ln1_b': d_ln1_b,
               'b_in': jnp.concatenate([dbin_sgu, dbin_conv, dbin_gate], axis=1),
               'sgu_ln_g': d_sgu_ln_g, 'sgu_ln_b': d_sgu_ln_b, 'sgu_w_s': d_w_s, 'sgu_b_s': d_b_s,
               'conv_b_dw': d_conv_b, 'conv_ln_g': d_conv_ln_g, 'conv_ln_b': d_conv_ln_b,
               'ln2_g': d_ln2_g, 'ln2_b': d_ln2_b, 'ln3_g': d_ln3_g, 'ln3_b': d_ln3_b}
    packed = _pack_rows([small_g[n] for n in SMALL] + [d_conv_w, loss_part])
    reduced = _all_reduce_small(packed, name="all_reduce_small")
    n_small_rows = sum(w[n].size for n in SMALL) // LANES
    conv_rows = CONV_WIDTH * B // LANES
    d_small, m_small, v_small = _adamw_plain(
        reduced[:n_small_rows], _pack_rows([w[n] for n in SMALL]), _pack_rows([mom[n] for n in SMALL]),
        _pack_rows([var[n] for n in SMALL]), name="adamw_small")
    off = 0
    for n in SMALL:
        rows = w[n].size // LANES
        grads[n] = reduced[off:off + rows].reshape(w[n].shape)
        deltas[n] = d_small[off:off + rows].reshape(w[n].shape)
        new_m[n] = m_small[off:off + rows].reshape(w[n].shape)
        new_v[n] = v_small[off:off + rows].reshape(w[n].shape)
        off += rows
    conv_g_full = reduced[off:off + conv_rows].reshape(CONV_WIDTH, B)
    bs = w['conv_w_dw'].shape[1]
    my_block = 4 * lax.axis_index("x") + 2 * lax.axis_index("y") + lax.axis_index("c")
    grads['conv_w_dw'] = lax.dynamic_slice(conv_g_full, (0, my_block * bs), (CONV_WIDTH, bs))
    deltas['conv_w_dw'], new_m['conv_w_dw'], new_v['conv_w_dw'] = _adamw_plain(
        grads['conv_w_dw'], w['conv_w_dw'], mom['conv_w_dw'], var['conv_w_dw'], name="adamw_conv_w")
    loss = reduced[off + conv_rows, 0]

    def lead(a):
        return a[None]

    return (loss, grad_x[None], *[lead(grads[n]) for n in WEIGHTS], *[lead(deltas[n]) for n in WEIGHTS],
            *[lead(new_m[n]) for n in WEIGHTS], *[lead(new_v[n]) for n in WEIGHTS])
```

```python
import functools
import math

import jax
import jax.numpy as jnp
from jax import lax
from jax.experimental import pallas as pl
from jax.experimental.pallas import tpu as pltpu

F32 = jnp.float32
BF16 = jnp.bfloat16

ALPHA = 2.0 ** 0.25
LN_EPS = 1e-5
CONV_WIDTH = 31
CONV_HALO = 32
CONV_ROWS = 64
CONV_WPAD = 32
CHUNK = 64
GMLP_BLOCK = 128
A_GROUPS = 8
N_DEV = 8
LANES = 128

ADAM_LR = 0.001
ADAM_B1 = 0.9
ADAM_B2 = 0.999
ADAM_EPS = 1e-08
ADAM_WD = 0.01
ADAM_STEP = 10
ADAM_C1 = 1.0 - ADAM_B1 ** ADAM_STEP
ADAM_C2 = 1.0 - ADAM_B2 ** ADAM_STEP

VMEM_LIMIT_BYTES = 60 * 2 ** 20
MESH = pl.DeviceIdType.MESH
ANY = pl.BlockSpec(memory_space=pl.ANY)

WEIGHTS = ['ffn1_w_gu', 'ffn1_w_down', 'ln1_g', 'ln1_b', 'w_in', 'b_in', 'sgu_ln_g', 'sgu_ln_b', 'sgu_w_s',
           'sgu_b_s', 'w_a_proj', 'conv_w_dw', 'conv_b_dw', 'conv_ln_g', 'conv_ln_b', 'w_b_proj', 'w_out',
           'ln2_g', 'ln2_b', 'ffn2_w_gu', 'ffn2_w_down', 'ln3_g', 'ln3_b']
BIG = {'ffn1_w_gu': 1, 'ffn1_w_down': 0, 'w_in': 1, 'w_a_proj': 1, 'w_b_proj': 1, 'w_out': 0,
       'ffn2_w_gu': 1, 'ffn2_w_down': 0}
SMALL = [n for n in WEIGHTS if n not in BIG and n != 'conv_w_dw']


def _tile(n, pref, mult=8):
    best = None
    for d in range(mult, min(n, pref) + 1, mult):
        if n % d == 0:
            best = d
    return n if best is None else best


def _params(*sem):
    return pltpu.CompilerParams(dimension_semantics=sem, vmem_limit_bytes=VMEM_LIMIT_BYTES)


def _dot(a, b):
    return jnp.dot(a, b, preferred_element_type=F32)


def _dot_nt(a, b):
    return lax.dot_general(a, b, (((1,), (1,)), ((), ())), preferred_element_type=F32)


def _dot_tn(a, b):
    return lax.dot_general(a, b, (((0,), (0,)), ((), ())), preferred_element_type=F32)


def _sig(x):
    return 1.0 / (1.0 + jnp.exp(-x))


_GELU_K = math.sqrt(2.0 / math.pi)
_GELU_C = 0.044715


def _gelu(x):
    t = jnp.tanh(_GELU_K * (x + _GELU_C * x * x * x))
    return 0.5 * x * (1.0 + t)


def _gelu_grad(x):
    x2 = x * x
    t = jnp.tanh(_GELU_K * (x + _GELU_C * x2 * x))
    return 0.5 * (1.0 + t) + 0.5 * x * (1.0 - t * t) * (_GELU_K * (1.0 + 3.0 * _GELU_C * x2))


def _ln_stats(r):
    mu = jnp.mean(r, axis=-1, keepdims=True)
    rc = r - mu
    var = jnp.mean(rc * rc, axis=-1, keepdims=True)
    rstd = lax.rsqrt(var + LN_EPS)
    return rc * rstd, rstd


def _ln_bwd(dy, xh, rstd, g):
    dxh = dy * g
    m1 = jnp.mean(dxh, axis=-1, keepdims=True)
    m2 = jnp.mean(dxh * xh, axis=-1, keepdims=True)
    return rstd * (dxh - m1 - xh * m2)


def _colsum(v):
    return jnp.sum(v, axis=0, keepdims=True)


def _chunk_mask(transposed):
    shift = CHUNK.bit_length() - 1
    r = lax.broadcasted_iota(jnp.int32, (GMLP_BLOCK, GMLP_BLOCK), 0) >> shift
    c = lax.broadcasted_iota(jnp.int32, (GMLP_BLOCK, GMLP_BLOCK), 1) >> shift
    return (r <= c) if transposed else (c <= r)


class _Comm:
    def __init__(self, inputs, out_shape, scratch, run):
        self.inputs, self.out_shape, self.scratch, self.run = list(inputs), list(out_shape), list(scratch), run
        self.parts = [len(self.out_shape)]

    def split(self, outs):
        res, o = [], 0
        for n in self.parts:
            res.append(list(outs[o:o + n]))
            o += n
        return res


def _join(*comms):
    comms = [c for c in comms if c is not None]
    if not comms:
        return None

    def run(ins, outs, sems, phase):
        i = o = s = 0
        for c in comms:
            c.run(ins[i:i + len(c.inputs)], outs[o:o + len(c.out_shape)], sems[s:s + len(c.scratch)], phase)
            i, o, s = i + len(c.inputs), o + len(c.out_shape), s + len(c.scratch)

    joined = _Comm(sum((c.inputs for c in comms), []), sum((c.out_shape for c in comms), []),
                   sum((c.scratch for c in comms), []), run)
    joined.parts = [len(c.out_shape) for c in comms]
    return joined


def _call(body, *, name, grid, in_specs, out_specs, out_shape, args, sem, scratch_shapes=(), aliases=None, carry=None):
    in_specs, out_specs, out_shape = list(in_specs), list(out_specs), list(out_shape)
    scratch_shapes = list(scratch_shapes)
    if carry is None:
        return pl.pallas_call(body, name=name, grid=grid, in_specs=in_specs, out_specs=out_specs,
                              out_shape=out_shape, scratch_shapes=scratch_shapes,
                              input_output_aliases=aliases or {}, compiler_params=_params(*sem))(*args)
    n_in, n_out, n_scr = len(args), len(out_shape), len(scratch_shapes)
    c_in, c_out = len(carry.inputs), len(carry.out_shape)

    def wrapped(*refs):
        ins, c_ins = refs[:n_in], refs[n_in:n_in + c_in]
        o0 = n_in + c_in
        outs, c_outs = refs[o0:o0 + n_out], refs[o0 + n_out:o0 + n_out + c_out]
        s0 = o0 + n_out + c_out
        scr, c_sems = refs[s0:s0 + n_scr], refs[s0 + n_scr:]
        ids = [pl.program_id(a) for a in range(len(grid))]
        first = functools.reduce(jnp.logical_and, [i == 0 for i in ids])
        last = functools.reduce(jnp.logical_and, [i == g - 1 for i, g in zip(ids, grid)])

        @pl.when(first)
        def _():
            carry.run(c_ins, c_outs, c_sems, "start")

        body(*ins, *outs, *scr)

        @pl.when(last)
        def _():
            carry.run(c_ins, c_outs, c_sems, "finish")

    res = pl.pallas_call(
        wrapped, name=name, grid=grid, in_specs=in_specs + [ANY] * c_in, out_specs=out_specs + [ANY] * c_out,
        out_shape=out_shape + carry.out_shape, scratch_shapes=scratch_shapes + carry.scratch,
        input_output_aliases=aliases or {},
        compiler_params=pltpu.CompilerParams(dimension_semantics=("arbitrary",) * len(grid),
                                             vmem_limit_bytes=VMEM_LIMIT_BYTES, has_side_effects=True),
    )(*args, *carry.inputs)
    return list(res[:n_out]), list(res[n_out:])


def _comm_call(comm, *, name):
    n_in, n_out = len(comm.inputs), len(comm.out_shape)

    def body(*refs):
        ins, outs, sems = refs[:n_in], refs[n_in:n_in + n_out], refs[n_in + n_out:]
        comm.run(ins, outs, sems, "start")
        comm.run(ins, outs, sems, "finish")

    return list(pl.pallas_call(
        body, name=name, in_specs=[ANY] * n_in, out_specs=[ANY] * n_out, out_shape=comm.out_shape,
        scratch_shapes=comm.scratch, compiler_params=pltpu.CompilerParams(has_side_effects=True))(*comm.inputs))


def _ffn_fwd(xh, lg, lb, wgu, wd, *, affine, name, final=None, carry=None):
    T, D = xh.shape
    F = wd.shape[0]
    tm = _tile(T, 512)
    tn = _tile(F, 512, LANES)
    nj = F // tn
    is_final = final is not None

    def body(*refs):
        if is_final:
            (xh_ref, lg_ref, lb_ref, wg_ref, wu_ref, wd_ref, ng_ref, nb_ref, tgt_ref,
             gate_ref, up_ref, xb_sc, dr_ref, loss_ref, dng_ref, dnb_ref, acc_sc) = refs
        else:
            (xh_ref, lg_ref, lb_ref, wg_ref, wu_ref, wd_ref,
             gate_ref, up_ref, xb_sc, xho_ref, rstd_ref, acc_sc) = refs
        i = pl.program_id(0)
        j = pl.program_id(1)

        def xin():
            v = xh_ref[...]
            return v * lg_ref[...] + lb_ref[...] if affine else v

        @pl.when(j == 0)
        def _():
            xb_sc[...] = xin().astype(BF16)
            acc_sc[...] = jnp.zeros_like(acc_sc)

        xb = xb_sc[...]
        g = _dot(xb, wg_ref[...])
        u = _dot(xb, wu_ref[...])
        gate_ref[...] = g.astype(BF16)
        up_ref[...] = u.astype(BF16)
        h = g * _sig(g) * u
        acc_sc[...] += _dot(h.astype(BF16), wd_ref[...])

        @pl.when(j == nj - 1)
        def _():
            r = ALPHA * xin() + 0.5 * acc_sc[...]
            xho, rstd = _ln_stats(r)
            if not is_final:
                xho_ref[...] = xho
                rstd_ref[...] = jnp.broadcast_to(rstd, (tm, LANES))
            else:
                @pl.when(i == 0)
                def _():
                    loss_ref[...] = jnp.zeros_like(loss_ref)
                    dng_ref[...] = jnp.zeros_like(dng_ref)
                    dnb_ref[...] = jnp.zeros_like(dnb_ref)
                ng = ng_ref[...]
                e = xho * ng + nb_ref[...] - tgt_ref[...]
                part = _colsum(jnp.sum(e * e, axis=1, keepdims=True)) * (0.5 / D)
                loss_ref[...] += jnp.broadcast_to(part, loss_ref.shape)
                dy = e * (1.0 / D)
                dng_ref[...] += _colsum(dy * xho)
                dnb_ref[...] += _colsum(dy)
                dr_ref[...] = _ln_bwd(dy, xho, rstd, ng)

    row = pl.BlockSpec((tm, D), lambda i, j: (i, 0))
    vec = pl.BlockSpec((1, D), lambda i, j: (0, 0))
    hid = pl.BlockSpec((tm, tn), lambda i, j: (i, j))
    in_specs = [row, vec, vec,
                pl.BlockSpec((D, tn), lambda i, j: (0, j)),
                pl.BlockSpec((D, tn), lambda i, j: (0, j + nj)),
                pl.BlockSpec((tn, D), lambda i, j: (j, 0))]
    args = [xh, lg, lb, wgu, wgu, wd]
    out_shape = [jax.ShapeDtypeStruct((T, F), BF16), jax.ShapeDtypeStruct((T, F), BF16),
                 jax.ShapeDtypeStruct((T, D), BF16)]
    out_specs = [hid, hid, row]
    if is_final:
        in_specs += [vec, vec, row]
        args += list(final)
        out_shape += [jax.ShapeDtypeStruct((T, D), F32), jax.ShapeDtypeStruct((8, LANES), F32),
                      jax.ShapeDtypeStruct((1, D), F32), jax.ShapeDtypeStruct((1, D), F32)]
        out_specs += [row, pl.BlockSpec((8, LANES), lambda i, j: (0, 0)), vec, vec]
        sem = ("arbitrary", "arbitrary")
    else:
        out_shape += [jax.ShapeDtypeStruct((T, D), F32), jax.ShapeDtypeStruct((T, LANES), F32)]
        out_specs += [row, pl.BlockSpec((tm, LANES), lambda i, j: (i, 0))]
        sem = ("parallel", "arbitrary")
    return _call(body, name=name, grid=(T // tm, nj), in_specs=in_specs, out_specs=out_specs, out_shape=out_shape,
                 scratch_shapes=[pltpu.VMEM((tm, D), F32)], sem=sem, args=args, carry=carry)


def _ffn_bwd(dr, gate, up, wgu, wd, *, name, prev=None, carry=None):
    T, D = dr.shape
    F = wd.shape[0]
    tm = _tile(T, 512)
    tn = _tile(F, 512, LANES)
    nj = F // tn
    has_prev = prev is not None

    def body(*refs):
        if has_prev:
            (dr_ref, gate_ref, up_ref, wd_ref, wg_ref, wu_ref, xh_ref, rstd_ref, lg_ref,
             h_ref, dg_ref, du_ref, dprev_ref, dlg_ref, dlb_ref, df_sc, dx_sc) = refs
        else:
            (dr_ref, gate_ref, up_ref, wd_ref, wg_ref, wu_ref,
             h_ref, dg_ref, du_ref, dprev_ref, df_sc, dx_sc) = refs
        i = pl.program_id(0)
        j = pl.program_id(1)

        @pl.when(j == 0)
        def _():
            d = dr_ref[...]
            df_sc[...] = (0.5 * d).astype(BF16)
            dx_sc[...] = ALPHA * d

        g = gate_ref[...].astype(F32)
        u = up_ref[...].astype(F32)
        dh = _dot_nt(df_sc[...], wd_ref[...])
        s = _sig(g)
        sil = g * s
        h_ref[...] = (sil * u).astype(BF16)
        dg = (dh * u * (s * (1.0 + g * (1.0 - s)))).astype(BF16)
        du = (dh * sil).astype(BF16)
        dg_ref[...] = dg
        du_ref[...] = du
        dx_sc[...] += _dot_nt(dg, wg_ref[...]) + _dot_nt(du, wu_ref[...])

        @pl.when(j == nj - 1)
        def _():
            dxin = dx_sc[...]
            if not has_prev:
                dprev_ref[...] = dxin
            else:
                @pl.when(i == 0)
                def _():
                    dlg_ref[...] = jnp.zeros_like(dlg_ref)
                    dlb_ref[...] = jnp.zeros_like(dlb_ref)
                xh = xh_ref[...]
                dlg_ref[...] += _colsum(dxin * xh)
                dlb_ref[...] += _colsum(dxin)
                dprev_ref[...] = _ln_bwd(dxin, xh, rstd_ref[:, 0:1], lg_ref[...])

    row = pl.BlockSpec((tm, D), lambda i, j: (i, 0))
    vec = pl.BlockSpec((1, D), lambda i, j: (0, 0))
    hid = pl.BlockSpec((tm, tn), lambda i, j: (i, j))
    in_specs = [row, hid, hid,
                pl.BlockSpec((tn, D), lambda i, j: (j, 0)),
                pl.BlockSpec((D, tn), lambda i, j: (0, j)),
                pl.BlockSpec((D, tn), lambda i, j: (0, j + nj))]
    args = [dr, gate, up, wd, wgu, wgu]
    out_shape = [jax.ShapeDtypeStruct((T, F), BF16)] * 3 + [jax.ShapeDtypeStruct((T, D), F32)]
    out_specs = [hid, hid, hid, row]
    if has_prev:
        in_specs += [row, pl.BlockSpec((tm, LANES), lambda i, j: (i, 0)), vec]
        args += list(prev)
        out_shape += [jax.ShapeDtypeStruct((1, D), F32)] * 2
        out_specs += [vec, vec]
        sem = ("arbitrary", "arbitrary")
    else:
        sem = ("parallel", "arbitrary")
    return _call(body, name=name, grid=(T // tm, nj), in_specs=in_specs, out_specs=out_specs, out_shape=out_shape,
                 scratch_shapes=[pltpu.VMEM((tm, D), BF16), pltpu.VMEM((tm, D), F32)], sem=sem, args=args,
                 carry=carry)


def _mm_tn(a, b, *, name, tm_pref, tn_pref, scale=1.0, a_affine=None, into=None, col_off=0, n_total=None,
           carry=None):
    T, M = a.shape
    N = b.shape[1]
    n_total = N if n_total is None else n_total
    tM = _tile(M, tm_pref, LANES)
    tN = _tile(N, tn_pref, LANES)
    tk = _tile(T, 1024)
    nt = T // tk
    assert col_off % tN == 0
    off_blocks = col_off // tN
    has_aff = a_affine is not None
    has_into = into is not None

    def body(*refs):
        refs = list(refs)
        a_ref = refs.pop(0)
        if has_aff:
            lg_ref = refs.pop(0)
            lb_ref = refs.pop(0)
        b_ref = refs.pop(0)
        if has_into:
            refs.pop(0)
        o_ref, acc_sc = refs
        t = pl.program_id(2)

        @pl.when(t == 0)
        def _():
            acc_sc[...] = jnp.zeros_like(acc_sc)

        av = a_ref[...]
        if has_aff:
            av = av * lg_ref[...] + lb_ref[...]
        acc_sc[...] += _dot_tn(av.astype(BF16), b_ref[...].astype(BF16))

        @pl.when(t == nt - 1)
        def _():
            o_ref[...] = (acc_sc[...] * scale).astype(BF16)

    in_specs = [pl.BlockSpec((tk, tM), lambda m, n, t: (t, m))]
    args = [a]
    if has_aff:
        in_specs += [pl.BlockSpec((1, tM), lambda m, n, t: (0, m))] * 2
        args += list(a_affine)
    in_specs.append(pl.BlockSpec((tk, tN), lambda m, n, t: (t, n)))
    args.append(b)
    aliases = {}
    if has_into:
        aliases = {len(args): 0}
        in_specs.append(ANY)
        args.append(into)
    res = _call(body, name=name, grid=(M // tM, N // tN, nt), in_specs=in_specs,
                out_specs=[pl.BlockSpec((tM, tN), lambda m, n, t: (m, n + off_blocks))],
                out_shape=[jax.ShapeDtypeStruct((M, n_total), BF16)],
                scratch_shapes=[pltpu.VMEM((tM, tN), F32)], aliases=aliases,
                sem=("parallel", "parallel", "arbitrary"), args=args, carry=carry)
    return res[0] if carry is None else (res[0][0], res[1])


def _inproj_fwd(xh, lg, lb, w, bias, *, name):
    T, D = xh.shape
    N = w.shape[1]
    tm = _tile(T, 1024)
    tn = _tile(N, 1024, LANES)

    def body(xh_ref, lg_ref, lb_ref, w_ref, b_ref, o_ref, xb_ref):
        @pl.when(pl.program_id(1) == 0)
        def _():
            xb_ref[...] = (xh_ref[...] * lg_ref[...] + lb_ref[...]).astype(BF16)
        o_ref[...] = _dot(xb_ref[...], w_ref[...]) + b_ref[...]

    return pl.pallas_call(
        body, name=name, grid=(T // tm, N // tn),
        in_specs=[pl.BlockSpec((tm, D), lambda i, j: (i, 0)),
                  pl.BlockSpec((1, D), lambda i, j: (0, 0)), pl.BlockSpec((1, D), lambda i, j: (0, 0)),
                  pl.BlockSpec((D, tn), lambda i, j: (0, j)), pl.BlockSpec((1, tn), lambda i, j: (0, j))],
        out_specs=[pl.BlockSpec((tm, tn), lambda i, j: (i, j)), pl.BlockSpec((tm, D), lambda i, j: (i, 0))],
        out_shape=[jax.ShapeDtypeStruct((T, N), F32), jax.ShapeDtypeStruct((T, D), BF16)],
        compiler_params=_params("parallel", "arbitrary"))(xh, lg, lb, w, bias)


def _inproj_bwd(dproj, w, dr_next, xh, rstd, lg, *, name, carry=None):
    T, N = dproj.shape
    D = w.shape[0]
    tm = _tile(T, 512)
    tn = _tile(N, 1024, LANES)
    nj = N // tn

    def body(dp_ref, w_ref, drn_ref, xh_ref, rstd_ref, lg_ref, dprev_ref, dlg_ref, dlb_ref, dx_sc):
        i = pl.program_id(0)
        j = pl.program_id(1)

        @pl.when(j == 0)
        def _():
            dx_sc[...] = ALPHA * drn_ref[...]

        dx_sc[...] += _dot_nt(dp_ref[...], w_ref[...])

        @pl.when(j == nj - 1)
        def _():
            @pl.when(i == 0)
            def _():
                dlg_ref[...] = jnp.zeros_like(dlg_ref)
                dlb_ref[...] = jnp.zeros_like(dlb_ref)
            dx = dx_sc[...]
            x_hat = xh_ref[...]
            dlg_ref[...] += _colsum(dx * x_hat)
            dlb_ref[...] += _colsum(dx)
            dprev_ref[...] = _ln_bwd(dx, x_hat, rstd_ref[:, 0:1], lg_ref[...])

    row = pl.BlockSpec((tm, D), lambda i, j: (i, 0))
    vec = pl.BlockSpec((1, D), lambda i, j: (0, 0))
    return _call(
        body, name=name, grid=(T // tm, nj),
        in_specs=[pl.BlockSpec((tm, tn), lambda i, j: (i, j)), pl.BlockSpec((D, tn), lambda i, j: (0, j)),
                  row, row, pl.BlockSpec((tm, LANES), lambda i, j: (i, 0)), vec],
        out_specs=[row, vec, vec],
        out_shape=[jax.ShapeDtypeStruct((T, D), F32), jax.ShapeDtypeStruct((1, D), F32),
                   jax.ShapeDtypeStruct((1, D), F32)],
        scratch_shapes=[pltpu.VMEM((tm, D), F32)], sem=("arbitrary", "arbitrary"),
        args=[dproj, w, dr_next, xh, rstd, lg], carry=carry)


def _sgu_fwd(proj, ln_g, ln_b, w_s, b_sb, *, name):
    T = proj.shape[0]
    A = proj.shape[1] // 8
    hd = A // A_GROUPS
    tm = _tile(T, 256, GMLP_BLOCK)

    def body(u_ref, v_ref, g_ref, b_ref, ws_ref, bs_ref, o_ref):
        gu = _gelu(u_ref[...])
        vh, _ = _ln_stats(_gelu(v_ref[...]))
        vn = (vh * g_ref[...] + b_ref[...]).astype(BF16)
        mask = _chunk_mask(False)
        for h in range(A_GROUPS):
            wm = jnp.where(mask, ws_ref[h], 0.0).astype(BF16)
            cols = slice(h * hd, (h + 1) * hd)
            for n in range(tm // GMLP_BLOCK):
                rows = slice(n * GMLP_BLOCK, (n + 1) * GMLP_BLOCK)
                s = _dot(wm, vn[rows, cols]) + bs_ref[h][:, :hd]
                o_ref[rows, cols] = (gu[rows, cols] * s).astype(BF16)

    vec = pl.BlockSpec((1, A), lambda i: (0, 0))
    full = pl.BlockSpec((A_GROUPS, GMLP_BLOCK, GMLP_BLOCK), lambda i: (0, 0, 0))
    return pl.pallas_call(
        body, name=name, grid=(T // tm,),
        in_specs=[pl.BlockSpec((tm, A), lambda i: (i, 0)), pl.BlockSpec((tm, A), lambda i: (i, 1)),
                  vec, vec, full, full],
        out_specs=pl.BlockSpec((tm, A), lambda i: (i, 0)),
        out_shape=jax.ShapeDtypeStruct((T, A), BF16),
        compiler_params=_params("parallel"))(proj, proj, ln_g, ln_b, w_s, b_sb)


def _sgu_bwd(proj, dsg, ln_g, ln_b, w_s, w_st, b_sb, dproj, *, name):
    T = proj.shape[0]
    A = proj.shape[1] // 8
    hd = A // A_GROUPS
    tm = _tile(T, 256, GMLP_BLOCK)
    nt = T // tm

    def body(u_ref, v_ref, dsg_ref, g_ref, b_ref, ws_ref, wst_ref, bs_ref, _alias,
             dp_ref, dbin_ref, dlg_ref, dlb_ref, dws_ref, dbs_ref, dvn_sc, dgu_sc, dbs_sc):
        i = pl.program_id(0)

        @pl.when(i == 0)
        def _():
            dbin_ref[...] = jnp.zeros_like(dbin_ref)
            dlg_ref[...] = jnp.zeros_like(dlg_ref)
            dlb_ref[...] = jnp.zeros_like(dlb_ref)
            dws_ref[...] = jnp.zeros_like(dws_ref)
            dbs_sc[...] = jnp.zeros_like(dbs_sc)

        u = u_ref[...]
        v = v_ref[...]
        gu = _gelu(u)
        vh, rstd = _ln_stats(_gelu(v))
        gain = g_ref[...]
        vn = (vh * gain + b_ref[...]).astype(BF16)
        dsg_v = dsg_ref[...]
        mask = _chunk_mask(False)
        mask_t = _chunk_mask(True)
        for h in range(A_GROUPS):
            wm = jnp.where(mask, ws_ref[h], 0.0).astype(BF16)
            wmt = jnp.where(mask_t, wst_ref[h], 0.0).astype(BF16)
            cols = slice(h * hd, (h + 1) * hd)
            for n in range(tm // GMLP_BLOCK):
                rows = slice(n * GMLP_BLOCK, (n + 1) * GMLP_BLOCK)
                vb = vn[rows, cols]
                s = _dot(wm, vb) + bs_ref[h][:, :hd]
                d_out = dsg_v[rows, cols]
                dgu_sc[rows, cols] = d_out * s
                ds = d_out * gu[rows, cols]
                ds_b = ds.astype(BF16)
                dws_ref[h] += _dot_nt(ds_b, vb)
                dbs_sc[h] += ds
                dvn_sc[rows, cols] = _dot(wmt, ds_b)
        dvn = dvn_sc[...]
        dlg_ref[...] += _colsum(dvn * vh)
        dlb_ref[...] += _colsum(dvn)
        dv = _ln_bwd(dvn, vh, rstd, gain) * _gelu_grad(v)
        du = dgu_sc[...] * _gelu_grad(u)
        dp_ref[:, 0:A] = du.astype(BF16)
        dp_ref[:, A:2 * A] = dv.astype(BF16)
        dbin_ref[:, 0:A] += _colsum(du)
        dbin_ref[:, A:2 * A] += _colsum(dv)

        @pl.when(i == nt - 1)
        def _():
            for h in range(A_GROUPS):
                dws_ref[h] = jnp.where(mask, dws_ref[h], 0.0)
                dbs_ref[h:h + 1, :] = _colsum(dbs_sc[h].T)

    vec = pl.BlockSpec((1, A), lambda i: (0, 0))
    full = pl.BlockSpec((A_GROUPS, GMLP_BLOCK, GMLP_BLOCK), lambda i: (0, 0, 0))
    tile = pl.BlockSpec((tm, A), lambda i: (i, 0))
    return pl.pallas_call(
        body, name=name, grid=(nt,),
        in_specs=[tile, pl.BlockSpec((tm, A), lambda i: (i, 1)), tile, vec, vec, full, full, full, ANY],
        out_specs=[pl.BlockSpec((tm, 2 * A), lambda i: (i, 0)), pl.BlockSpec((1, 2 * A), lambda i: (0, 0)),
                   vec, vec, full, pl.BlockSpec((A_GROUPS, GMLP_BLOCK), lambda i: (0, 0))],
        out_shape=[jax.ShapeDtypeStruct(dproj.shape, BF16), jax.ShapeDtypeStruct((1, 2 * A), F32),
                   jax.ShapeDtypeStruct((1, A), F32), jax.ShapeDtypeStruct((1, A), F32),
                   jax.ShapeDtypeStruct((A_GROUPS, GMLP_BLOCK, GMLP_BLOCK), F32),
                   jax.ShapeDtypeStruct((A_GROUPS, GMLP_BLOCK), F32)],
        scratch_shapes=[pltpu.VMEM((tm, A), F32), pltpu.VMEM((tm, A), F32),
                        pltpu.VMEM((A_GROUPS, GMLP_BLOCK, hd), F32)],
        input_output_aliases={8: 0},
        compiler_params=_params("arbitrary"))(proj, proj, dsg, ln_g, ln_b, w_s, w_st, b_sb, dproj)


def _conv_tiles(T, B):
    tm = _tile(T, 256, CONV_ROWS)
    lb = min(LANES, B)
    return tm, tm // CONV_HALO, lb


def _conv_fwd(proj, w_dw, b_dw, ln_g, ln_b, *, name):
    T = proj.shape[0]
    B = proj.shape[1] // 8
    tm, nh, lb = _conv_tiles(T, B)

    def body(ap_ref, gp_ref, a_ref, g_ref, w_ref, bdw_ref, lg_ref, lb_ref, c_ref, cv_ref, z_sc):
        i = pl.program_id(0)
        z_sc[0:CONV_HALO, :] = jnp.where(i > 0, ap_ref[...] * _sig(gp_ref[...]), 0.0)
        z_sc[CONV_HALO:CONV_HALO + tm, :] = a_ref[...] * _sig(g_ref[...])
        for cb in range(B // lb):
            ls = slice(cb * lb, (cb + 1) * lb)
            for rc in range(tm // CONV_ROWS):
                base = rc * CONV_ROWS + CONV_HALO - (CONV_WIDTH - 1)
                acc = jnp.zeros((CONV_ROWS, lb), F32)
                for k in range(CONV_WIDTH):
                    acc = acc + w_ref[k:k + 1, ls] * z_sc[base + k:base + k + CONV_ROWS, ls]
                c_ref[rc * CONV_ROWS:(rc + 1) * CONV_ROWS, ls] = acc + bdw_ref[:, ls]
        xh, _ = _ln_stats(c_ref[...])
        y = xh * lg_ref[...] + lb_ref[...]
        cv_ref[...] = (y * _sig(y)).astype(BF16)

    vec = pl.BlockSpec((1, B), lambda i: (0, 0))
    halo_a = pl.BlockSpec((CONV_HALO, B), lambda i: (jnp.maximum(i * nh - 1, 0), 2))
    halo_g = pl.BlockSpec((CONV_HALO, B), lambda i: (jnp.maximum(i * nh - 1, 0), 3))
    return pl.pallas_call(
        body, name=name, grid=(T // tm,),
        in_specs=[halo_a, halo_g, pl.BlockSpec((tm, B), lambda i: (i, 2)), pl.BlockSpec((tm, B), lambda i: (i, 3)),
                  pl.BlockSpec((CONV_WPAD, B), lambda i: (0, 0)), vec, vec, vec],
        out_specs=[pl.BlockSpec((tm, B), lambda i: (i, 0))] * 2,
        out_shape=[jax.ShapeDtypeStruct((T, B), F32), jax.ShapeDtypeStruct((T, B), BF16)],
        scratch_shapes=[pltpu.VMEM((CONV_HALO + tm, B), F32)],
        compiler_params=_params("parallel"))(proj, proj, proj, proj, w_dw, b_dw, ln_g, ln_b)


def _conv_bwd_ln(dcv, c, ln_g, ln_b, *, name):
    T, B = c.shape
    tm = _tile(T, 512)

    def body(dcv_ref, c_ref, lg_ref, lb_ref, dc_ref, dlg_ref, dlb_ref, dbdw_ref):
        @pl.when(pl.program_id(0) == 0)
        def _():
            dlg_ref[...] = jnp.zeros_like(dlg_ref)
            dlb_ref[...] = jnp.zeros_like(dlb_ref)
            dbdw_ref[...] = jnp.zeros_like(dbdw_ref)
        gain = lg_ref[...]
        xh, rstd = _ln_stats(c_ref[...])
        y = xh * gain + lb_ref[...]
        s = _sig(y)
        dy = dcv_ref[...] * (s * (1.0 + y * (1.0 - s)))
        dlg_ref[...] += _colsum(dy * xh)
        dlb_ref[...] += _colsum(dy)
        dc = _ln_bwd(dy, xh, rstd, gain)
        dc_ref[...] = dc
        dbdw_ref[...] += _colsum(dc)

    tile = pl.BlockSpec((tm, B), lambda i: (i, 0))
    vec = pl.BlockSpec((1, B), lambda i: (0, 0))
    return pl.pallas_call(
        body, name=name, grid=(T // tm,), in_specs=[tile, tile, vec, vec], out_specs=[tile, vec, vec, vec],
        out_shape=[jax.ShapeDtypeStruct((T, B), F32)] + [jax.ShapeDtypeStruct((1, B), F32)] * 3,
        compiler_params=_params("arbitrary"))(dcv, c, ln_g, ln_b)


def _conv_bwd(proj, dc, w_dw, dproj, *, name, carry=None):
    T = proj.shape[0]
    B = proj.shape[1] // 8
    tm, nh, lb = _conv_tiles(T, B)
    nt = T // tm
    n_halo = T // CONV_HALO

    def body(ap_ref, gp_ref, a_ref, g_ref, dc_ref, dcn_ref, w_ref, _alias,
             dp_ref, dbin_ref, dw_ref, z_sc, dc_sc, dz_sc, dw_sc):
        i = pl.program_id(0)

        @pl.when(i == 0)
        def _():
            dbin_ref[...] = jnp.zeros_like(dbin_ref)
            dw_sc[...] = jnp.zeros_like(dw_sc)

        a = a_ref[...]
        s = _sig(g_ref[...])
        z_sc[0:CONV_HALO, :] = jnp.where(i > 0, ap_ref[...] * _sig(gp_ref[...]), 0.0)
        z_sc[CONV_HALO:CONV_HALO + tm, :] = a * s
        dc_sc[0:tm, :] = dc_ref[...]
        dc_sc[tm:tm + CONV_HALO, :] = jnp.where(i < nt - 1, dcn_ref[...], 0.0)
        for cb in range(B // lb):
            ls = slice(cb * lb, (cb + 1) * lb)
            for rc in range(tm // CONV_ROWS):
                r0 = rc * CONV_ROWS
                dcc = dc_sc[r0:r0 + CONV_ROWS, ls]
                acc = jnp.zeros((CONV_ROWS, lb), F32)
                for k in range(CONV_WIDTH):
                    up = r0 + (CONV_WIDTH - 1) - k
                    acc = acc + w_ref[k:k + 1, ls] * dc_sc[up:up + CONV_ROWS, ls]
                    dn = r0 + CONV_HALO - (CONV_WIDTH - 1) + k
                    prod = dcc * z_sc[dn:dn + CONV_ROWS, ls]
                    dw_sc[8 * k:8 * k + 8, ls] += jnp.sum(prod.reshape(CONV_ROWS // 8, 8, lb), axis=0)
                dz_sc[r0:r0 + CONV_ROWS, ls] = acc
        dz = dz_sc[...]
        da = dz * s
        dg = dz * a * s * (1.0 - s)
        dp_ref[:, 0:B] = da.astype(BF16)
        dp_ref[:, B:2 * B] = dg.astype(BF16)
        dbin_ref[:, 0:B] += _colsum(da)
        dbin_ref[:, B:2 * B] += _colsum(dg)

        @pl.when(i == nt - 1)
        def _():
            for k in range(CONV_WIDTH):
                dw_ref[k:k + 1, :] = _colsum(dw_sc[8 * k:8 * k + 8, :])

    halo_a = pl.BlockSpec((CONV_HALO, B), lambda i: (jnp.maximum(i * nh - 1, 0), 2))
    halo_g = pl.BlockSpec((CONV_HALO, B), lambda i: (jnp.maximum(i * nh - 1, 0), 3))
    halo_dc = pl.BlockSpec((CONV_HALO, B), lambda i: (jnp.minimum((i + 1) * nh, n_halo - 1), 0))
    return _call(
        body, name=name, grid=(nt,),
        in_specs=[halo_a, halo_g, pl.BlockSpec((tm, B), lambda i: (i, 2)), pl.BlockSpec((tm, B), lambda i: (i, 3)),
                  pl.BlockSpec((tm, B), lambda i: (i, 0)), halo_dc,
                  pl.BlockSpec((CONV_WPAD, B), lambda i: (0, 0)), ANY],
        out_specs=[pl.BlockSpec((tm, 2 * B), lambda i: (i, 1)), pl.BlockSpec((1, 2 * B), lambda i: (0, 0)),
                   pl.BlockSpec((CONV_WIDTH, B), lambda i: (0, 0))],
        out_shape=[jax.ShapeDtypeStruct(dproj.shape, BF16), jax.ShapeDtypeStruct((1, 2 * B), F32),
                   jax.ShapeDtypeStruct((CONV_WIDTH, B), F32)],
        scratch_shapes=[pltpu.VMEM((CONV_HALO + tm, B), F32), pltpu.VMEM((tm + CONV_HALO, B), F32),
                        pltpu.VMEM((tm, B), F32), pltpu.VMEM((8 * CONV_WIDTH, B), F32)],
        aliases={7: 0}, sem=("arbitrary",), args=[proj, proj, proj, proj, dc, dc, w_dw, dproj], carry=carry)


def _mix_fwd_gate(sg, cv, proj, wa, wb, *, name):
    T, A = sg.shape
    D = wa.shape[1]
    tm = _tile(T, 256)

    def body(sg_ref, cv_ref, la_ref, lb_ref, wa_ref, wb_ref, ya_ref, yb_ref, m_ref):
        ya = _dot(sg_ref[...], wa_ref[...])
        yb = _dot(cv_ref[...], wb_ref[...])
        ya_ref[...] = ya.astype(BF16)
        yb_ref[...] = yb.astype(BF16)
        m_ref[...] = (_sig(la_ref[...]) * ya + _sig(lb_ref[...]) * yb).astype(BF16)

    act = pl.BlockSpec((tm, A), lambda i: (i, 0))
    wide = pl.BlockSpec((tm, D), lambda i: (i, 0))
    wspec = pl.BlockSpec((A, D), lambda i: (0, 0))
    return pl.pallas_call(
        body, name=name, grid=(T // tm,),
        in_specs=[act, act, pl.BlockSpec((tm, D), lambda i: (i, 2)), pl.BlockSpec((tm, D), lambda i: (i, 3)),
                  wspec, wspec],
        out_specs=[wide] * 3, out_shape=[jax.ShapeDtypeStruct((T, D), BF16)] * 3,
        compiler_params=_params("parallel"))(sg, cv, proj, proj, wa, wb)


def _mix_fwd_out(m, wout, xh, lg, lb, *, name):
    T, D = xh.shape
    tm = _tile(T, 512)

    def body(m_ref, w_ref, xh_ref, lg_ref, lb_ref, xho_ref, rstd_ref):
        r = ALPHA * (xh_ref[...] * lg_ref[...] + lb_ref[...]) + _dot(m_ref[...], w_ref[...])
        xho, rstd = _ln_stats(r)
        xho_ref[...] = xho
        rstd_ref[...] = jnp.broadcast_to(rstd, (tm, LANES))

    row = pl.BlockSpec((tm, D), lambda i: (i, 0))
    vec = pl.BlockSpec((1, D), lambda i: (0, 0))
    return pl.pallas_call(
        body, name=name, grid=(T // tm,),
        in_specs=[row, pl.BlockSpec((D, D), lambda i: (0, 0)), row, vec, vec],
        out_specs=[row, pl.BlockSpec((tm, LANES), lambda i: (i, 0))],
        out_shape=[jax.ShapeDtypeStruct((T, D), F32), jax.ShapeDtypeStruct((T, LANES), F32)],
        compiler_params=_params("parallel"))(m, wout, xh, lg, lb)


def _mix_bwd_gate(dr, wout, proj, ya, yb, *, name, carry=None):
    T, D = dr.shape
    N = proj.shape[1]
    tm = _tile(T, 256)

    def body(dr_ref, w_ref, la_ref, lb_ref, ya_ref, yb_ref, dya_ref, dyb_ref, dp_ref, dbin_ref):
        @pl.when(pl.program_id(0) == 0)
        def _():
            dbin_ref[...] = jnp.zeros_like(dbin_ref)
        dm = _dot_nt(dr_ref[...].astype(BF16), w_ref[...])
        sa = _sig(la_ref[...])
        sb = _sig(lb_ref[...])
        dya_ref[...] = (dm * sa).astype(BF16)
        dyb_ref[...] = (dm * sb).astype(BF16)
        dla = dm * ya_ref[...].astype(F32) * sa * (1.0 - sa)
        dlb = dm * yb_ref[...].astype(F32) * sb * (1.0 - sb)
        dp_ref[:, 0:D] = dla.astype(BF16)
        dp_ref[:, D:2 * D] = dlb.astype(BF16)
        dbin_ref[:, 0:D] += _colsum(dla)
        dbin_ref[:, D:2 * D] += _colsum(dlb)

    row = pl.BlockSpec((tm, D), lambda i: (i, 0))
    return _call(
        body, name=name, grid=(T // tm,),
        in_specs=[row, pl.BlockSpec((D, D), lambda i: (0, 0)), pl.BlockSpec((tm, D), lambda i: (i, 2)),
                  pl.BlockSpec((tm, D), lambda i: (i, 3)), row, row],
        out_specs=[row, row, pl.BlockSpec((tm, 2 * D), lambda i: (i, 1)), pl.BlockSpec((1, 2 * D), lambda i: (0, 0))],
        out_shape=[jax.ShapeDtypeStruct((T, D), BF16), jax.ShapeDtypeStruct((T, D), BF16),
                   jax.ShapeDtypeStruct((T, N), BF16), jax.ShapeDtypeStruct((1, 2 * D), F32)],
        sem=("arbitrary",), args=[dr, wout, proj, proj, ya, yb], carry=carry)


def _mix_bwd_proj(dya, dyb, wa, wb, *, name):
    T, D = dya.shape
    A = wa.shape[0]
    tm = _tile(T, 512)

    def body(dya_ref, dyb_ref, wa_ref, wb_ref, dsg_ref, dcv_ref):
        dsg_ref[...] = _dot_nt(dya_ref[...], wa_ref[...])
        dcv_ref[...] = _dot_nt(dyb_ref[...], wb_ref[...])

    row = pl.BlockSpec((tm, D), lambda i: (i, 0))
    wspec = pl.BlockSpec((A, D), lambda i: (0, 0))
    act = pl.BlockSpec((tm, A), lambda i: (i, 0))
    return pl.pallas_call(
        body, name=name, grid=(T // tm,), in_specs=[row, row, wspec, wspec], out_specs=[act, act],
        out_shape=[jax.ShapeDtypeStruct((T, A), F32)] * 2,
        compiler_params=_params("parallel"))(dya, dyb, wa, wb)


def _mesh_pos():
    return lax.axis_index("x"), lax.axis_index("y"), lax.axis_index("c")


def _shard_view(ref, p, shape, axis):
    r, c = shape
    if axis == 0:
        return ref.at[pl.ds(pl.multiple_of(p * r, 16), r), :]
    return ref.at[:, pl.ds(pl.multiple_of(p * c, LANES), c)]


def _all_gather(shards, axes):
    n = len(shards)
    shapes = [s.shape for s in shards]

    def run(ins, outs, sems, phase):
        send_sems, recv_sems, local_sems = sems
        x, y, c = _mesh_pos()
        me, sibling = (x, y, c), (x, y, 1 - c)
        chips = [(1 - x, y), (x, 1 - y), (1 - x, 1 - y)]

        def view(t, pos):
            px, py, pc = pos
            return _shard_view(outs[t], 4 * px + 2 * py + pc, shapes[t], axes[t])

        def copy(t, k, block, to, src=None):
            return pltpu.make_async_remote_copy(
                src_ref=view(t, block) if src is None else src, dst_ref=view(t, block),
                send_sem=send_sems.at[7 * t + k], recv_sem=recv_sems.at[7 * t + k],
                device_id=to, device_id_type=MESH)

        mine = [pltpu.make_async_copy(ins[t], view(t, me), local_sems.at[t]) for t in range(n)]
        first = []
        for t in range(n):
            first.append(copy(t, 0, me, sibling, src=ins[t]))
            first += [copy(t, 1 + j, me, (*chip, c), src=ins[t]) for j, chip in enumerate(chips)]
        if phase == "start":
            for cp in mine + first:
                cp.start()
            return
        passed = []
        for t in range(n):
            for j, chip in enumerate(chips):
                copy(t, 1 + j, (*chip, c), me).wait_recv()
                fwd = copy(t, 4 + j, (*chip, c), sibling)
                fwd.start()
                passed.append(fwd)
        for t in range(n):
            copy(t, 0, sibling, me).wait_recv()
            for j, chip in enumerate(chips):
                copy(t, 4 + j, (*chip, 1 - c), me).wait_recv()
        for cp in first + passed:
            cp.wait_send()
        for cp in mine:
            cp.wait()

    out_shape = [jax.ShapeDtypeStruct((N_DEV * s.shape[0], s.shape[1]) if ax == 0
                                      else (s.shape[0], N_DEV * s.shape[1]), s.dtype)
                 for s, ax in zip(shards, axes)]
    return _Comm(shards, out_shape, [pltpu.SemaphoreType.DMA((7 * n,)), pltpu.SemaphoreType.DMA((7 * n,)),
                                     pltpu.SemaphoreType.DMA((n,))], run)


def _rs_to_sibling(grads, shapes, axes):
    n = len(grads)

    def run(gs, outs, sems, phase):
        send_sems, recv_sems = sems
        x, y, c = _mesh_pos()
        copies = [pltpu.make_async_remote_copy(
            src_ref=_shard_view(gs[t], 2 * k + (1 - c), shapes[t], axes[t]), dst_ref=outs[t].at[k],
            send_sem=send_sems.at[4 * t + k], recv_sem=recv_sems.at[4 * t + k],
            device_id=(x, y, 1 - c), device_id_type=MESH) for t in range(n) for k in range(4)]
        if phase == "start":
            for cp in copies:
                cp.start()
            return
        for cp in copies:
            cp.wait_recv()
        for cp in copies:
            cp.wait_send()

    return _Comm(grads, [jax.ShapeDtypeStruct((4,) + tuple(s), BF16) for s in shapes],
                 [pltpu.SemaphoreType.DMA((4 * n,)), pltpu.SemaphoreType.DMA((4 * n,))], run)


def _rs_pair_sum(g, recv, cidx, shape, axis, *, name):
    r, c = shape
    tr = _tile(r, max(8, (1 << 20) // c), 16)
    nr = r // tr

    def body(c_ref, g_ref, rv_ref, o_ref):
        o_ref[...] = (g_ref[...].astype(F32) + rv_ref[...].astype(F32)).astype(BF16)

    if axis == 1:
        g_spec = pl.BlockSpec((tr, c), lambda k, i, s: (i, 2 * k + s[0]))
    else:
        g_spec = pl.BlockSpec((tr, c), lambda k, i, s: ((2 * k + s[0]) * nr + i, 0))
    blk = pl.BlockSpec((None, tr, c), lambda k, i, s: (k, i, 0))
    return pl.pallas_call(
        body, name=name,
        grid_spec=pltpu.PrefetchScalarGridSpec(num_scalar_prefetch=1, grid=(4, nr), in_specs=[g_spec, blk],
                                               out_specs=blk),
        out_shape=jax.ShapeDtypeStruct((4, r, c), BF16),
        compiler_params=_params("parallel", "parallel"))(cidx, g, recv)


def _rs_to_chips(parts):
    n = len(parts)

    def run(ps, outs, sems, phase):
        send_sems, recv_sems, local_sems = sems
        x, y, c = _mesh_pos()
        my_chip = 2 * x + y
        peers = [(1 - x, y), (x, 1 - y), (1 - x, 1 - y)]
        local = [pltpu.make_async_copy(ps[t].at[my_chip], outs[t].at[my_chip], local_sems.at[t]) for t in range(n)]
        sends = [pltpu.make_async_remote_copy(
            src_ref=ps[t].at[2 * px + py], dst_ref=outs[t].at[my_chip],
            send_sem=send_sems.at[3 * t + j], recv_sem=recv_sems.at[3 * t + j],
            device_id=(px, py, c), device_id_type=MESH) for t in range(n) for j, (px, py) in enumerate(peers)]
        if phase == "start":
            for cp in local + sends:
                cp.start()
            return
        for t in range(n):
            for j, (px, py) in enumerate(peers):
                pltpu.make_async_remote_copy(
                    src_ref=ps[t].at[2 * px + py], dst_ref=outs[t].at[2 * px + py],
                    send_sem=send_sems.at[3 * t + j], recv_sem=recv_sems.at[3 * t + j],
                    device_id=(x, y, c), device_id_type=MESH).wait_recv()
        for cp in sends:
            cp.wait_send()
        for cp in local:
            cp.wait()

    return _Comm(parts, [jax.ShapeDtypeStruct(p.shape, BF16) for p in parts],
                 [pltpu.SemaphoreType.DMA((3 * n,)), pltpu.SemaphoreType.DMA((3 * n,)),
                  pltpu.SemaphoreType.DMA((n,))], run)


def _all_reduce_small(buf, *, name):
    R = buf.shape[0]

    def body(in_ref, out_ref, slots, send_sems, recv_sems, local_sem):
        x, y, c = _mesh_pos()
        me = 4 * x + 2 * y + c
        local = pltpu.make_async_copy(in_ref, slots.at[me], local_sem)
        local.start()
        flips = [(fx, fy, fc) for fx in (0, 1) for fy in (0, 1) for fc in (0, 1)][1:]
        peers = [(1 - x if fx else x, 1 - y if fy else y, 1 - c if fc else c) for fx, fy, fc in flips]
        sends = []
        for k, peer in enumerate(peers):
            cp = pltpu.make_async_remote_copy(src_ref=in_ref, dst_ref=slots.at[me], send_sem=send_sems.at[k],
                                              recv_sem=recv_sems.at[k], device_id=peer, device_id_type=MESH)
            cp.start()
            sends.append(cp)
        for k, (px, py, pc) in enumerate(peers):
            pltpu.make_async_remote_copy(src_ref=in_ref, dst_ref=slots.at[4 * px + 2 * py + pc],
                                         send_sem=send_sems.at[k], recv_sem=recv_sems.at[k],
                                         device_id=(x, y, c), device_id_type=MESH).wait_recv()
        for cp in sends:
            cp.wait_send()
        local.wait()
        acc = slots[0]
        for p in range(1, N_DEV):
            acc = acc + slots[p]
        out_ref[...] = acc

    vm = pl.BlockSpec(memory_space=pltpu.VMEM)
    return pl.pallas_call(
        body, name=name, in_specs=[vm], out_specs=vm, out_shape=jax.ShapeDtypeStruct(buf.shape, F32),
        scratch_shapes=[pltpu.VMEM((N_DEV, R, LANES), F32), pltpu.SemaphoreType.DMA((7,)),
                        pltpu.SemaphoreType.DMA((7,)), pltpu.SemaphoreType.DMA],
        compiler_params=pltpu.CompilerParams(has_side_effects=True, vmem_limit_bytes=VMEM_LIMIT_BYTES))(buf)


def _adam_math(g, w, m, v):
    m_new = ADAM_B1 * m + (1.0 - ADAM_B1) * g
    v_new = ADAM_B2 * v + (1.0 - ADAM_B2) * (g * g)
    m_hat = m_new / ADAM_C1
    v_hat = v_new / ADAM_C2
    delta = -ADAM_LR * (m_hat / (jnp.sqrt(v_hat) + ADAM_EPS) + ADAM_WD * w)
    return delta, m_new, v_new


def _adamw_sharded(q, w, m, v, *, name):
    r, c = w.shape
    tr = _tile(r, max(8, (1 << 18) // c), 16)

    def body(q_ref, w_ref, m_ref, v_ref, g_ref, d_ref, mo_ref, vo_ref):
        g = ((q_ref[0].astype(F32) + q_ref[1].astype(F32)) + q_ref[2].astype(F32)) + q_ref[3].astype(F32)
        g_ref[...] = g
        d_ref[...], mo_ref[...], vo_ref[...] = _adam_math(g, w_ref[...], m_ref[...], v_ref[...])

    blk = pl.BlockSpec((tr, c), lambda i: (i, 0))
    return pl.pallas_call(
        body, name=name, grid=(r // tr,),
        in_specs=[pl.BlockSpec((4, tr, c), lambda i: (0, i, 0)), blk, blk, blk], out_specs=[blk] * 4,
        out_shape=[jax.ShapeDtypeStruct((r, c), F32)] * 4,
        compiler_params=_params("parallel"))(q, w, m, v)


def _adamw_plain(g, w, m, v, *, name):
    r, c = w.shape
    tr = _tile(r, 512)

    def body(g_ref, w_ref, m_ref, v_ref, d_ref, mo_ref, vo_ref):
        d_ref[...], mo_ref[...], vo_ref[...] = _adam_math(g_ref[...], w_ref[...], m_ref[...], v_ref[...])

    blk = pl.BlockSpec((tr, c), lambda i: (i, 0))
    return pl.pallas_call(
        body, name=name, grid=(r // tr,), in_specs=[blk] * 4, out_specs=[blk] * 3,
        out_shape=[jax.ShapeDtypeStruct((r, c), F32)] * 3,
        compiler_params=_params("parallel"))(g, w, m, v)


def _pack_rows(arrays):
    return jnp.concatenate([a.reshape(-1, LANES) for a in arrays], axis=0)


def kernel(x, ffn1_w_gu, ffn1_w_down, ln1_g, ln1_b, w_in, b_in, sgu_ln_g, sgu_ln_b, sgu_w_s, sgu_b_s, w_a_proj, conv_w_dw, conv_b_dw, conv_ln_g, conv_ln_b, w_b_proj, w_out, ln2_g, ln2_b, ffn2_w_gu, ffn2_w_down, ln3_g, ln3_b, loss_target, m_ffn1_w_gu, m_ffn1_w_down, m_ln1_g, m_ln1_b, m_w_in, m_b_in, m_sgu_ln_g, m_sgu_ln_b, m_sgu_w_s, m_sgu_b_s, m_w_a_proj, m_conv_w_dw, m_conv_b_dw, m_conv_ln_g, m_conv_ln_b, m_w_b_proj, m_w_out, m_ln2_g, m_ln2_b, m_ffn2_w_gu, m_ffn2_w_down, m_ln3_g, m_ln3_b, v_ffn1_w_gu, v_ffn1_w_down, v_ln1_g, v_ln1_b, v_w_in, v_b_in, v_sgu_ln_g, v_sgu_ln_b, v_sgu_w_s, v_sgu_b_s, v_w_a_proj, v_conv_w_dw, v_conv_b_dw, v_conv_ln_g, v_conv_ln_b, v_w_b_proj, v_w_out, v_ln2_g, v_ln2_b, v_ffn2_w_gu, v_ffn2_w_down, v_ln3_g, v_ln3_b):
    given = dict(locals())
    w = {n: given[n][0] for n in WEIGHTS}
    mom = {n: given["m_" + n][0] for n in WEIGHTS}
    var = {n: given["v_" + n][0] for n in WEIGHTS}
    xt = x[0]
    target = loss_target[0]
    T, D = xt.shape
    A = w['w_a_proj'].shape[0]

    big_names = list(BIG)
    early = ['ffn1_w_gu', 'ffn1_w_down']
    late = [n for n in big_names if n not in early]
    conv_w_pad = jnp.pad(w['conv_w_dw'], ((0, CONV_WPAD - CONV_WIDTH), (0, 0)))
    w_bf = {n: w[n].astype(BF16) for n in big_names}
    full = dict(zip(early, _comm_call(_all_gather([w_bf[n] for n in early], [BIG[n] for n in early]),
                                      name="all_gather_ffn1")))
    gather_late = _all_gather([w_bf[n] for n in late] + [conv_w_pad], [BIG[n] for n in late] + [1])

    def row(v):
        return v.reshape(1, -1)

    ones = jnp.ones((1, D), F32)
    zeros = jnp.zeros((1, D), F32)
    w_s = w['sgu_w_s']
    w_st = jnp.swapaxes(w_s, 1, 2)
    b_sb = jnp.broadcast_to(w['sgu_b_s'][:, :, None], w_s.shape)

    (gate1, up1, xb0, xh1, rstd1), gathered = _ffn_fwd(xt, ones, zeros, full['ffn1_w_gu'], full['ffn1_w_down'],
                                                  affine=False, name="ffn1_fwd", carry=gather_late)
    full.update(zip(late, gathered[:-1]))
    conv_w_full = gathered[-1]
    g1, b1 = row(w['ln1_g']), row(w['ln1_b'])
    proj, xb1 = _inproj_fwd(xh1, g1, b1, full['w_in'], row(w['b_in']), name="inproj_fwd")
    sg = _sgu_fwd(proj, row(w['sgu_ln_g']), row(w['sgu_ln_b']), w_s, b_sb, name="sgu_fwd")
    conv_out, cv = _conv_fwd(proj, conv_w_full, row(w['conv_b_dw']), row(w['conv_ln_g']), row(w['conv_ln_b']),
                             name="conv_fwd")
    ya, yb, mixed = _mix_fwd_gate(sg, cv, proj, full['w_a_proj'], full['w_b_proj'], name="mix_fwd_gate")
    xh2, rstd2 = _mix_fwd_out(mixed, full['w_out'], xh1, g1, b1, name="mix_fwd_out")
    g2, b2 = row(w['ln2_g']), row(w['ln2_b'])
    gate2, up2, xb2, dr3, loss_part, d_ln3_g, d_ln3_b = _ffn_fwd(
        xh2, g2, b2, full['ffn2_w_gu'], full['ffn2_w_down'], affine=True, name="ffn2_fwd_loss",
        final=(row(w['ln3_g']), row(w['ln3_b']), target))

    F = full['ffn2_w_down'].shape[0]
    h2, dgate2, dup2, dr2, d_ln2_g, d_ln2_b = _ffn_bwd(dr3, gate2, up2, full['ffn2_w_gu'], full['ffn2_w_down'],
                                                      name="ffn2_bwd", prev=(xh2, rstd2, g2))
    G, P, Q = {}, {}, {}
    cidx = lax.axis_index("c").astype(jnp.int32).reshape(1)

    def to_sibling(names):
        return _rs_to_sibling([G[n] for n in names], [w[n].shape for n in names], [BIG[n] for n in names])

    def pair_sum(names, received):
        for n, rv in zip(names, received):
            P[n] = _rs_pair_sum(G[n], rv, cidx, w[n].shape, BIG[n], name="rs_pair_sum_" + n)

    def to_chips(names):
        return _rs_to_chips([P[n] for n in names])

    G['ffn2_w_down'] = _mm_tn(h2, dr3, name="dw_ffn2_down", tm_pref=1408, tn_pref=2048, scale=0.5)
    gu, rv = _mm_tn(xb2, dgate2, name="dw_ffn2_gate", tm_pref=2048, tn_pref=1408,
                    n_total=2 * F, carry=to_sibling(['ffn2_w_down']))
    pair_sum(['ffn2_w_down'], rv)
    G['ffn2_w_gu'], q = _mm_tn(xb2, dup2, name="dw_ffn2_up", tm_pref=2048, tn_pref=1408,
                               into=gu, col_off=F, n_total=2 * F, carry=to_chips(['ffn2_w_down']))
    Q['ffn2_w_down'] = q[0]

    (dya, dyb, dproj, dbin_gate), rv = _mix_bwd_gate(dr2, full['w_out'], proj, ya, yb, name="mix_bwd_gate",
                                                     carry=to_sibling(['ffn2_w_gu']))
    pair_sum(['ffn2_w_gu'], rv)
    dsg, dcv = _mix_bwd_proj(dya, dyb, full['w_a_proj'], full['w_b_proj'], name="mix_bwd_proj")
    dproj, dbin_sgu, d_sgu_ln_g, d_sgu_ln_b, d_w_s, d_b_s = _sgu_bwd(
        proj, dsg, row(w['sgu_ln_g']), row(w['sgu_ln_b']), w_s, w_st, b_sb, dproj, name="sgu_bwd")
    dconv, d_conv_ln_g, d_conv_ln_b, d_conv_b = _conv_bwd_ln(dcv, conv_out, row(w['conv_ln_g']),
                                                            row(w['conv_ln_b']), name="conv_bwd_ln")
    (dproj, dbin_conv, d_conv_w), q = _conv_bwd(proj, dconv, conv_w_full, dproj, name="conv_bwd",
                                                carry=to_chips(['ffn2_w_gu']))
    Q['ffn2_w_gu'] = q[0]

    mid = ['w_out', 'w_a_proj', 'w_b_proj']
    G['w_out'] = _mm_tn(mixed, dr2, name="dw_out", tm_pref=2048, tn_pref=1024)
    G['w_a_proj'] = _mm_tn(sg, dya, name="dw_a_proj", tm_pref=1024, tn_pref=2048)
    G['w_b_proj'] = _mm_tn(cv, dyb, name="dw_b_proj", tm_pref=1024, tn_pref=2048)
    G['w_in'], rv = _mm_tn(xb1, dproj, name="dw_in", tm_pref=2048, tn_pref=1024, carry=to_sibling(mid))
    pair_sum(mid, rv)
    both = _join(to_chips(mid), to_sibling(['w_in']))
    (dr1, d_ln1_g, d_ln1_b), moved = _inproj_bwd(dproj, full['w_in'], dr2, xh1, rstd1, g1, name="inproj_bwd",
                                                 carry=both)
    q, rv = both.split(moved)
    Q.update(zip(mid, q))
    pair_sum(['w_in'], rv)

    h1, dgate1, dup1, grad_x = _ffn_bwd(dr1, gate1, up1, full['ffn1_w_gu'], full['ffn1_w_down'], name="ffn1_bwd")
    G['ffn1_w_down'], q = _mm_tn(h1, dr1, name="dw_ffn1_down", tm_pref=1408, tn_pref=2048, scale=0.5,
                                 carry=to_chips(['w_in']))
    Q['w_in'] = q[0]
    gu, rv = _mm_tn(xb0, dgate1, name="dw_ffn1_gate", tm_pref=2048, tn_pref=1408, n_total=2 * F,
                    carry=to_sibling(['ffn1_w_down']))
    pair_sum(['ffn1_w_down'], rv)
    G['ffn1_w_gu'], q = _mm_tn(xb0, dup1, name="dw_ffn1_up", tm_pref=2048, tn_pref=1408, into=gu, col_off=F,
                               n_total=2 * F, carry=to_chips(['ffn1_w_down']))
    Q['ffn1_w_down'] = q[0]
    pair_sum(['ffn1_w_gu'], _comm_call(to_sibling(['ffn1_w_gu']), name="rs_to_sibling_last"))
    Q['ffn1_w_gu'] = _comm_call(to_chips(['ffn1_w_gu']), name="rs_to_chips_last")[0]

    grads, deltas, new_m, new_v = {}, {}, {}, {}
    for n in big_names:
        grads[n], deltas[n], new_m[n], new_v[n] = _adamw_sharded(Q[n], w[n], mom[n], var[n], name="adamw_" + n)

    B = conv_w_full.shape[1]
    small_g = {'ln1_g': d_ln1_g, 'ln1_b': d_ln1_b,
               'b_in': jnp.concatenate([dbin_sgu, dbin_conv, dbin_gate], axis=1),
               'sgu_ln_g': d_sgu_ln_g, 'sgu_ln_b': d_sgu_ln_b, 'sgu_w_s': d_w_s, 'sgu_b_s': d_b_s,
               'conv_b_dw': d_conv_b, 'conv_ln_g': d_conv_ln_g, 'conv_ln_b': d_conv_ln_b,
               'ln2_g': d_ln2_g, 'ln2_b': d_ln2_b, 'ln3_g': d_ln3_g, 'ln3_b': d_ln3_b}
    packed = _pack_rows([small_g[n] for n in SMALL] + [d_conv_w, loss_part])
    reduced = _all_reduce_small(packed, name="all_reduce_small")
    n_small_rows = sum(w[n].size for n in SMALL) // LANES
    conv_rows = CONV_WIDTH * B // LANES
    d_small, m_small, v_small = _adamw_plain(
        reduced[:n_small_rows], _pack_rows([w[n] for n in SMALL]), _pack_rows([mom[n] for n in SMALL]),
        _pack_rows([var[n] for n in SMALL]), name="adamw_small")
    off = 0
    for n in SMALL:
        rows = w[n].size // LANES
        grads[n] = reduced[off:off + rows].reshape(w[n].shape)
        deltas[n] = d_small[off:off + rows].reshape(w[n].shape)
        new_m[n] = m_small[off:off + rows].reshape(w[n].shape)
        new_v[n] = v_small[off:off + rows].reshape(w[n].shape)
        off += rows
    conv_g_full = reduced[off:off + conv_rows].reshape(CONV_WIDTH, B)
    bs = w['conv_w_dw'].shape[1]
    my_block = 4 * lax.axis_index("x") + 2 * lax.axis_index("y") + lax.axis_index("c")
    grads['conv_w_dw'] = lax.dynamic_slice(conv_g_full, (0, my_block * bs), (CONV_WIDTH, bs))
    deltas['conv_w_dw'], new_m['conv_w_dw'], new_v['conv_w_dw'] = _adamw_plain(
        grads['conv_w_dw'], w['conv_w_dw'], mom['conv_w_dw'], var['conv_w_dw'], name="adamw_conv_w")
    loss = reduced[off + conv_rows, 0]

    def lead(a):
        return a[None]

    return (loss, grad_x[None], *[lead(grads[n]) for n in WEIGHTS], *[lead(deltas[n]) for n in WEIGHTS],
            *[lead(new_m[n]) for n in WEIGHTS], *[lead(new_v[n]) for n in WEIGHTS])
```

```python
import functools
import math

import jax
import jax.numpy as jnp
from jax import lax
from jax.experimental import pallas as pl
from jax.experimental.pallas import tpu as pltpu

F32 = jnp.float32
BF16 = jnp.bfloat16

ALPHA = 2.0 ** 0.25
LN_EPS = 1e-5
CONV_WIDTH = 31
CONV_HALO = 32
CONV_ROWS = 64
CONV_WPAD = 32
CHUNK = 64
GMLP_BLOCK = 128
A_GROUPS = 8
N_DEV = 8
LANES = 128

ADAM_LR = 0.001
ADAM_B1 = 0.9
ADAM_B2 = 0.999
ADAM_EPS = 1e-08
ADAM_WD = 0.01
ADAM_STEP = 10
ADAM_C1 = 1.0 - ADAM_B1 ** ADAM_STEP
ADAM_C2 = 1.0 - ADAM_B2 ** ADAM_STEP

VMEM_LIMIT_BYTES = 60 * 2 ** 20
MESH = pl.DeviceIdType.MESH
ANY = pl.BlockSpec(memory_space=pl.ANY)

WEIGHTS = ['ffn1_w_gu', 'ffn1_w_down', 'ln1_g', 'ln1_b', 'w_in', 'b_in', 'sgu_ln_g', 'sgu_ln_b', 'sgu_w_s',
           'sgu_b_s', 'w_a_proj', 'conv_w_dw', 'conv_b_dw', 'conv_ln_g', 'conv_ln_b', 'w_b_proj', 'w_out',
           'ln2_g', 'ln2_b', 'ffn2_w_gu', 'ffn2_w_down', 'ln3_g', 'ln3_b']
BIG = {'ffn1_w_gu': 1, 'ffn1_w_down': 0, 'w_in': 1, 'w_a_proj': 1, 'w_b_proj': 1, 'w_out': 0,
       'ffn2_w_gu': 1, 'ffn2_w_down': 0}
SMALL = [n for n in WEIGHTS if n not in BIG and n != 'conv_w_dw']


def _tile(n, pref, mult=8):
    best = None
    for d in range(mult, min(n, pref) + 1, mult):
        if n % d == 0:
            best = d
    return n if best is None else best


def _params(*sem):
    return pltpu.CompilerParams(dimension_semantics=sem, vmem_limit_bytes=VMEM_LIMIT_BYTES)


def _dot(a, b):
    return jnp.dot(a, b, preferred_element_type=F32)


def _dot_nt(a, b):
    return lax.dot_general(a, b, (((1,), (1,)), ((), ())), preferred_element_type=F32)


def _dot_tn(a, b):
    return lax.dot_general(a, b, (((0,), (0,)), ((), ())), preferred_element_type=F32)


def _sig(x):
    return 1.0 / (1.0 + jnp.exp(-x))


_GELU_K = math.sqrt(2.0 / math.pi)
_GELU_C = 0.044715


def _gelu(x):
    t = jnp.tanh(_GELU_K * (x + _GELU_C * x * x * x))
    return 0.5 * x * (1.0 + t)


def _gelu_grad(x):
    x2 = x * x
    t = jnp.tanh(_GELU_K * (x + _GELU_C * x2 * x))
    return 0.5 * (1.0 + t) + 0.5 * x * (1.0 - t * t) * (_GELU_K * (1.0 + 3.0 * _GELU_C * x2))


def _ln_stats(r):
    mu = jnp.mean(r, axis=-1, keepdims=True)
    rc = r - mu
    var = jnp.mean(rc * rc, axis=-1, keepdims=True)
    rstd = lax.rsqrt(var + LN_EPS)
    return rc * rstd, rstd


def _ln_bwd(dy, xh, rstd, g):
    dxh = dy * g
    m1 = jnp.mean(dxh, axis=-1, keepdims=True)
    m2 = jnp.mean(dxh * xh, axis=-1, keepdims=True)
    return rstd * (dxh - m1 - xh * m2)


def _colsum(v):
    return jnp.sum(v, axis=0, keepdims=True)


def _chunk_mask(transposed):
    shift = CHUNK.bit_length() - 1
    r = lax.broadcasted_iota(jnp.int32, (GMLP_BLOCK, GMLP_BLOCK), 0) >> shift
    c = lax.broadcasted_iota(jnp.int32, (GMLP_BLOCK, GMLP_BLOCK), 1) >> shift
    return (r <= c) if transposed else (c <= r)


class _Comm:
    def __init__(self, inputs, out_shape, scratch, run):
        self.inputs, self.out_shape, self.scratch, self.run = list(inputs), list(out_shape), list(scratch), run
        self.parts = [len(self.out_shape)]

    def split(self, outs):
        res, o = [], 0
        for n in self.parts:
            res.append(list(outs[o:o + n]))
            o += n
        return res


def _join(*comms):
    comms = [c for c in comms if c is not None]
    if not comms:
        return None

    def run(ins, outs, sems, phase):
        i = o = s = 0
        for c in comms:
            c.run(ins[i:i + len(c.inputs)], outs[o:o + len(c.out_shape)], sems[s:s + len(c.scratch)], phase)
            i, o, s = i + len(c.inputs), o + len(c.out_shape), s + len(c.scratch)

    joined = _Comm(sum((c.inputs for c in comms), []), sum((c.out_shape for c in comms), []),
                   sum((c.scratch for c in comms), []), run)
    joined.parts = [len(c.out_shape) for c in comms]
    return joined


def _call(body, *, name, grid, in_specs, out_specs, out_shape, args, sem, scratch_shapes=(), aliases=None, carry=None):
    in_specs, out_specs, out_shape = list(in_specs), list(out_specs), list(out_shape)
    scratch_shapes = list(scratch_shapes)
    if carry is None:
        return pl.pallas_call(body, name=name, grid=grid, in_specs=in_specs, out_specs=out_specs,
                              out_shape=out_shape, scratch_shapes=scratch_shapes,
                              input_output_aliases=aliases or {}, compiler_params=_params(*sem))(*args)
    n_in, n_out, n_scr = len(args), len(out_shape), len(scratch_shapes)
    c_in, c_out = len(carry.inputs), len(carry.out_shape)

    def wrapped(*refs):
        ins, c_ins = refs[:n_in], refs[n_in:n_in + c_in]
        o0 = n_in + c_in
        outs, c_outs = refs[o0:o0 + n_out], refs[o0 + n_out:o0 + n_out + c_out]
        s0 = o0 + n_out + c_out
        scr, c_sems = refs[s0:s0 + n_scr], refs[s0 + n_scr:]
        ids = [pl.program_id(a) for a in range(len(grid))]
        first = functools.reduce(jnp.logical_and, [i == 0 for i in ids])
        last = functools.reduce(jnp.logical_and, [i == g - 1 for i, g in zip(ids, grid)])

        @pl.when(first)
        def _():
            carry.run(c_ins, c_outs, c_sems, "start")

        body(*ins, *outs, *scr)

        @pl.when(last)
        def _():
            carry.run(c_ins, c_outs, c_sems, "finish")

    res = pl.pallas_call(
        wrapped, name=name, grid=grid, in_specs=in_specs + [ANY] * c_in, out_specs=out_specs + [ANY] * c_out,
        out_shape=out_shape + carry.out_shape, scratch_shapes=scratch_shapes + carry.scratch,
        input_output_aliases=aliases or {},
        compiler_params=pltpu.CompilerParams(dimension_semantics=("arbitrary",) * len(grid),
                                             vmem_limit_bytes=VMEM_LIMIT_BYTES, has_side_effects=True),
    )(*args, *carry.inputs)
    return list(res[:n_out]), list(res[n_out:])


def _comm_call(comm, *, name):
    n_in, n_out = len(comm.inputs), len(comm.out_shape)

    def body(*refs):
        ins, outs, sems = refs[:n_in], refs[n_in:n_in + n_out], refs[n_in + n_out:]
        comm.run(ins, outs, sems, "start")
        comm.run(ins, outs, sems, "finish")

    return list(pl.pallas_call(
        body, name=name, in_specs=[ANY] * n_in, out_specs=[ANY] * n_out, out_shape=comm.out_shape,
        scratch_shapes=comm.scratch, compiler_params=pltpu.CompilerParams(has_side_effects=True))(*comm.inputs))


def _when(cond, fn):
    if isinstance(cond, bool):
        if cond:
            fn()
    else:
        pl.when(cond)(fn)


def _ffn_tiles(T, F):
    return _tile(T, 512), _tile(F, 512, LANES)


def _row_chunks(tm, rows=128):
    rows = _tile(tm, rows)
    return [slice(r, r + rows) for r in range(0, tm, rows)]


def _hidden_loop(nj, step):
    def pair(jj, c):
        step(2 * jj, 0)
        step(2 * jj + 1, 1)
        return c
    if nj // 2:
        lax.fori_loop(0, nj // 2, pair, 0)
    if nj % 2:
        step(nj - 1, 0)


def _ffn_fwd(xh, lg, lb, wgu, wd, *, affine, name, final=None, carry=None):
    T, D = xh.shape
    F = wd.shape[0]
    tm, tn = _ffn_tiles(T, F)
    nj, nt = F // tn, T // tm
    is_final = final is not None

    def body(*refs):
        if is_final:
            (xh_ref, lg_ref, lb_ref, wgu_hbm, wd_hbm, ng_ref, nb_ref, tgt_hbm,
             gate_hbm, up_hbm, xb_ref, dr_hbm, loss_ref, dng_ref, dnb_ref,
             acc_sc, wg_buf, wu_buf, wd_buf, g_buf, u_buf, w_sem, o_sem, tgt_sc, dr_sc, t_sem) = refs
        else:
            (xh_ref, lg_ref, lb_ref, wgu_hbm, wd_hbm,
             gate_hbm, up_hbm, xb_ref, xho_ref, rstd_ref,
             acc_sc, wg_buf, wu_buf, wd_buf, g_buf, u_buf, w_sem, o_sem) = refs
        i = pl.program_id(0)
        rows = pl.ds(pl.multiple_of(i * tm, tm), tm)

        def cols(j, base=0):
            return pl.ds(pl.multiple_of(base + j * tn, LANES), tn)

        def w_copies(j, slot):
            return (pltpu.make_async_copy(wgu_hbm.at[:, cols(j)], wg_buf.at[slot], w_sem.at[slot]),
                    pltpu.make_async_copy(wgu_hbm.at[:, cols(j, F)], wu_buf.at[slot], w_sem.at[2 + slot]),
                    pltpu.make_async_copy(wd_hbm.at[cols(j), :], wd_buf.at[slot], w_sem.at[4 + slot]))

        def o_copies(j, slot):
            return (pltpu.make_async_copy(g_buf.at[slot], gate_hbm.at[rows, cols(j)], o_sem.at[slot]),
                    pltpu.make_async_copy(u_buf.at[slot], up_hbm.at[rows, cols(j)], o_sem.at[2 + slot]))

        def start(copies):
            for cp in copies:
                cp.start()

        def wait(copies):
            for cp in copies:
                cp.wait()

        def xin(rs):
            v = xh_ref[rs, :]
            return v * lg_ref[...] + lb_ref[...] if affine else v

        _when(i == 0, lambda: start(w_copies(0, 0)))
        if is_final:
            tgt_in = pltpu.make_async_copy(tgt_hbm.at[rows, :], tgt_sc, t_sem.at[0])
            dr_out = pltpu.make_async_copy(dr_sc, dr_hbm.at[rows, :], t_sem.at[1])
            tgt_in.start()
        for rs in _row_chunks(tm):
            xb_ref[rs, :] = xin(rs).astype(BF16)
        acc_sc[...] = jnp.zeros_like(acc_sc)

        def step(j, slot):
            _when(j + 1 < nj, lambda: start(w_copies(j + 1, 1 - slot)))
            wait(w_copies(j, slot))
            xb = xb_ref[...]
            g = _dot(xb, wg_buf[slot])
            u = _dot(xb, wu_buf[slot])
            _when(j >= 2, lambda: wait(o_copies(j - 2, slot)))
            g_buf[slot] = g.astype(BF16)
            u_buf[slot] = u.astype(BF16)
            start(o_copies(j, slot))
            h = g * _sig(g) * u
            acc_sc[...] += _dot(h.astype(BF16), wd_buf[slot])

        _hidden_loop(nj, step)
        _when(i + 1 < nt, lambda: start(w_copies(0, 0)))
        for j in range(max(nj - 2, 0), nj):
            wait(o_copies(j, j % 2))

        if is_final:
            @pl.when(i == 0)
            def _():
                loss_ref[...] = jnp.zeros_like(loss_ref)
                dng_ref[...] = jnp.zeros_like(dng_ref)
                dnb_ref[...] = jnp.zeros_like(dnb_ref)
            tgt_in.wait()
            _when(i > 0, dr_out.wait)
        for rs in _row_chunks(tm):
            r = ALPHA * xin(rs) + 0.5 * acc_sc[rs, :]
            xho, rstd = _ln_stats(r)
            if not is_final:
                xho_ref[rs, :] = xho
                rstd_ref[rs, :] = jnp.broadcast_to(rstd, (rs.stop - rs.start, LANES))
            else:
                ng = ng_ref[...]
                e = xho * ng + nb_ref[...] - tgt_sc[rs, :]
                part = _colsum(jnp.sum(e * e, axis=1, keepdims=True)) * (0.5 / D)
                loss_ref[...] += jnp.broadcast_to(part, loss_ref.shape)
                dy = e * (1.0 / D)
                dng_ref[...] += _colsum(dy * xho)
                dnb_ref[...] += _colsum(dy)
                dr_sc[rs, :] = _ln_bwd(dy, xho, rstd, ng)
        if is_final:
            dr_out.start()
            _when(i == nt - 1, dr_out.wait)

    row = pl.BlockSpec((tm, D), lambda i: (i, 0))
    vec = pl.BlockSpec((1, D), lambda i: (0, 0))
    in_specs = [row, vec, vec, ANY, ANY]
    args = [xh, lg, lb, wgu, wd]
    out_shape = [jax.ShapeDtypeStruct((T, F), BF16), jax.ShapeDtypeStruct((T, F), BF16),
                 jax.ShapeDtypeStruct((T, D), BF16)]
    out_specs = [ANY, ANY, row]
    scratch = [pltpu.VMEM((tm, D), F32),
               pltpu.VMEM((2, D, tn), BF16), pltpu.VMEM((2, D, tn), BF16), pltpu.VMEM((2, tn, D), BF16),
               pltpu.VMEM((2, tm, tn), BF16), pltpu.VMEM((2, tm, tn), BF16),
               pltpu.SemaphoreType.DMA((6,)), pltpu.SemaphoreType.DMA((4,))]
    if is_final:
        in_specs += [vec, vec, ANY]
        args += list(final)
        out_shape += [jax.ShapeDtypeStruct((T, D), F32), jax.ShapeDtypeStruct((8, LANES), F32),
                      jax.ShapeDtypeStruct((1, D), F32), jax.ShapeDtypeStruct((1, D), F32)]
        out_specs += [ANY, pl.BlockSpec((8, LANES), lambda i: (0, 0)), vec, vec]
        scratch += [pltpu.VMEM((tm, D), F32), pltpu.VMEM((tm, D), F32), pltpu.SemaphoreType.DMA((2,))]
    else:
        out_shape += [jax.ShapeDtypeStruct((T, D), F32), jax.ShapeDtypeStruct((T, LANES), F32)]
        out_specs += [row, pl.BlockSpec((tm, LANES), lambda i: (i, 0))]
    return _call(body, name=name, grid=(nt,), in_specs=in_specs, out_specs=out_specs, out_shape=out_shape,
                 scratch_shapes=scratch, sem=("arbitrary",), args=args, carry=carry)


def _ffn_bwd(dr, gate, up, wgu, wd, *, name, prev=None, carry=None):
    T, D = dr.shape
    F = wd.shape[0]
    tm, tn = _ffn_tiles(T, F)
    nj, nt = F // tn, T // tm
    has_prev = prev is not None

    def body(*refs):
        if has_prev:
            (dr_ref, gate_hbm, up_hbm, wgu_hbm, wd_hbm, xh_hbm, rstd_ref, lg_ref,
             h_hbm, dg_hbm, du_hbm, dprev_hbm, dlg_ref, dlb_ref, *scr) = refs
            xh_sc = scr.pop()
        else:
            (dr_ref, gate_hbm, up_hbm, wgu_hbm, wd_hbm,
             h_hbm, dg_hbm, du_hbm, dprev_hbm, *scr) = refs
        (df_sc, dx_sc, wg_buf, wu_buf, wd_buf, gi_buf, ui_buf, h_buf, dg_buf, du_buf, i_sem, o_sem,
         dp_sc, t_sem) = scr
        i = pl.program_id(0)
        rows = pl.ds(pl.multiple_of(i * tm, tm), tm)
        dp_out = pltpu.make_async_copy(dp_sc, dprev_hbm.at[rows, :], t_sem.at[0])
        if has_prev:
            xh_in = pltpu.make_async_copy(xh_hbm.at[rows, :], xh_sc, t_sem.at[1])
            xh_in.start()

        def cols(j, base=0):
            return pl.ds(pl.multiple_of(base + j * tn, LANES), tn)

        def i_copies(j, slot, tile=None):
            at = rows if tile is None else pl.ds(pl.multiple_of(tile * tm, tm), tm)
            return (pltpu.make_async_copy(wgu_hbm.at[:, cols(j)], wg_buf.at[slot], i_sem.at[slot]),
                    pltpu.make_async_copy(wgu_hbm.at[:, cols(j, F)], wu_buf.at[slot], i_sem.at[2 + slot]),
                    pltpu.make_async_copy(wd_hbm.at[cols(j), :], wd_buf.at[slot], i_sem.at[4 + slot]),
                    pltpu.make_async_copy(gate_hbm.at[at, cols(j)], gi_buf.at[slot], i_sem.at[6 + slot]),
                    pltpu.make_async_copy(up_hbm.at[at, cols(j)], ui_buf.at[slot], i_sem.at[8 + slot]))

        def o_copies(j, slot):
            return (pltpu.make_async_copy(h_buf.at[slot], h_hbm.at[rows, cols(j)], o_sem.at[slot]),
                    pltpu.make_async_copy(dg_buf.at[slot], dg_hbm.at[rows, cols(j)], o_sem.at[2 + slot]),
                    pltpu.make_async_copy(du_buf.at[slot], du_hbm.at[rows, cols(j)], o_sem.at[4 + slot]))

        def start(copies):
            for cp in copies:
                cp.start()

        def wait(copies):
            for cp in copies:
                cp.wait()

        _when(i == 0, lambda: start(i_copies(0, 0)))
        for rs in _row_chunks(tm):
            d = dr_ref[rs, :]
            df_sc[rs, :] = (0.5 * d).astype(BF16)
            dx_sc[rs, :] = ALPHA * d

        def step(j, slot):
            _when(j + 1 < nj, lambda: start(i_copies(j + 1, 1 - slot)))
            wait(i_copies(j, slot))
            g = gi_buf[slot].astype(F32)
            u = ui_buf[slot].astype(F32)
            dh = _dot_nt(df_sc[...], wd_buf[slot])
            s = _sig(g)
            sil = g * s
            dg = (dh * u * (s * (1.0 + g * (1.0 - s)))).astype(BF16)
            du = (dh * sil).astype(BF16)
            _when(j >= 2, lambda: wait(o_copies(j - 2, slot)))
            h_buf[slot] = (sil * u).astype(BF16)
            dg_buf[slot] = dg
            du_buf[slot] = du
            start(o_copies(j, slot))
            dx_sc[...] += _dot_nt(dg, wg_buf[slot])
            dx_sc[...] += _dot_nt(du, wu_buf[slot])

        _hidden_loop(nj, step)
        _when(i + 1 < nt, lambda: start(i_copies(0, 0, i + 1)))
        for j in range(max(nj - 2, 0), nj):
            wait(o_copies(j, j % 2))

        if has_prev:
            @pl.when(i == 0)
            def _():
                dlg_ref[...] = jnp.zeros_like(dlg_ref)
                dlb_ref[...] = jnp.zeros_like(dlb_ref)
            xh_in.wait()
        _when(i > 0, dp_out.wait)
        for rs in _row_chunks(tm):
            dxin = dx_sc[rs, :]
            if not has_prev:
                dp_sc[rs, :] = dxin
            else:
                x_hat = xh_sc[rs, :]
                dlg_ref[...] += _colsum(dxin * x_hat)
                dlb_ref[...] += _colsum(dxin)
                dp_sc[rs, :] = _ln_bwd(dxin, x_hat, rstd_ref[rs, 0:1], lg_ref[...])
        dp_out.start()
        _when(i == nt - 1, dp_out.wait)

    row = pl.BlockSpec((tm, D), lambda i: (i, 0))
    vec = pl.BlockSpec((1, D), lambda i: (0, 0))
    in_specs = [row, ANY, ANY, ANY, ANY]
    args = [dr, gate, up, wgu, wd]
    out_shape = [jax.ShapeDtypeStruct((T, F), BF16)] * 3 + [jax.ShapeDtypeStruct((T, D), F32)]
    out_specs = [ANY, ANY, ANY, ANY]
    scratch = [pltpu.VMEM((tm, D), BF16), pltpu.VMEM((tm, D), F32),
               pltpu.VMEM((2, D, tn), BF16), pltpu.VMEM((2, D, tn), BF16), pltpu.VMEM((2, tn, D), BF16)]
    scratch += [pltpu.VMEM((2, tm, tn), BF16)] * 5
    scratch += [pltpu.SemaphoreType.DMA((10,)), pltpu.SemaphoreType.DMA((6,)),
                pltpu.VMEM((tm, D), F32), pltpu.SemaphoreType.DMA((2,))]
    if has_prev:
        in_specs += [ANY, pl.BlockSpec((tm, LANES), lambda i: (i, 0)), vec]
        args += list(prev)
        out_shape += [jax.ShapeDtypeStruct((1, D), F32)] * 2
        out_specs += [vec, vec]
        scratch += [pltpu.VMEM((tm, D), F32)]
    return _call(body, name=name, grid=(nt,), in_specs=in_specs, out_specs=out_specs, out_shape=out_shape,
                 scratch_shapes=scratch, sem=("arbitrary",), args=args, carry=carry)


def _ffn_fwd_gridded(xh, lg, lb, wgu, wd, *, affine, name, final=None, carry=None):
    T, D = xh.shape
    F = wd.shape[0]
    tm = _tile(T, 512)
    tn = _tile(F, 512, LANES)
    nj = F // tn
    is_final = final is not None

    def body(*refs):
        if is_final:
            (xh_ref, lg_ref, lb_ref, wg_ref, wu_ref, wd_ref, ng_ref, nb_ref, tgt_ref,
             gate_ref, up_ref, xb_sc, dr_ref, loss_ref, dng_ref, dnb_ref, acc_sc) = refs
        else:
            (xh_ref, lg_ref, lb_ref, wg_ref, wu_ref, wd_ref,
             gate_ref, up_ref, xb_sc, xho_ref, rstd_ref, acc_sc) = refs
        i = pl.program_id(0)
        j = pl.program_id(1)

        def xin():
            v = xh_ref[...]
            return v * lg_ref[...] + lb_ref[...] if affine else v

        @pl.when(j == 0)
        def _():
            xb_sc[...] = xin().astype(BF16)
            acc_sc[...] = jnp.zeros_like(acc_sc)

        xb = xb_sc[...]
        g = _dot(xb, wg_ref[...])
        u = _dot(xb, wu_ref[...])
        gate_ref[...] = g.astype(BF16)
        up_ref[...] = u.astype(BF16)
        h = g * _sig(g) * u
        acc_sc[...] += _dot(h.astype(BF16), wd_ref[...])

        @pl.when(j == nj - 1)
        def _():
            r = ALPHA * xin() + 0.5 * acc_sc[...]
            xho, rstd = _ln_stats(r)
            if not is_final:
                xho_ref[...] = xho
                rstd_ref[...] = jnp.broadcast_to(rstd, (tm, LANES))
            else:
                @pl.when(i == 0)
                def _():
                    loss_ref[...] = jnp.zeros_like(loss_ref)
                    dng_ref[...] = jnp.zeros_like(dng_ref)
                    dnb_ref[...] = jnp.zeros_like(dnb_ref)
                ng = ng_ref[...]
                e = xho * ng + nb_ref[...] - tgt_ref[...]
                part = _colsum(jnp.sum(e * e, axis=1, keepdims=True)) * (0.5 / D)
                loss_ref[...] += jnp.broadcast_to(part, loss_ref.shape)
                dy = e * (1.0 / D)
                dng_ref[...] += _colsum(dy * xho)
                dnb_ref[...] += _colsum(dy)
                dr_ref[...] = _ln_bwd(dy, xho, rstd, ng)

    row = pl.BlockSpec((tm, D), lambda i, j: (i, 0))
    vec = pl.BlockSpec((1, D), lambda i, j: (0, 0))
    hid = pl.BlockSpec((tm, tn), lambda i, j: (i, j))
    in_specs = [row, vec, vec,
                pl.BlockSpec((D, tn), lambda i, j: (0, j)),
                pl.BlockSpec((D, tn), lambda i, j: (0, j + nj)),
                pl.BlockSpec((tn, D), lambda i, j: (j, 0))]
    args = [xh, lg, lb, wgu, wgu, wd]
    out_shape = [jax.ShapeDtypeStruct((T, F), BF16), jax.ShapeDtypeStruct((T, F), BF16),
                 jax.ShapeDtypeStruct((T, D), BF16)]
    out_specs = [hid, hid, row]
    if is_final:
        in_specs += [vec, vec, row]
        args += list(final)
        out_shape += [jax.ShapeDtypeStruct((T, D), F32), jax.ShapeDtypeStruct((8, LANES), F32),
                      jax.ShapeDtypeStruct((1, D), F32), jax.ShapeDtypeStruct((1, D), F32)]
        out_specs += [row, pl.BlockSpec((8, LANES), lambda i, j: (0, 0)), vec, vec]
        sem = ("arbitrary", "arbitrary")
    else:
        out_shape += [jax.ShapeDtypeStruct((T, D), F32), jax.ShapeDtypeStruct((T, LANES), F32)]
        out_specs += [row, pl.BlockSpec((tm, LANES), lambda i, j: (i, 0))]
        sem = ("parallel", "arbitrary")
    return _call(body, name=name, grid=(T // tm, nj), in_specs=in_specs, out_specs=out_specs, out_shape=out_shape,
                 scratch_shapes=[pltpu.VMEM((tm, D), F32)], sem=sem, args=args, carry=carry)


def _ffn_bwd_gridded(dr, gate, up, wgu, wd, *, name, prev=None, carry=None):
    T, D = dr.shape
    F = wd.shape[0]
    tm = _tile(T, 512)
    tn = _tile(F, 512, LANES)
    nj = F // tn
    has_prev = prev is not None

    def body(*refs):
        if has_prev:
            (dr_ref, gate_ref, up_ref, wd_ref, wg_ref, wu_ref, xh_ref, rstd_ref, lg_ref,
             h_ref, dg_ref, du_ref, dprev_ref, dlg_ref, dlb_ref, df_sc, dx_sc) = refs
        else:
            (dr_ref, gate_ref, up_ref, wd_ref, wg_ref, wu_ref,
             h_ref, dg_ref, du_ref, dprev_ref, df_sc, dx_sc) = refs
        i = pl.program_id(0)
        j = pl.program_id(1)

        @pl.when(j == 0)
        def _():
            d = dr_ref[...]
            df_sc[...] = (0.5 * d).astype(BF16)
            dx_sc[...] = ALPHA * d

        g = gate_ref[...].astype(F32)
        u = up_ref[...].astype(F32)
        dh = _dot_nt(df_sc[...], wd_ref[...])
        s = _sig(g)
        sil = g * s
        h_ref[...] = (sil * u).astype(BF16)
        dg = (dh * u * (s * (1.0 + g * (1.0 - s)))).astype(BF16)
        du = (dh * sil).astype(BF16)
        dg_ref[...] = dg
        du_ref[...] = du
        dx_sc[...] += _dot_nt(dg, wg_ref[...]) + _dot_nt(du, wu_ref[...])

        @pl.when(j == nj - 1)
        def _():
            dxin = dx_sc[...]
            if not has_prev:
                dprev_ref[...] = dxin
            else:
                @pl.when(i == 0)
                def _():
                    dlg_ref[...] = jnp.zeros_like(dlg_ref)
                    dlb_ref[...] = jnp.zeros_like(dlb_ref)
                xh = xh_ref[...]
                dlg_ref[...] += _colsum(dxin * xh)
                dlb_ref[...] += _colsum(dxin)
                dprev_ref[...] = _ln_bwd(dxin, xh, rstd_ref[:, 0:1], lg_ref[...])

    row = pl.BlockSpec((tm, D), lambda i, j: (i, 0))
    vec = pl.BlockSpec((1, D), lambda i, j: (0, 0))
    hid = pl.BlockSpec((tm, tn), lambda i, j: (i, j))
    in_specs = [row, hid, hid,
                pl.BlockSpec((tn, D), lambda i, j: (j, 0)),
                pl.BlockSpec((D, tn), lambda i, j: (0, j)),
                pl.BlockSpec((D, tn), lambda i, j: (0, j + nj))]
    args = [dr, gate, up, wd, wgu, wgu]
    out_shape = [jax.ShapeDtypeStruct((T, F), BF16)] * 3 + [jax.ShapeDtypeStruct((T, D), F32)]
    out_specs = [hid, hid, hid, row]
    if has_prev:
        in_specs += [row, pl.BlockSpec((tm, LANES), lambda i, j: (i, 0)), vec]
        args += list(prev)
        out_shape += [jax.ShapeDtypeStruct((1, D), F32)] * 2
        out_specs += [vec, vec]
        sem = ("arbitrary", "arbitrary")
    else:
        sem = ("parallel", "arbitrary")
    return _call(body, name=name, grid=(T // tm, nj), in_specs=in_specs, out_specs=out_specs, out_shape=out_shape,
                 scratch_shapes=[pltpu.VMEM((tm, D), BF16), pltpu.VMEM((tm, D), F32)], sem=sem, args=args,
                 carry=carry)


def _mm_tn(a, b, *, name, tm_pref, tn_pref, scale=1.0, a_affine=None, into=None, col_off=0, n_total=None,
           carry=None):
    T, M = a.shape
    N = b.shape[1]
    n_total = N if n_total is None else n_total
    tM = _tile(M, tm_pref, LANES)
    tN = _tile(N, tn_pref, LANES)
    tk = _tile(T, 1024)
    nt = T // tk
    assert col_off % tN == 0
    off_blocks = col_off // tN
    has_aff = a_affine is not None
    has_into = into is not None

    def body(*refs):
        refs = list(refs)
        a_ref = refs.pop(0)
        if has_aff:
            lg_ref = refs.pop(0)
            lb_ref = refs.pop(0)
        b_ref = refs.pop(0)
        if has_into:
            refs.pop(0)
        o_ref, acc_sc = refs
        t = pl.program_id(2)

        @pl.when(t == 0)
        def _():
            acc_sc[...] = jnp.zeros_like(acc_sc)

        av = a_ref[...]
        if has_aff:
            av = av * lg_ref[...] + lb_ref[...]
        acc_sc[...] += _dot_tn(av.astype(BF16), b_ref[...].astype(BF16))

        @pl.when(t == nt - 1)
        def _():
            o_ref[...] = (acc_sc[...] * scale).astype(BF16)

    in_specs = [pl.BlockSpec((tk, tM), lambda m, n, t: (t, m))]
    args = [a]
    if has_aff:
        in_specs += [pl.BlockSpec((1, tM), lambda m, n, t: (0, m))] * 2
        args += list(a_affine)
    in_specs.append(pl.BlockSpec((tk, tN), lambda m, n, t: (t, n)))
    args.append(b)
    aliases = {}
    if has_into:
        aliases = {len(args): 0}
        in_specs.append(ANY)
        args.append(into)
    res = _call(body, name=name, grid=(M // tM, N // tN, nt), in_specs=in_specs,
                out_specs=[pl.BlockSpec((tM, tN), lambda m, n, t: (m, n + off_blocks))],
                out_shape=[jax.ShapeDtypeStruct((M, n_total), BF16)],
                scratch_shapes=[pltpu.VMEM((tM, tN), F32)], aliases=aliases,
                sem=("parallel", "parallel", "arbitrary"), args=args, carry=carry)
    return res[0] if carry is None else (res[0][0], res[1])


def _inproj_fwd(xh, lg, lb, w, bias, *, name):
    T, D = xh.shape
    N = w.shape[1]
    tm = _tile(T, 1024)
    tn = _tile(N, 1024, LANES)

    def body(xh_ref, lg_ref, lb_ref, w_ref, b_ref, o_ref, xb_ref):
        @pl.when(pl.program_id(1) == 0)
        def _():
            xb_ref[...] = (xh_ref[...] * lg_ref[...] + lb_ref[...]).astype(BF16)
        o_ref[...] = _dot(xb_ref[...], w_ref[...]) + b_ref[...]

    return pl.pallas_call(
        body, name=name, grid=(T // tm, N // tn),
        in_specs=[pl.BlockSpec((tm, D), lambda i, j: (i, 0)),
                  pl.BlockSpec((1, D), lambda i, j: (0, 0)), pl.BlockSpec((1, D), lambda i, j: (0, 0)),
                  pl.BlockSpec((D, tn), lambda i, j: (0, j)), pl.BlockSpec((1, tn), lambda i, j: (0, j))],
        out_specs=[pl.BlockSpec((tm, tn), lambda i, j: (i, j)), pl.BlockSpec((tm, D), lambda i, j: (i, 0))],
        out_shape=[jax.ShapeDtypeStruct((T, N), F32), jax.ShapeDtypeStruct((T, D), BF16)],
        compiler_params=_params("parallel", "arbitrary"))(xh, lg, lb, w, bias)


def _inproj_bwd(dproj, w, dr_next, xh, rstd, lg, *, name, carry=None):
    T, N = dproj.shape
    D = w.shape[0]
    tm = _tile(T, 512)
    tn = _tile(N, 1024, LANES)
    nj = N // tn

    def body(dp_ref, w_ref, drn_ref, xh_ref, rstd_ref, lg_ref, dprev_ref, dlg_ref, dlb_ref, dx_sc):
        i = pl.program_id(0)
        j = pl.program_id(1)

        @pl.when(j == 0)
        def _():
            dx_sc[...] = ALPHA * drn_ref[...]

        dx_sc[...] += _dot_nt(dp_ref[...], w_ref[...])

        @pl.when(j == nj - 1)
        def _():
            @pl.when(i == 0)
            def _():
                dlg_ref[...] = jnp.zeros_like(dlg_ref)
                dlb_ref[...] = jnp.zeros_like(dlb_ref)
            dx = dx_sc[...]
            x_hat = xh_ref[...]
            dlg_ref[...] += _colsum(dx * x_hat)
            dlb_ref[...] += _colsum(dx)
            dprev_ref[...] = _ln_bwd(dx, x_hat, rstd_ref[:, 0:1], lg_ref[...])

    row = pl.BlockSpec((tm, D), lambda i, j: (i, 0))
    vec = pl.BlockSpec((1, D), lambda i, j: (0, 0))
    return _call(
        body, name=name, grid=(T // tm, nj),
        in_specs=[pl.BlockSpec((tm, tn), lambda i, j: (i, j)), pl.BlockSpec((D, tn), lambda i, j: (0, j)),
                  row, row, pl.BlockSpec((tm, LANES), lambda i, j: (i, 0)), vec],
        out_specs=[row, vec, vec],
        out_shape=[jax.ShapeDtypeStruct((T, D), F32), jax.ShapeDtypeStruct((1, D), F32),
                   jax.ShapeDtypeStruct((1, D), F32)],
        scratch_shapes=[pltpu.VMEM((tm, D), F32)], sem=("arbitrary", "arbitrary"),
        args=[dproj, w, dr_next, xh, rstd, lg], carry=carry)


def _sgu_fwd(proj, ln_g, ln_b, w_s, b_sb, *, name):
    T = proj.shape[0]
    A = proj.shape[1] // 8
    hd = A // A_GROUPS
    tm = _tile(T, 256, GMLP_BLOCK)

    def body(u_ref, v_ref, g_ref, b_ref, ws_ref, bs_ref, o_ref):
        gu = _gelu(u_ref[...])
        vh, _ = _ln_stats(_gelu(v_ref[...]))
        vn = (vh * g_ref[...] + b_ref[...]).astype(BF16)
        mask = _chunk_mask(False)
        for h in range(A_GROUPS):
            wm = jnp.where(mask, ws_ref[h], 0.0).astype(BF16)
            cols = slice(h * hd, (h + 1) * hd)
            for n in range(tm // GMLP_BLOCK):
                rows = slice(n * GMLP_BLOCK, (n + 1) * GMLP_BLOCK)
                s = _dot(wm, vn[rows, cols]) + bs_ref[h][:, :hd]
                o_ref[rows, cols] = (gu[rows, cols] * s).astype(BF16)

    vec = pl.BlockSpec((1, A), lambda i: (0, 0))
    full = pl.BlockSpec((A_GROUPS, GMLP_BLOCK, GMLP_BLOCK), lambda i: (0, 0, 0))
    return pl.pallas_call(
        body, name=name, grid=(T // tm,),
        in_specs=[pl.BlockSpec((tm, A), lambda i: (i, 0)), pl.BlockSpec((tm, A), lambda i: (i, 1)),
                  vec, vec, full, full],
        out_specs=pl.BlockSpec((tm, A), lambda i: (i, 0)),
        out_shape=jax.ShapeDtypeStruct((T, A), BF16),
        compiler_params=_params("parallel"))(proj, proj, ln_g, ln_b, w_s, b_sb)


def _sgu_bwd(proj, dsg, ln_g, ln_b, w_s, w_st, b_sb, dproj, *, name):
    T = proj.shape[0]
    A = proj.shape[1] // 8
    hd = A // A_GROUPS
    tm = _tile(T, 256, GMLP_BLOCK)
    nt = T // tm

    def body(u_ref, v_ref, dsg_ref, g_ref, b_ref, ws_ref, wst_ref, bs_ref, _alias,
             dp_ref, dbin_ref, dlg_ref, dlb_ref, dws_ref, dbs_ref, dvn_sc, dgu_sc, dbs_sc):
        i = pl.program_id(0)

        @pl.when(i == 0)
        def _():
            dbin_ref[...] = jnp.zeros_like(dbin_ref)
            dlg_ref[...] = jnp.zeros_like(dlg_ref)
            dlb_ref[...] = jnp.zeros_like(dlb_ref)
            dws_ref[...] = jnp.zeros_like(dws_ref)
            dbs_sc[...] = jnp.zeros_like(dbs_sc)

        u = u_ref[...]
        v = v_ref[...]
        gu = _gelu(u)
        vh, rstd = _ln_stats(_gelu(v))
        gain = g_ref[...]
        vn = (vh * gain + b_ref[...]).astype(BF16)
        dsg_v = dsg_ref[...]
        mask = _chunk_mask(False)
        mask_t = _chunk_mask(True)
        for h in range(A_GROUPS):
            wm = jnp.where(mask, ws_ref[h], 0.0).astype(BF16)
            wmt = jnp.where(mask_t, wst_ref[h], 0.0).astype(BF16)
            cols = slice(h * hd, (h + 1) * hd)
            for n in range(tm // GMLP_BLOCK):
                rows = slice(n * GMLP_BLOCK, (n + 1) * GMLP_BLOCK)
                vb = vn[rows, cols]
                s = _dot(wm, vb) + bs_ref[h][:, :hd]
                d_out = dsg_v[rows, cols]
                dgu_sc[rows, cols] = d_out * s
                ds = d_out * gu[rows, cols]
                ds_b = ds.astype(BF16)
                dws_ref[h] += _dot_nt(ds_b, vb)
                dbs_sc[h] += ds
                dvn_sc[rows, cols] = _dot(wmt, ds_b)
        dvn = dvn_sc[...]
        dlg_ref[...] += _colsum(dvn * vh)
        dlb_ref[...] += _colsum(dvn)
        dv = _ln_bwd(dvn, vh, rstd, gain) * _gelu_grad(v)
        du = dgu_sc[...] * _gelu_grad(u)
        dp_ref[:, 0:A] = du.astype(BF16)
        dp_ref[:, A:2 * A] = dv.astype(BF16)
        dbin_ref[:, 0:A] += _colsum(du)
        dbin_ref[:, A:2 * A] += _colsum(dv)

        @pl.when(i == nt - 1)
        def _():
            for h in range(A_GROUPS):
                dws_ref[h] = jnp.where(mask, dws_ref[h], 0.0)
                dbs_ref[h:h + 1, :] = _colsum(dbs_sc[h].T)

    vec = pl.BlockSpec((1, A), lambda i: (0, 0))
    full = pl.BlockSpec((A_GROUPS, GMLP_BLOCK, GMLP_BLOCK), lambda i: (0, 0, 0))
    tile = pl.BlockSpec((tm, A), lambda i: (i, 0))
    return pl.pallas_call(
        body, name=name, grid=(nt,),
        in_specs=[tile, pl.BlockSpec((tm, A), lambda i: (i, 1)), tile, vec, vec, full, full, full, ANY],
        out_specs=[pl.BlockSpec((tm, 2 * A), lambda i: (i, 0)), pl.BlockSpec((1, 2 * A), lambda i: (0, 0)),
                   vec, vec, full, pl.BlockSpec((A_GROUPS, GMLP_BLOCK), lambda i: (0, 0))],
        out_shape=[jax.ShapeDtypeStruct(dproj.shape, BF16), jax.ShapeDtypeStruct((1, 2 * A), F32),
                   jax.ShapeDtypeStruct((1, A), F32), jax.ShapeDtypeStruct((1, A), F32),
                   jax.ShapeDtypeStruct((A_GROUPS, GMLP_BLOCK, GMLP_BLOCK), F32),
                   jax.ShapeDtypeStruct((A_GROUPS, GMLP_BLOCK), F32)],
        scratch_shapes=[pltpu.VMEM((tm, A), F32), pltpu.VMEM((tm, A), F32),
                        pltpu.VMEM((A_GROUPS, GMLP_BLOCK, hd), F32)],
        input_output_aliases={8: 0},
        compiler_params=_params("arbitrary"))(proj, proj, dsg, ln_g, ln_b, w_s, w_st, b_sb, dproj)


def _conv_tiles(T, B):
    tm = _tile(T, 256, CONV_ROWS)
    lb = min(LANES, B)
    return tm, tm // CONV_HALO, lb


def _fill_phases(src, dst, B, lb):
    rows = dst.shape[1]
    for p in range(1, 8):
        for cb in range(B // lb):
            ls = slice(cb * lb, (cb + 1) * lb)
            dst[p - 1, :, ls] = src[p:p + rows, ls]


def _shifted(src, phases, off, ls):
    m, p = divmod(off, 8)
    if p == 0:
        return src[off:off + CONV_ROWS, ls]
    return phases[p - 1, 8 * m:8 * m + CONV_ROWS, ls]


def _conv_fwd(proj, w_dw, b_dw, ln_g, ln_b, *, name):
    T = proj.shape[0]
    B = proj.shape[1] // 8
    tm, nh, lb = _conv_tiles(T, B)

    def body(ap_ref, gp_ref, a_ref, g_ref, w_ref, bdw_ref, lg_ref, lb_ref, c_ref, cv_ref, z_sc, zp_sc):
        i = pl.program_id(0)
        z_sc[0:CONV_HALO, :] = jnp.where(i > 0, ap_ref[...] * _sig(gp_ref[...]), 0.0)
        z_sc[CONV_HALO:CONV_HALO + tm, :] = a_ref[...] * _sig(g_ref[...])
        _fill_phases(z_sc, zp_sc, B, lb)
        for cb in range(B // lb):
            ls = slice(cb * lb, (cb + 1) * lb)
            for rc in range(tm // CONV_ROWS):
                acc = jnp.zeros((CONV_ROWS, lb), F32)
                for k in range(CONV_WIDTH):
                    off = rc * CONV_ROWS + CONV_HALO - (CONV_WIDTH - 1) + k
                    acc = acc + w_ref[k:k + 1, ls] * _shifted(z_sc, zp_sc, off, ls)
                c_ref[rc * CONV_ROWS:(rc + 1) * CONV_ROWS, ls] = acc + bdw_ref[:, ls]
        xh, _ = _ln_stats(c_ref[...])
        y = xh * lg_ref[...] + lb_ref[...]
        cv_ref[...] = (y * _sig(y)).astype(BF16)

    vec = pl.BlockSpec((1, B), lambda i: (0, 0))
    halo_a = pl.BlockSpec((CONV_HALO, B), lambda i: (jnp.maximum(i * nh - 1, 0), 2))
    halo_g = pl.BlockSpec((CONV_HALO, B), lambda i: (jnp.maximum(i * nh - 1, 0), 3))
    return pl.pallas_call(
        body, name=name, grid=(T // tm,),
        in_specs=[halo_a, halo_g, pl.BlockSpec((tm, B), lambda i: (i, 2)), pl.BlockSpec((tm, B), lambda i: (i, 3)),
                  pl.BlockSpec((CONV_WPAD, B), lambda i: (0, 0)), vec, vec, vec],
        out_specs=[pl.BlockSpec((tm, B), lambda i: (i, 0))] * 2,
        out_shape=[jax.ShapeDtypeStruct((T, B), F32), jax.ShapeDtypeStruct((T, B), BF16)],
        scratch_shapes=[pltpu.VMEM((CONV_HALO + tm, B), F32), pltpu.VMEM((7, CONV_HALO + tm - 8, B), F32)],
        compiler_params=_params("parallel"))(proj, proj, proj, proj, w_dw, b_dw, ln_g, ln_b)


def _conv_bwd_ln(dcv, c, ln_g, ln_b, *, name):
    T, B = c.shape
    tm = _tile(T, 512)

    def body(dcv_ref, c_ref, lg_ref, lb_ref, dc_ref, dlg_ref, dlb_ref, dbdw_ref):
        @pl.when(pl.program_id(0) == 0)
        def _():
            dlg_ref[...] = jnp.zeros_like(dlg_ref)
            dlb_ref[...] = jnp.zeros_like(dlb_ref)
            dbdw_ref[...] = jnp.zeros_like(dbdw_ref)
        gain = lg_ref[...]
        xh, rstd = _ln_stats(c_ref[...])
        y = xh * gain + lb_ref[...]
        s = _sig(y)
        dy = dcv_ref[...] * (s * (1.0 + y * (1.0 - s)))
        dlg_ref[...] += _colsum(dy * xh)
        dlb_ref[...] += _colsum(dy)
        dc = _ln_bwd(dy, xh, rstd, gain)
        dc_ref[...] = dc
        dbdw_ref[...] += _colsum(dc)

    tile = pl.BlockSpec((tm, B), lambda i: (i, 0))
    vec = pl.BlockSpec((1, B), lambda i: (0, 0))
    return pl.pallas_call(
        body, name=name, grid=(T // tm,), in_specs=[tile, tile, vec, vec], out_specs=[tile, vec, vec, vec],
        out_shape=[jax.ShapeDtypeStruct((T, B), F32)] + [jax.ShapeDtypeStruct((1, B), F32)] * 3,
        compiler_params=_params("arbitrary"))(dcv, c, ln_g, ln_b)


def _conv_bwd(proj, dc, w_dw, dproj, *, name, carry=None):
    T = proj.shape[0]
    B = proj.shape[1] // 8
    tm, nh, lb = _conv_tiles(T, B)
    nt = T // tm
    n_halo = T // CONV_HALO

    def body(ap_ref, gp_ref, a_ref, g_ref, dc_ref, dcn_ref, w_ref, _alias,
             dp_ref, dbin_ref, dw_ref, z_sc, dc_sc, dz_sc, dw_sc, zp_sc, dcp_sc):
        i = pl.program_id(0)

        @pl.when(i == 0)
        def _():
            dbin_ref[...] = jnp.zeros_like(dbin_ref)
            dw_sc[...] = jnp.zeros_like(dw_sc)

        a = a_ref[...]
        s = _sig(g_ref[...])
        z_sc[0:CONV_HALO, :] = jnp.where(i > 0, ap_ref[...] * _sig(gp_ref[...]), 0.0)
        z_sc[CONV_HALO:CONV_HALO + tm, :] = a * s
        dc_sc[0:tm, :] = dc_ref[...]
        dc_sc[tm:tm + CONV_HALO, :] = jnp.where(i < nt - 1, dcn_ref[...], 0.0)
        _fill_phases(z_sc, zp_sc, B, lb)
        _fill_phases(dc_sc, dcp_sc, B, lb)
        for cb in range(B // lb):
            ls = slice(cb * lb, (cb + 1) * lb)
            for rc in range(tm // CONV_ROWS):
                r0 = rc * CONV_ROWS
                acc = jnp.zeros((CONV_ROWS, lb), F32)
                for k in range(CONV_WIDTH):
                    acc = acc + w_ref[k:k + 1, ls] * _shifted(dc_sc, dcp_sc, r0 + (CONV_WIDTH - 1) - k, ls)
                dz_sc[r0:r0 + CONV_ROWS, ls] = acc
            for k in range(CONV_WIDTH):
                part = jnp.zeros((8, lb), F32)
                for rc in range(tm // CONV_ROWS):
                    r0 = rc * CONV_ROWS
                    prod = dc_sc[r0:r0 + CONV_ROWS, ls] * _shifted(
                        z_sc, zp_sc, r0 + CONV_HALO - (CONV_WIDTH - 1) + k, ls)
                    part = part + jnp.sum(prod.reshape(CONV_ROWS // 8, 8, lb), axis=0)
                dw_sc[8 * k:8 * k + 8, ls] += part
        dz = dz_sc[...]
        da = dz * s
        dg = dz * a * s * (1.0 - s)
        dp_ref[:, 0:B] = da.astype(BF16)
        dp_ref[:, B:2 * B] = dg.astype(BF16)
        dbin_ref[:, 0:B] += _colsum(da)
        dbin_ref[:, B:2 * B] += _colsum(dg)

        @pl.when(i == nt - 1)
        def _():
            for k in range(CONV_WIDTH):
                dw_ref[k:k + 1, :] = _colsum(dw_sc[8 * k:8 * k + 8, :])

    halo_a = pl.BlockSpec((CONV_HALO, B), lambda i: (jnp.maximum(i * nh - 1, 0), 2))
    halo_g = pl.BlockSpec((CONV_HALO, B), lambda i: (jnp.maximum(i * nh - 1, 0), 3))
    halo_dc = pl.BlockSpec((CONV_HALO, B), lambda i: (jnp.minimum((i + 1) * nh, n_halo - 1), 0))
    return _call(
        body, name=name, grid=(nt,),
        in_specs=[halo_a, halo_g, pl.BlockSpec((tm, B), lambda i: (i, 2)), pl.BlockSpec((tm, B), lambda i: (i, 3)),
                  pl.BlockSpec((tm, B), lambda i: (i, 0)), halo_dc,
                  pl.BlockSpec((CONV_WPAD, B), lambda i: (0, 0)), ANY],
        out_specs=[pl.BlockSpec((tm, 2 * B), lambda i: (i, 1)), pl.BlockSpec((1, 2 * B), lambda i: (0, 0)),
                   pl.BlockSpec((CONV_WIDTH, B), lambda i: (0, 0))],
        out_shape=[jax.ShapeDtypeStruct(dproj.shape, BF16), jax.ShapeDtypeStruct((1, 2 * B), F32),
                   jax.ShapeDtypeStruct((CONV_WIDTH, B), F32)],
        scratch_shapes=[pltpu.VMEM((CONV_HALO + tm, B), F32), pltpu.VMEM((tm + CONV_HALO, B), F32),
                        pltpu.VMEM((tm, B), F32), pltpu.VMEM((8 * CONV_WIDTH, B), F32),
                        pltpu.VMEM((7, CONV_HALO + tm - 8, B), F32), pltpu.VMEM((7, CONV_HALO + tm - 8, B), F32)],
        aliases={7: 0}, sem=("arbitrary",), args=[proj, proj, proj, proj, dc, dc, w_dw, dproj], carry=carry)


def _mix_fwd_gate(sg, cv, proj, wa, wb, *, name):
    T, A = sg.shape
    D = wa.shape[1]
    tm = _tile(T, 256)

    def body(sg_ref, cv_ref, la_ref, lb_ref, wa_ref, wb_ref, ya_ref, yb_ref, m_ref):
        ya = _dot(sg_ref[...], wa_ref[...])
        yb = _dot(cv_ref[...], wb_ref[...])
        ya_ref[...] = ya.astype(BF16)
        yb_ref[...] = yb.astype(BF16)
        m_ref[...] = (_sig(la_ref[...]) * ya + _sig(lb_ref[...]) * yb).astype(BF16)

    act = pl.BlockSpec((tm, A), lambda i: (i, 0))
    wide = pl.BlockSpec((tm, D), lambda i: (i, 0))
    wspec = pl.BlockSpec((A, D), lambda i: (0, 0))
    return pl.pallas_call(
        body, name=name, grid=(T // tm,),
        in_specs=[act, act, pl.BlockSpec((tm, D), lambda i: (i, 2)), pl.BlockSpec((tm, D), lambda i: (i, 3)),
                  wspec, wspec],
        out_specs=[wide] * 3, out_shape=[jax.ShapeDtypeStruct((T, D), BF16)] * 3,
        compiler_params=_params("parallel"))(sg, cv, proj, proj, wa, wb)


def _mix_fwd_out(m, wout, xh, lg, lb, *, name):
    T, D = xh.shape
    tm = _tile(T, 512)

    def body(m_ref, w_ref, xh_ref, lg_ref, lb_ref, xho_ref, rstd_ref):
        r = ALPHA * (xh_ref[...] * lg_ref[...] + lb_ref[...]) + _dot(m_ref[...], w_ref[...])
        xho, rstd = _ln_stats(r)
        xho_ref[...] = xho
        rstd_ref[...] = jnp.broadcast_to(rstd, (tm, LANES))

    row = pl.BlockSpec((tm, D), lambda i: (i, 0))
    vec = pl.BlockSpec((1, D), lambda i: (0, 0))
    return pl.pallas_call(
        body, name=name, grid=(T // tm,),
        in_specs=[row, pl.BlockSpec((D, D), lambda i: (0, 0)), row, vec, vec],
        out_specs=[row, pl.BlockSpec((tm, LANES), lambda i: (i, 0))],
        out_shape=[jax.ShapeDtypeStruct((T, D), F32), jax.ShapeDtypeStruct((T, LANES), F32)],
        compiler_params=_params("parallel"))(m, wout, xh, lg, lb)


def _mix_bwd_gate(dr, wout, proj, ya, yb, *, name, carry=None):
    T, D = dr.shape
    N = proj.shape[1]
    tm = _tile(T, 256)

    def body(dr_ref, w_ref, la_ref, lb_ref, ya_ref, yb_ref, dya_ref, dyb_ref, dp_ref, dbin_ref):
        @pl.when(pl.program_id(0) == 0)
        def _():
            dbin_ref[...] = jnp.zeros_like(dbin_ref)
        dm = _dot_nt(dr_ref[...].astype(BF16), w_ref[...])
        sa = _sig(la_ref[...])
        sb = _sig(lb_ref[...])
        dya_ref[...] = (dm * sa).astype(BF16)
        dyb_ref[...] = (dm * sb).astype(BF16)
        dla = dm * ya_ref[...].astype(F32) * sa * (1.0 - sa)
        dlb = dm * yb_ref[...].astype(F32) * sb * (1.0 - sb)
        dp_ref[:, 0:D] = dla.astype(BF16)
        dp_ref[:, D:2 * D] = dlb.astype(BF16)
        dbin_ref[:, 0:D] += _colsum(dla)
        dbin_ref[:, D:2 * D] += _colsum(dlb)

    row = pl.BlockSpec((tm, D), lambda i: (i, 0))
    return _call(
        body, name=name, grid=(T // tm,),
        in_specs=[row, pl.BlockSpec((D, D), lambda i: (0, 0)), pl.BlockSpec((tm, D), lambda i: (i, 2)),
                  pl.BlockSpec((tm, D), lambda i: (i, 3)), row, row],
        out_specs=[row, row, pl.BlockSpec((tm, 2 * D), lambda i: (i, 1)), pl.BlockSpec((1, 2 * D), lambda i: (0, 0))],
        out_shape=[jax.ShapeDtypeStruct((T, D), BF16), jax.ShapeDtypeStruct((T, D), BF16),
                   jax.ShapeDtypeStruct((T, N), BF16), jax.ShapeDtypeStruct((1, 2 * D), F32)],
        sem=("arbitrary",), args=[dr, wout, proj, proj, ya, yb], carry=carry)


def _mix_bwd_proj(dya, dyb, wa, wb, *, name):
    T, D = dya.shape
    A = wa.shape[0]
    tm = _tile(T, 512)

    def body(dya_ref, dyb_ref, wa_ref, wb_ref, dsg_ref, dcv_ref):
        dsg_ref[...] = _dot_nt(dya_ref[...], wa_ref[...])
        dcv_ref[...] = _dot_nt(dyb_ref[...], wb_ref[...])

    row = pl.BlockSpec((tm, D), lambda i: (i, 0))
    wspec = pl.BlockSpec((A, D), lambda i: (0, 0))
    act = pl.BlockSpec((tm, A), lambda i: (i, 0))
    return pl.pallas_call(
        body, name=name, grid=(T // tm,), in_specs=[row, row, wspec, wspec], out_specs=[act, act],
        out_shape=[jax.ShapeDtypeStruct((T, A), F32)] * 2,
        compiler_params=_params("parallel"))(dya, dyb, wa, wb)


def _mesh_pos():
    return lax.axis_index("x"), lax.axis_index("y"), lax.axis_index("c")


def _shard_view(ref, p, shape, axis):
    r, c = shape
    if axis == 0:
        return ref.at[pl.ds(pl.multiple_of(p * r, 16), r), :]
    return ref.at[:, pl.ds(pl.multiple_of(p * c, LANES), c)]


def _all_gather(shards, axes):
    n = len(shards)
    shapes = [s.shape for s in shards]

    def run(ins, outs, sems, phase):
        send_sems, recv_sems, local_sems = sems
        x, y, c = _mesh_pos()
        me, sibling = (x, y, c), (x, y, 1 - c)
        chips = [(1 - x, y), (x, 1 - y), (1 - x, 1 - y)]

        def view(t, pos):
            px, py, pc = pos
            return _shard_view(outs[t], 4 * px + 2 * py + pc, shapes[t], axes[t])

        def copy(t, k, block, to, src=None):
            return pltpu.make_async_remote_copy(
                src_ref=view(t, block) if src is None else src, dst_ref=view(t, block),
                send_sem=send_sems.at[7 * t + k], recv_sem=recv_sems.at[7 * t + k],
                device_id=to, device_id_type=MESH)

        mine = [pltpu.make_async_copy(ins[t], view(t, me), local_sems.at[t]) for t in range(n)]
        first = []
        for t in range(n):
            first.append(copy(t, 0, me, sibling, src=ins[t]))
            first += [copy(t, 1 + j, me, (*chip, c), src=ins[t]) for j, chip in enumerate(chips)]
        if phase == "start":
            for cp in mine + first:
                cp.start()
            return
        passed = []
        for t in range(n):
            for j, chip in enumerate(chips):
                copy(t, 1 + j, (*chip, c), me).wait_recv()
                fwd = copy(t, 4 + j, (*chip, c), sibling)
                fwd.start()
                passed.append(fwd)
        for t in range(n):
            copy(t, 0, sibling, me).wait_recv()
            for j, chip in enumerate(chips):
                copy(t, 4 + j, (*chip, 1 - c), me).wait_recv()
        for cp in first + passed:
            cp.wait_send()
        for cp in mine:
            cp.wait()

    out_shape = [jax.ShapeDtypeStruct((N_DEV * s.shape[0], s.shape[1]) if ax == 0
                                      else (s.shape[0], N_DEV * s.shape[1]), s.dtype)
                 for s, ax in zip(shards, axes)]
    return _Comm(shards, out_shape, [pltpu.SemaphoreType.DMA((7 * n,)), pltpu.SemaphoreType.DMA((7 * n,)),
                                     pltpu.SemaphoreType.DMA((n,))], run)


def _rs_to_sibling(grads, shapes, axes):
    n = len(grads)

    def run(gs, outs, sems, phase):
        send_sems, recv_sems = sems
        x, y, c = _mesh_pos()
        copies = [pltpu.make_async_remote_copy(
            src_ref=_shard_view(gs[t], 2 * k + (1 - c), shapes[t], axes[t]), dst_ref=outs[t].at[k],
            send_sem=send_sems.at[4 * t + k], recv_sem=recv_sems.at[4 * t + k],
            device_id=(x, y, 1 - c), device_id_type=MESH) for t in range(n) for k in range(4)]
        if phase == "start":
            for cp in copies:
                cp.start()
            return
        for cp in copies:
            cp.wait_recv()
        for cp in copies:
            cp.wait_send()

    return _Comm(grads, [jax.ShapeDtypeStruct((4,) + tuple(s), BF16) for s in shapes],
                 [pltpu.SemaphoreType.DMA((4 * n,)), pltpu.SemaphoreType.DMA((4 * n,))], run)


def _rs_pair_sum(g, recv, cidx, shape, axis, *, name):
    r, c = shape
    tr = _tile(r, max(8, (1 << 20) // c), 16)
    nr = r // tr

    def body(c_ref, g_ref, rv_ref, o_ref):
        o_ref[...] = (g_ref[...].astype(F32) + rv_ref[...].astype(F32)).astype(BF16)

    if axis == 1:
        g_spec = pl.BlockSpec((tr, c), lambda k, i, s: (i, 2 * k + s[0]))
    else:
        g_spec = pl.BlockSpec((tr, c), lambda k, i, s: ((2 * k + s[0]) * nr + i, 0))
    blk = pl.BlockSpec((None, tr, c), lambda k, i, s: (k, i, 0))
    return pl.pallas_call(
        body, name=name,
        grid_spec=pltpu.PrefetchScalarGridSpec(num_scalar_prefetch=1, grid=(4, nr), in_specs=[g_spec, blk],
                                               out_specs=blk),
        out_shape=jax.ShapeDtypeStruct((4, r, c), BF16),
        compiler_params=_params("parallel", "parallel"))(cidx, g, recv)


def _rs_to_chips(parts):
    n = len(parts)

    def run(ps, outs, sems, phase):
        send_sems, recv_sems, local_sems = sems
        x, y, c = _mesh_pos()
        my_chip = 2 * x + y
        peers = [(1 - x, y), (x, 1 - y), (1 - x, 1 - y)]
        local = [pltpu.make_async_copy(ps[t].at[my_chip], outs[t].at[my_chip], local_sems.at[t]) for t in range(n)]
        sends = [pltpu.make_async_remote_copy(
            src_ref=ps[t].at[2 * px + py], dst_ref=outs[t].at[my_chip],
            send_sem=send_sems.at[3 * t + j], recv_sem=recv_sems.at[3 * t + j],
            device_id=(px, py, c), device_id_type=MESH) for t in range(n) for j, (px, py) in enumerate(peers)]
        if phase == "start":
            for cp in local + sends:
                cp.start()
            return
        for t in range(n):
            for j, (px, py) in enumerate(peers):
                pltpu.make_async_remote_copy(
                    src_ref=ps[t].at[2 * px + py], dst_ref=outs[t].at[2 * px + py],
                    send_sem=send_sems.at[3 * t + j], recv_sem=recv_sems.at[3 * t + j],
                    device_id=(x, y, c), device_id_type=MESH).wait_recv()
        for cp in sends:
            cp.wait_send()
        for cp in local:
            cp.wait()

    return _Comm(parts, [jax.ShapeDtypeStruct(p.shape, BF16) for p in parts],
                 [pltpu.SemaphoreType.DMA((3 * n,)), pltpu.SemaphoreType.DMA((3 * n,)),
                  pltpu.SemaphoreType.DMA((n,))], run)


def _all_reduce_small(buf, *, name):
    R = buf.shape[0]

    def body(in_ref, out_ref, slots, send_sems, recv_sems, local_sem):
        x, y, c = _mesh_pos()
        me = 4 * x + 2 * y + c
        local = pltpu.make_async_copy(in_ref, slots.at[me], local_sem)
        local.start()
        flips = [(fx, fy, fc) for fx in (0, 1) for fy in (0, 1) for fc in (0, 1)][1:]
        peers = [(1 - x if fx else x, 1 - y if fy else y, 1 - c if fc else c) for fx, fy, fc in flips]
        sends = []
        for k, peer in enumerate(peers):
            cp = pltpu.make_async_remote_copy(src_ref=in_ref, dst_ref=slots.at[me], send_sem=send_sems.at[k],
                                              recv_sem=recv_sems.at[k], device_id=peer, device_id_type=MESH)
            cp.start()
            sends.append(cp)
        for k, (px, py, pc) in enumerate(peers):
            pltpu.make_async_remote_copy(src_ref=in_ref, dst_ref=slots.at[4 * px + 2 * py + pc],
                                         send_sem=send_sems.at[k], recv_sem=recv_sems.at[k],
                                         device_id=(x, y, c), device_id_type=MESH).wait_recv()
        for cp in sends:
            cp.wait_send()
        local.wait()
        acc = slots[0]
        for p in range(1, N_DEV):
            acc = acc + slots[p]
        out_ref[...] = acc

    vm = pl.BlockSpec(memory_space=pltpu.VMEM)
    return pl.pallas_call(
        body, name=name, in_specs=[vm], out_specs=vm, out_shape=jax.ShapeDtypeStruct(buf.shape, F32),
        scratch_shapes=[pltpu.VMEM((N_DEV, R, LANES), F32), pltpu.SemaphoreType.DMA((7,)),
                        pltpu.SemaphoreType.DMA((7,)), pltpu.SemaphoreType.DMA],
        compiler_params=pltpu.CompilerParams(has_side_effects=True, vmem_limit_bytes=VMEM_LIMIT_BYTES))(buf)


def _adam_math(g, w, m, v):
    m_new = ADAM_B1 * m + (1.0 - ADAM_B1) * g
    v_new = ADAM_B2 * v + (1.0 - ADAM_B2) * (g * g)
    m_hat = m_new / ADAM_C1
    v_hat = v_new / ADAM_C2
    delta = -ADAM_LR * (m_hat / (jnp.sqrt(v_hat) + ADAM_EPS) + ADAM_WD * w)
    return delta, m_new, v_new


def _adamw_sharded(q, w, m, v, *, name):
    r, c = w.shape
    tr = _tile(r, max(8, (1 << 18) // c), 16)

    def body(q_ref, w_ref, m_ref, v_ref, g_ref, d_ref, mo_ref, vo_ref):
        g = ((q_ref[0].astype(F32) + q_ref[1].astype(F32)) + q_ref[2].astype(F32)) + q_ref[3].astype(F32)
        g_ref[...] = g
        d_ref[...], mo_ref[...], vo_ref[...] = _adam_math(g, w_ref[...], m_ref[...], v_ref[...])

    blk = pl.BlockSpec((tr, c), lambda i: (i, 0))
    return pl.pallas_call(
        body, name=name, grid=(r // tr,),
        in_specs=[pl.BlockSpec((4, tr, c), lambda i: (0, i, 0)), blk, blk, blk], out_specs=[blk] * 4,
        out_shape=[jax.ShapeDtypeStruct((r, c), F32)] * 4,
        compiler_params=_params("parallel"))(q, w, m, v)


def _adamw_plain(g, w, m, v, *, name):
    r, c = w.shape
    tr = _tile(r, 512)

    def body(g_ref, w_ref, m_ref, v_ref, d_ref, mo_ref, vo_ref):
        d_ref[...], mo_ref[...], vo_ref[...] = _adam_math(g_ref[...], w_ref[...], m_ref[...], v_ref[...])

    blk = pl.BlockSpec((tr, c), lambda i: (i, 0))
    return pl.pallas_call(
        body, name=name, grid=(r // tr,), in_specs=[blk] * 4, out_specs=[blk] * 3,
        out_shape=[jax.ShapeDtypeStruct((r, c), F32)] * 3,
        compiler_params=_params("parallel"))(g, w, m, v)


def _pack_rows(arrays):
    return jnp.concatenate([a.reshape(-1, LANES) for a in arrays], axis=0)


def kernel(x, ffn1_w_gu, ffn1_w_down, ln1_g, ln1_b, w_in, b_in, sgu_ln_g, sgu_ln_b, sgu_w_s, sgu_b_s, w_a_proj, conv_w_dw, conv_b_dw, conv_ln_g, conv_ln_b, w_b_proj, w_out, ln2_g, ln2_b, ffn2_w_gu, ffn2_w_down, ln3_g, ln3_b, loss_target, m_ffn1_w_gu, m_ffn1_w_down, m_ln1_g, m_ln1_b, m_w_in, m_b_in, m_sgu_ln_g, m_sgu_ln_b, m_sgu_w_s, m_sgu_b_s, m_w_a_proj, m_conv_w_dw, m_conv_b_dw, m_conv_ln_g, m_conv_ln_b, m_w_b_proj, m_w_out, m_ln2_g, m_ln2_b, m_ffn2_w_gu, m_ffn2_w_down, m_ln3_g, m_ln3_b, v_ffn1_w_gu, v_ffn1_w_down, v_ln1_g, v_ln1_b, v_w_in, v_b_in, v_sgu_ln_g, v_sgu_ln_b, v_sgu_w_s, v_sgu_b_s, v_w_a_proj, v_conv_w_dw, v_conv_b_dw, v_conv_ln_g, v_conv_ln_b, v_w_b_proj, v_w_out, v_ln2_g, v_ln2_b, v_ffn2_w_gu, v_ffn2_w_down, v_ln3_g, v_ln3_b):
    given = dict(locals())
    w = {n: given[n][0] for n in WEIGHTS}
    mom = {n: given["m_" + n][0] for n in WEIGHTS}
    var = {n: given["v_" + n][0] for n in WEIGHTS}
    xt = x[0]
    target = loss_target[0]
    T, D = xt.shape
    A = w['w_a_proj'].shape[0]

    big_names = list(BIG)
    early = ['ffn1_w_gu', 'ffn1_w_down']
    late = [n for n in big_names if n not in early]
    conv_w_pad = jnp.pad(w['conv_w_dw'], ((0, CONV_WPAD - CONV_WIDTH), (0, 0)))
    w_bf = {n: w[n].astype(BF16) for n in big_names}
    full = dict(zip(early, _comm_call(_all_gather([w_bf[n] for n in early], [BIG[n] for n in early]),
                                      name="all_gather_ffn1")))
    gather_late = _all_gather([w_bf[n] for n in late] + [conv_w_pad], [BIG[n] for n in late] + [1])

    def row(v):
        return v.reshape(1, -1)

    ones = jnp.ones((1, D), F32)
    zeros = jnp.zeros((1, D), F32)
    w_s = w['sgu_w_s']
    w_st = jnp.swapaxes(w_s, 1, 2)
    b_sb = jnp.broadcast_to(w['sgu_b_s'][:, :, None], w_s.shape)

    (gate1, up1, xb0, xh1, rstd1), gathered = _ffn_fwd(xt, ones, zeros, full['ffn1_w_gu'], full['ffn1_w_down'],
                                                  affine=False, name="ffn1_fwd", carry=gather_late)
    full.update(zip(late, gathered[:-1]))
    conv_w_full = gathered[-1]
    g1, b1 = row(w['ln1_g']), row(w['ln1_b'])
    proj, xb1 = _inproj_fwd(xh1, g1, b1, full['w_in'], row(w['b_in']), name="inproj_fwd")
    sg = _sgu_fwd(proj, row(w['sgu_ln_g']), row(w['sgu_ln_b']), w_s, b_sb, name="sgu_fwd")
    conv_out, cv = _conv_fwd(proj, conv_w_full, row(w['conv_b_dw']), row(w['conv_ln_g']), row(w['conv_ln_b']),
                             name="conv_fwd")
    ya, yb, mixed = _mix_fwd_gate(sg, cv, proj, full['w_a_proj'], full['w_b_proj'], name="mix_fwd_gate")
    xh2, rstd2 = _mix_fwd_out(mixed, full['w_out'], xh1, g1, b1, name="mix_fwd_out")
    g2, b2 = row(w['ln2_g']), row(w['ln2_b'])
    gate2, up2, xb2, dr3, loss_part, d_ln3_g, d_ln3_b = _ffn_fwd(
        xh2, g2, b2, full['ffn2_w_gu'], full['ffn2_w_down'], affine=True, name="ffn2_fwd_loss",
        final=(row(w['ln3_g']), row(w['ln3_b']), target))

    F = full['ffn2_w_down'].shape[0]
    h2, dgate2, dup2, dr2, d_ln2_g, d_ln2_b = _ffn_bwd(dr3, gate2, up2, full['ffn2_w_gu'], full['ffn2_w_down'],
                                                      name="ffn2_bwd", prev=(xh2, rstd2, g2))
    G, P, Q = {}, {}, {}
    cidx = lax.axis_index("c").astype(jnp.int32).reshape(1)

    def to_sibling(names):
        return _rs_to_sibling([G[n] for n in names], [w[n].shape for n in names], [BIG[n] for n in names])

    def pair_sum(names, received):
        for n, rv in zip(names, received):
            P[n] = _rs_pair_sum(G[n], rv, cidx, w[n].shape, BIG[n], name="rs_pair_sum_" + n)

    def to_chips(names):
        return _rs_to_chips([P[n] for n in names])

    G['ffn2_w_down'] = _mm_tn(h2, dr3, name="dw_ffn2_down", tm_pref=1408, tn_pref=2048, scale=0.5)
    gu, rv = _mm_tn(xb2, dgate2, name="dw_ffn2_gate", tm_pref=2048, tn_pref=1408,
                    n_total=2 * F, carry=to_sibling(['ffn2_w_down']))
    pair_sum(['ffn2_w_down'], rv)
    G['ffn2_w_gu'], q = _mm_tn(xb2, dup2, name="dw_ffn2_up", tm_pref=2048, tn_pref=1408,
                               into=gu, col_off=F, n_total=2 * F, carry=to_chips(['ffn2_w_down']))
    Q['ffn2_w_down'] = q[0]

    (dya, dyb, dproj, dbin_gate), rv = _mix_bwd_gate(dr2, full['w_out'], proj, ya, yb, name="mix_bwd_gate",
                                                     carry=to_sibling(['ffn2_w_gu']))
    pair_sum(['ffn2_w_gu'], rv)
    dsg, dcv = _mix_bwd_proj(dya, dyb, full['w_a_proj'], full['w_b_proj'], name="mix_bwd_proj")
    dproj, dbin_sgu, d_sgu_ln_g, d_sgu_ln_b, d_w_s, d_b_s = _sgu_bwd(
        proj, dsg, row(w['sgu_ln_g']), row(w['sgu_ln_b']), w_s, w_st, b_sb, dproj, name="sgu_bwd")
    dconv, d_conv_ln_g, d_conv_ln_b, d_conv_b = _conv_bwd_ln(dcv, conv_out, row(w['conv_ln_g']),
                                                            row(w['conv_ln_b']), name="conv_bwd_ln")
    (dproj, dbin_conv, d_conv_w), q = _conv_bwd(proj, dconv, conv_w_full, dproj, name="conv_bwd",
                                                carry=to_chips(['ffn2_w_gu']))
    Q['ffn2_w_gu'] = q[0]

    mid = ['w_out', 'w_a_proj', 'w_b_proj']
    G['w_out'] = _mm_tn(mixed, dr2, name="dw_out", tm_pref=2048, tn_pref=1024)
    G['w_a_proj'] = _mm_tn(sg, dya, name="dw_a_proj", tm_pref=1024, tn_pref=2048)
    G['w_b_proj'] = _mm_tn(cv, dyb, name="dw_b_proj", tm_pref=1024, tn_pref=2048)
    G['w_in'], rv = _mm_tn(xb1, dproj, name="dw_in", tm_pref=2048, tn_pref=1024, carry=to_sibling(mid))
    pair_sum(mid, rv)
    both = _join(to_chips(mid), to_sibling(['w_in']))
    (dr1, d_ln1_g, d_ln1_b), moved = _inproj_bwd(dproj, full['w_in'], dr2, xh1, rstd1, g1, name="inproj_bwd",
                                                 carry=both)
    q, rv = both.split(moved)
    Q.update(zip(mid, q))
    pair_sum(['w_in'], rv)

    h1, dgate1, dup1, grad_x = _ffn_bwd(dr1, gate1, up1, full['ffn1_w_gu'], full['ffn1_w_down'], name="ffn1_bwd")
    G['ffn1_w_down'], q = _mm_tn(h1, dr1, name="dw_ffn1_down", tm_pref=1408, tn_pref=2048, scale=0.5,
                                 carry=to_chips(['w_in']))
    Q['w_in'] = q[0]
    gu, rv = _mm_tn(xb0, dgate1, name="dw_ffn1_gate", tm_pref=2048, tn_pref=1408, n_total=2 * F,
                    carry=to_sibling(['ffn1_w_down']))
    pair_sum(['ffn1_w_down'], rv)
    G['ffn1_w_gu'], q = _mm_tn(xb0, dup1, name="dw_ffn1_up", tm_pref=2048, tn_pref=1408, into=gu, col_off=F,
                               n_total=2 * F, carry=to_chips(['ffn1_w_down']))
    Q['ffn1_w_down'] = q[0]
    pair_sum(['ffn1_w_gu'], _comm_call(to_sibling(['ffn1_w_gu']), name="rs_to_sibling_last"))
    Q['ffn1_w_gu'] = _comm_call(to_chips(['ffn1_w_gu']), name="rs_to_chips_last")[0]

    grads, deltas, new_m, new_v = {}, {}, {}, {}
    for n in big_names:
        grads[n], deltas[n], new_m[n], new_v[n] = _adamw_sharded(Q[n], w[n], mom[n], var[n], name="adamw_" + n)

    B = conv_w_full.shape[1]
    small_g = {'ln1_g': d_ln1_g, 'ln1_b': d_ln1_b,
               'b_in': jnp.concatenate([dbin_sgu, dbin_conv, dbin_gate], axis=1),
               'sgu_ln_g': d_sgu_ln_g, 'sgu_ln_b': d_sgu_ln_b, 'sgu_w_s': d_w_s, 'sgu_b_s': d_b_s,
               'conv_b_dw': d_conv_b, 'conv_ln_g': d_conv_ln_g, 'conv_ln_b': d_conv_ln_b,
               'ln2_g': d_ln2_g, 'ln2_b': d_ln2_b, 'ln3_g': d_ln3_g, 'ln3_b': d_ln3_b}
    packed = _pack_rows([small_g[n] for n in SMALL] + [d_conv_w, loss_part])
    reduced = _all_reduce_small(packed, name="all_reduce_small")
    n_small_rows = sum(w[n].size for n in SMALL) // LANES
    conv_rows = CONV_WIDTH * B // LANES
    d_small, m_small, v_small = _adamw_plain(
        reduced[:n_small_rows], _pack_rows([w[n] for n in SMALL]), _pack_rows([mom[n] for n in SMALL]),
        _pack_rows([var[n] for n in SMALL]), name="adamw_small")
    off = 0
    for n in SMALL:
        rows = w[n].size // LANES
        grads[n] = reduced[off:off + rows].reshape(w[n].shape)
        deltas[n] = d_small[off:off + rows].reshape(w[n].shape)
        new_m[n] = m_small[off:off + rows].reshape(w[n].shape)
        new_v[n] = v_small[off:off + rows].reshape(w[n].shape)
        off += rows
    conv_g_full = reduced[off:off + conv_rows].reshape(CONV_WIDTH, B)
    bs = w['conv_w_dw'].shape[1]
    my_block = 4 * lax.axis_index("x") + 2 * lax.axis_index("y") + lax.axis_index("c")
    grads['conv_w_dw'] = lax.dynamic_slice(conv_g_full, (0, my_block * bs), (CONV_WIDTH, bs))
    deltas['conv_w_dw'], new_m['conv_w_dw'], new_v['conv_w_dw'] = _adamw_plain(
        grads['conv_w_dw'], w['conv_w_dw'], mom['conv_w_dw'], var['conv_w_dw'], name="adamw_conv_w")
    loss = reduced[off + conv_rows, 0]

    def lead(a):
        return a[None]

    return (loss, grad_x[None], *[lead(grads[n]) for n in WEIGHTS], *[lead(deltas[n]) for n in WEIGHTS],
            *[lead(new_m[n]) for n in WEIGHTS], *[lead(new_v[n]) for n in WEIGHTS])
```

```python
import functools
import math

import jax
import jax.numpy as jnp
from jax import lax
from jax.experimental import pallas as pl
from jax.experimental.pallas import tpu as pltpu

F32 = jnp.float32
BF16 = jnp.bfloat16

ALPHA = 2.0 ** 0.25
LN_EPS = 1e-5
CONV_WIDTH = 31
CONV_HALO = 32
CONV_ROWS = 64
CONV_WPAD = 32
CHUNK = 64
GMLP_BLOCK = 128
A_GROUPS = 8
N_DEV = 8
LANES = 128

ADAM_LR = 0.001
ADAM_B1 = 0.9
ADAM_B2 = 0.999
ADAM_EPS = 1e-08
ADAM_WD = 0.01
ADAM_STEP = 10
ADAM_C1 = 1.0 - ADAM_B1 ** ADAM_STEP
ADAM_C2 = 1.0 - ADAM_B2 ** ADAM_STEP

VMEM_LIMIT_BYTES = 60 * 2 ** 20
MESH = pl.DeviceIdType.MESH
ANY = pl.BlockSpec(memory_space=pl.ANY)

WEIGHTS = ['ffn1_w_gu', 'ffn1_w_down', 'ln1_g', 'ln1_b', 'w_in', 'b_in', 'sgu_ln_g', 'sgu_ln_b', 'sgu_w_s',
           'sgu_b_s', 'w_a_proj', 'conv_w_dw', 'conv_b_dw', 'conv_ln_g', 'conv_ln_b', 'w_b_proj', 'w_out',
           'ln2_g', 'ln2_b', 'ffn2_w_gu', 'ffn2_w_down', 'ln3_g', 'ln3_b']
BIG = {'ffn1_w_gu': 1, 'ffn1_w_down': 0, 'w_in': 1, 'w_a_proj': 1, 'w_b_proj': 1, 'w_out': 0,
       'ffn2_w_gu': 1, 'ffn2_w_down': 0}
SMALL = [n for n in WEIGHTS if n not in BIG and n != 'conv_w_dw']


def _tile(n, pref, mult=8):
    best = None
    for d in range(mult, min(n, pref) + 1, mult):
        if n % d == 0:
            best = d
    return n if best is None else best


def _params(*sem):
    return pltpu.CompilerParams(dimension_semantics=sem, vmem_limit_bytes=VMEM_LIMIT_BYTES)


def _dot(a, b):
    return jnp.dot(a, b, preferred_element_type=F32)


def _dot_nt(a, b):
    return lax.dot_general(a, b, (((1,), (1,)), ((), ())), preferred_element_type=F32)


def _dot_tn(a, b):
    return lax.dot_general(a, b, (((0,), (0,)), ((), ())), preferred_element_type=F32)


def _sig(x):
    return 1.0 / (1.0 + jnp.exp(-x))


_GELU_K = math.sqrt(2.0 / math.pi)
_GELU_C = 0.044715


def _gelu(x):
    t = jnp.tanh(_GELU_K * (x + _GELU_C * x * x * x))
    return 0.5 * x * (1.0 + t)


def _gelu_grad(x):
    x2 = x * x
    t = jnp.tanh(_GELU_K * (x + _GELU_C * x2 * x))
    return 0.5 * (1.0 + t) + 0.5 * x * (1.0 - t * t) * (_GELU_K * (1.0 + 3.0 * _GELU_C * x2))


def _ln_stats(r):
    mu = jnp.mean(r, axis=-1, keepdims=True)
    rc = r - mu
    var = jnp.mean(rc * rc, axis=-1, keepdims=True)
    rstd = lax.rsqrt(var + LN_EPS)
    return rc * rstd, rstd


def _ln_bwd(dy, xh, rstd, g):
    dxh = dy * g
    m1 = jnp.mean(dxh, axis=-1, keepdims=True)
    m2 = jnp.mean(dxh * xh, axis=-1, keepdims=True)
    return rstd * (dxh - m1 - xh * m2)


def _colsum(v):
    return jnp.sum(v, axis=0, keepdims=True)


def _chunk_mask(transposed):
    shift = CHUNK.bit_length() - 1
    r = lax.broadcasted_iota(jnp.int32, (GMLP_BLOCK, GMLP_BLOCK), 0) >> shift
    c = lax.broadcasted_iota(jnp.int32, (GMLP_BLOCK, GMLP_BLOCK), 1) >> shift
    return (r <= c) if transposed else (c <= r)


class _Comm:
    def __init__(self, inputs, out_shape, scratch, run, when=(0.0, 1.0)):
        self.inputs, self.out_shape, self.scratch, self.run = list(inputs), list(out_shape), list(scratch), run
        self.when = tuple(when)
        self.parts = [len(self.out_shape)]

    def split(self, outs):
        res, o = [], 0
        for n in self.parts:
            res.append(list(outs[o:o + n]))
            o += n
        return res


def _join(*comms):
    comms = [c for c in comms if c is not None]
    if not comms:
        return None
    assert all(c.when == (0.0, 1.0) for c in comms)

    def run(ins, outs, sems, phase):
        i = o = s = 0
        for c in comms:
            c.run(ins[i:i + len(c.inputs)], outs[o:o + len(c.out_shape)], sems[s:s + len(c.scratch)], phase)
            i, o, s = i + len(c.inputs), o + len(c.out_shape), s + len(c.scratch)

    joined = _Comm(sum((c.inputs for c in comms), []), sum((c.out_shape for c in comms), []),
                   sum((c.scratch for c in comms), []), run)
    joined.parts = [len(c.out_shape) for c in comms]
    return joined


def _call(body, *, name, grid, in_specs, out_specs, out_shape, args, sem, scratch_shapes=(), aliases=None, carry=None):
    in_specs, out_specs, out_shape = list(in_specs), list(out_specs), list(out_shape)
    scratch_shapes = list(scratch_shapes)
    if carry is None:
        return pl.pallas_call(body, name=name, grid=grid, in_specs=in_specs, out_specs=out_specs,
                              out_shape=out_shape, scratch_shapes=scratch_shapes,
                              input_output_aliases=aliases or {}, compiler_params=_params(*sem))(*args)
    n_in, n_out, n_scr = len(args), len(out_shape), len(scratch_shapes)
    c_in, c_out = len(carry.inputs), len(carry.out_shape)
    n_steps = math.prod(grid)
    at_step = [int(round(f * (n_steps - 1))) for f in carry.when]
    assert at_step[0] == 0 and at_step[-1] == n_steps - 1 and at_step == sorted(at_step)

    def wrapped(*refs):
        ins, c_ins = refs[:n_in], refs[n_in:n_in + c_in]
        o0 = n_in + c_in
        outs, c_outs = refs[o0:o0 + n_out], refs[o0 + n_out:o0 + n_out + c_out]
        s0 = o0 + n_out + c_out
        scr, c_sems = refs[s0:s0 + n_scr], refs[s0 + n_scr:]
        step = 0
        for a, g in enumerate(grid):
            step = step * g + pl.program_id(a)
        pl.when(step == 0)(functools.partial(carry.run, c_ins, c_outs, c_sems, 0))
        body(*ins, *outs, *scr)
        for k in range(1, len(at_step)):
            pl.when(step == at_step[k])(functools.partial(carry.run, c_ins, c_outs, c_sems, k))

    res = pl.pallas_call(
        wrapped, name=name, grid=grid, in_specs=in_specs + [ANY] * c_in, out_specs=out_specs + [ANY] * c_out,
        out_shape=out_shape + carry.out_shape, scratch_shapes=scratch_shapes + carry.scratch,
        input_output_aliases=aliases or {},
        compiler_params=pltpu.CompilerParams(dimension_semantics=("arbitrary",) * len(grid),
                                             vmem_limit_bytes=VMEM_LIMIT_BYTES, has_side_effects=True),
    )(*args, *carry.inputs)
    return list(res[:n_out]), list(res[n_out:])


def _comm_call(comm, *, name):
    n_in, n_out = len(comm.inputs), len(comm.out_shape)

    def body(*refs):
        ins, outs, sems = refs[:n_in], refs[n_in:n_in + n_out], refs[n_in + n_out:]
        for k in range(len(comm.when)):
            comm.run(ins, outs, sems, k)

    return list(pl.pallas_call(
        body, name=name, in_specs=[ANY] * n_in, out_specs=[ANY] * n_out, out_shape=comm.out_shape,
        scratch_shapes=comm.scratch, compiler_params=pltpu.CompilerParams(has_side_effects=True))(*comm.inputs))


def _when(cond, fn):
    if isinstance(cond, bool):
        if cond:
            fn()
    else:
        pl.when(cond)(fn)


def _ffn_tiles(T, F):
    return _tile(T, 512), _tile(F, 512, LANES)


def _row_chunks(tm, rows=128):
    rows = _tile(tm, rows)
    return [slice(r, r + rows) for r in range(0, tm, rows)]


def _hidden_loop(nj, step):
    def pair(jj, c):
        step(2 * jj, 0)
        step(2 * jj + 1, 1)
        return c
    if nj // 2:
        lax.fori_loop(0, nj // 2, pair, 0)
    if nj % 2:
        step(nj - 1, 0)


def _ffn_fwd_looped(xh, lg, lb, wgu, wd, *, affine, name, final=None, carry=None):
    T, D = xh.shape
    F = wd.shape[0]
    tm, tn = _ffn_tiles(T, F)
    nj, nt = F // tn, T // tm
    is_final = final is not None

    def body(*refs):
        if is_final:
            (xh_ref, lg_ref, lb_ref, wgu_hbm, wd_hbm, ng_ref, nb_ref, tgt_hbm,
             gate_hbm, up_hbm, xb_ref, dr_hbm, loss_ref, dng_ref, dnb_ref,
             acc_sc, wg_buf, wu_buf, wd_buf, g_buf, u_buf, w_sem, o_sem, tgt_sc, dr_sc, t_sem) = refs
        else:
            (xh_ref, lg_ref, lb_ref, wgu_hbm, wd_hbm,
             gate_hbm, up_hbm, xb_ref, xho_ref, rstd_ref,
             acc_sc, wg_buf, wu_buf, wd_buf, g_buf, u_buf, w_sem, o_sem) = refs
        i = pl.program_id(0)
        rows = pl.ds(pl.multiple_of(i * tm, tm), tm)

        def cols(j, base=0):
            return pl.ds(pl.multiple_of(base + j * tn, LANES), tn)

        def w_copies(j, slot):
            return (pltpu.make_async_copy(wgu_hbm.at[:, cols(j)], wg_buf.at[slot], w_sem.at[slot]),
                    pltpu.make_async_copy(wgu_hbm.at[:, cols(j, F)], wu_buf.at[slot], w_sem.at[2 + slot]),
                    pltpu.make_async_copy(wd_hbm.at[cols(j), :], wd_buf.at[slot], w_sem.at[4 + slot]))

        def o_copies(j, slot):
            return (pltpu.make_async_copy(g_buf.at[slot], gate_hbm.at[rows, cols(j)], o_sem.at[slot]),
                    pltpu.make_async_copy(u_buf.at[slot], up_hbm.at[rows, cols(j)], o_sem.at[2 + slot]))

        def start(copies):
            for cp in copies:
                cp.start()

        def wait(copies):
            for cp in copies:
                cp.wait()

        def xin(rs):
            v = xh_ref[rs, :]
            return v * lg_ref[...] + lb_ref[...] if affine else v

        _when(i == 0, lambda: start(w_copies(0, 0)))
        if is_final:
            tgt_in = pltpu.make_async_copy(tgt_hbm.at[rows, :], tgt_sc, t_sem.at[0])
            dr_out = pltpu.make_async_copy(dr_sc, dr_hbm.at[rows, :], t_sem.at[1])
            tgt_in.start()
        for rs in _row_chunks(tm):
            xb_ref[rs, :] = xin(rs).astype(BF16)
        acc_sc[...] = jnp.zeros_like(acc_sc)

        def step(j, slot):
            _when(j + 1 < nj, lambda: start(w_copies(j + 1, 1 - slot)))
            wait(w_copies(j, slot))
            xb = xb_ref[...]
            g = _dot(xb, wg_buf[slot])
            u = _dot(xb, wu_buf[slot])
            _when(j >= 2, lambda: wait(o_copies(j - 2, slot)))
            g_buf[slot] = g.astype(BF16)
            u_buf[slot] = u.astype(BF16)
            start(o_copies(j, slot))
            h = g * _sig(g) * u
            acc_sc[...] += _dot(h.astype(BF16), wd_buf[slot])

        _hidden_loop(nj, step)
        _when(i + 1 < nt, lambda: start(w_copies(0, 0)))
        for j in range(max(nj - 2, 0), nj):
            wait(o_copies(j, j % 2))

        if is_final:
            @pl.when(i == 0)
            def _():
                loss_ref[...] = jnp.zeros_like(loss_ref)
                dng_ref[...] = jnp.zeros_like(dng_ref)
                dnb_ref[...] = jnp.zeros_like(dnb_ref)
            tgt_in.wait()
            _when(i > 0, dr_out.wait)
        for rs in _row_chunks(tm):
            r = ALPHA * xin(rs) + 0.5 * acc_sc[rs, :]
            xho, rstd = _ln_stats(r)
            if not is_final:
                xho_ref[rs, :] = xho
                rstd_ref[rs, :] = jnp.broadcast_to(rstd, (rs.stop - rs.start, LANES))
            else:
                ng = ng_ref[...]
                e = xho * ng + nb_ref[...] - tgt_sc[rs, :]
                part = _colsum(jnp.sum(e * e, axis=1, keepdims=True)) * (0.5 / D)
                loss_ref[...] += jnp.broadcast_to(part, loss_ref.shape)
                dy = e * (1.0 / D)
                dng_ref[...] += _colsum(dy * xho)
                dnb_ref[...] += _colsum(dy)
                dr_sc[rs, :] = _ln_bwd(dy, xho, rstd, ng)
        if is_final:
            dr_out.start()
            _when(i == nt - 1, dr_out.wait)

    row = pl.BlockSpec((tm, D), lambda i: (i, 0))
    vec = pl.BlockSpec((1, D), lambda i: (0, 0))
    in_specs = [row, vec, vec, ANY, ANY]
    args = [xh, lg, lb, wgu, wd]
    out_shape = [jax.ShapeDtypeStruct((T, F), BF16), jax.ShapeDtypeStruct((T, F), BF16),
                 jax.ShapeDtypeStruct((T, D), BF16)]
    out_specs = [ANY, ANY, row]
    scratch = [pltpu.VMEM((tm, D), F32),
               pltpu.VMEM((2, D, tn), BF16), pltpu.VMEM((2, D, tn), BF16), pltpu.VMEM((2, tn, D), BF16),
               pltpu.VMEM((2, tm, tn), BF16), pltpu.VMEM((2, tm, tn), BF16),
               pltpu.SemaphoreType.DMA((6,)), pltpu.SemaphoreType.DMA((4,))]
    if is_final:
        in_specs += [vec, vec, ANY]
        args += list(final)
        out_shape += [jax.ShapeDtypeStruct((T, D), F32), jax.ShapeDtypeStruct((8, LANES), F32),
                      jax.ShapeDtypeStruct((1, D), F32), jax.ShapeDtypeStruct((1, D), F32)]
        out_specs += [ANY, pl.BlockSpec((8, LANES), lambda i: (0, 0)), vec, vec]
        scratch += [pltpu.VMEM((tm, D), F32), pltpu.VMEM((tm, D), F32), pltpu.SemaphoreType.DMA((2,))]
    else:
        out_shape += [jax.ShapeDtypeStruct((T, D), F32), jax.ShapeDtypeStruct((T, LANES), F32)]
        out_specs += [row, pl.BlockSpec((tm, LANES), lambda i: (i, 0))]
    return _call(body, name=name, grid=(nt,), in_specs=in_specs, out_specs=out_specs, out_shape=out_shape,
                 scratch_shapes=scratch, sem=("arbitrary",), args=args, carry=carry)


def _ffn_bwd_looped(dr, gate, up, wgu, wd, *, name, prev=None, carry=None):
    T, D = dr.shape
    F = wd.shape[0]
    tm, tn = _ffn_tiles(T, F)
    nj, nt = F // tn, T // tm
    has_prev = prev is not None

    def body(*refs):
        if has_prev:
            (dr_ref, gate_hbm, up_hbm, wgu_hbm, wd_hbm, xh_hbm, rstd_ref, lg_ref,
             h_hbm, dg_hbm, du_hbm, dprev_hbm, dlg_ref, dlb_ref, *scr) = refs
            xh_sc = scr.pop()
        else:
            (dr_ref, gate_hbm, up_hbm, wgu_hbm, wd_hbm,
             h_hbm, dg_hbm, du_hbm, dprev_hbm, *scr) = refs
        (df_sc, dx_sc, wg_buf, wu_buf, wd_buf, gi_buf, ui_buf, h_buf, dg_buf, du_buf, i_sem, o_sem,
         dp_sc, t_sem) = scr
        i = pl.program_id(0)
        rows = pl.ds(pl.multiple_of(i * tm, tm), tm)
        dp_out = pltpu.make_async_copy(dp_sc, dprev_hbm.at[rows, :], t_sem.at[0])
        if has_prev:
            xh_in = pltpu.make_async_copy(xh_hbm.at[rows, :], xh_sc, t_sem.at[1])
            xh_in.start()

        def cols(j, base=0):
            return pl.ds(pl.multiple_of(base + j * tn, LANES), tn)

        def i_copies(j, slot, tile=None):
            at = rows if tile is None else pl.ds(pl.multiple_of(tile * tm, tm), tm)
            return (pltpu.make_async_copy(wgu_hbm.at[:, cols(j)], wg_buf.at[slot], i_sem.at[slot]),
                    pltpu.make_async_copy(wgu_hbm.at[:, cols(j, F)], wu_buf.at[slot], i_sem.at[2 + slot]),
                    pltpu.make_async_copy(wd_hbm.at[cols(j), :], wd_buf.at[slot], i_sem.at[4 + slot]),
                    pltpu.make_async_copy(gate_hbm.at[at, cols(j)], gi_buf.at[slot], i_sem.at[6 + slot]),
                    pltpu.make_async_copy(up_hbm.at[at, cols(j)], ui_buf.at[slot], i_sem.at[8 + slot]))

        def o_copies(j, slot):
            return (pltpu.make_async_copy(h_buf.at[slot], h_hbm.at[rows, cols(j)], o_sem.at[slot]),
                    pltpu.make_async_copy(dg_buf.at[slot], dg_hbm.at[rows, cols(j)], o_sem.at[2 + slot]),
                    pltpu.make_async_copy(du_buf.at[slot], du_hbm.at[rows, cols(j)], o_sem.at[4 + slot]))

        def start(copies):
            for cp in copies:
                cp.start()

        def wait(copies):
            for cp in copies:
                cp.wait()

        _when(i == 0, lambda: start(i_copies(0, 0)))
        for rs in _row_chunks(tm):
            d = dr_ref[rs, :]
            df_sc[rs, :] = (0.5 * d).astype(BF16)
            dx_sc[rs, :] = ALPHA * d

        def step(j, slot):
            _when(j + 1 < nj, lambda: start(i_copies(j + 1, 1 - slot)))
            wait(i_copies(j, slot))
            g = gi_buf[slot].astype(F32)
            u = ui_buf[slot].astype(F32)
            dh = _dot_nt(df_sc[...], wd_buf[slot])
            s = _sig(g)
            sil = g * s
            dg = (dh * u * (s * (1.0 + g * (1.0 - s)))).astype(BF16)
            du = (dh * sil).astype(BF16)
            _when(j >= 2, lambda: wait(o_copies(j - 2, slot)))
            h_buf[slot] = (sil * u).astype(BF16)
            dg_buf[slot] = dg
            du_buf[slot] = du
            start(o_copies(j, slot))
            dx_sc[...] += _dot_nt(dg, wg_buf[slot])
            dx_sc[...] += _dot_nt(du, wu_buf[slot])

        _hidden_loop(nj, step)
        _when(i + 1 < nt, lambda: start(i_copies(0, 0, i + 1)))
        for j in range(max(nj - 2, 0), nj):
            wait(o_copies(j, j % 2))

        if has_prev:
            @pl.when(i == 0)
            def _():
                dlg_ref[...] = jnp.zeros_like(dlg_ref)
                dlb_ref[...] = jnp.zeros_like(dlb_ref)
            xh_in.wait()
        _when(i > 0, dp_out.wait)
        for rs in _row_chunks(tm):
            dxin = dx_sc[rs, :]
            if not has_prev:
                dp_sc[rs, :] = dxin
            else:
                x_hat = xh_sc[rs, :]
                dlg_ref[...] += _colsum(dxin * x_hat)
                dlb_ref[...] += _colsum(dxin)
                dp_sc[rs, :] = _ln_bwd(dxin, x_hat, rstd_ref[rs, 0:1], lg_ref[...])
        dp_out.start()
        _when(i == nt - 1, dp_out.wait)

    row = pl.BlockSpec((tm, D), lambda i: (i, 0))
    vec = pl.BlockSpec((1, D), lambda i: (0, 0))
    in_specs = [row, ANY, ANY, ANY, ANY]
    args = [dr, gate, up, wgu, wd]
    out_shape = [jax.ShapeDtypeStruct((T, F), BF16)] * 3 + [jax.ShapeDtypeStruct((T, D), F32)]
    out_specs = [ANY, ANY, ANY, ANY]
    scratch = [pltpu.VMEM((tm, D), BF16), pltpu.VMEM((tm, D), F32),
               pltpu.VMEM((2, D, tn), BF16), pltpu.VMEM((2, D, tn), BF16), pltpu.VMEM((2, tn, D), BF16)]
    scratch += [pltpu.VMEM((2, tm, tn), BF16)] * 5
    scratch += [pltpu.SemaphoreType.DMA((10,)), pltpu.SemaphoreType.DMA((6,)),
                pltpu.VMEM((tm, D), F32), pltpu.SemaphoreType.DMA((2,))]
    if has_prev:
        in_specs += [ANY, pl.BlockSpec((tm, LANES), lambda i: (i, 0)), vec]
        args += list(prev)
        out_shape += [jax.ShapeDtypeStruct((1, D), F32)] * 2
        out_specs += [vec, vec]
        scratch += [pltpu.VMEM((tm, D), F32)]
    return _call(body, name=name, grid=(nt,), in_specs=in_specs, out_specs=out_specs, out_shape=out_shape,
                 scratch_shapes=scratch, sem=("arbitrary",), args=args, carry=carry)


def _ffn_fwd(xh, lg, lb, wgu, wd, *, affine, name, final=None, carry=None):
    T, D = xh.shape
    F = wd.shape[0]
    tm = _tile(T, 512)
    tn = _tile(F, 512, LANES)
    nj = F // tn
    is_final = final is not None

    def body(*refs):
        if is_final:
            (xh_ref, lg_ref, lb_ref, wg_ref, wu_ref, wd_ref, ng_ref, nb_ref, tgt_ref,
             gate_ref, up_ref, xb_sc, dr_ref, loss_ref, dng_ref, dnb_ref, acc_sc) = refs
        else:
            (xh_ref, lg_ref, lb_ref, wg_ref, wu_ref, wd_ref,
             gate_ref, up_ref, xb_sc, xho_ref, rstd_ref, acc_sc) = refs
        i = pl.program_id(0)
        j = pl.program_id(1)

        def xin():
            v = xh_ref[...]
            return v * lg_ref[...] + lb_ref[...] if affine else v

        @pl.when(j == 0)
        def _():
            xb_sc[...] = xin().astype(BF16)
            acc_sc[...] = jnp.zeros_like(acc_sc)

        xb = xb_sc[...]
        g = _dot(xb, wg_ref[...])
        u = _dot(xb, wu_ref[...])
        gate_ref[...] = g.astype(BF16)
        up_ref[...] = u.astype(BF16)
        h = g * _sig(g) * u
        acc_sc[...] += _dot(h.astype(BF16), wd_ref[...])

        @pl.when(j == nj - 1)
        def _():
            r = ALPHA * xin() + 0.5 * acc_sc[...]
            xho, rstd = _ln_stats(r)
            if not is_final:
                xho_ref[...] = xho
                rstd_ref[...] = jnp.broadcast_to(rstd, (tm, LANES))
            else:
                @pl.when(i == 0)
                def _():
                    loss_ref[...] = jnp.zeros_like(loss_ref)
                    dng_ref[...] = jnp.zeros_like(dng_ref)
                    dnb_ref[...] = jnp.zeros_like(dnb_ref)
                ng = ng_ref[...]
                e = xho * ng + nb_ref[...] - tgt_ref[...]
                part = _colsum(jnp.sum(e * e, axis=1, keepdims=True)) * (0.5 / D)
                loss_ref[...] += jnp.broadcast_to(part, loss_ref.shape)
                dy = e * (1.0 / D)
                dng_ref[...] += _colsum(dy * xho)
                dnb_ref[...] += _colsum(dy)
                dr_ref[...] = _ln_bwd(dy, xho, rstd, ng)

    row = pl.BlockSpec((tm, D), lambda i, j: (i, 0))
    vec = pl.BlockSpec((1, D), lambda i, j: (0, 0))
    hid = pl.BlockSpec((tm, tn), lambda i, j: (i, j))
    in_specs = [row, vec, vec,
                pl.BlockSpec((D, tn), lambda i, j: (0, j)),
                pl.BlockSpec((D, tn), lambda i, j: (0, j + nj)),
                pl.BlockSpec((tn, D), lambda i, j: (j, 0))]
    args = [xh, lg, lb, wgu, wgu, wd]
    out_shape = [jax.ShapeDtypeStruct((T, F), BF16), jax.ShapeDtypeStruct((T, F), BF16),
                 jax.ShapeDtypeStruct((T, D), BF16)]
    out_specs = [hid, hid, row]
    if is_final:
        in_specs += [vec, vec, row]
        args += list(final)
        out_shape += [jax.ShapeDtypeStruct((T, D), F32), jax.ShapeDtypeStruct((8, LANES), F32),
                      jax.ShapeDtypeStruct((1, D), F32), jax.ShapeDtypeStruct((1, D), F32)]
        out_specs += [row, pl.BlockSpec((8, LANES), lambda i, j: (0, 0)), vec, vec]
        sem = ("arbitrary", "arbitrary")
    else:
        out_shape += [jax.ShapeDtypeStruct((T, D), F32), jax.ShapeDtypeStruct((T, LANES), F32)]
        out_specs += [row, pl.BlockSpec((tm, LANES), lambda i, j: (i, 0))]
        sem = ("parallel", "arbitrary")
    return _call(body, name=name, grid=(T // tm, nj), in_specs=in_specs, out_specs=out_specs, out_shape=out_shape,
                 scratch_shapes=[pltpu.VMEM((tm, D), F32)], sem=sem, args=args, carry=carry)


def _ffn_bwd(dr, gate, up, wgu, wd, *, name, prev=None, carry=None):
    T, D = dr.shape
    F = wd.shape[0]
    tm = _tile(T, 512)
    tn = _tile(F, 512, LANES)
    nj = F // tn
    has_prev = prev is not None

    def body(*refs):
        if has_prev:
            (dr_ref, gate_ref, up_ref, wd_ref, wg_ref, wu_ref, xh_ref, rstd_ref, lg_ref,
             h_ref, dg_ref, du_ref, dprev_ref, dlg_ref, dlb_ref, df_sc, dx_sc) = refs
        else:
            (dr_ref, gate_ref, up_ref, wd_ref, wg_ref, wu_ref,
             h_ref, dg_ref, du_ref, dprev_ref, df_sc, dx_sc) = refs
        i = pl.program_id(0)
        j = pl.program_id(1)

        @pl.when(j == 0)
        def _():
            d = dr_ref[...]
            df_sc[...] = (0.5 * d).astype(BF16)
            dx_sc[...] = ALPHA * d

        g = gate_ref[...].astype(F32)
        u = up_ref[...].astype(F32)
        dh = _dot_nt(df_sc[...], wd_ref[...])
        s = _sig(g)
        sil = g * s
        h_ref[...] = (sil * u).astype(BF16)
        dg = (dh * u * (s * (1.0 + g * (1.0 - s)))).astype(BF16)
        du = (dh * sil).astype(BF16)
        dg_ref[...] = dg
        du_ref[...] = du
        dx_sc[...] += _dot_nt(dg, wg_ref[...]) + _dot_nt(du, wu_ref[...])

        @pl.when(j == nj - 1)
        def _():
            dxin = dx_sc[...]
            if not has_prev:
                dprev_ref[...] = dxin
            else:
                @pl.when(i == 0)
                def _():
                    dlg_ref[...] = jnp.zeros_like(dlg_ref)
                    dlb_ref[...] = jnp.zeros_like(dlb_ref)
                xh = xh_ref[...]
                dlg_ref[...] += _colsum(dxin * xh)
                dlb_ref[...] += _colsum(dxin)
                dprev_ref[...] = _ln_bwd(dxin, xh, rstd_ref[:, 0:1], lg_ref[...])

    row = pl.BlockSpec((tm, D), lambda i, j: (i, 0))
    vec = pl.BlockSpec((1, D), lambda i, j: (0, 0))
    hid = pl.BlockSpec((tm, tn), lambda i, j: (i, j))
    in_specs = [row, hid, hid,
                pl.BlockSpec((tn, D), lambda i, j: (j, 0)),
                pl.BlockSpec((D, tn), lambda i, j: (0, j)),
                pl.BlockSpec((D, tn), lambda i, j: (0, j + nj))]
    args = [dr, gate, up, wd, wgu, wgu]
    out_shape = [jax.ShapeDtypeStruct((T, F), BF16)] * 3 + [jax.ShapeDtypeStruct((T, D), F32)]
    out_specs = [hid, hid, hid, row]
    if has_prev:
        in_specs += [row, pl.BlockSpec((tm, LANES), lambda i, j: (i, 0)), vec]
        args += list(prev)
        out_shape += [jax.ShapeDtypeStruct((1, D), F32)] * 2
        out_specs += [vec, vec]
        sem = ("arbitrary", "arbitrary")
    else:
        sem = ("parallel", "arbitrary")
    return _call(body, name=name, grid=(T // tm, nj), in_specs=in_specs, out_specs=out_specs, out_shape=out_shape,
                 scratch_shapes=[pltpu.VMEM((tm, D), BF16), pltpu.VMEM((tm, D), F32)], sem=sem, args=args,
                 carry=carry)


def _mm_tn(a, b, *, name, tm_pref, tn_pref, scale=1.0, a_affine=None, into=None, col_off=0, n_total=None,
           carry=None):
    T, M = a.shape
    N = b.shape[1]
    n_total = N if n_total is None else n_total
    tM = _tile(M, tm_pref, LANES)
    tN = _tile(N, tn_pref, LANES)
    tk = _tile(T, 1024)
    nt = T // tk
    assert col_off % tN == 0
    off_blocks = col_off // tN
    has_aff = a_affine is not None
    has_into = into is not None

    def body(*refs):
        refs = list(refs)
        a_ref = refs.pop(0)
        if has_aff:
            lg_ref = refs.pop(0)
            lb_ref = refs.pop(0)
        b_ref = refs.pop(0)
        if has_into:
            refs.pop(0)
        o_ref, acc_sc = refs
        t = pl.program_id(2)

        @pl.when(t == 0)
        def _():
            acc_sc[...] = jnp.zeros_like(acc_sc)

        av = a_ref[...]
        if has_aff:
            av = av * lg_ref[...] + lb_ref[...]
        acc_sc[...] += _dot_tn(av.astype(BF16), b_ref[...].astype(BF16))

        @pl.when(t == nt - 1)
        def _():
            o_ref[...] = (acc_sc[...] * scale).astype(BF16)

    in_specs = [pl.BlockSpec((tk, tM), lambda m, n, t: (t, m))]
    args = [a]
    if has_aff:
        in_specs += [pl.BlockSpec((1, tM), lambda m, n, t: (0, m))] * 2
        args += list(a_affine)
    in_specs.append(pl.BlockSpec((tk, tN), lambda m, n, t: (t, n)))
    args.append(b)
    aliases = {}
    if has_into:
        aliases = {len(args): 0}
        in_specs.append(ANY)
        args.append(into)
    res = _call(body, name=name, grid=(M // tM, N // tN, nt), in_specs=in_specs,
                out_specs=[pl.BlockSpec((tM, tN), lambda m, n, t: (m, n + off_blocks))],
                out_shape=[jax.ShapeDtypeStruct((M, n_total), BF16)],
                scratch_shapes=[pltpu.VMEM((tM, tN), F32)], aliases=aliases,
                sem=("parallel", "parallel", "arbitrary"), args=args, carry=carry)
    return res[0] if carry is None else (res[0][0], res[1])


def _inproj_fwd(xh, lg, lb, w, bias, *, name):
    T, D = xh.shape
    N = w.shape[1]
    tm = _tile(T, 1024)
    tn = _tile(N, 1024, LANES)

    def body(xh_ref, lg_ref, lb_ref, w_ref, b_ref, o_ref, xb_ref):
        @pl.when(pl.program_id(1) == 0)
        def _():
            xb_ref[...] = (xh_ref[...] * lg_ref[...] + lb_ref[...]).astype(BF16)
        o_ref[...] = _dot(xb_ref[...], w_ref[...]) + b_ref[...]

    return pl.pallas_call(
        body, name=name, grid=(T // tm, N // tn),
        in_specs=[pl.BlockSpec((tm, D), lambda i, j: (i, 0)),
                  pl.BlockSpec((1, D), lambda i, j: (0, 0)), pl.BlockSpec((1, D), lambda i, j: (0, 0)),
                  pl.BlockSpec((D, tn), lambda i, j: (0, j)), pl.BlockSpec((1, tn), lambda i, j: (0, j))],
        out_specs=[pl.BlockSpec((tm, tn), lambda i, j: (i, j)), pl.BlockSpec((tm, D), lambda i, j: (i, 0))],
        out_shape=[jax.ShapeDtypeStruct((T, N), F32), jax.ShapeDtypeStruct((T, D), BF16)],
        compiler_params=_params("parallel", "arbitrary"))(xh, lg, lb, w, bias)


def _inproj_bwd(dproj, w, dr_next, xh, rstd, lg, *, name, carry=None):
    T, N = dproj.shape
    D = w.shape[0]
    tm = _tile(T, 512)
    tn = _tile(N, 1024, LANES)
    nj = N // tn

    def body(dp_ref, w_ref, drn_ref, xh_ref, rstd_ref, lg_ref, dprev_ref, dlg_ref, dlb_ref, dx_sc):
        i = pl.program_id(0)
        j = pl.program_id(1)

        @pl.when(j == 0)
        def _():
            dx_sc[...] = ALPHA * drn_ref[...]

        dx_sc[...] += _dot_nt(dp_ref[...], w_ref[...])

        @pl.when(j == nj - 1)
        def _():
            @pl.when(i == 0)
            def _():
                dlg_ref[...] = jnp.zeros_like(dlg_ref)
                dlb_ref[...] = jnp.zeros_like(dlb_ref)
            dx = dx_sc[...]
            x_hat = xh_ref[...]
            dlg_ref[...] += _colsum(dx * x_hat)
            dlb_ref[...] += _colsum(dx)
            dprev_ref[...] = _ln_bwd(dx, x_hat, rstd_ref[:, 0:1], lg_ref[...])

    row = pl.BlockSpec((tm, D), lambda i, j: (i, 0))
    vec = pl.BlockSpec((1, D), lambda i, j: (0, 0))
    return _call(
        body, name=name, grid=(T // tm, nj),
        in_specs=[pl.BlockSpec((tm, tn), lambda i, j: (i, j)), pl.BlockSpec((D, tn), lambda i, j: (0, j)),
                  row, row, pl.BlockSpec((tm, LANES), lambda i, j: (i, 0)), vec],
        out_specs=[row, vec, vec],
        out_shape=[jax.ShapeDtypeStruct((T, D), F32), jax.ShapeDtypeStruct((1, D), F32),
                   jax.ShapeDtypeStruct((1, D), F32)],
        scratch_shapes=[pltpu.VMEM((tm, D), F32)], sem=("arbitrary", "arbitrary"),
        args=[dproj, w, dr_next, xh, rstd, lg], carry=carry)


def _sgu_fwd(proj, ln_g, ln_b, w_s, b_sb, *, name):
    T = proj.shape[0]
    A = proj.shape[1] // 8
    hd = A // A_GROUPS
    tm = _tile(T, 256, GMLP_BLOCK)

    def body(u_ref, v_ref, g_ref, b_ref, ws_ref, bs_ref, o_ref):
        gu = _gelu(u_ref[...])
        vh, _ = _ln_stats(_gelu(v_ref[...]))
        vn = (vh * g_ref[...] + b_ref[...]).astype(BF16)
        mask = _chunk_mask(False)
        for h in range(A_GROUPS):
            wm = jnp.where(mask, ws_ref[h], 0.0).astype(BF16)
            cols = slice(h * hd, (h + 1) * hd)
            for n in range(tm // GMLP_BLOCK):
                rows = slice(n * GMLP_BLOCK, (n + 1) * GMLP_BLOCK)
                s = _dot(wm, vn[rows, cols]) + bs_ref[h][:, :hd]
                o_ref[rows, cols] = (gu[rows, cols] * s).astype(BF16)

    vec = pl.BlockSpec((1, A), lambda i: (0, 0))
    full = pl.BlockSpec((A_GROUPS, GMLP_BLOCK, GMLP_BLOCK), lambda i: (0, 0, 0))
    return pl.pallas_call(
        body, name=name, grid=(T // tm,),
        in_specs=[pl.BlockSpec((tm, A), lambda i: (i, 0)), pl.BlockSpec((tm, A), lambda i: (i, 1)),
                  vec, vec, full, full],
        out_specs=pl.BlockSpec((tm, A), lambda i: (i, 0)),
        out_shape=jax.ShapeDtypeStruct((T, A), BF16),
        compiler_params=_params("parallel"))(proj, proj, ln_g, ln_b, w_s, b_sb)


def _sgu_bwd(proj, dsg, ln_g, ln_b, w_s, w_st, b_sb, dproj, *, name):
    T = proj.shape[0]
    A = proj.shape[1] // 8
    hd = A // A_GROUPS
    tm = _tile(T, 256, GMLP_BLOCK)
    nt = T // tm

    def body(u_ref, v_ref, dsg_ref, g_ref, b_ref, ws_ref, wst_ref, bs_ref, _alias,
             dp_ref, dbin_ref, dlg_ref, dlb_ref, dws_ref, dbs_ref, dvn_sc, dgu_sc, dbs_sc):
        i = pl.program_id(0)

        @pl.when(i == 0)
        def _():
            dbin_ref[...] = jnp.zeros_like(dbin_ref)
            dlg_ref[...] = jnp.zeros_like(dlg_ref)
            dlb_ref[...] = jnp.zeros_like(dlb_ref)
            dws_ref[...] = jnp.zeros_like(dws_ref)
            dbs_sc[...] = jnp.zeros_like(dbs_sc)

        u = u_ref[...]
        v = v_ref[...]
        gu = _gelu(u)
        vh, rstd = _ln_stats(_gelu(v))
        gain = g_ref[...]
        vn = (vh * gain + b_ref[...]).astype(BF16)
        dsg_v = dsg_ref[...]
        mask = _chunk_mask(False)
        mask_t = _chunk_mask(True)
        for h in range(A_GROUPS):
            wm = jnp.where(mask, ws_ref[h], 0.0).astype(BF16)
            wmt = jnp.where(mask_t, wst_ref[h], 0.0).astype(BF16)
            cols = slice(h * hd, (h + 1) * hd)
            for n in range(tm // GMLP_BLOCK):
                rows = slice(n * GMLP_BLOCK, (n + 1) * GMLP_BLOCK)
                vb = vn[rows, cols]
                s = _dot(wm, vb) + bs_ref[h][:, :hd]
                d_out = dsg_v[rows, cols]
                dgu_sc[rows, cols] = d_out * s
                ds = d_out * gu[rows, cols]
                ds_b = ds.astype(BF16)
                dws_ref[h] += _dot_nt(ds_b, vb)
                dbs_sc[h] += ds
                dvn_sc[rows, cols] = _dot(wmt, ds_b)
        dvn = dvn_sc[...]
        dlg_ref[...] += _colsum(dvn * vh)
        dlb_ref[...] += _colsum(dvn)
        dv = _ln_bwd(dvn, vh, rstd, gain) * _gelu_grad(v)
        du = dgu_sc[...] * _gelu_grad(u)
        dp_ref[:, 0:A] = du.astype(BF16)
        dp_ref[:, A:2 * A] = dv.astype(BF16)
        dbin_ref[:, 0:A] += _colsum(du)
        dbin_ref[:, A:2 * A] += _colsum(dv)

        @pl.when(i == nt - 1)
        def _():
            for h in range(A_GROUPS):
                dws_ref[h] = jnp.where(mask, dws_ref[h], 0.0)
                dbs_ref[h:h + 1, :] = _colsum(dbs_sc[h].T)

    vec = pl.BlockSpec((1, A), lambda i: (0, 0))
    full = pl.BlockSpec((A_GROUPS, GMLP_BLOCK, GMLP_BLOCK), lambda i: (0, 0, 0))
    tile = pl.BlockSpec((tm, A), lambda i: (i, 0))
    return pl.pallas_call(
        body, name=name, grid=(nt,),
        in_specs=[tile, pl.BlockSpec((tm, A), lambda i: (i, 1)), tile, vec, vec, full, full, full, ANY],
        out_specs=[pl.BlockSpec((tm, 2 * A), lambda i: (i, 0)), pl.BlockSpec((1, 2 * A), lambda i: (0, 0)),
                   vec, vec, full, pl.BlockSpec((A_GROUPS, GMLP_BLOCK), lambda i: (0, 0))],
        out_shape=[jax.ShapeDtypeStruct(dproj.shape, BF16), jax.ShapeDtypeStruct((1, 2 * A), F32),
                   jax.ShapeDtypeStruct((1, A), F32), jax.ShapeDtypeStruct((1, A), F32),
                   jax.ShapeDtypeStruct((A_GROUPS, GMLP_BLOCK, GMLP_BLOCK), F32),
                   jax.ShapeDtypeStruct((A_GROUPS, GMLP_BLOCK), F32)],
        scratch_shapes=[pltpu.VMEM((tm, A), F32), pltpu.VMEM((tm, A), F32),
                        pltpu.VMEM((A_GROUPS, GMLP_BLOCK, hd), F32)],
        input_output_aliases={8: 0},
        compiler_params=_params("arbitrary"))(proj, proj, dsg, ln_g, ln_b, w_s, w_st, b_sb, dproj)


def _conv_tiles(T, B):
    tm = _tile(T, 256, CONV_ROWS)
    lb = min(LANES, B)
    return tm, tm // CONV_HALO, lb


def _fill_phases(src, dst, B, lb):
    rows = dst.shape[1]
    for p in range(1, 8):
        for cb in range(B // lb):
            ls = slice(cb * lb, (cb + 1) * lb)
            dst[p - 1, :, ls] = src[p:p + rows, ls]


def _shifted(src, phases, off, ls):
    m, p = divmod(off, 8)
    if p == 0:
        return src[off:off + CONV_ROWS, ls]
    return phases[p - 1, 8 * m:8 * m + CONV_ROWS, ls]


def _conv_fwd(proj, w_dw, b_dw, ln_g, ln_b, *, name):
    T = proj.shape[0]
    B = proj.shape[1] // 8
    tm, nh, lb = _conv_tiles(T, B)

    def body(ap_ref, gp_ref, a_ref, g_ref, w_ref, bdw_ref, lg_ref, lb_ref, c_ref, cv_ref, z_sc, zp_sc):
        i = pl.program_id(0)
        z_sc[0:CONV_HALO, :] = jnp.where(i > 0, ap_ref[...] * _sig(gp_ref[...]), 0.0)
        z_sc[CONV_HALO:CONV_HALO + tm, :] = a_ref[...] * _sig(g_ref[...])
        _fill_phases(z_sc, zp_sc, B, lb)
        for cb in range(B // lb):
            ls = slice(cb * lb, (cb + 1) * lb)
            for rc in range(tm // CONV_ROWS):
                acc = jnp.zeros((CONV_ROWS, lb), F32)
                for k in range(CONV_WIDTH):
                    off = rc * CONV_ROWS + CONV_HALO - (CONV_WIDTH - 1) + k
                    acc = acc + w_ref[k:k + 1, ls] * _shifted(z_sc, zp_sc, off, ls)
                c_ref[rc * CONV_ROWS:(rc + 1) * CONV_ROWS, ls] = acc + bdw_ref[:, ls]
        xh, _ = _ln_stats(c_ref[...])
        y = xh * lg_ref[...] + lb_ref[...]
        cv_ref[...] = (y * _sig(y)).astype(BF16)

    vec = pl.BlockSpec((1, B), lambda i: (0, 0))
    halo_a = pl.BlockSpec((CONV_HALO, B), lambda i: (jnp.maximum(i * nh - 1, 0), 2))
    halo_g = pl.BlockSpec((CONV_HALO, B), lambda i: (jnp.maximum(i * nh - 1, 0), 3))
    return pl.pallas_call(
        body, name=name, grid=(T // tm,),
        in_specs=[halo_a, halo_g, pl.BlockSpec((tm, B), lambda i: (i, 2)), pl.BlockSpec((tm, B), lambda i: (i, 3)),
                  pl.BlockSpec((CONV_WPAD, B), lambda i: (0, 0)), vec, vec, vec],
        out_specs=[pl.BlockSpec((tm, B), lambda i: (i, 0))] * 2,
        out_shape=[jax.ShapeDtypeStruct((T, B), F32), jax.ShapeDtypeStruct((T, B), BF16)],
        scratch_shapes=[pltpu.VMEM((CONV_HALO + tm, B), F32), pltpu.VMEM((7, CONV_HALO + tm - 8, B), F32)],
        compiler_params=_params("parallel"))(proj, proj, proj, proj, w_dw, b_dw, ln_g, ln_b)


def _conv_bwd_ln(dcv, c, ln_g, ln_b, *, name):
    T, B = c.shape
    tm = _tile(T, 512)

    def body(dcv_ref, c_ref, lg_ref, lb_ref, dc_ref, dlg_ref, dlb_ref, dbdw_ref):
        @pl.when(pl.program_id(0) == 0)
        def _():
            dlg_ref[...] = jnp.zeros_like(dlg_ref)
            dlb_ref[...] = jnp.zeros_like(dlb_ref)
            dbdw_ref[...] = jnp.zeros_like(dbdw_ref)
        gain = lg_ref[...]
        xh, rstd = _ln_stats(c_ref[...])
        y = xh * gain + lb_ref[...]
        s = _sig(y)
        dy = dcv_ref[...] * (s * (1.0 + y * (1.0 - s)))
        dlg_ref[...] += _colsum(dy * xh)
        dlb_ref[...] += _colsum(dy)
        dc = _ln_bwd(dy, xh, rstd, gain)
        dc_ref[...] = dc
        dbdw_ref[...] += _colsum(dc)

    tile = pl.BlockSpec((tm, B), lambda i: (i, 0))
    vec = pl.BlockSpec((1, B), lambda i: (0, 0))
    return pl.pallas_call(
        body, name=name, grid=(T // tm,), in_specs=[tile, tile, vec, vec], out_specs=[tile, vec, vec, vec],
        out_shape=[jax.ShapeDtypeStruct((T, B), F32)] + [jax.ShapeDtypeStruct((1, B), F32)] * 3,
        compiler_params=_params("arbitrary"))(dcv, c, ln_g, ln_b)


def _conv_bwd(proj, dc, w_dw, dproj, *, name, carry=None):
    T = proj.shape[0]
    B = proj.shape[1] // 8
    tm, nh, lb = _conv_tiles(T, B)
    nt = T // tm
    n_halo = T // CONV_HALO

    def body(ap_ref, gp_ref, a_ref, g_ref, dc_ref, dcn_ref, w_ref, _alias,
             dp_ref, dbin_ref, dw_ref, z_sc, dc_sc, dz_sc, dw_sc, zp_sc, dcp_sc):
        i = pl.program_id(0)

        @pl.when(i == 0)
        def _():
            dbin_ref[...] = jnp.zeros_like(dbin_ref)
            dw_sc[...] = jnp.zeros_like(dw_sc)

        a = a_ref[...]
        s = _sig(g_ref[...])
        z_sc[0:CONV_HALO, :] = jnp.where(i > 0, ap_ref[...] * _sig(gp_ref[...]), 0.0)
        z_sc[CONV_HALO:CONV_HALO + tm, :] = a * s
        dc_sc[0:tm, :] = dc_ref[...]
        dc_sc[tm:tm + CONV_HALO, :] = jnp.where(i < nt - 1, dcn_ref[...], 0.0)
        _fill_phases(z_sc, zp_sc, B, lb)
        _fill_phases(dc_sc, dcp_sc, B, lb)
        for cb in range(B // lb):
            ls = slice(cb * lb, (cb + 1) * lb)
            for rc in range(tm // CONV_ROWS):
                r0 = rc * CONV_ROWS
                acc = jnp.zeros((CONV_ROWS, lb), F32)
                for k in range(CONV_WIDTH):
                    acc = acc + w_ref[k:k + 1, ls] * _shifted(dc_sc, dcp_sc, r0 + (CONV_WIDTH - 1) - k, ls)
                dz_sc[r0:r0 + CONV_ROWS, ls] = acc
            for k in range(CONV_WIDTH):
                part = jnp.zeros((8, lb), F32)
                for rc in range(tm // CONV_ROWS):
                    r0 = rc * CONV_ROWS
                    prod = dc_sc[r0:r0 + CONV_ROWS, ls] * _shifted(
                        z_sc, zp_sc, r0 + CONV_HALO - (CONV_WIDTH - 1) + k, ls)
                    part = part + jnp.sum(prod.reshape(CONV_ROWS // 8, 8, lb), axis=0)
                dw_sc[8 * k:8 * k + 8, ls] += part
        dz = dz_sc[...]
        da = dz * s
        dg = dz * a * s * (1.0 - s)
        dp_ref[:, 0:B] = da.astype(BF16)
        dp_ref[:, B:2 * B] = dg.astype(BF16)
        dbin_ref[:, 0:B] += _colsum(da)
        dbin_ref[:, B:2 * B] += _colsum(dg)

        @pl.when(i == nt - 1)
        def _():
            for k in range(CONV_WIDTH):
                dw_ref[k:k + 1, :] = _colsum(dw_sc[8 * k:8 * k + 8, :])

    halo_a = pl.BlockSpec((CONV_HALO, B), lambda i: (jnp.maximum(i * nh - 1, 0), 2))
    halo_g = pl.BlockSpec((CONV_HALO, B), lambda i: (jnp.maximum(i * nh - 1, 0), 3))
    halo_dc = pl.BlockSpec((CONV_HALO, B), lambda i: (jnp.minimum((i + 1) * nh, n_halo - 1), 0))
    return _call(
        body, name=name, grid=(nt,),
        in_specs=[halo_a, halo_g, pl.BlockSpec((tm, B), lambda i: (i, 2)), pl.BlockSpec((tm, B), lambda i: (i, 3)),
                  pl.BlockSpec((tm, B), lambda i: (i, 0)), halo_dc,
                  pl.BlockSpec((CONV_WPAD, B), lambda i: (0, 0)), ANY],
        out_specs=[pl.BlockSpec((tm, 2 * B), lambda i: (i, 1)), pl.BlockSpec((1, 2 * B), lambda i: (0, 0)),
                   pl.BlockSpec((CONV_WIDTH, B), lambda i: (0, 0))],
        out_shape=[jax.ShapeDtypeStruct(dproj.shape, BF16), jax.ShapeDtypeStruct((1, 2 * B), F32),
                   jax.ShapeDtypeStruct((CONV_WIDTH, B), F32)],
        scratch_shapes=[pltpu.VMEM((CONV_HALO + tm, B), F32), pltpu.VMEM((tm + CONV_HALO, B), F32),
                        pltpu.VMEM((tm, B), F32), pltpu.VMEM((8 * CONV_WIDTH, B), F32),
                        pltpu.VMEM((7, CONV_HALO + tm - 8, B), F32), pltpu.VMEM((7, CONV_HALO + tm - 8, B), F32)],
        aliases={7: 0}, sem=("arbitrary",), args=[proj, proj, proj, proj, dc, dc, w_dw, dproj], carry=carry)


def _mix_fwd_gate(sg, cv, proj, wa, wb, *, name):
    T, A = sg.shape
    D = wa.shape[1]
    tm = _tile(T, 256)

    def body(sg_ref, cv_ref, la_ref, lb_ref, wa_ref, wb_ref, ya_ref, yb_ref, m_ref):
        ya = _dot(sg_ref[...], wa_ref[...])
        yb = _dot(cv_ref[...], wb_ref[...])
        ya_ref[...] = ya.astype(BF16)
        yb_ref[...] = yb.astype(BF16)
        m_ref[...] = (_sig(la_ref[...]) * ya + _sig(lb_ref[...]) * yb).astype(BF16)

    act = pl.BlockSpec((tm, A), lambda i: (i, 0))
    wide = pl.BlockSpec((tm, D), lambda i: (i, 0))
    wspec = pl.BlockSpec((A, D), lambda i: (0, 0))
    return pl.pallas_call(
        body, name=name, grid=(T // tm,),
        in_specs=[act, act, pl.BlockSpec((tm, D), lambda i: (i, 2)), pl.BlockSpec((tm, D), lambda i: (i, 3)),
                  wspec, wspec],
        out_specs=[wide] * 3, out_shape=[jax.ShapeDtypeStruct((T, D), BF16)] * 3,
        compiler_params=_params("parallel"))(sg, cv, proj, proj, wa, wb)


def _mix_fwd_out(m, wout, xh, lg, lb, *, name):
    T, D = xh.shape
    tm = _tile(T, 512)

    def body(m_ref, w_ref, xh_ref, lg_ref, lb_ref, xho_ref, rstd_ref):
        r = ALPHA * (xh_ref[...] * lg_ref[...] + lb_ref[...]) + _dot(m_ref[...], w_ref[...])
        xho, rstd = _ln_stats(r)
        xho_ref[...] = xho
        rstd_ref[...] = jnp.broadcast_to(rstd, (tm, LANES))

    row = pl.BlockSpec((tm, D), lambda i: (i, 0))
    vec = pl.BlockSpec((1, D), lambda i: (0, 0))
    return pl.pallas_call(
        body, name=name, grid=(T // tm,),
        in_specs=[row, pl.BlockSpec((D, D), lambda i: (0, 0)), row, vec, vec],
        out_specs=[row, pl.BlockSpec((tm, LANES), lambda i: (i, 0))],
        out_shape=[jax.ShapeDtypeStruct((T, D), F32), jax.ShapeDtypeStruct((T, LANES), F32)],
        compiler_params=_params("parallel"))(m, wout, xh, lg, lb)


def _mix_bwd_gate(dr, wout, proj, ya, yb, *, name, carry=None):
    T, D = dr.shape
    N = proj.shape[1]
    tm = _tile(T, 256)

    def body(dr_ref, w_ref, la_ref, lb_ref, ya_ref, yb_ref, dya_ref, dyb_ref, dp_ref, dbin_ref):
        @pl.when(pl.program_id(0) == 0)
        def _():
            dbin_ref[...] = jnp.zeros_like(dbin_ref)
        dm = _dot_nt(dr_ref[...].astype(BF16), w_ref[...])
        sa = _sig(la_ref[...])
        sb = _sig(lb_ref[...])
        dya_ref[...] = (dm * sa).astype(BF16)
        dyb_ref[...] = (dm * sb).astype(BF16)
        dla = dm * ya_ref[...].astype(F32) * sa * (1.0 - sa)
        dlb = dm * yb_ref[...].astype(F32) * sb * (1.0 - sb)
        dp_ref[:, 0:D] = dla.astype(BF16)
        dp_ref[:, D:2 * D] = dlb.astype(BF16)
        dbin_ref[:, 0:D] += _colsum(dla)
        dbin_ref[:, D:2 * D] += _colsum(dlb)

    row = pl.BlockSpec((tm, D), lambda i: (i, 0))
    return _call(
        body, name=name, grid=(T // tm,),
        in_specs=[row, pl.BlockSpec((D, D), lambda i: (0, 0)), pl.BlockSpec((tm, D), lambda i: (i, 2)),
                  pl.BlockSpec((tm, D), lambda i: (i, 3)), row, row],
        out_specs=[row, row, pl.BlockSpec((tm, 2 * D), lambda i: (i, 1)), pl.BlockSpec((1, 2 * D), lambda i: (0, 0))],
        out_shape=[jax.ShapeDtypeStruct((T, D), BF16), jax.ShapeDtypeStruct((T, D), BF16),
                   jax.ShapeDtypeStruct((T, N), BF16), jax.ShapeDtypeStruct((1, 2 * D), F32)],
        sem=("arbitrary",), args=[dr, wout, proj, proj, ya, yb], carry=carry)


def _mix_bwd_proj(dya, dyb, wa, wb, *, name):
    T, D = dya.shape
    A = wa.shape[0]
    tm = _tile(T, 512)

    def body(dya_ref, dyb_ref, wa_ref, wb_ref, dsg_ref, dcv_ref):
        dsg_ref[...] = _dot_nt(dya_ref[...], wa_ref[...])
        dcv_ref[...] = _dot_nt(dyb_ref[...], wb_ref[...])

    row = pl.BlockSpec((tm, D), lambda i: (i, 0))
    wspec = pl.BlockSpec((A, D), lambda i: (0, 0))
    act = pl.BlockSpec((tm, A), lambda i: (i, 0))
    return pl.pallas_call(
        body, name=name, grid=(T // tm,), in_specs=[row, row, wspec, wspec], out_specs=[act, act],
        out_shape=[jax.ShapeDtypeStruct((T, A), F32)] * 2,
        compiler_params=_params("parallel"))(dya, dyb, wa, wb)


def _mesh_pos():
    return lax.axis_index("x"), lax.axis_index("y"), lax.axis_index("c")


def _shard_view(ref, p, shape, axis):
    r, c = shape
    if axis == 0:
        return ref.at[pl.ds(pl.multiple_of(p * r, 16), r), :]
    return ref.at[:, pl.ds(pl.multiple_of(p * c, LANES), c)]


def _all_gather(shards, axes, progressive=False):
    n = len(shards)
    shapes = [s.shape for s in shards]
    sizes = [s.size * s.dtype.itemsize for s in shards]
    if progressive:
        done = [sum(sizes[:t + 1]) / sum(sizes) for t in range(n)]
        when = (0.0,) + tuple(min(0.97, 0.04 + 0.93 * d) for d in done) + (1.0,)
    else:
        when = (0.0, 1.0)

    def run(ins, outs, sems, phase):
        send_sems, recv_sems, local_sems = sems
        x, y, c = _mesh_pos()
        me, sibling = (x, y, c), (x, y, 1 - c)
        chips = [(1 - x, y), (x, 1 - y), (1 - x, 1 - y)]

        def view(t, pos):
            px, py, pc = pos
            return _shard_view(outs[t], 4 * px + 2 * py + pc, shapes[t], axes[t])

        def copy(t, k, block, to, src=None):
            return pltpu.make_async_remote_copy(
                src_ref=view(t, block) if src is None else src, dst_ref=view(t, block),
                send_sem=send_sems.at[7 * t + k], recv_sem=recv_sems.at[7 * t + k],
                device_id=to, device_id_type=MESH)

        mine = [pltpu.make_async_copy(ins[t], view(t, me), local_sems.at[t]) for t in range(n)]
        first = []
        for t in range(n):
            first.append(copy(t, 0, me, sibling, src=ins[t]))
            first += [copy(t, 1 + j, me, (*chip, c), src=ins[t]) for j, chip in enumerate(chips)]
        if phase == 0:
            for cp in mine + first:
                cp.start()
            return

        def forward(t):
            for j, chip in enumerate(chips):
                copy(t, 1 + j, (*chip, c), me).wait_recv()
                copy(t, 4 + j, (*chip, c), sibling).start()

        if progressive and phase <= n:
            forward(phase - 1)
            return
        if not progressive:
            for t in range(n):
                forward(t)
        passed = [copy(t, 4 + j, (*chip, c), sibling) for t in range(n) for j, chip in enumerate(chips)]
        for t in range(n):
            copy(t, 0, sibling, me).wait_recv()
            for j, chip in enumerate(chips):
                copy(t, 4 + j, (*chip, 1 - c), me).wait_recv()
        for cp in first + passed:
            cp.wait_send()
        for cp in mine:
            cp.wait()

    out_shape = [jax.ShapeDtypeStruct((N_DEV * s.shape[0], s.shape[1]) if ax == 0
                                      else (s.shape[0], N_DEV * s.shape[1]), s.dtype)
                 for s, ax in zip(shards, axes)]
    return _Comm(shards, out_shape, [pltpu.SemaphoreType.DMA((7 * n,)), pltpu.SemaphoreType.DMA((7 * n,)),
                                     pltpu.SemaphoreType.DMA((n,))], run, when)


def _rs_to_sibling(grads, shapes, axes):
    n = len(grads)

    def run(gs, outs, sems, phase):
        send_sems, recv_sems = sems
        x, y, c = _mesh_pos()
        copies = [pltpu.make_async_remote_copy(
            src_ref=_shard_view(gs[t], 2 * k + (1 - c), shapes[t], axes[t]), dst_ref=outs[t].at[k],
            send_sem=send_sems.at[4 * t + k], recv_sem=recv_sems.at[4 * t + k],
            device_id=(x, y, 1 - c), device_id_type=MESH) for t in range(n) for k in range(4)]
        if phase == 0:
            for cp in copies:
                cp.start()
            return
        for cp in copies:
            cp.wait_recv()
        for cp in copies:
            cp.wait_send()

    return _Comm(grads, [jax.ShapeDtypeStruct((4,) + tuple(s), BF16) for s in shapes],
                 [pltpu.SemaphoreType.DMA((4 * n,)), pltpu.SemaphoreType.DMA((4 * n,))], run)


def _rs_pair_sum(g, recv, cidx, shape, axis, *, name):
    r, c = shape
    tr = _tile(r, max(8, (1 << 20) // c), 16)
    nr = r // tr

    def body(c_ref, g_ref, rv_ref, o_ref):
        o_ref[...] = (g_ref[...].astype(F32) + rv_ref[...].astype(F32)).astype(BF16)

    if axis == 1:
        g_spec = pl.BlockSpec((tr, c), lambda k, i, s: (i, 2 * k + s[0]))
    else:
        g_spec = pl.BlockSpec((tr, c), lambda k, i, s: ((2 * k + s[0]) * nr + i, 0))
    blk = pl.BlockSpec((None, tr, c), lambda k, i, s: (k, i, 0))
    return pl.pallas_call(
        body, name=name,
        grid_spec=pltpu.PrefetchScalarGridSpec(num_scalar_prefetch=1, grid=(4, nr), in_specs=[g_spec, blk],
                                               out_specs=blk),
        out_shape=jax.ShapeDtypeStruct((4, r, c), BF16),
        compiler_params=_params("parallel", "parallel"))(cidx, g, recv)


def _rs_to_chips(parts):
    n = len(parts)

    def run(ps, outs, sems, phase):
        send_sems, recv_sems, local_sems = sems
        x, y, c = _mesh_pos()
        my_chip = 2 * x + y
        peers = [(1 - x, y), (x, 1 - y), (1 - x, 1 - y)]
        local = [pltpu.make_async_copy(ps[t].at[my_chip], outs[t].at[my_chip], local_sems.at[t]) for t in range(n)]
        sends = [pltpu.make_async_remote_copy(
            src_ref=ps[t].at[2 * px + py], dst_ref=outs[t].at[my_chip],
            send_sem=send_sems.at[3 * t + j], recv_sem=recv_sems.at[3 * t + j],
            device_id=(px, py, c), device_id_type=MESH) for t in range(n) for j, (px, py) in enumerate(peers)]
        if phase == 0:
            for cp in local + sends:
                cp.start()
            return
        for t in range(n):
            for j, (px, py) in enumerate(peers):
                pltpu.make_async_remote_copy(
                    src_ref=ps[t].at[2 * px + py], dst_ref=outs[t].at[2 * px + py],
                    send_sem=send_sems.at[3 * t + j], recv_sem=recv_sems.at[3 * t + j],
                    device_id=(x, y, c), device_id_type=MESH).wait_recv()
        for cp in sends:
            cp.wait_send()
        for cp in local:
            cp.wait()

    return _Comm(parts, [jax.ShapeDtypeStruct(p.shape, BF16) for p in parts],
                 [pltpu.SemaphoreType.DMA((3 * n,)), pltpu.SemaphoreType.DMA((3 * n,)),
                  pltpu.SemaphoreType.DMA((n,))], run)


def _exchange_small(buf):
    def run(ins, outs, sems, phase):
        (in_ref,), (slots,) = ins, outs
        send_sems, recv_sems, local_sem = sems
        x, y, c = _mesh_pos()
        me = 4 * x + 2 * y + c
        local = pltpu.make_async_copy(in_ref, slots.at[me], local_sem.at[0])
        flips = [(fx, fy, fc) for fx in (0, 1) for fy in (0, 1) for fc in (0, 1)][1:]
        peers = [(1 - x if fx else x, 1 - y if fy else y, 1 - c if fc else c) for fx, fy, fc in flips]
        sends = [pltpu.make_async_remote_copy(src_ref=in_ref, dst_ref=slots.at[me], send_sem=send_sems.at[k],
                                              recv_sem=recv_sems.at[k], device_id=peer, device_id_type=MESH)
                 for k, peer in enumerate(peers)]
        if phase == 0:
            for cp in [local] + sends:
                cp.start()
            return
        for k, (px, py, pc) in enumerate(peers):
            pltpu.make_async_remote_copy(src_ref=in_ref, dst_ref=slots.at[4 * px + 2 * py + pc],
                                         send_sem=send_sems.at[k], recv_sem=recv_sems.at[k],
                                         device_id=(x, y, c), device_id_type=MESH).wait_recv()
        for cp in sends:
            cp.wait_send()
        local.wait()

    return _Comm([buf], [jax.ShapeDtypeStruct((N_DEV,) + buf.shape, F32)],
                 [pltpu.SemaphoreType.DMA((7,)), pltpu.SemaphoreType.DMA((7,)), pltpu.SemaphoreType.DMA((1,))], run)


def _sum_slots(slots, *, name):
    _, R, C = slots.shape
    tr = _tile(R, 512)

    def body(s_ref, o_ref):
        acc = s_ref[0]
        for p in range(1, N_DEV):
            acc = acc + s_ref[p]
        o_ref[...] = acc

    return pl.pallas_call(
        body, name=name, grid=(R // tr,), in_specs=[pl.BlockSpec((N_DEV, tr, C), lambda i: (0, i, 0))],
        out_specs=pl.BlockSpec((tr, C), lambda i: (i, 0)), out_shape=jax.ShapeDtypeStruct((R, C), F32),
        compiler_params=_params("parallel"))(slots)


def _adam_math(g, w, m, v):
    m_new = ADAM_B1 * m + (1.0 - ADAM_B1) * g
    v_new = ADAM_B2 * v + (1.0 - ADAM_B2) * (g * g)
    m_hat = m_new / ADAM_C1
    v_hat = v_new / ADAM_C2
    delta = -ADAM_LR * (m_hat / (jnp.sqrt(v_hat) + ADAM_EPS) + ADAM_WD * w)
    return delta, m_new, v_new


def _adamw_sharded(q, w, m, v, *, name):
    r, c = w.shape
    tr = _tile(r, max(8, (1 << 18) // c), 16)

    def body(q_ref, w_ref, m_ref, v_ref, g_ref, d_ref, mo_ref, vo_ref):
        g = ((q_ref[0].astype(F32) + q_ref[1].astype(F32)) + q_ref[2].astype(F32)) + q_ref[3].astype(F32)
        g_ref[...] = g
        d_ref[...], mo_ref[...], vo_ref[...] = _adam_math(g, w_ref[...], m_ref[...], v_ref[...])

    blk = pl.BlockSpec((tr, c), lambda i: (i, 0))
    return pl.pallas_call(
        body, name=name, grid=(r // tr,),
        in_specs=[pl.BlockSpec((4, tr, c), lambda i: (0, i, 0)), blk, blk, blk], out_specs=[blk] * 4,
        out_shape=[jax.ShapeDtypeStruct((r, c), F32)] * 4,
        compiler_params=_params("parallel"))(q, w, m, v)


def _adamw_plain(g, w, m, v, *, name):
    r, c = w.shape
    tr = _tile(r, 512)

    def body(g_ref, w_ref, m_ref, v_ref, d_ref, mo_ref, vo_ref):
        d_ref[...], mo_ref[...], vo_ref[...] = _adam_math(g_ref[...], w_ref[...], m_ref[...], v_ref[...])

    blk = pl.BlockSpec((tr, c), lambda i: (i, 0))
    return pl.pallas_call(
        body, name=name, grid=(r // tr,), in_specs=[blk] * 4, out_specs=[blk] * 3,
        out_shape=[jax.ShapeDtypeStruct((r, c), F32)] * 3,
        compiler_params=_params("parallel"))(g, w, m, v)


def _pack_rows(arrays):
    return jnp.concatenate([a.reshape(-1, LANES) for a in arrays], axis=0)


def kernel(x, ffn1_w_gu, ffn1_w_down, ln1_g, ln1_b, w_in, b_in, sgu_ln_g, sgu_ln_b, sgu_w_s, sgu_b_s, w_a_proj, conv_w_dw, conv_b_dw, conv_ln_g, conv_ln_b, w_b_proj, w_out, ln2_g, ln2_b, ffn2_w_gu, ffn2_w_down, ln3_g, ln3_b, loss_target, m_ffn1_w_gu, m_ffn1_w_down, m_ln1_g, m_ln1_b, m_w_in, m_b_in, m_sgu_ln_g, m_sgu_ln_b, m_sgu_w_s, m_sgu_b_s, m_w_a_proj, m_conv_w_dw, m_conv_b_dw, m_conv_ln_g, m_conv_ln_b, m_w_b_proj, m_w_out, m_ln2_g, m_ln2_b, m_ffn2_w_gu, m_ffn2_w_down, m_ln3_g, m_ln3_b, v_ffn1_w_gu, v_ffn1_w_down, v_ln1_g, v_ln1_b, v_w_in, v_b_in, v_sgu_ln_g, v_sgu_ln_b, v_sgu_w_s, v_sgu_b_s, v_w_a_proj, v_conv_w_dw, v_conv_b_dw, v_conv_ln_g, v_conv_ln_b, v_w_b_proj, v_w_out, v_ln2_g, v_ln2_b, v_ffn2_w_gu, v_ffn2_w_down, v_ln3_g, v_ln3_b):
    given = dict(locals())
    w = {n: given[n][0] for n in WEIGHTS}
    mom = {n: given["m_" + n][0] for n in WEIGHTS}
    var = {n: given["v_" + n][0] for n in WEIGHTS}
    xt = x[0]
    target = loss_target[0]
    T, D = xt.shape
    A = w['w_a_proj'].shape[0]

    big_names = list(BIG)
    early = ['ffn1_w_gu', 'ffn1_w_down']
    late = [n for n in big_names if n not in early]
    conv_w_pad = jnp.pad(w['conv_w_dw'], ((0, CONV_WPAD - CONV_WIDTH), (0, 0)))
    w_bf = {n: w[n].astype(BF16) for n in big_names}
    full = dict(zip(early, _comm_call(_all_gather([w_bf[n] for n in early], [BIG[n] for n in early]),
                                      name="all_gather_ffn1")))
    gather_late = _all_gather([w_bf[n] for n in late] + [conv_w_pad], [BIG[n] for n in late] + [1],
                              progressive=True)

    def row(v):
        return v.reshape(1, -1)

    ones = jnp.ones((1, D), F32)
    zeros = jnp.zeros((1, D), F32)
    w_s = w['sgu_w_s']
    w_st = jnp.swapaxes(w_s, 1, 2)
    b_sb = jnp.broadcast_to(w['sgu_b_s'][:, :, None], w_s.shape)

    (gate1, up1, xb0, xh1, rstd1), gathered = _ffn_fwd(xt, ones, zeros, full['ffn1_w_gu'], full['ffn1_w_down'],
                                                  affine=False, name="ffn1_fwd", carry=gather_late)
    full.update(zip(late, gathered[:-1]))
    conv_w_full = gathered[-1]
    g1, b1 = row(w['ln1_g']), row(w['ln1_b'])
    proj, xb1 = _inproj_fwd(xh1, g1, b1, full['w_in'], row(w['b_in']), name="inproj_fwd")
    sg = _sgu_fwd(proj, row(w['sgu_ln_g']), row(w['sgu_ln_b']), w_s, b_sb, name="sgu_fwd")
    conv_out, cv = _conv_fwd(proj, conv_w_full, row(w['conv_b_dw']), row(w['conv_ln_g']), row(w['conv_ln_b']),
                             name="conv_fwd")
    ya, yb, mixed = _mix_fwd_gate(sg, cv, proj, full['w_a_proj'], full['w_b_proj'], name="mix_fwd_gate")
    xh2, rstd2 = _mix_fwd_out(mixed, full['w_out'], xh1, g1, b1, name="mix_fwd_out")
    g2, b2 = row(w['ln2_g']), row(w['ln2_b'])
    gate2, up2, xb2, dr3, loss_part, d_ln3_g, d_ln3_b = _ffn_fwd(
        xh2, g2, b2, full['ffn2_w_gu'], full['ffn2_w_down'], affine=True, name="ffn2_fwd_loss",
        final=(row(w['ln3_g']), row(w['ln3_b']), target))

    F = full['ffn2_w_down'].shape[0]
    h2, dgate2, dup2, dr2, d_ln2_g, d_ln2_b = _ffn_bwd(dr3, gate2, up2, full['ffn2_w_gu'], full['ffn2_w_down'],
                                                      name="ffn2_bwd", prev=(xh2, rstd2, g2))
    G, P, Q = {}, {}, {}
    cidx = lax.axis_index("c").astype(jnp.int32).reshape(1)

    def to_sibling(names):
        return _rs_to_sibling([G[n] for n in names], [w[n].shape for n in names], [BIG[n] for n in names])

    def pair_sum(names, received):
        for n, rv in zip(names, received):
            P[n] = _rs_pair_sum(G[n], rv, cidx, w[n].shape, BIG[n], name="rs_pair_sum_" + n)

    def to_chips(names):
        return _rs_to_chips([P[n] for n in names])

    G['ffn2_w_down'] = _mm_tn(h2, dr3, name="dw_ffn2_down", tm_pref=1408, tn_pref=2048, scale=0.5)
    gu, rv = _mm_tn(xb2, dgate2, name="dw_ffn2_gate", tm_pref=2048, tn_pref=1408,
                    n_total=2 * F, carry=to_sibling(['ffn2_w_down']))
    pair_sum(['ffn2_w_down'], rv)
    G['ffn2_w_gu'], q = _mm_tn(xb2, dup2, name="dw_ffn2_up", tm_pref=2048, tn_pref=1408,
                               into=gu, col_off=F, n_total=2 * F, carry=to_chips(['ffn2_w_down']))
    Q['ffn2_w_down'] = q[0]

    (dya, dyb, dproj, dbin_gate), rv = _mix_bwd_gate(dr2, full['w_out'], proj, ya, yb, name="mix_bwd_gate",
                                                     carry=to_sibling(['ffn2_w_gu']))
    pair_sum(['ffn2_w_gu'], rv)
    dsg, dcv = _mix_bwd_proj(dya, dyb, full['w_a_proj'], full['w_b_proj'], name="mix_bwd_proj")
    dproj, dbin_sgu, d_sgu_ln_g, d_sgu_ln_b, d_w_s, d_b_s = _sgu_bwd(
        proj, dsg, row(w['sgu_ln_g']), row(w['sgu_ln_b']), w_s, w_st, b_sb, dproj, name="sgu_bwd")
    dconv, d_conv_ln_g, d_conv_ln_b, d_conv_b = _conv_bwd_ln(dcv, conv_out, row(w['conv_ln_g']),
                                                            row(w['conv_ln_b']), name="conv_bwd_ln")
    (dproj, dbin_conv, d_conv_w), q = _conv_bwd(proj, dconv, conv_w_full, dproj, name="conv_bwd",
                                                carry=to_chips(['ffn2_w_gu']))
    Q['ffn2_w_gu'] = q[0]

    mid = ['w_out', 'w_a_proj', 'w_b_proj']
    G['w_out'] = _mm_tn(mixed, dr2, name="dw_out", tm_pref=2048, tn_pref=1024)
    G['w_a_proj'] = _mm_tn(sg, dya, name="dw_a_proj", tm_pref=1024, tn_pref=2048)
    G['w_b_proj'] = _mm_tn(cv, dyb, name="dw_b_proj", tm_pref=1024, tn_pref=2048)
    G['w_in'], rv = _mm_tn(xb1, dproj, name="dw_in", tm_pref=2048, tn_pref=1024, carry=to_sibling(mid))
    pair_sum(mid, rv)
    both = _join(to_chips(mid), to_sibling(['w_in']))
    (dr1, d_ln1_g, d_ln1_b), moved = _inproj_bwd(dproj, full['w_in'], dr2, xh1, rstd1, g1, name="inproj_bwd",
                                                 carry=both)
    q, rv = both.split(moved)
    Q.update(zip(mid, q))
    pair_sum(['w_in'], rv)

    h1, dgate1, dup1, grad_x = _ffn_bwd(dr1, gate1, up1, full['ffn1_w_gu'], full['ffn1_w_down'], name="ffn1_bwd")
    G['ffn1_w_down'], q = _mm_tn(h1, dr1, name="dw_ffn1_down", tm_pref=1408, tn_pref=2048, scale=0.5,
                                 carry=to_chips(['w_in']))
    Q['w_in'] = q[0]
    small_g = {'ln1_g': d_ln1_g, 'ln1_b': d_ln1_b,
               'b_in': jnp.concatenate([dbin_sgu, dbin_conv, dbin_gate], axis=1),
               'sgu_ln_g': d_sgu_ln_g, 'sgu_ln_b': d_sgu_ln_b, 'sgu_w_s': d_w_s, 'sgu_b_s': d_b_s,
               'conv_b_dw': d_conv_b, 'conv_ln_g': d_conv_ln_g, 'conv_ln_b': d_conv_ln_b,
               'ln2_g': d_ln2_g, 'ln2_b': d_ln2_b, 'ln3_g': d_ln3_g, 'ln3_b': d_ln3_b}
    packed = _pack_rows([small_g[n] for n in SMALL] + [d_conv_w, loss_part])
    both = _join(to_sibling(['ffn1_w_down']), _exchange_small(packed))
    gu, moved = _mm_tn(xb0, dgate1, name="dw_ffn1_gate", tm_pref=2048, tn_pref=1408, n_total=2 * F, carry=both)
    rv, slots = both.split(moved)
    pair_sum(['ffn1_w_down'], rv)
    reduced = _sum_slots(slots[0], name="sum_small")
    G['ffn1_w_gu'], q = _mm_tn(xb0, dup1, name="dw_ffn1_up", tm_pref=2048, tn_pref=1408, into=gu, col_off=F,
                               n_total=2 * F, carry=to_chips(['ffn1_w_down']))
    Q['ffn1_w_down'] = q[0]
    pair_sum(['ffn1_w_gu'], _comm_call(to_sibling(['ffn1_w_gu']), name="rs_to_sibling_last"))
    Q['ffn1_w_gu'] = _comm_call(to_chips(['ffn1_w_gu']), name="rs_to_chips_last")[0]

    grads, deltas, new_m, new_v = {}, {}, {}, {}
    for n in big_names:
        grads[n], deltas[n], new_m[n], new_v[n] = _adamw_sharded(Q[n], w[n], mom[n], var[n], name="adamw_" + n)

    B = conv_w_full.shape[1]
    n_small_rows = sum(w[n].size for n in SMALL) // LANES
    conv_rows = CONV_WIDTH * B // LANES
    d_small, m_small, v_small = _adamw_plain(
        reduced[:n_small_rows], _pack_rows([w[n] for n in SMALL]), _pack_rows([mom[n] for n in SMALL]),
        _pack_rows([var[n] for n in SMALL]), name="adamw_small")
    off = 0
    for n in SMALL:
        rows = w[n].size // LANES
        grads[n] = reduced[off:off + rows].reshape(w[n].shape)
        deltas[n] = d_small[off:off + rows].reshape(w[n].shape)
        new_m[n] = m_small[off:off + rows].reshape(w[n].shape)
        new_v[n] = v_small[off:off + rows].reshape(w[n].shape)
        off += rows
    conv_g_full = reduced[off:off + conv_rows].reshape(CONV_WIDTH, B)
    bs = w['conv_w_dw'].shape[1]
    my_block = 4 * lax.axis_index("x") + 2 * lax.axis_index("y") + lax.axis_index("c")
    grads['conv_w_dw'] = lax.dynamic_slice(conv_g_full, (0, my_block * bs), (CONV_WIDTH, bs))
    deltas['conv_w_dw'], new_m['conv_w_dw'], new_v['conv_w_dw'] = _adamw_plain(
        grads['conv_w_dw'], w['conv_w_dw'], mom['conv_w_dw'], var['conv_w_dw'], name="adamw_conv_w")
    loss = reduced[off + conv_rows, 0]

    def lead(a):
        return a[None]

    return (loss, grad_x[None], *[lead(grads[n]) for n in WEIGHTS], *[lead(deltas[n]) for n in WEIGHTS],
            *[lead(new_m[n]) for n in WEIGHTS], *[lead(new_v[n]) for n in WEIGHTS])
```

```python
import functools
import math

import jax
import jax.numpy as jnp
from jax import lax
from jax.experimental import pallas as pl
from jax.experimental.pallas import tpu as pltpu

F32 = jnp.float32
BF16 = jnp.bfloat16

ALPHA = 2.0 ** 0.25
LN_EPS = 1e-5
CONV_WIDTH = 31
CONV_HALO = 32
CONV_ROWS = 64
CONV_WPAD = 32
CHUNK = 64
GMLP_BLOCK = 128
A_GROUPS = 8
N_DEV = 8
LANES = 128

ADAM_LR = 0.001
ADAM_B1 = 0.9
ADAM_B2 = 0.999
ADAM_EPS = 1e-08
ADAM_WD = 0.01
ADAM_STEP = 10
ADAM_C1 = 1.0 - ADAM_B1 ** ADAM_STEP
ADAM_C2 = 1.0 - ADAM_B2 ** ADAM_STEP

VMEM_LIMIT_BYTES = 60 * 2 ** 20
MESH = pl.DeviceIdType.MESH
ANY = pl.BlockSpec(memory_space=pl.ANY)

WEIGHTS = ['ffn1_w_gu', 'ffn1_w_down', 'ln1_g', 'ln1_b', 'w_in', 'b_in', 'sgu_ln_g', 'sgu_ln_b', 'sgu_w_s',
           'sgu_b_s', 'w_a_proj', 'conv_w_dw', 'conv_b_dw', 'conv_ln_g', 'conv_ln_b', 'w_b_proj', 'w_out',
           'ln2_g', 'ln2_b', 'ffn2_w_gu', 'ffn2_w_down', 'ln3_g', 'ln3_b']
BIG = {'ffn1_w_gu': 1, 'ffn1_w_down': 0, 'w_in': 1, 'w_a_proj': 1, 'w_b_proj': 1, 'w_out': 0,
       'ffn2_w_gu': 1, 'ffn2_w_down': 0}
SMALL = [n for n in WEIGHTS if n not in BIG and n != 'conv_w_dw']


def _tile(n, pref, mult=8):
    best = None
    for d in range(mult, min(n, pref) + 1, mult):
        if n % d == 0:
            best = d
    return n if best is None else best


def _params(*sem):
    return pltpu.CompilerParams(dimension_semantics=sem, vmem_limit_bytes=VMEM_LIMIT_BYTES)


def _dot(a, b):
    return jnp.dot(a, b, preferred_element_type=F32)


def _dot_nt(a, b):
    return lax.dot_general(a, b, (((1,), (1,)), ((), ())), preferred_element_type=F32)


def _dot_tn(a, b):
    return lax.dot_general(a, b, (((0,), (0,)), ((), ())), preferred_element_type=F32)


def _sig(x):
    return 1.0 / (1.0 + jnp.exp(-x))


_GELU_K = math.sqrt(2.0 / math.pi)
_GELU_C = 0.044715


def _gelu(x):
    t = jnp.tanh(_GELU_K * (x + _GELU_C * x * x * x))
    return 0.5 * x * (1.0 + t)


def _gelu_grad(x):
    x2 = x * x
    t = jnp.tanh(_GELU_K * (x + _GELU_C * x2 * x))
    return 0.5 * (1.0 + t) + 0.5 * x * (1.0 - t * t) * (_GELU_K * (1.0 + 3.0 * _GELU_C * x2))


def _ln_stats(r):
    mu = jnp.mean(r, axis=-1, keepdims=True)
    rc = r - mu
    var = jnp.mean(rc * rc, axis=-1, keepdims=True)
    rstd = lax.rsqrt(var + LN_EPS)
    return rc * rstd, rstd


def _ln_bwd(dy, xh, rstd, g):
    dxh = dy * g
    m1 = jnp.mean(dxh, axis=-1, keepdims=True)
    m2 = jnp.mean(dxh * xh, axis=-1, keepdims=True)
    return rstd * (dxh - m1 - xh * m2)


def _colsum(v):
    return jnp.sum(v, axis=0, keepdims=True)


def _chunk_mask(transposed):
    shift = CHUNK.bit_length() - 1
    r = lax.broadcasted_iota(jnp.int32, (GMLP_BLOCK, GMLP_BLOCK), 0) >> shift
    c = lax.broadcasted_iota(jnp.int32, (GMLP_BLOCK, GMLP_BLOCK), 1) >> shift
    return (r <= c) if transposed else (c <= r)


class _Comm:
    def __init__(self, inputs, out_shape, scratch, run, when=(0.0, 1.0)):
        self.inputs, self.out_shape, self.scratch, self.run = list(inputs), list(out_shape), list(scratch), run
        self.when = tuple(when)
        self.parts = [len(self.out_shape)]

    def split(self, outs):
        res, o = [], 0
        for n in self.parts:
            res.append(list(outs[o:o + n]))
            o += n
        return res


def _join(*comms):
    comms = [c for c in comms if c is not None]
    if not comms:
        return None
    assert all(c.when == (0.0, 1.0) for c in comms)

    def run(ins, outs, sems, phase):
        i = o = s = 0
        for c in comms:
            c.run(ins[i:i + len(c.inputs)], outs[o:o + len(c.out_shape)], sems[s:s + len(c.scratch)], phase)
            i, o, s = i + len(c.inputs), o + len(c.out_shape), s + len(c.scratch)

    joined = _Comm(sum((c.inputs for c in comms), []), sum((c.out_shape for c in comms), []),
                   sum((c.scratch for c in comms), []), run)
    joined.parts = [len(c.out_shape) for c in comms]
    return joined


def _call(body, *, name, grid, in_specs, out_specs, out_shape, args, sem, scratch_shapes=(), aliases=None, carry=None):
    in_specs, out_specs, out_shape = list(in_specs), list(out_specs), list(out_shape)
    scratch_shapes = list(scratch_shapes)
    if carry is None:
        return pl.pallas_call(body, name=name, grid=grid, in_specs=in_specs, out_specs=out_specs,
                              out_shape=out_shape, scratch_shapes=scratch_shapes,
                              input_output_aliases=aliases or {}, compiler_params=_params(*sem))(*args)
    n_in, n_out, n_scr = len(args), len(out_shape), len(scratch_shapes)
    c_in, c_out = len(carry.inputs), len(carry.out_shape)
    n_steps = math.prod(grid)
    at_step = [int(round(f * (n_steps - 1))) for f in carry.when]
    assert at_step[0] == 0 and at_step[-1] == n_steps - 1 and at_step == sorted(at_step)

    def wrapped(*refs):
        ins, c_ins = refs[:n_in], refs[n_in:n_in + c_in]
        o0 = n_in + c_in
        outs, c_outs = refs[o0:o0 + n_out], refs[o0 + n_out:o0 + n_out + c_out]
        s0 = o0 + n_out + c_out
        scr, c_sems = refs[s0:s0 + n_scr], refs[s0 + n_scr:]
        step = 0
        for a, g in enumerate(grid):
            step = step * g + pl.program_id(a)
        pl.when(step == 0)(functools.partial(carry.run, c_ins, c_outs, c_sems, 0))
        body(*ins, *outs, *scr)
        for k in range(1, len(at_step)):
            pl.when(step == at_step[k])(functools.partial(carry.run, c_ins, c_outs, c_sems, k))

    res = pl.pallas_call(
        wrapped, name=name, grid=grid, in_specs=in_specs + [ANY] * c_in, out_specs=out_specs + [ANY] * c_out,
        out_shape=out_shape + carry.out_shape, scratch_shapes=scratch_shapes + carry.scratch,
        input_output_aliases=aliases or {},
        compiler_params=pltpu.CompilerParams(dimension_semantics=("arbitrary",) * len(grid),
                                             vmem_limit_bytes=VMEM_LIMIT_BYTES, has_side_effects=True),
    )(*args, *carry.inputs)
    return list(res[:n_out]), list(res[n_out:])


def _comm_call(comm, *, name):
    n_in, n_out = len(comm.inputs), len(comm.out_shape)

    def body(*refs):
        ins, outs, sems = refs[:n_in], refs[n_in:n_in + n_out], refs[n_in + n_out:]
        for k in range(len(comm.when)):
            comm.run(ins, outs, sems, k)

    return list(pl.pallas_call(
        body, name=name, in_specs=[ANY] * n_in, out_specs=[ANY] * n_out, out_shape=comm.out_shape,
        scratch_shapes=comm.scratch, compiler_params=pltpu.CompilerParams(has_side_effects=True))(*comm.inputs))


def _when(cond, fn):
    if isinstance(cond, bool):
        if cond:
            fn()
    else:
        pl.when(cond)(fn)


def _ffn_tiles(T, F):
    return _tile(T, 512), _tile(F, 512, LANES)


def _row_chunks(tm, rows=128):
    rows = _tile(tm, rows)
    return [slice(r, r + rows) for r in range(0, tm, rows)]


def _hidden_loop(nj, step):
    def pair(jj, c):
        step(2 * jj, 0)
        step(2 * jj + 1, 1)
        return c
    if nj // 2:
        lax.fori_loop(0, nj // 2, pair, 0)
    if nj % 2:
        step(nj - 1, 0)


def _ffn_fwd_looped(xh, lg, lb, wgu, wd, *, affine, name, final=None, carry=None):
    T, D = xh.shape
    F = wd.shape[0]
    tm, tn = _ffn_tiles(T, F)
    nj, nt = F // tn, T // tm
    is_final = final is not None

    def body(*refs):
        if is_final:
            (xh_ref, lg_ref, lb_ref, wgu_hbm, wd_hbm, ng_ref, nb_ref, tgt_hbm,
             gate_hbm, up_hbm, xb_ref, dr_hbm, loss_ref, dng_ref, dnb_ref,
             acc_sc, wg_buf, wu_buf, wd_buf, g_buf, u_buf, w_sem, o_sem, tgt_sc, dr_sc, t_sem) = refs
        else:
            (xh_ref, lg_ref, lb_ref, wgu_hbm, wd_hbm,
             gate_hbm, up_hbm, xb_ref, xho_ref, rstd_ref,
             acc_sc, wg_buf, wu_buf, wd_buf, g_buf, u_buf, w_sem, o_sem) = refs
        i = pl.program_id(0)
        rows = pl.ds(pl.multiple_of(i * tm, tm), tm)

        def cols(j, base=0):
            return pl.ds(pl.multiple_of(base + j * tn, LANES), tn)

        def w_copies(j, slot):
            return (pltpu.make_async_copy(wgu_hbm.at[:, cols(j)], wg_buf.at[slot], w_sem.at[slot]),
                    pltpu.make_async_copy(wgu_hbm.at[:, cols(j, F)], wu_buf.at[slot], w_sem.at[2 + slot]),
                    pltpu.make_async_copy(wd_hbm.at[cols(j), :], wd_buf.at[slot], w_sem.at[4 + slot]))

        def o_copies(j, slot):
            return (pltpu.make_async_copy(g_buf.at[slot], gate_hbm.at[rows, cols(j)], o_sem.at[slot]),
                    pltpu.make_async_copy(u_buf.at[slot], up_hbm.at[rows, cols(j)], o_sem.at[2 + slot]))

        def start(copies):
            for cp in copies:
                cp.start()

        def wait(copies):
            for cp in copies:
                cp.wait()

        def xin(rs):
            v = xh_ref[rs, :]
            return v * lg_ref[...] + lb_ref[...] if affine else v

        _when(i == 0, lambda: start(w_copies(0, 0)))
        if is_final:
            tgt_in = pltpu.make_async_copy(tgt_hbm.at[rows, :], tgt_sc, t_sem.at[0])
            dr_out = pltpu.make_async_copy(dr_sc, dr_hbm.at[rows, :], t_sem.at[1])
            tgt_in.start()
        for rs in _row_chunks(tm):
            xb_ref[rs, :] = xin(rs).astype(BF16)
        acc_sc[...] = jnp.zeros_like(acc_sc)

        def step(j, slot):
            _when(j + 1 < nj, lambda: start(w_copies(j + 1, 1 - slot)))
            wait(w_copies(j, slot))
            xb = xb_ref[...]
            g = _dot(xb, wg_buf[slot])
            u = _dot(xb, wu_buf[slot])
            _when(j >= 2, lambda: wait(o_copies(j - 2, slot)))
            g_buf[slot] = g.astype(BF16)
            u_buf[slot] = u.astype(BF16)
            start(o_copies(j, slot))
            h = g * _sig(g) * u
            acc_sc[...] += _dot(h.astype(BF16), wd_buf[slot])

        _hidden_loop(nj, step)
        _when(i + 1 < nt, lambda: start(w_copies(0, 0)))
        for j in range(max(nj - 2, 0), nj):
            wait(o_copies(j, j % 2))

        if is_final:
            @pl.when(i == 0)
            def _():
                loss_ref[...] = jnp.zeros_like(loss_ref)
                dng_ref[...] = jnp.zeros_like(dng_ref)
                dnb_ref[...] = jnp.zeros_like(dnb_ref)
            tgt_in.wait()
            _when(i > 0, dr_out.wait)
        for rs in _row_chunks(tm):
            r = ALPHA * xin(rs) + 0.5 * acc_sc[rs, :]
            xho, rstd = _ln_stats(r)
            if not is_final:
                xho_ref[rs, :] = xho
                rstd_ref[rs, :] = jnp.broadcast_to(rstd, (rs.stop - rs.start, LANES))
            else:
                ng = ng_ref[...]
                e = xho * ng + nb_ref[...] - tgt_sc[rs, :]
                part = _colsum(jnp.sum(e * e, axis=1, keepdims=True)) * (0.5 / D)
                loss_ref[...] += jnp.broadcast_to(part, loss_ref.shape)
                dy = e * (1.0 / D)
                dng_ref[...] += _colsum(dy * xho)
                dnb_ref[...] += _colsum(dy)
                dr_sc[rs, :] = _ln_bwd(dy, xho, rstd, ng)
        if is_final:
            dr_out.start()
            _when(i == nt - 1, dr_out.wait)

    row = pl.BlockSpec((tm, D), lambda i: (i, 0))
    vec = pl.BlockSpec((1, D), lambda i: (0, 0))
    in_specs = [row, vec, vec, ANY, ANY]
    args = [xh, lg, lb, wgu, wd]
    out_shape = [jax.ShapeDtypeStruct((T, F), BF16), jax.ShapeDtypeStruct((T, F), BF16),
                 jax.ShapeDtypeStruct((T, D), BF16)]
    out_specs = [ANY, ANY, row]
    scratch = [pltpu.VMEM((tm, D), F32),
               pltpu.VMEM((2, D, tn), BF16), pltpu.VMEM((2, D, tn), BF16), pltpu.VMEM((2, tn, D), BF16),
               pltpu.VMEM((2, tm, tn), BF16), pltpu.VMEM((2, tm, tn), BF16),
               pltpu.SemaphoreType.DMA((6,)), pltpu.SemaphoreType.DMA((4,))]
    if is_final:
        in_specs += [vec, vec, ANY]
        args += list(final)
        out_shape += [jax.ShapeDtypeStruct((T, D), F32), jax.ShapeDtypeStruct((8, LANES), F32),
                      jax.ShapeDtypeStruct((1, D), F32), jax.ShapeDtypeStruct((1, D), F32)]
        out_specs += [ANY, pl.BlockSpec((8, LANES), lambda i: (0, 0)), vec, vec]
        scratch += [pltpu.VMEM((tm, D), F32), pltpu.VMEM((tm, D), F32), pltpu.SemaphoreType.DMA((2,))]
    else:
        out_shape += [jax.ShapeDtypeStruct((T, D), F32), jax.ShapeDtypeStruct((T, LANES), F32)]
        out_specs += [row, pl.BlockSpec((tm, LANES), lambda i: (i, 0))]
    return _call(body, name=name, grid=(nt,), in_specs=in_specs, out_specs=out_specs, out_shape=out_shape,
                 scratch_shapes=scratch, sem=("arbitrary",), args=args, carry=carry)


def _ffn_bwd_looped(dr, gate, up, wgu, wd, *, name, prev=None, carry=None):
    T, D = dr.shape
    F = wd.shape[0]
    tm, tn = _ffn_tiles(T, F)
    nj, nt = F // tn, T // tm
    has_prev = prev is not None

    def body(*refs):
        if has_prev:
            (dr_ref, gate_hbm, up_hbm, wgu_hbm, wd_hbm, xh_hbm, rstd_ref, lg_ref,
             h_hbm, dg_hbm, du_hbm, dprev_hbm, dlg_ref, dlb_ref, *scr) = refs
            xh_sc = scr.pop()
        else:
            (dr_ref, gate_hbm, up_hbm, wgu_hbm, wd_hbm,
             h_hbm, dg_hbm, du_hbm, dprev_hbm, *scr) = refs
        (df_sc, dx_sc, wg_buf, wu_buf, wd_buf, gi_buf, ui_buf, h_buf, dg_buf, du_buf, i_sem, o_sem,
         dp_sc, t_sem) = scr
        i = pl.program_id(0)
        rows = pl.ds(pl.multiple_of(i * tm, tm), tm)
        dp_out = pltpu.make_async_copy(dp_sc, dprev_hbm.at[rows, :], t_sem.at[0])
        if has_prev:
            xh_in = pltpu.make_async_copy(xh_hbm.at[rows, :], xh_sc, t_sem.at[1])
            xh_in.start()

        def cols(j, base=0):
            return pl.ds(pl.multiple_of(base + j * tn, LANES), tn)

        def i_copies(j, slot, tile=None):
            at = rows if tile is None else pl.ds(pl.multiple_of(tile * tm, tm), tm)
            return (pltpu.make_async_copy(wgu_hbm.at[:, cols(j)], wg_buf.at[slot], i_sem.at[slot]),
                    pltpu.make_async_copy(wgu_hbm.at[:, cols(j, F)], wu_buf.at[slot], i_sem.at[2 + slot]),
                    pltpu.make_async_copy(wd_hbm.at[cols(j), :], wd_buf.at[slot], i_sem.at[4 + slot]),
                    pltpu.make_async_copy(gate_hbm.at[at, cols(j)], gi_buf.at[slot], i_sem.at[6 + slot]),
                    pltpu.make_async_copy(up_hbm.at[at, cols(j)], ui_buf.at[slot], i_sem.at[8 + slot]))

        def o_copies(j, slot):
            return (pltpu.make_async_copy(h_buf.at[slot], h_hbm.at[rows, cols(j)], o_sem.at[slot]),
                    pltpu.make_async_copy(dg_buf.at[slot], dg_hbm.at[rows, cols(j)], o_sem.at[2 + slot]),
                    pltpu.make_async_copy(du_buf.at[slot], du_hbm.at[rows, cols(j)], o_sem.at[4 + slot]))

        def start(copies):
            for cp in copies:
                cp.start()

        def wait(copies):
            for cp in copies:
                cp.wait()

        _when(i == 0, lambda: start(i_copies(0, 0)))
        for rs in _row_chunks(tm):
            d = dr_ref[rs, :]
            df_sc[rs, :] = (0.5 * d).astype(BF16)
            dx_sc[rs, :] = ALPHA * d

        def step(j, slot):
            _when(j + 1 < nj, lambda: start(i_copies(j + 1, 1 - slot)))
            wait(i_copies(j, slot))
            g = gi_buf[slot].astype(F32)
            u = ui_buf[slot].astype(F32)
            dh = _dot_nt(df_sc[...], wd_buf[slot])
            s = _sig(g)
            sil = g * s
            dg = (dh * u * (s * (1.0 + g * (1.0 - s)))).astype(BF16)
            du = (dh * sil).astype(BF16)
            _when(j >= 2, lambda: wait(o_copies(j - 2, slot)))
            h_buf[slot] = (sil * u).astype(BF16)
            dg_buf[slot] = dg
            du_buf[slot] = du
            start(o_copies(j, slot))
            dx_sc[...] += _dot_nt(dg, wg_buf[slot])
            dx_sc[...] += _dot_nt(du, wu_buf[slot])

        _hidden_loop(nj, step)
        _when(i + 1 < nt, lambda: start(i_copies(0, 0, i + 1)))
        for j in range(max(nj - 2, 0), nj):
            wait(o_copies(j, j % 2))

        if has_prev:
            @pl.when(i == 0)
            def _():
                dlg_ref[...] = jnp.zeros_like(dlg_ref)
                dlb_ref[...] = jnp.zeros_like(dlb_ref)
            xh_in.wait()
        _when(i > 0, dp_out.wait)
        for rs in _row_chunks(tm):
            dxin = dx_sc[rs, :]
            if not has_prev:
                dp_sc[rs, :] = dxin
            else:
                x_hat = xh_sc[rs, :]
                dlg_ref[...] += _colsum(dxin * x_hat)
                dlb_ref[...] += _colsum(dxin)
                dp_sc[rs, :] = _ln_bwd(dxin, x_hat, rstd_ref[rs, 0:1], lg_ref[...])
        dp_out.start()
        _when(i == nt - 1, dp_out.wait)

    row = pl.BlockSpec((tm, D), lambda i: (i, 0))
    vec = pl.BlockSpec((1, D), lambda i: (0, 0))
    in_specs = [row, ANY, ANY, ANY, ANY]
    args = [dr, gate, up, wgu, wd]
    out_shape = [jax.ShapeDtypeStruct((T, F), BF16)] * 3 + [jax.ShapeDtypeStruct((T, D), F32)]
    out_specs = [ANY, ANY, ANY, ANY]
    scratch = [pltpu.VMEM((tm, D), BF16), pltpu.VMEM((tm, D), F32),
               pltpu.VMEM((2, D, tn), BF16), pltpu.VMEM((2, D, tn), BF16), pltpu.VMEM((2, tn, D), BF16)]
    scratch += [pltpu.VMEM((2, tm, tn), BF16)] * 5
    scratch += [pltpu.SemaphoreType.DMA((10,)), pltpu.SemaphoreType.DMA((6,)),
                pltpu.VMEM((tm, D), F32), pltpu.SemaphoreType.DMA((2,))]
    if has_prev:
        in_specs += [ANY, pl.BlockSpec((tm, LANES), lambda i: (i, 0)), vec]
        args += list(prev)
        out_shape += [jax.ShapeDtypeStruct((1, D), F32)] * 2
        out_specs += [vec, vec]
        scratch += [pltpu.VMEM((tm, D), F32)]
    return _call(body, name=name, grid=(nt,), in_specs=in_specs, out_specs=out_specs, out_shape=out_shape,
                 scratch_shapes=scratch, sem=("arbitrary",), args=args, carry=carry)


def _ffn_fwd(xh, lg, lb, wgu, wd, *, affine, name, final=None, carry=None):
    T, D = xh.shape
    F = wd.shape[0]
    tm = _tile(T, 512)
    tn = _tile(F, 512, LANES)
    nj = F // tn
    is_final = final is not None

    def body(*refs):
        if is_final:
            (xh_ref, lg_ref, lb_ref, wg_ref, wu_ref, wd_ref, ng_ref, nb_ref, tgt_ref,
             gate_ref, up_ref, xb_sc, dr_ref, loss_ref, dng_ref, dnb_ref, acc_sc) = refs
        else:
            (xh_ref, lg_ref, lb_ref, wg_ref, wu_ref, wd_ref,
             gate_ref, up_ref, xb_sc, xho_ref, rstd_ref, acc_sc) = refs
        i = pl.program_id(0)
        j = pl.program_id(1)

        def xin():
            v = xh_ref[...]
            return v * lg_ref[...] + lb_ref[...] if affine else v

        @pl.when(j == 0)
        def _():
            xb_sc[...] = xin().astype(BF16)
            acc_sc[...] = jnp.zeros_like(acc_sc)

        xb = xb_sc[...]
        g = _dot(xb, wg_ref[...])
        u = _dot(xb, wu_ref[...])
        gate_ref[...] = g.astype(BF16)
        up_ref[...] = u.astype(BF16)
        h = g * _sig(g) * u
        acc_sc[...] += _dot(h.astype(BF16), wd_ref[...])

        @pl.when(j == nj - 1)
        def _():
            if is_final:
                @pl.when(i == 0)
                def _():
                    loss_ref[...] = jnp.zeros_like(loss_ref)
                    dng_ref[...] = jnp.zeros_like(dng_ref)
                    dnb_ref[...] = jnp.zeros_like(dnb_ref)
            for rs in _row_chunks(tm):
                v = xh_ref[rs, :]
                if affine:
                    v = v * lg_ref[...] + lb_ref[...]
                r = ALPHA * v + 0.5 * acc_sc[rs, :]
                xho, rstd = _ln_stats(r)
                if not is_final:
                    xho_ref[rs, :] = xho
                    rstd_ref[rs, :] = jnp.broadcast_to(rstd, (rs.stop - rs.start, LANES))
                else:
                    ng = ng_ref[...]
                    e = xho * ng + nb_ref[...] - tgt_ref[rs, :]
                    part = _colsum(jnp.sum(e * e, axis=1, keepdims=True)) * (0.5 / D)
                    loss_ref[...] += jnp.broadcast_to(part, loss_ref.shape)
                    dy = e * (1.0 / D)
                    dng_ref[...] += _colsum(dy * xho)
                    dnb_ref[...] += _colsum(dy)
                    dr_ref[rs, :] = _ln_bwd(dy, xho, rstd, ng)

    row = pl.BlockSpec((tm, D), lambda i, j: (i, 0))
    vec = pl.BlockSpec((1, D), lambda i, j: (0, 0))
    hid = pl.BlockSpec((tm, tn), lambda i, j: (i, j))
    in_specs = [row, vec, vec,
                pl.BlockSpec((D, tn), lambda i, j: (0, j)),
                pl.BlockSpec((D, tn), lambda i, j: (0, j + nj)),
                pl.BlockSpec((tn, D), lambda i, j: (j, 0))]
    args = [xh, lg, lb, wgu, wgu, wd]
    out_shape = [jax.ShapeDtypeStruct((T, F), BF16), jax.ShapeDtypeStruct((T, F), BF16),
                 jax.ShapeDtypeStruct((T, D), BF16)]
    out_specs = [hid, hid, row]
    if is_final:
        in_specs += [vec, vec, row]
        args += list(final)
        out_shape += [jax.ShapeDtypeStruct((T, D), F32), jax.ShapeDtypeStruct((8, LANES), F32),
                      jax.ShapeDtypeStruct((1, D), F32), jax.ShapeDtypeStruct((1, D), F32)]
        out_specs += [row, pl.BlockSpec((8, LANES), lambda i, j: (0, 0)), vec, vec]
        sem = ("arbitrary", "arbitrary")
    else:
        out_shape += [jax.ShapeDtypeStruct((T, D), F32), jax.ShapeDtypeStruct((T, LANES), F32)]
        out_specs += [row, pl.BlockSpec((tm, LANES), lambda i, j: (i, 0))]
        sem = ("parallel", "arbitrary")
    return _call(body, name=name, grid=(T // tm, nj), in_specs=in_specs, out_specs=out_specs, out_shape=out_shape,
                 scratch_shapes=[pltpu.VMEM((tm, D), F32)], sem=sem, args=args, carry=carry)


def _ffn_bwd(dr, gate, up, wgu, wd, *, name, prev=None, carry=None):
    T, D = dr.shape
    F = wd.shape[0]
    tm = _tile(T, 512)
    tn = _tile(F, 512, LANES)
    nj = F // tn
    has_prev = prev is not None

    def body(*refs):
        if has_prev:
            (dr_ref, gate_ref, up_ref, wd_ref, wg_ref, wu_ref, xh_ref, rstd_ref, lg_ref,
             h_ref, dg_ref, du_ref, dprev_ref, dlg_ref, dlb_ref, df_sc, dx_sc) = refs
        else:
            (dr_ref, gate_ref, up_ref, wd_ref, wg_ref, wu_ref,
             h_ref, dg_ref, du_ref, dprev_ref, df_sc, dx_sc) = refs
        i = pl.program_id(0)
        j = pl.program_id(1)

        @pl.when(j == 0)
        def _():
            d = dr_ref[...]
            df_sc[...] = (0.5 * d).astype(BF16)
            dx_sc[...] = ALPHA * d

        g = gate_ref[...].astype(F32)
        u = up_ref[...].astype(F32)
        dh = _dot_nt(df_sc[...], wd_ref[...])
        s = _sig(g)
        sil = g * s
        h_ref[...] = (sil * u).astype(BF16)
        dg = (dh * u * (s * (1.0 + g * (1.0 - s)))).astype(BF16)
        du = (dh * sil).astype(BF16)
        dg_ref[...] = dg
        du_ref[...] = du
        dx_sc[...] += _dot_nt(dg, wg_ref[...]) + _dot_nt(du, wu_ref[...])

        @pl.when(j == nj - 1)
        def _():
            dxin = dx_sc[...]
            if not has_prev:
                dprev_ref[...] = dxin
            else:
                @pl.when(i == 0)
                def _():
                    dlg_ref[...] = jnp.zeros_like(dlg_ref)
                    dlb_ref[...] = jnp.zeros_like(dlb_ref)
                xh = xh_ref[...]
                dlg_ref[...] += _colsum(dxin * xh)
                dlb_ref[...] += _colsum(dxin)
                dprev_ref[...] = _ln_bwd(dxin, xh, rstd_ref[:, 0:1], lg_ref[...])

    row = pl.BlockSpec((tm, D), lambda i, j: (i, 0))
    vec = pl.BlockSpec((1, D), lambda i, j: (0, 0))
    hid = pl.BlockSpec((tm, tn), lambda i, j: (i, j))
    in_specs = [row, hid, hid,
                pl.BlockSpec((tn, D), lambda i, j: (j, 0)),
                pl.BlockSpec((D, tn), lambda i, j: (0, j)),
                pl.BlockSpec((D, tn), lambda i, j: (0, j + nj))]
    args = [dr, gate, up, wd, wgu, wgu]
    out_shape = [jax.ShapeDtypeStruct((T, F), BF16)] * 3 + [jax.ShapeDtypeStruct((T, D), F32)]
    out_specs = [hid, hid, hid, row]
    if has_prev:
        in_specs += [row, pl.BlockSpec((tm, LANES), lambda i, j: (i, 0)), vec]
        args += list(prev)
        out_shape += [jax.ShapeDtypeStruct((1, D), F32)] * 2
        out_specs += [vec, vec]
        sem = ("arbitrary", "arbitrary")
    else:
        sem = ("parallel", "arbitrary")
    return _call(body, name=name, grid=(T // tm, nj), in_specs=in_specs, out_specs=out_specs, out_shape=out_shape,
                 scratch_shapes=[pltpu.VMEM((tm, D), BF16), pltpu.VMEM((tm, D), F32)], sem=sem, args=args,
                 carry=carry)


def _mm_tn(a, b, *, name, tm_pref, tn_pref, scale=1.0, a_affine=None, into=None, col_off=0, n_total=None,
           carry=None):
    T, M = a.shape
    N = b.shape[1]
    n_total = N if n_total is None else n_total
    tM = _tile(M, tm_pref, LANES)
    tN = _tile(N, tn_pref, LANES)
    tk = _tile(T, 1024)
    nt = T // tk
    assert col_off % tN == 0
    off_blocks = col_off // tN
    has_aff = a_affine is not None
    has_into = into is not None

    def body(*refs):
        refs = list(refs)
        a_ref = refs.pop(0)
        if has_aff:
            lg_ref = refs.pop(0)
            lb_ref = refs.pop(0)
        b_ref = refs.pop(0)
        if has_into:
            refs.pop(0)
        o_ref, acc_sc = refs
        t = pl.program_id(2)

        @pl.when(t == 0)
        def _():
            acc_sc[...] = jnp.zeros_like(acc_sc)

        av = a_ref[...]
        if has_aff:
            av = av * lg_ref[...] + lb_ref[...]
        acc_sc[...] += _dot_tn(av.astype(BF16), b_ref[...].astype(BF16))

        @pl.when(t == nt - 1)
        def _():
            o_ref[...] = (acc_sc[...] * scale).astype(BF16)

    in_specs = [pl.BlockSpec((tk, tM), lambda m, n, t: (t, m))]
    args = [a]
    if has_aff:
        in_specs += [pl.BlockSpec((1, tM), lambda m, n, t: (0, m))] * 2
        args += list(a_affine)
    in_specs.append(pl.BlockSpec((tk, tN), lambda m, n, t: (t, n)))
    args.append(b)
    aliases = {}
    if has_into:
        aliases = {len(args): 0}
        in_specs.append(ANY)
        args.append(into)
    res = _call(body, name=name, grid=(M // tM, N // tN, nt), in_specs=in_specs,
                out_specs=[pl.BlockSpec((tM, tN), lambda m, n, t: (m, n + off_blocks))],
                out_shape=[jax.ShapeDtypeStruct((M, n_total), BF16)],
                scratch_shapes=[pltpu.VMEM((tM, tN), F32)], aliases=aliases,
                sem=("parallel", "parallel", "arbitrary"), args=args, carry=carry)
    return res[0] if carry is None else (res[0][0], res[1])


def _inproj_fwd(xh, lg, lb, w, bias, *, name):
    T, D = xh.shape
    N = w.shape[1]
    tm = _tile(T, 1024)
    tn = _tile(N, 1024, LANES)

    def body(xh_ref, lg_ref, lb_ref, w_ref, b_ref, o_ref, xb_ref):
        @pl.when(pl.program_id(1) == 0)
        def _():
            xb_ref[...] = (xh_ref[...] * lg_ref[...] + lb_ref[...]).astype(BF16)
        o_ref[...] = _dot(xb_ref[...], w_ref[...]) + b_ref[...]

    return pl.pallas_call(
        body, name=name, grid=(T // tm, N // tn),
        in_specs=[pl.BlockSpec((tm, D), lambda i, j: (i, 0)),
                  pl.BlockSpec((1, D), lambda i, j: (0, 0)), pl.BlockSpec((1, D), lambda i, j: (0, 0)),
                  pl.BlockSpec((D, tn), lambda i, j: (0, j)), pl.BlockSpec((1, tn), lambda i, j: (0, j))],
        out_specs=[pl.BlockSpec((tm, tn), lambda i, j: (i, j)), pl.BlockSpec((tm, D), lambda i, j: (i, 0))],
        out_shape=[jax.ShapeDtypeStruct((T, N), F32), jax.ShapeDtypeStruct((T, D), BF16)],
        compiler_params=_params("parallel", "arbitrary"))(xh, lg, lb, w, bias)


def _inproj_bwd(dproj, w, dr_next, xh, rstd, lg, *, name, carry=None):
    T, N = dproj.shape
    D = w.shape[0]
    tm = _tile(T, 512)
    tn = _tile(N, 2048, LANES)
    nj = N // tn

    def body(dp_ref, w_ref, drn_ref, xh_ref, rstd_ref, lg_ref, dprev_ref, dlg_ref, dlb_ref, dx_sc):
        i = pl.program_id(0)
        j = pl.program_id(1)

        @pl.when(j == 0)
        def _():
            for rs in _row_chunks(tm):
                dx_sc[rs, :] = ALPHA * drn_ref[rs, :]

        dx_sc[...] += _dot_nt(dp_ref[...], w_ref[...])

        @pl.when(j == nj - 1)
        def _():
            @pl.when(i == 0)
            def _():
                dlg_ref[...] = jnp.zeros_like(dlg_ref)
                dlb_ref[...] = jnp.zeros_like(dlb_ref)
            for rs in _row_chunks(tm):
                dx = dx_sc[rs, :]
                x_hat = xh_ref[rs, :]
                dlg_ref[...] += _colsum(dx * x_hat)
                dlb_ref[...] += _colsum(dx)
                dprev_ref[rs, :] = _ln_bwd(dx, x_hat, rstd_ref[rs, 0:1], lg_ref[...])

    row = pl.BlockSpec((tm, D), lambda i, j: (i, 0))
    vec = pl.BlockSpec((1, D), lambda i, j: (0, 0))
    return _call(
        body, name=name, grid=(T // tm, nj),
        in_specs=[pl.BlockSpec((tm, tn), lambda i, j: (i, j)), pl.BlockSpec((D, tn), lambda i, j: (0, j)),
                  row, row, pl.BlockSpec((tm, LANES), lambda i, j: (i, 0)), vec],
        out_specs=[row, vec, vec],
        out_shape=[jax.ShapeDtypeStruct((T, D), F32), jax.ShapeDtypeStruct((1, D), F32),
                   jax.ShapeDtypeStruct((1, D), F32)],
        scratch_shapes=[pltpu.VMEM((tm, D), F32)], sem=("arbitrary", "arbitrary"),
        args=[dproj, w, dr_next, xh, rstd, lg], carry=carry)


def _sgu_fwd(proj, ln_g, ln_b, w_s, b_sb, *, name):
    T = proj.shape[0]
    A = proj.shape[1] // 8
    hd = A // A_GROUPS
    tm = _tile(T, 256, GMLP_BLOCK)

    def body(u_ref, v_ref, g_ref, b_ref, ws_ref, bs_ref, o_ref):
        gu = _gelu(u_ref[...])
        vh, _ = _ln_stats(_gelu(v_ref[...]))
        vn = (vh * g_ref[...] + b_ref[...]).astype(BF16)
        mask = _chunk_mask(False)
        for h in range(A_GROUPS):
            wm = jnp.where(mask, ws_ref[h], 0.0).astype(BF16)
            cols = slice(h * hd, (h + 1) * hd)
            for n in range(tm // GMLP_BLOCK):
                rows = slice(n * GMLP_BLOCK, (n + 1) * GMLP_BLOCK)
                s = _dot(wm, vn[rows, cols]) + bs_ref[h][:, :hd]
                o_ref[rows, cols] = (gu[rows, cols] * s).astype(BF16)

    vec = pl.BlockSpec((1, A), lambda i: (0, 0))
    full = pl.BlockSpec((A_GROUPS, GMLP_BLOCK, GMLP_BLOCK), lambda i: (0, 0, 0))
    return pl.pallas_call(
        body, name=name, grid=(T // tm,),
        in_specs=[pl.BlockSpec((tm, A), lambda i: (i, 0)), pl.BlockSpec((tm, A), lambda i: (i, 1)),
                  vec, vec, full, full],
        out_specs=pl.BlockSpec((tm, A), lambda i: (i, 0)),
        out_shape=jax.ShapeDtypeStruct((T, A), BF16),
        compiler_params=_params("parallel"))(proj, proj, ln_g, ln_b, w_s, b_sb)


def _sgu_bwd(proj, dsg, ln_g, ln_b, w_s, w_st, b_sb, dproj, *, name):
    T = proj.shape[0]
    A = proj.shape[1] // 8
    hd = A // A_GROUPS
    tm = _tile(T, 256, GMLP_BLOCK)
    nt = T // tm

    def body(u_ref, v_ref, dsg_ref, g_ref, b_ref, ws_ref, wst_ref, bs_ref, _alias,
             dp_ref, dbin_ref, dlg_ref, dlb_ref, dws_ref, dbs_ref, dvn_sc, dgu_sc, dbs_sc):
        i = pl.program_id(0)

        @pl.when(i == 0)
        def _():
            dbin_ref[...] = jnp.zeros_like(dbin_ref)
            dlg_ref[...] = jnp.zeros_like(dlg_ref)
            dlb_ref[...] = jnp.zeros_like(dlb_ref)
            dws_ref[...] = jnp.zeros_like(dws_ref)
            dbs_sc[...] = jnp.zeros_like(dbs_sc)

        u = u_ref[...]
        v = v_ref[...]
        gu = _gelu(u)
        vh, rstd = _ln_stats(_gelu(v))
        gain = g_ref[...]
        vn = (vh * gain + b_ref[...]).astype(BF16)
        dsg_v = dsg_ref[...]
        mask = _chunk_mask(False)
        mask_t = _chunk_mask(True)
        for h in range(A_GROUPS):
            wm = jnp.where(mask, ws_ref[h], 0.0).astype(BF16)
            wmt = jnp.where(mask_t, wst_ref[h], 0.0).astype(BF16)
            cols = slice(h * hd, (h + 1) * hd)
            for n in range(tm // GMLP_BLOCK):
                rows = slice(n * GMLP_BLOCK, (n + 1) * GMLP_BLOCK)
                vb = vn[rows, cols]
                s = _dot(wm, vb) + bs_ref[h][:, :hd]
                d_out = dsg_v[rows, cols]
                dgu_sc[rows, cols] = d_out * s
                ds = d_out * gu[rows, cols]
                ds_b = ds.astype(BF16)
                dws_ref[h] += _dot_nt(ds_b, vb)
                dbs_sc[h] += ds
                dvn_sc[rows, cols] = _dot(wmt, ds_b)
        dvn = dvn_sc[...]
        dlg_ref[...] += _colsum(dvn * vh)
        dlb_ref[...] += _colsum(dvn)
        dv = _ln_bwd(dvn, vh, rstd, gain) * _gelu_grad(v)
        du = dgu_sc[...] * _gelu_grad(u)
        dp_ref[:, 0:A] = du.astype(BF16)
        dp_ref[:, A:2 * A] = dv.astype(BF16)
        dbin_ref[:, 0:A] += _colsum(du)
        dbin_ref[:, A:2 * A] += _colsum(dv)

        @pl.when(i == nt - 1)
        def _():
            for h in range(A_GROUPS):
                dws_ref[h] = jnp.where(mask, dws_ref[h], 0.0)
                dbs_ref[h:h + 1, :] = _colsum(dbs_sc[h].T)

    vec = pl.BlockSpec((1, A), lambda i: (0, 0))
    full = pl.BlockSpec((A_GROUPS, GMLP_BLOCK, GMLP_BLOCK), lambda i: (0, 0, 0))
    tile = pl.BlockSpec((tm, A), lambda i: (i, 0))
    return pl.pallas_call(
        body, name=name, grid=(nt,),
        in_specs=[tile, pl.BlockSpec((tm, A), lambda i: (i, 1)), tile, vec, vec, full, full, full, ANY],
        out_specs=[pl.BlockSpec((tm, 2 * A), lambda i: (i, 0)), pl.BlockSpec((1, 2 * A), lambda i: (0, 0)),
                   vec, vec, full, pl.BlockSpec((A_GROUPS, GMLP_BLOCK), lambda i: (0, 0))],
        out_shape=[jax.ShapeDtypeStruct(dproj.shape, BF16), jax.ShapeDtypeStruct((1, 2 * A), F32),
                   jax.ShapeDtypeStruct((1, A), F32), jax.ShapeDtypeStruct((1, A), F32),
                   jax.ShapeDtypeStruct((A_GROUPS, GMLP_BLOCK, GMLP_BLOCK), F32),
                   jax.ShapeDtypeStruct((A_GROUPS, GMLP_BLOCK), F32)],
        scratch_shapes=[pltpu.VMEM((tm, A), F32), pltpu.VMEM((tm, A), F32),
                        pltpu.VMEM((A_GROUPS, GMLP_BLOCK, hd), F32)],
        input_output_aliases={8: 0},
        compiler_params=_params("arbitrary"))(proj, proj, dsg, ln_g, ln_b, w_s, w_st, b_sb, dproj)


def _conv_tiles(T, B):
    tm = _tile(T, 256, CONV_ROWS)
    lb = min(LANES, B)
    return tm, tm // CONV_HALO, lb


def _fill_phases(src, dst, B, lb):
    rows = dst.shape[1]
    for p in range(1, 8):
        for cb in range(B // lb):
            ls = slice(cb * lb, (cb + 1) * lb)
            dst[p - 1, :, ls] = src[p:p + rows, ls]


def _shifted(src, phases, off, ls):
    m, p = divmod(off, 8)
    if p == 0:
        return src[off:off + CONV_ROWS, ls]
    return phases[p - 1, 8 * m:8 * m + CONV_ROWS, ls]


def _conv_fwd(proj, w_dw, b_dw, ln_g, ln_b, *, name):
    T = proj.shape[0]
    B = proj.shape[1] // 8
    tm, nh, lb = _conv_tiles(T, B)

    def body(ap_ref, gp_ref, a_ref, g_ref, w_ref, bdw_ref, lg_ref, lb_ref, c_ref, cv_ref, z_sc, zp_sc):
        i = pl.program_id(0)
        z_sc[0:CONV_HALO, :] = jnp.where(i > 0, ap_ref[...] * _sig(gp_ref[...]), 0.0)
        z_sc[CONV_HALO:CONV_HALO + tm, :] = a_ref[...] * _sig(g_ref[...])
        _fill_phases(z_sc, zp_sc, B, lb)
        for cb in range(B // lb):
            ls = slice(cb * lb, (cb + 1) * lb)
            for rc in range(tm // CONV_ROWS):
                acc = jnp.zeros((CONV_ROWS, lb), F32)
                for k in range(CONV_WIDTH):
                    off = rc * CONV_ROWS + CONV_HALO - (CONV_WIDTH - 1) + k
                    acc = acc + w_ref[k:k + 1, ls] * _shifted(z_sc, zp_sc, off, ls)
                c_ref[rc * CONV_ROWS:(rc + 1) * CONV_ROWS, ls] = acc + bdw_ref[:, ls]
        xh, _ = _ln_stats(c_ref[...])
        y = xh * lg_ref[...] + lb_ref[...]
        cv_ref[...] = (y * _sig(y)).astype(BF16)

    vec = pl.BlockSpec((1, B), lambda i: (0, 0))
    halo_a = pl.BlockSpec((CONV_HALO, B), lambda i: (jnp.maximum(i * nh - 1, 0), 2))
    halo_g = pl.BlockSpec((CONV_HALO, B), lambda i: (jnp.maximum(i * nh - 1, 0), 3))
    return pl.pallas_call(
        body, name=name, grid=(T // tm,),
        in_specs=[halo_a, halo_g, pl.BlockSpec((tm, B), lambda i: (i, 2)), pl.BlockSpec((tm, B), lambda i: (i, 3)),
                  pl.BlockSpec((CONV_WPAD, B), lambda i: (0, 0)), vec, vec, vec],
        out_specs=[pl.BlockSpec((tm, B), lambda i: (i, 0))] * 2,
        out_shape=[jax.ShapeDtypeStruct((T, B), F32), jax.ShapeDtypeStruct((T, B), BF16)],
        scratch_shapes=[pltpu.VMEM((CONV_HALO + tm, B), F32), pltpu.VMEM((7, CONV_HALO + tm - 8, B), F32)],
        compiler_params=_params("parallel"))(proj, proj, proj, proj, w_dw, b_dw, ln_g, ln_b)


def _conv_bwd_ln(dcv, c, ln_g, ln_b, *, name):
    T, B = c.shape
    tm = _tile(T, 512)

    def body(dcv_ref, c_ref, lg_ref, lb_ref, dc_ref, dlg_ref, dlb_ref, dbdw_ref):
        @pl.when(pl.program_id(0) == 0)
        def _():
            dlg_ref[...] = jnp.zeros_like(dlg_ref)
            dlb_ref[...] = jnp.zeros_like(dlb_ref)
            dbdw_ref[...] = jnp.zeros_like(dbdw_ref)
        gain = lg_ref[...]
        xh, rstd = _ln_stats(c_ref[...])
        y = xh * gain + lb_ref[...]
        s = _sig(y)
        dy = dcv_ref[...] * (s * (1.0 + y * (1.0 - s)))
        dlg_ref[...] += _colsum(dy * xh)
        dlb_ref[...] += _colsum(dy)
        dc = _ln_bwd(dy, xh, rstd, gain)
        dc_ref[...] = dc
        dbdw_ref[...] += _colsum(dc)

    tile = pl.BlockSpec((tm, B), lambda i: (i, 0))
    vec = pl.BlockSpec((1, B), lambda i: (0, 0))
    return pl.pallas_call(
        body, name=name, grid=(T // tm,), in_specs=[tile, tile, vec, vec], out_specs=[tile, vec, vec, vec],
        out_shape=[jax.ShapeDtypeStruct((T, B), F32)] + [jax.ShapeDtypeStruct((1, B), F32)] * 3,
        compiler_params=_params("arbitrary"))(dcv, c, ln_g, ln_b)


def _conv_bwd(proj, dc, w_dw, dproj, *, name, carry=None):
    T = proj.shape[0]
    B = proj.shape[1] // 8
    tm, nh, lb = _conv_tiles(T, B)
    nt = T // tm
    n_halo = T // CONV_HALO

    def body(ap_ref, gp_ref, a_ref, g_ref, dc_ref, dcn_ref, w_ref, _alias,
             dp_ref, dbin_ref, dw_ref, z_sc, dc_sc, dz_sc, dw_sc, zp_sc, dcp_sc):
        i = pl.program_id(0)

        @pl.when(i == 0)
        def _():
            dbin_ref[...] = jnp.zeros_like(dbin_ref)
            dw_sc[...] = jnp.zeros_like(dw_sc)

        a = a_ref[...]
        s = _sig(g_ref[...])
        z_sc[0:CONV_HALO, :] = jnp.where(i > 0, ap_ref[...] * _sig(gp_ref[...]), 0.0)
        z_sc[CONV_HALO:CONV_HALO + tm, :] = a * s
        dc_sc[0:tm, :] = dc_ref[...]
        dc_sc[tm:tm + CONV_HALO, :] = jnp.where(i < nt - 1, dcn_ref[...], 0.0)
        _fill_phases(z_sc, zp_sc, B, lb)
        _fill_phases(dc_sc, dcp_sc, B, lb)
        for cb in range(B // lb):
            ls = slice(cb * lb, (cb + 1) * lb)
            for rc in range(tm // CONV_ROWS):
                r0 = rc * CONV_ROWS
                acc = jnp.zeros((CONV_ROWS, lb), F32)
                for k in range(CONV_WIDTH):
                    acc = acc + w_ref[k:k + 1, ls] * _shifted(dc_sc, dcp_sc, r0 + (CONV_WIDTH - 1) - k, ls)
                dz_sc[r0:r0 + CONV_ROWS, ls] = acc
            for k in range(CONV_WIDTH):
                part = jnp.zeros((8, lb), F32)
                for rc in range(tm // CONV_ROWS):
                    r0 = rc * CONV_ROWS
                    prod = dc_sc[r0:r0 + CONV_ROWS, ls] * _shifted(
                        z_sc, zp_sc, r0 + CONV_HALO - (CONV_WIDTH - 1) + k, ls)
                    part = part + jnp.sum(prod.reshape(CONV_ROWS // 8, 8, lb), axis=0)
                dw_sc[8 * k:8 * k + 8, ls] += part
        dz = dz_sc[...]
        da = dz * s
        dg = dz * a * s * (1.0 - s)
        dp_ref[:, 0:B] = da.astype(BF16)
        dp_ref[:, B:2 * B] = dg.astype(BF16)
        dbin_ref[:, 0:B] += _colsum(da)
        dbin_ref[:, B:2 * B] += _colsum(dg)

        @pl.when(i == nt - 1)
        def _():
            for k in range(CONV_WIDTH):
                dw_ref[k:k + 1, :] = _colsum(dw_sc[8 * k:8 * k + 8, :])

    halo_a = pl.BlockSpec((CONV_HALO, B), lambda i: (jnp.maximum(i * nh - 1, 0), 2))
    halo_g = pl.BlockSpec((CONV_HALO, B), lambda i: (jnp.maximum(i * nh - 1, 0), 3))
    halo_dc = pl.BlockSpec((CONV_HALO, B), lambda i: (jnp.minimum((i + 1) * nh, n_halo - 1), 0))
    return _call(
        body, name=name, grid=(nt,),
        in_specs=[halo_a, halo_g, pl.BlockSpec((tm, B), lambda i: (i, 2)), pl.BlockSpec((tm, B), lambda i: (i, 3)),
                  pl.BlockSpec((tm, B), lambda i: (i, 0)), halo_dc,
                  pl.BlockSpec((CONV_WPAD, B), lambda i: (0, 0)), ANY],
        out_specs=[pl.BlockSpec((tm, 2 * B), lambda i: (i, 1)), pl.BlockSpec((1, 2 * B), lambda i: (0, 0)),
                   pl.BlockSpec((CONV_WIDTH, B), lambda i: (0, 0))],
        out_shape=[jax.ShapeDtypeStruct(dproj.shape, BF16), jax.ShapeDtypeStruct((1, 2 * B), F32),
                   jax.ShapeDtypeStruct((CONV_WIDTH, B), F32)],
        scratch_shapes=[pltpu.VMEM((CONV_HALO + tm, B), F32), pltpu.VMEM((tm + CONV_HALO, B), F32),
                        pltpu.VMEM((tm, B), F32), pltpu.VMEM((8 * CONV_WIDTH, B), F32),
                        pltpu.VMEM((7, CONV_HALO + tm - 8, B), F32), pltpu.VMEM((7, CONV_HALO + tm - 8, B), F32)],
        aliases={7: 0}, sem=("arbitrary",), args=[proj, proj, proj, proj, dc, dc, w_dw, dproj], carry=carry)


def _mix_fwd_gate(sg, cv, proj, wa, wb, *, name):
    T, A = sg.shape
    D = wa.shape[1]
    tm = _tile(T, 256)

    def body(sg_ref, cv_ref, la_ref, lb_ref, wa_ref, wb_ref, ya_ref, yb_ref, m_ref):
        ya = _dot(sg_ref[...], wa_ref[...])
        yb = _dot(cv_ref[...], wb_ref[...])
        ya_ref[...] = ya.astype(BF16)
        yb_ref[...] = yb.astype(BF16)
        m_ref[...] = (_sig(la_ref[...]) * ya + _sig(lb_ref[...]) * yb).astype(BF16)

    act = pl.BlockSpec((tm, A), lambda i: (i, 0))
    wide = pl.BlockSpec((tm, D), lambda i: (i, 0))
    wspec = pl.BlockSpec((A, D), lambda i: (0, 0))
    return pl.pallas_call(
        body, name=name, grid=(T // tm,),
        in_specs=[act, act, pl.BlockSpec((tm, D), lambda i: (i, 2)), pl.BlockSpec((tm, D), lambda i: (i, 3)),
                  wspec, wspec],
        out_specs=[wide] * 3, out_shape=[jax.ShapeDtypeStruct((T, D), BF16)] * 3,
        compiler_params=_params("parallel"))(sg, cv, proj, proj, wa, wb)


def _mix_fwd_out(m, wout, xh, lg, lb, *, name):
    T, D = xh.shape
    tm = _tile(T, 512)

    def body(m_ref, w_ref, xh_ref, lg_ref, lb_ref, xho_ref, rstd_ref):
        r = ALPHA * (xh_ref[...] * lg_ref[...] + lb_ref[...]) + _dot(m_ref[...], w_ref[...])
        xho, rstd = _ln_stats(r)
        xho_ref[...] = xho
        rstd_ref[...] = jnp.broadcast_to(rstd, (tm, LANES))

    row = pl.BlockSpec((tm, D), lambda i: (i, 0))
    vec = pl.BlockSpec((1, D), lambda i: (0, 0))
    return pl.pallas_call(
        body, name=name, grid=(T // tm,),
        in_specs=[row, pl.BlockSpec((D, D), lambda i: (0, 0)), row, vec, vec],
        out_specs=[row, pl.BlockSpec((tm, LANES), lambda i: (i, 0))],
        out_shape=[jax.ShapeDtypeStruct((T, D), F32), jax.ShapeDtypeStruct((T, LANES), F32)],
        compiler_params=_params("parallel"))(m, wout, xh, lg, lb)


def _mix_bwd_gate(dr, wout, proj, ya, yb, *, name, carry=None):
    T, D = dr.shape
    N = proj.shape[1]
    tm = _tile(T, 256)

    def body(dr_ref, w_ref, la_ref, lb_ref, ya_ref, yb_ref, dya_ref, dyb_ref, dp_ref, dbin_ref):
        @pl.when(pl.program_id(0) == 0)
        def _():
            dbin_ref[...] = jnp.zeros_like(dbin_ref)
        dm = _dot_nt(dr_ref[...].astype(BF16), w_ref[...])
        sa = _sig(la_ref[...])
        sb = _sig(lb_ref[...])
        dya_ref[...] = (dm * sa).astype(BF16)
        dyb_ref[...] = (dm * sb).astype(BF16)
        dla = dm * ya_ref[...].astype(F32) * sa * (1.0 - sa)
        dlb = dm * yb_ref[...].astype(F32) * sb * (1.0 - sb)
        dp_ref[:, 0:D] = dla.astype(BF16)
        dp_ref[:, D:2 * D] = dlb.astype(BF16)
        dbin_ref[:, 0:D] += _colsum(dla)
        dbin_ref[:, D:2 * D] += _colsum(dlb)

    row = pl.BlockSpec((tm, D), lambda i: (i, 0))
    return _call(
        body, name=name, grid=(T // tm,),
        in_specs=[row, pl.BlockSpec((D, D), lambda i: (0, 0)), pl.BlockSpec((tm, D), lambda i: (i, 2)),
                  pl.BlockSpec((tm, D), lambda i: (i, 3)), row, row],
        out_specs=[row, row, pl.BlockSpec((tm, 2 * D), lambda i: (i, 1)), pl.BlockSpec((1, 2 * D), lambda i: (0, 0))],
        out_shape=[jax.ShapeDtypeStruct((T, D), BF16), jax.ShapeDtypeStruct((T, D), BF16),
                   jax.ShapeDtypeStruct((T, N), BF16), jax.ShapeDtypeStruct((1, 2 * D), F32)],
        sem=("arbitrary",), args=[dr, wout, proj, proj, ya, yb], carry=carry)


def _mix_bwd_proj(dya, dyb, wa, wb, *, name):
    T, D = dya.shape
    A = wa.shape[0]
    tm = _tile(T, 512)

    def body(dya_ref, dyb_ref, wa_ref, wb_ref, dsg_ref, dcv_ref):
        dsg_ref[...] = _dot_nt(dya_ref[...], wa_ref[...])
        dcv_ref[...] = _dot_nt(dyb_ref[...], wb_ref[...])

    row = pl.BlockSpec((tm, D), lambda i: (i, 0))
    wspec = pl.BlockSpec((A, D), lambda i: (0, 0))
    act = pl.BlockSpec((tm, A), lambda i: (i, 0))
    return pl.pallas_call(
        body, name=name, grid=(T // tm,), in_specs=[row, row, wspec, wspec], out_specs=[act, act],
        out_shape=[jax.ShapeDtypeStruct((T, A), F32)] * 2,
        compiler_params=_params("parallel"))(dya, dyb, wa, wb)


def _mesh_pos():
    return lax.axis_index("x"), lax.axis_index("y"), lax.axis_index("c")


def _shard_view(ref, p, shape, axis):
    r, c = shape
    if axis == 0:
        return ref.at[pl.ds(pl.multiple_of(p * r, 16), r), :]
    return ref.at[:, pl.ds(pl.multiple_of(p * c, LANES), c)]


def _all_gather(shards, axes, progressive=False):
    n = len(shards)
    shapes = [s.shape for s in shards]
    sizes = [s.size * s.dtype.itemsize for s in shards]
    if progressive:
        done = [sum(sizes[:t + 1]) / sum(sizes) for t in range(n)]
        when = (0.0,) + tuple(min(0.97, 0.04 + 0.93 * d) for d in done) + (1.0,)
    else:
        when = (0.0, 1.0)

    def run(ins, outs, sems, phase):
        send_sems, recv_sems, local_sems = sems
        x, y, c = _mesh_pos()
        me, sibling = (x, y, c), (x, y, 1 - c)
        chips = [(1 - x, y), (x, 1 - y), (1 - x, 1 - y)]

        def view(t, pos):
            px, py, pc = pos
            return _shard_view(outs[t], 4 * px + 2 * py + pc, shapes[t], axes[t])

        def copy(t, k, block, to, src=None):
            return pltpu.make_async_remote_copy(
                src_ref=view(t, block) if src is None else src, dst_ref=view(t, block),
                send_sem=send_sems.at[7 * t + k], recv_sem=recv_sems.at[7 * t + k],
                device_id=to, device_id_type=MESH)

        mine = [pltpu.make_async_copy(ins[t], view(t, me), local_sems.at[t]) for t in range(n)]
        first = []
        for t in range(n):
            first.append(copy(t, 0, me, sibling, src=ins[t]))
            first += [copy(t, 1 + j, me, (*chip, c), src=ins[t]) for j, chip in enumerate(chips)]
        if phase == 0:
            for cp in mine + first:
                cp.start()
            return

        def forward(t):
            for j, chip in enumerate(chips):
                copy(t, 1 + j, (*chip, c), me).wait_recv()
                copy(t, 4 + j, (*chip, c), sibling).start()

        if progressive and phase <= n:
            forward(phase - 1)
            return
        if not progressive:
            for t in range(n):
                forward(t)
        passed = [copy(t, 4 + j, (*chip, c), sibling) for t in range(n) for j, chip in enumerate(chips)]
        for t in range(n):
            copy(t, 0, sibling, me).wait_recv()
            for j, chip in enumerate(chips):
                copy(t, 4 + j, (*chip, 1 - c), me).wait_recv()
        for cp in first + passed:
            cp.wait_send()
        for cp in mine:
            cp.wait()

    out_shape = [jax.ShapeDtypeStruct((N_DEV * s.shape[0], s.shape[1]) if ax == 0
                                      else (s.shape[0], N_DEV * s.shape[1]), s.dtype)
                 for s, ax in zip(shards, axes)]
    return _Comm(shards, out_shape, [pltpu.SemaphoreType.DMA((7 * n,)), pltpu.SemaphoreType.DMA((7 * n,)),
                                     pltpu.SemaphoreType.DMA((n,))], run, when)


def _all_gather_routed(shards, axes):
    n = len(shards)
    shapes = [s.shape for s in shards]
    assert all(s[0] % 32 == 0 for s in shapes)

    def run(ins, outs, sems, phase):
        send_sems, recv_sems, local_sems = sems
        x, y, c = _mesh_pos()
        me, sibling = (x, y, c), (x, y, 1 - c)
        nx, ny, nd = (1 - x, y), (x, 1 - y), (1 - x, 1 - y)

        def block(t, chip, core):
            return _shard_view(outs[t], 4 * chip[0] + 2 * chip[1] + core, shapes[t], axes[t])

        def half(t, chip, core, h):
            hr = shapes[t][0] // 2
            return block(t, chip, core).at[pl.ds(h * hr, hr), :]

        def own_half(t, h):
            hr = shapes[t][0] // 2
            return ins[t].at[pl.ds(h * hr, hr), :]

        def copy(t, k, src, dst, to):
            return pltpu.make_async_remote_copy(src_ref=src, dst_ref=dst, send_sem=send_sems.at[10 * t + k],
                                                recv_sem=recv_sems.at[10 * t + k], device_id=to, device_id_type=MESH)

        def arrived(t, k, dst):
            copy(t, k, dst, dst, me).wait_recv()

        mine = [pltpu.make_async_copy(ins[t], block(t, (x, y), c), local_sems.at[t]) for t in range(n)]
        own = []
        for t in range(n):
            own += [copy(t, 0, ins[t], block(t, (x, y), c), sibling),
                    copy(t, 1, own_half(t, 0), half(t, (x, y), c, 0), (*nx, c)),
                    copy(t, 2, own_half(t, 1), half(t, (x, y), c, 1), (*nx, c)),
                    copy(t, 3, own_half(t, 1), half(t, (x, y), c, 1), (*ny, c)),
                    copy(t, 4, own_half(t, 0), half(t, (x, y), c, 0), (*ny, c))]
        if phase == 0:
            for cp in mine + own:
                cp.start()
            return
        relays = []

        def relay(t, k, view, to):
            cp = copy(t, k, view, view, to)
            cp.start()
            relays.append(cp)

        for t in range(n):
            arrived(t, 1, half(t, nx, c, 0))
            relay(t, 5, half(t, nx, c, 0), (*ny, c))
            arrived(t, 3, half(t, ny, c, 1))
            relay(t, 6, half(t, ny, c, 1), (*nx, c))
            arrived(t, 2, half(t, nx, c, 1))
            relay(t, 7, block(t, nx, c), sibling)
            arrived(t, 4, half(t, ny, c, 0))
            relay(t, 8, block(t, ny, c), sibling)
            arrived(t, 5, half(t, nd, c, 0))
            arrived(t, 6, half(t, nd, c, 1))
            relay(t, 9, block(t, nd, c), sibling)
        for t in range(n):
            arrived(t, 0, block(t, (x, y), 1 - c))
            arrived(t, 7, block(t, nx, 1 - c))
            arrived(t, 8, block(t, ny, 1 - c))
            arrived(t, 9, block(t, nd, 1 - c))
        for cp in own + relays:
            cp.wait_send()
        for cp in mine:
            cp.wait()

    out_shape = [jax.ShapeDtypeStruct((N_DEV * s.shape[0], s.shape[1]) if ax == 0
                                      else (s.shape[0], N_DEV * s.shape[1]), s.dtype)
                 for s, ax in zip(shards, axes)]
    return _Comm(shards, out_shape, [pltpu.SemaphoreType.DMA((10 * n,)), pltpu.SemaphoreType.DMA((10 * n,)),
                                     pltpu.SemaphoreType.DMA((n,))], run)


def _rs_to_sibling(grads, shapes, axes):
    n = len(grads)

    def run(gs, outs, sems, phase):
        send_sems, recv_sems = sems
        x, y, c = _mesh_pos()
        copies = [pltpu.make_async_remote_copy(
            src_ref=_shard_view(gs[t], 2 * k + (1 - c), shapes[t], axes[t]), dst_ref=outs[t].at[k],
            send_sem=send_sems.at[4 * t + k], recv_sem=recv_sems.at[4 * t + k],
            device_id=(x, y, 1 - c), device_id_type=MESH) for t in range(n) for k in range(4)]
        if phase == 0:
            for cp in copies:
                cp.start()
            return
        for cp in copies:
            cp.wait_recv()
        for cp in copies:
            cp.wait_send()

    return _Comm(grads, [jax.ShapeDtypeStruct((4,) + tuple(s), BF16) for s in shapes],
                 [pltpu.SemaphoreType.DMA((4 * n,)), pltpu.SemaphoreType.DMA((4 * n,))], run)


def _rs_pair_sum(g, recv, cidx, shape, axis, *, name):
    r, c = shape
    tr = _tile(r, max(8, (1 << 20) // c), 16)
    nr = r // tr

    def body(c_ref, g_ref, rv_ref, o_ref):
        o_ref[...] = (g_ref[...].astype(F32) + rv_ref[...].astype(F32)).astype(BF16)

    if axis == 1:
        g_spec = pl.BlockSpec((tr, c), lambda k, i, s: (i, 2 * k + s[0]))
    else:
        g_spec = pl.BlockSpec((tr, c), lambda k, i, s: ((2 * k + s[0]) * nr + i, 0))
    blk = pl.BlockSpec((None, tr, c), lambda k, i, s: (k, i, 0))
    return pl.pallas_call(
        body, name=name,
        grid_spec=pltpu.PrefetchScalarGridSpec(num_scalar_prefetch=1, grid=(4, nr), in_specs=[g_spec, blk],
                                               out_specs=blk),
        out_shape=jax.ShapeDtypeStruct((4, r, c), BF16),
        compiler_params=_params("parallel", "parallel"))(cidx, g, recv)


def _rs_to_chips(parts):
    n = len(parts)

    def run(ps, outs, sems, phase):
        send_sems, recv_sems, local_sems = sems
        x, y, c = _mesh_pos()
        my_chip = 2 * x + y
        peers = [(1 - x, y), (x, 1 - y), (1 - x, 1 - y)]
        local = [pltpu.make_async_copy(ps[t].at[my_chip], outs[t].at[my_chip], local_sems.at[t]) for t in range(n)]
        sends = [pltpu.make_async_remote_copy(
            src_ref=ps[t].at[2 * px + py], dst_ref=outs[t].at[my_chip],
            send_sem=send_sems.at[3 * t + j], recv_sem=recv_sems.at[3 * t + j],
            device_id=(px, py, c), device_id_type=MESH) for t in range(n) for j, (px, py) in enumerate(peers)]
        if phase == 0:
            for cp in local + sends:
                cp.start()
            return
        for t in range(n):
            for j, (px, py) in enumerate(peers):
                pltpu.make_async_remote_copy(
                    src_ref=ps[t].at[2 * px + py], dst_ref=outs[t].at[2 * px + py],
                    send_sem=send_sems.at[3 * t + j], recv_sem=recv_sems.at[3 * t + j],
                    device_id=(x, y, c), device_id_type=MESH).wait_recv()
        for cp in sends:
            cp.wait_send()
        for cp in local:
            cp.wait()

    return _Comm(parts, [jax.ShapeDtypeStruct(p.shape, BF16) for p in parts],
                 [pltpu.SemaphoreType.DMA((3 * n,)), pltpu.SemaphoreType.DMA((3 * n,)),
                  pltpu.SemaphoreType.DMA((n,))], run)


def _exchange_small(buf):
    def run(ins, outs, sems, phase):
        (in_ref,), (slots,) = ins, outs
        send_sems, recv_sems, local_sem = sems
        x, y, c = _mesh_pos()
        me = 4 * x + 2 * y + c
        local = pltpu.make_async_copy(in_ref, slots.at[me], local_sem.at[0])
        flips = [(fx, fy, fc) for fx in (0, 1) for fy in (0, 1) for fc in (0, 1)][1:]
        peers = [(1 - x if fx else x, 1 - y if fy else y, 1 - c if fc else c) for fx, fy, fc in flips]
        sends = [pltpu.make_async_remote_copy(src_ref=in_ref, dst_ref=slots.at[me], send_sem=send_sems.at[k],
                                              recv_sem=recv_sems.at[k], device_id=peer, device_id_type=MESH)
                 for k, peer in enumerate(peers)]
        if phase == 0:
            for cp in [local] + sends:
                cp.start()
            return
        for k, (px, py, pc) in enumerate(peers):
            pltpu.make_async_remote_copy(src_ref=in_ref, dst_ref=slots.at[4 * px + 2 * py + pc],
                                         send_sem=send_sems.at[k], recv_sem=recv_sems.at[k],
                                         device_id=(x, y, c), device_id_type=MESH).wait_recv()
        for cp in sends:
            cp.wait_send()
        local.wait()

    return _Comm([buf], [jax.ShapeDtypeStruct((N_DEV,) + buf.shape, F32)],
                 [pltpu.SemaphoreType.DMA((7,)), pltpu.SemaphoreType.DMA((7,)), pltpu.SemaphoreType.DMA((1,))], run)


def _sum_slots(slots, *, name):
    _, R, C = slots.shape
    tr = _tile(R, 512)

    def body(s_ref, o_ref):
        acc = s_ref[0]
        for p in range(1, N_DEV):
            acc = acc + s_ref[p]
        o_ref[...] = acc

    return pl.pallas_call(
        body, name=name, grid=(R // tr,), in_specs=[pl.BlockSpec((N_DEV, tr, C), lambda i: (0, i, 0))],
        out_specs=pl.BlockSpec((tr, C), lambda i: (i, 0)), out_shape=jax.ShapeDtypeStruct((R, C), F32),
        compiler_params=_params("parallel"))(slots)


def _adam_math(g, w, m, v):
    m_new = ADAM_B1 * m + (1.0 - ADAM_B1) * g
    v_new = ADAM_B2 * v + (1.0 - ADAM_B2) * (g * g)
    m_hat = m_new / ADAM_C1
    v_hat = v_new / ADAM_C2
    delta = -ADAM_LR * (m_hat / (jnp.sqrt(v_hat) + ADAM_EPS) + ADAM_WD * w)
    return delta, m_new, v_new


def _adamw_sharded(q, w, m, v, *, name):
    r, c = w.shape
    tr = _tile(r, max(8, (1 << 18) // c), 16)

    def body(q_ref, w_ref, m_ref, v_ref, g_ref, d_ref, mo_ref, vo_ref):
        g = ((q_ref[0].astype(F32) + q_ref[1].astype(F32)) + q_ref[2].astype(F32)) + q_ref[3].astype(F32)
        g_ref[...] = g
        d_ref[...], mo_ref[...], vo_ref[...] = _adam_math(g, w_ref[...], m_ref[...], v_ref[...])

    blk = pl.BlockSpec((tr, c), lambda i: (i, 0))
    return pl.pallas_call(
        body, name=name, grid=(r // tr,),
        in_specs=[pl.BlockSpec((4, tr, c), lambda i: (0, i, 0)), blk, blk, blk], out_specs=[blk] * 4,
        out_shape=[jax.ShapeDtypeStruct((r, c), F32)] * 4,
        compiler_params=_params("parallel"))(q, w, m, v)


def _adamw_plain(g, w, m, v, *, name):
    r, c = w.shape
    tr = _tile(r, 512)

    def body(g_ref, w_ref, m_ref, v_ref, d_ref, mo_ref, vo_ref):
        d_ref[...], mo_ref[...], vo_ref[...] = _adam_math(g_ref[...], w_ref[...], m_ref[...], v_ref[...])

    blk = pl.BlockSpec((tr, c), lambda i: (i, 0))
    return pl.pallas_call(
        body, name=name, grid=(r // tr,), in_specs=[blk] * 4, out_specs=[blk] * 3,
        out_shape=[jax.ShapeDtypeStruct((r, c), F32)] * 3,
        compiler_params=_params("parallel"))(g, w, m, v)


def _pack_rows(arrays):
    return jnp.concatenate([a.reshape(-1, LANES) for a in arrays], axis=0)


def kernel(x, ffn1_w_gu, ffn1_w_down, ln1_g, ln1_b, w_in, b_in, sgu_ln_g, sgu_ln_b, sgu_w_s, sgu_b_s, w_a_proj, conv_w_dw, conv_b_dw, conv_ln_g, conv_ln_b, w_b_proj, w_out, ln2_g, ln2_b, ffn2_w_gu, ffn2_w_down, ln3_g, ln3_b, loss_target, m_ffn1_w_gu, m_ffn1_w_down, m_ln1_g, m_ln1_b, m_w_in, m_b_in, m_sgu_ln_g, m_sgu_ln_b, m_sgu_w_s, m_sgu_b_s, m_w_a_proj, m_conv_w_dw, m_conv_b_dw, m_conv_ln_g, m_conv_ln_b, m_w_b_proj, m_w_out, m_ln2_g, m_ln2_b, m_ffn2_w_gu, m_ffn2_w_down, m_ln3_g, m_ln3_b, v_ffn1_w_gu, v_ffn1_w_down, v_ln1_g, v_ln1_b, v_w_in, v_b_in, v_sgu_ln_g, v_sgu_ln_b, v_sgu_w_s, v_sgu_b_s, v_w_a_proj, v_conv_w_dw, v_conv_b_dw, v_conv_ln_g, v_conv_ln_b, v_w_b_proj, v_w_out, v_ln2_g, v_ln2_b, v_ffn2_w_gu, v_ffn2_w_down, v_ln3_g, v_ln3_b):
    given = dict(locals())
    w = {n: given[n][0] for n in WEIGHTS}
    mom = {n: given["m_" + n][0] for n in WEIGHTS}
    var = {n: given["v_" + n][0] for n in WEIGHTS}
    xt = x[0]
    target = loss_target[0]
    T, D = xt.shape
    A = w['w_a_proj'].shape[0]

    big_names = list(BIG)
    early = ['ffn1_w_gu', 'ffn1_w_down']
    late = [n for n in big_names if n not in early]
    conv_w_pad = jnp.pad(w['conv_w_dw'], ((0, CONV_WPAD - CONV_WIDTH), (0, 0)))
    w_bf = {n: w[n].astype(BF16) for n in big_names}
    full = dict(zip(early, _comm_call(_all_gather_routed([w_bf[n] for n in early], [BIG[n] for n in early]),
                                      name="all_gather_ffn1")))
    gather_late = _all_gather([w_bf[n] for n in late] + [conv_w_pad], [BIG[n] for n in late] + [1],
                              progressive=True)

    def row(v):
        return v.reshape(1, -1)

    ones = jnp.ones((1, D), F32)
    zeros = jnp.zeros((1, D), F32)
    w_s = w['sgu_w_s']
    w_st = jnp.swapaxes(w_s, 1, 2)
    b_sb = jnp.broadcast_to(w['sgu_b_s'][:, :, None], w_s.shape)

    (gate1, up1, xb0, xh1, rstd1), gathered = _ffn_fwd(xt, ones, zeros, full['ffn1_w_gu'], full['ffn1_w_down'],
                                                  affine=False, name="ffn1_fwd", carry=gather_late)
    full.update(zip(late, gathered[:-1]))
    conv_w_full = gathered[-1]
    g1, b1 = row(w['ln1_g']), row(w['ln1_b'])
    proj, xb1 = _inproj_fwd(xh1, g1, b1, full['w_in'], row(w['b_in']), name="inproj_fwd")
    sg = _sgu_fwd(proj, row(w['sgu_ln_g']), row(w['sgu_ln_b']), w_s, b_sb, name="sgu_fwd")
    conv_out, cv = _conv_fwd(proj, conv_w_full, row(w['conv_b_dw']), row(w['conv_ln_g']), row(w['conv_ln_b']),
                             name="conv_fwd")
    ya, yb, mixed = _mix_fwd_gate(sg, cv, proj, full['w_a_proj'], full['w_b_proj'], name="mix_fwd_gate")
    xh2, rstd2 = _mix_fwd_out(mixed, full['w_out'], xh1, g1, b1, name="mix_fwd_out")
    g2, b2 = row(w['ln2_g']), row(w['ln2_b'])
    gate2, up2, xb2, dr3, loss_part, d_ln3_g, d_ln3_b = _ffn_fwd(
        xh2, g2, b2, full['ffn2_w_gu'], full['ffn2_w_down'], affine=True, name="ffn2_fwd_loss",
        final=(row(w['ln3_g']), row(w['ln3_b']), target))

    F = full['ffn2_w_down'].shape[0]
    h2, dgate2, dup2, dr2, d_ln2_g, d_ln2_b = _ffn_bwd(dr3, gate2, up2, full['ffn2_w_gu'], full['ffn2_w_down'],
                                                      name="ffn2_bwd", prev=(xh2, rstd2, g2))
    G, P, Q = {}, {}, {}
    cidx = lax.axis_index("c").astype(jnp.int32).reshape(1)

    def to_sibling(names):
        return _rs_to_sibling([G[n] for n in names], [w[n].shape for n in names], [BIG[n] for n in names])

    def pair_sum(names, received):
        for n, rv in zip(names, received):
            P[n] = _rs_pair_sum(G[n], rv, cidx, w[n].shape, BIG[n], name="rs_pair_sum_" + n)

    def to_chips(names):
        return _rs_to_chips([P[n] for n in names])

    G['ffn2_w_down'] = _mm_tn(h2, dr3, name="dw_ffn2_down", tm_pref=1408, tn_pref=2048, scale=0.5)
    gu, rv = _mm_tn(xb2, dgate2, name="dw_ffn2_gate", tm_pref=2048, tn_pref=1408,
                    n_total=2 * F, carry=to_sibling(['ffn2_w_down']))
    pair_sum(['ffn2_w_down'], rv)
    G['ffn2_w_gu'], q = _mm_tn(xb2, dup2, name="dw_ffn2_up", tm_pref=2048, tn_pref=1408,
                               into=gu, col_off=F, n_total=2 * F, carry=to_chips(['ffn2_w_down']))
    Q['ffn2_w_down'] = q[0]

    (dya, dyb, dproj, dbin_gate), rv = _mix_bwd_gate(dr2, full['w_out'], proj, ya, yb, name="mix_bwd_gate",
                                                     carry=to_sibling(['ffn2_w_gu']))
    pair_sum(['ffn2_w_gu'], rv)
    dsg, dcv = _mix_bwd_proj(dya, dyb, full['w_a_proj'], full['w_b_proj'], name="mix_bwd_proj")
    dproj, dbin_sgu, d_sgu_ln_g, d_sgu_ln_b, d_w_s, d_b_s = _sgu_bwd(
        proj, dsg, row(w['sgu_ln_g']), row(w['sgu_ln_b']), w_s, w_st, b_sb, dproj, name="sgu_bwd")
    dconv, d_conv_ln_g, d_conv_ln_b, d_conv_b = _conv_bwd_ln(dcv, conv_out, row(w['conv_ln_g']),
                                                            row(w['conv_ln_b']), name="conv_bwd_ln")
    (dproj, dbin_conv, d_conv_w), q = _conv_bwd(proj, dconv, conv_w_full, dproj, name="conv_bwd",
                                                carry=to_chips(['ffn2_w_gu']))
    Q['ffn2_w_gu'] = q[0]

    mid = ['w_out', 'w_a_proj', 'w_b_proj']
    G['w_out'] = _mm_tn(mixed, dr2, name="dw_out", tm_pref=2048, tn_pref=1024)
    G['w_a_proj'] = _mm_tn(sg, dya, name="dw_a_proj", tm_pref=1024, tn_pref=2048)
    G['w_b_proj'] = _mm_tn(cv, dyb, name="dw_b_proj", tm_pref=1024, tn_pref=2048)
    G['w_in'], rv = _mm_tn(xb1, dproj, name="dw_in", tm_pref=2048, tn_pref=1024, carry=to_sibling(mid))
    pair_sum(mid, rv)
    both = _join(to_chips(mid), to_sibling(['w_in']))
    (dr1, d_ln1_g, d_ln1_b), moved = _inproj_bwd(dproj, full['w_in'], dr2, xh1, rstd1, g1, name="inproj_bwd",
                                                 carry=both)
    q, rv = both.split(moved)
    Q.update(zip(mid, q))
    pair_sum(['w_in'], rv)

    h1, dgate1, dup1, grad_x = _ffn_bwd(dr1, gate1, up1, full['ffn1_w_gu'], full['ffn1_w_down'], name="ffn1_bwd")
    G['ffn1_w_down'], q = _mm_tn(h1, dr1, name="dw_ffn1_down", tm_pref=1408, tn_pref=2048, scale=0.5,
                                 carry=to_chips(['w_in']))
    Q['w_in'] = q[0]
    small_g = {'ln1_g': d_ln1_g, 'ln1_b': d_ln1_b,
               'b_in': jnp.concatenate([dbin_sgu, dbin_conv, dbin_gate], axis=1),
               'sgu_ln_g': d_sgu_ln_g, 'sgu_ln_b': d_sgu_ln_b, 'sgu_w_s': d_w_s, 'sgu_b_s': d_b_s,
               'conv_b_dw': d_conv_b, 'conv_ln_g': d_conv_ln_g, 'conv_ln_b': d_conv_ln_b,
               'ln2_g': d_ln2_g, 'ln2_b': d_ln2_b, 'ln3_g': d_ln3_g, 'ln3_b': d_ln3_b}
    packed = _pack_rows([small_g[n] for n in SMALL] + [d_conv_w, loss_part])
    both = _join(to_sibling(['ffn1_w_down']), _exchange_small(packed))
    gu, moved = _mm_tn(xb0, dgate1, name="dw_ffn1_gate", tm_pref=2048, tn_pref=1408, n_total=2 * F, carry=both)
    rv, slots = both.split(moved)
    pair_sum(['ffn1_w_down'], rv)
    reduced = _sum_slots(slots[0], name="sum_small")
    G['ffn1_w_gu'], q = _mm_tn(xb0, dup1, name="dw_ffn1_up", tm_pref=2048, tn_pref=1408, into=gu, col_off=F,
                               n_total=2 * F, carry=to_chips(['ffn1_w_down']))
    Q['ffn1_w_down'] = q[0]
    pair_sum(['ffn1_w_gu'], _comm_call(to_sibling(['ffn1_w_gu']), name="rs_to_sibling_last"))
    Q['ffn1_w_gu'] = _comm_call(to_chips(['ffn1_w_gu']), name="rs_to_chips_last")[0]

    grads, deltas, new_m, new_v = {}, {}, {}, {}
    for n in big_names:
        grads[n], deltas[n], new_m[n], new_v[n] = _adamw_sharded(Q[n], w[n], mom[n], var[n], name="adamw_" + n)

    B = conv_w_full.shape[1]
    n_small_rows = sum(w[n].size for n in SMALL) // LANES
    conv_rows = CONV_WIDTH * B // LANES
    d_small, m_small, v_small = _adamw_plain(
        reduced[:n_small_rows], _pack_rows([w[n] for n in SMALL]), _pack_rows([mom[n] for n in SMALL]),
        _pack_rows([var[n] for n in SMALL]), name="adamw_small")
    off = 0
    for n in SMALL:
        rows = w[n].size // LANES
        grads[n] = reduced[off:off + rows].reshape(w[n].shape)
        deltas[n] = d_small[off:off + rows].reshape(w[n].shape)
        new_m[n] = m_small[off:off + rows].reshape(w[n].shape)
        new_v[n] = v_small[off:off + rows].reshape(w[n].shape)
        off += rows
    conv_g_full = reduced[off:off + conv_rows].reshape(CONV_WIDTH, B)
    bs = w['conv_w_dw'].shape[1]
    my_block = 4 * lax.axis_index("x") + 2 * lax.axis_index("y") + lax.axis_index("c")
    grads['conv_w_dw'] = lax.dynamic_slice(conv_g_full, (0, my_block * bs), (CONV_WIDTH, bs))
    deltas['conv_w_dw'], new_m['conv_w_dw'], new_v['conv_w_dw'] = _adamw_plain(
        grads['conv_w_dw'], w['conv_w_dw'], mom['conv_w_dw'], var['conv_w_dw'], name="adamw_conv_w")
    loss = reduced[off + conv_rows, 0]

    def lead(a):
        return a[None]

    return (loss, grad_x[None], *[lead(grads[n]) for n in WEIGHTS], *[lead(deltas[n]) for n in WEIGHTS],
            *[lead(new_m[n]) for n in WEIGHTS], *[lead(new_v[n]) for n in WEIGHTS])
```

```python
import functools
import math

import jax
import jax.numpy as jnp
from jax import lax
from jax.experimental import pallas as pl
from jax.experimental.pallas import tpu as pltpu

F32 = jnp.float32
BF16 = jnp.bfloat16

ALPHA = 2.0 ** 0.25
LN_EPS = 1e-5
CONV_WIDTH = 31
CONV_HALO = 32
CONV_ROWS = 64
CONV_WPAD = 32
CHUNK = 64
GMLP_BLOCK = 128
A_GROUPS = 8
N_DEV = 8
LANES = 128

ADAM_LR = 0.001
ADAM_B1 = 0.9
ADAM_B2 = 0.999
ADAM_EPS = 1e-08
ADAM_WD = 0.01
ADAM_STEP = 10
ADAM_C1 = 1.0 - ADAM_B1 ** ADAM_STEP
ADAM_C2 = 1.0 - ADAM_B2 ** ADAM_STEP

VMEM_LIMIT_BYTES = 60 * 2 ** 20
MESH = pl.DeviceIdType.MESH
ANY = pl.BlockSpec(memory_space=pl.ANY)

WEIGHTS = ['ffn1_w_gu', 'ffn1_w_down', 'ln1_g', 'ln1_b', 'w_in', 'b_in', 'sgu_ln_g', 'sgu_ln_b', 'sgu_w_s',
           'sgu_b_s', 'w_a_proj', 'conv_w_dw', 'conv_b_dw', 'conv_ln_g', 'conv_ln_b', 'w_b_proj', 'w_out',
           'ln2_g', 'ln2_b', 'ffn2_w_gu', 'ffn2_w_down', 'ln3_g', 'ln3_b']
BIG = {'ffn1_w_gu': 1, 'ffn1_w_down': 0, 'w_in': 1, 'w_a_proj': 1, 'w_b_proj': 1, 'w_out': 0,
       'ffn2_w_gu': 1, 'ffn2_w_down': 0}
SMALL = [n for n in WEIGHTS if n not in BIG and n != 'conv_w_dw']


def _tile(n, pref, mult=8):
    best = None
    for d in range(mult, min(n, pref) + 1, mult):
        if n % d == 0:
            best = d
    return n if best is None else best


def _params(*sem):
    return pltpu.CompilerParams(dimension_semantics=sem, vmem_limit_bytes=VMEM_LIMIT_BYTES)


def _dot(a, b):
    return jnp.dot(a, b, preferred_element_type=F32)


def _dot_nt(a, b):
    return lax.dot_general(a, b, (((1,), (1,)), ((), ())), preferred_element_type=F32)


def _dot_tn(a, b):
    return lax.dot_general(a, b, (((0,), (0,)), ((), ())), preferred_element_type=F32)


def _sig(x):
    return 1.0 / (1.0 + jnp.exp(-x))


_GELU_K = math.sqrt(2.0 / math.pi)
_GELU_C = 0.044715


def _gelu(x):
    t = jnp.tanh(_GELU_K * (x + _GELU_C * x * x * x))
    return 0.5 * x * (1.0 + t)


def _gelu_grad(x):
    x2 = x * x
    t = jnp.tanh(_GELU_K * (x + _GELU_C * x2 * x))
    return 0.5 * (1.0 + t) + 0.5 * x * (1.0 - t * t) * (_GELU_K * (1.0 + 3.0 * _GELU_C * x2))


def _ln_stats(r):
    mu = jnp.mean(r, axis=-1, keepdims=True)
    rc = r - mu
    var = jnp.mean(rc * rc, axis=-1, keepdims=True)
    rstd = lax.rsqrt(var + LN_EPS)
    return rc * rstd, rstd


def _ln_bwd(dy, xh, rstd, g):
    dxh = dy * g
    m1 = jnp.mean(dxh, axis=-1, keepdims=True)
    m2 = jnp.mean(dxh * xh, axis=-1, keepdims=True)
    return rstd * (dxh - m1 - xh * m2)


def _colsum(v):
    return jnp.sum(v, axis=0, keepdims=True)


def _chunk_mask(transposed):
    shift = CHUNK.bit_length() - 1
    r = lax.broadcasted_iota(jnp.int32, (GMLP_BLOCK, GMLP_BLOCK), 0) >> shift
    c = lax.broadcasted_iota(jnp.int32, (GMLP_BLOCK, GMLP_BLOCK), 1) >> shift
    return (r <= c) if transposed else (c <= r)


class _Comm:
    def __init__(self, inputs, out_shape, scratch, run, when=(0.0, 1.0)):
        self.inputs, self.out_shape, self.scratch, self.run = list(inputs), list(out_shape), list(scratch), run
        self.when = tuple(when)
        self.parts = [len(self.out_shape)]

    def split(self, outs):
        res, o = [], 0
        for n in self.parts:
            res.append(list(outs[o:o + n]))
            o += n
        return res


def _join(*comms):
    comms = [c for c in comms if c is not None]
    if not comms:
        return None
    assert all(c.when == (0.0, 1.0) for c in comms)

    def run(ins, outs, sems, phase):
        i = o = s = 0
        for c in comms:
            c.run(ins[i:i + len(c.inputs)], outs[o:o + len(c.out_shape)], sems[s:s + len(c.scratch)], phase)
            i, o, s = i + len(c.inputs), o + len(c.out_shape), s + len(c.scratch)

    joined = _Comm(sum((c.inputs for c in comms), []), sum((c.out_shape for c in comms), []),
                   sum((c.scratch for c in comms), []), run)
    joined.parts = [len(c.out_shape) for c in comms]
    return joined


def _call(body, *, name, grid, in_specs, out_specs, out_shape, args, sem, scratch_shapes=(), aliases=None, carry=None):
    in_specs, out_specs, out_shape = list(in_specs), list(out_specs), list(out_shape)
    scratch_shapes = list(scratch_shapes)
    if carry is None:
        return pl.pallas_call(body, name=name, grid=grid, in_specs=in_specs, out_specs=out_specs,
                              out_shape=out_shape, scratch_shapes=scratch_shapes,
                              input_output_aliases=aliases or {}, compiler_params=_params(*sem))(*args)
    n_in, n_out, n_scr = len(args), len(out_shape), len(scratch_shapes)
    c_in, c_out = len(carry.inputs), len(carry.out_shape)
    n_steps = math.prod(grid)
    at_step = [int(round(f * (n_steps - 1))) for f in carry.when]
    assert at_step[0] == 0 and at_step[-1] == n_steps - 1 and at_step == sorted(at_step)

    def wrapped(*refs):
        ins, c_ins = refs[:n_in], refs[n_in:n_in + c_in]
        o0 = n_in + c_in
        outs, c_outs = refs[o0:o0 + n_out], refs[o0 + n_out:o0 + n_out + c_out]
        s0 = o0 + n_out + c_out
        scr, c_sems = refs[s0:s0 + n_scr], refs[s0 + n_scr:]
        step = 0
        for a, g in enumerate(grid):
            step = step * g + pl.program_id(a)
        pl.when(step == 0)(functools.partial(carry.run, c_ins, c_outs, c_sems, 0))
        body(*ins, *outs, *scr)
        for k in range(1, len(at_step)):
            pl.when(step == at_step[k])(functools.partial(carry.run, c_ins, c_outs, c_sems, k))

    res = pl.pallas_call(
        wrapped, name=name, grid=grid, in_specs=in_specs + [ANY] * c_in, out_specs=out_specs + [ANY] * c_out,
        out_shape=out_shape + carry.out_shape, scratch_shapes=scratch_shapes + carry.scratch,
        input_output_aliases=aliases or {},
        compiler_params=pltpu.CompilerParams(dimension_semantics=("arbitrary",) * len(grid),
                                             vmem_limit_bytes=VMEM_LIMIT_BYTES, has_side_effects=True),
    )(*args, *carry.inputs)
    return list(res[:n_out]), list(res[n_out:])


def _comm_call(comm, *, name):
    n_in, n_out = len(comm.inputs), len(comm.out_shape)

    def body(*refs):
        ins, outs, sems = refs[:n_in], refs[n_in:n_in + n_out], refs[n_in + n_out:]
        for k in range(len(comm.when)):
            comm.run(ins, outs, sems, k)

    return list(pl.pallas_call(
        body, name=name, in_specs=[ANY] * n_in, out_specs=[ANY] * n_out, out_shape=comm.out_shape,
        scratch_shapes=comm.scratch, compiler_params=pltpu.CompilerParams(has_side_effects=True))(*comm.inputs))


def _when(cond, fn):
    if isinstance(cond, bool):
        if cond:
            fn()
    else:
        pl.when(cond)(fn)


def _ffn_tiles(T, F):
    return _tile(T, 512), _tile(F, 512, LANES)


def _row_chunks(tm, rows=128):
    rows = _tile(tm, rows)
    return [slice(r, r + rows) for r in range(0, tm, rows)]


def _hidden_loop(nj, step):
    def pair(jj, c):
        step(2 * jj, 0)
        step(2 * jj + 1, 1)
        return c
    if nj // 2:
        lax.fori_loop(0, nj // 2, pair, 0)
    if nj % 2:
        step(nj - 1, 0)


def _ffn_fwd_looped(xh, lg, lb, wgu, wd, *, affine, name, final=None, carry=None):
    T, D = xh.shape
    F = wd.shape[0]
    tm, tn = _ffn_tiles(T, F)
    nj, nt = F // tn, T // tm
    is_final = final is not None

    def body(*refs):
        if is_final:
            (xh_ref, lg_ref, lb_ref, wgu_hbm, wd_hbm, ng_ref, nb_ref, tgt_hbm,
             gate_hbm, up_hbm, xb_ref, dr_hbm, loss_ref, dng_ref, dnb_ref,
             acc_sc, wg_buf, wu_buf, wd_buf, g_buf, u_buf, w_sem, o_sem, tgt_sc, dr_sc, t_sem) = refs
        else:
            (xh_ref, lg_ref, lb_ref, wgu_hbm, wd_hbm,
             gate_hbm, up_hbm, xb_ref, xho_ref, rstd_ref,
             acc_sc, wg_buf, wu_buf, wd_buf, g_buf, u_buf, w_sem, o_sem) = refs
        i = pl.program_id(0)
        rows = pl.ds(pl.multiple_of(i * tm, tm), tm)

        def cols(j, base=0):
            return pl.ds(pl.multiple_of(base + j * tn, LANES), tn)

        def w_copies(j, slot):
            return (pltpu.make_async_copy(wgu_hbm.at[:, cols(j)], wg_buf.at[slot], w_sem.at[slot]),
                    pltpu.make_async_copy(wgu_hbm.at[:, cols(j, F)], wu_buf.at[slot], w_sem.at[2 + slot]),
                    pltpu.make_async_copy(wd_hbm.at[cols(j), :], wd_buf.at[slot], w_sem.at[4 + slot]))

        def o_copies(j, slot):
            return (pltpu.make_async_copy(g_buf.at[slot], gate_hbm.at[rows, cols(j)], o_sem.at[slot]),
                    pltpu.make_async_copy(u_buf.at[slot], up_hbm.at[rows, cols(j)], o_sem.at[2 + slot]))

        def start(copies):
            for cp in copies:
                cp.start()

        def wait(copies):
            for cp in copies:
                cp.wait()

        def xin(rs):
            v = xh_ref[rs, :]
            return v * lg_ref[...] + lb_ref[...] if affine else v

        _when(i == 0, lambda: start(w_copies(0, 0)))
        if is_final:
            tgt_in = pltpu.make_async_copy(tgt_hbm.at[rows, :], tgt_sc, t_sem.at[0])
            dr_out = pltpu.make_async_copy(dr_sc, dr_hbm.at[rows, :], t_sem.at[1])
            tgt_in.start()
        for rs in _row_chunks(tm):
            xb_ref[rs, :] = xin(rs).astype(BF16)
        acc_sc[...] = jnp.zeros_like(acc_sc)

        def step(j, slot):
            _when(j + 1 < nj, lambda: start(w_copies(j + 1, 1 - slot)))
            wait(w_copies(j, slot))
            xb = xb_ref[...]
            g = _dot(xb, wg_buf[slot])
            u = _dot(xb, wu_buf[slot])
            _when(j >= 2, lambda: wait(o_copies(j - 2, slot)))
            g_buf[slot] = g.astype(BF16)
            u_buf[slot] = u.astype(BF16)
            start(o_copies(j, slot))
            h = g * _sig(g) * u
            acc_sc[...] += _dot(h.astype(BF16), wd_buf[slot])

        _hidden_loop(nj, step)
        _when(i + 1 < nt, lambda: start(w_copies(0, 0)))
        for j in range(max(nj - 2, 0), nj):
            wait(o_copies(j, j % 2))

        if is_final:
            @pl.when(i == 0)
            def _():
                loss_ref[...] = jnp.zeros_like(loss_ref)
                dng_ref[...] = jnp.zeros_like(dng_ref)
                dnb_ref[...] = jnp.zeros_like(dnb_ref)
            tgt_in.wait()
            _when(i > 0, dr_out.wait)
        for rs in _row_chunks(tm):
            r = ALPHA * xin(rs) + 0.5 * acc_sc[rs, :]
            xho, rstd = _ln_stats(r)
            if not is_final:
                xho_ref[rs, :] = xho
                rstd_ref[rs, :] = jnp.broadcast_to(rstd, (rs.stop - rs.start, LANES))
            else:
                ng = ng_ref[...]
                e = xho * ng + nb_ref[...] - tgt_sc[rs, :]
                part = _colsum(jnp.sum(e * e, axis=1, keepdims=True)) * (0.5 / D)
                loss_ref[...] += jnp.broadcast_to(part, loss_ref.shape)
                dy = e * (1.0 / D)
                dng_ref[...] += _colsum(dy * xho)
                dnb_ref[...] += _colsum(dy)
                dr_sc[rs, :] = _ln_bwd(dy, xho, rstd, ng)
        if is_final:
            dr_out.start()
            _when(i == nt - 1, dr_out.wait)

    row = pl.BlockSpec((tm, D), lambda i: (i, 0))
    vec = pl.BlockSpec((1, D), lambda i: (0, 0))
    in_specs = [row, vec, vec, ANY, ANY]
    args = [xh, lg, lb, wgu, wd]
    out_shape = [jax.ShapeDtypeStruct((T, F), BF16), jax.ShapeDtypeStruct((T, F), BF16),
                 jax.ShapeDtypeStruct((T, D), BF16)]
    out_specs = [ANY, ANY, row]
    scratch = [pltpu.VMEM((tm, D), F32),
               pltpu.VMEM((2, D, tn), BF16), pltpu.VMEM((2, D, tn), BF16), pltpu.VMEM((2, tn, D), BF16),
               pltpu.VMEM((2, tm, tn), BF16), pltpu.VMEM((2, tm, tn), BF16),
               pltpu.SemaphoreType.DMA((6,)), pltpu.SemaphoreType.DMA((4,))]
    if is_final:
        in_specs += [vec, vec, ANY]
        args += list(final)
        out_shape += [jax.ShapeDtypeStruct((T, D), F32), jax.ShapeDtypeStruct((8, LANES), F32),
                      jax.ShapeDtypeStruct((1, D), F32), jax.ShapeDtypeStruct((1, D), F32)]
        out_specs += [ANY, pl.BlockSpec((8, LANES), lambda i: (0, 0)), vec, vec]
        scratch += [pltpu.VMEM((tm, D), F32), pltpu.VMEM((tm, D), F32), pltpu.SemaphoreType.DMA((2,))]
    else:
        out_shape += [jax.ShapeDtypeStruct((T, D), F32), jax.ShapeDtypeStruct((T, LANES), F32)]
        out_specs += [row, pl.BlockSpec((tm, LANES), lambda i: (i, 0))]
    return _call(body, name=name, grid=(nt,), in_specs=in_specs, out_specs=out_specs, out_shape=out_shape,
                 scratch_shapes=scratch, sem=("arbitrary",), args=args, carry=carry)


def _ffn_bwd_looped(dr, gate, up, wgu, wd, *, name, prev=None, carry=None):
    T, D = dr.shape
    F = wd.shape[0]
    tm, tn = _ffn_tiles(T, F)
    nj, nt = F // tn, T // tm
    has_prev = prev is not None

    def body(*refs):
        if has_prev:
            (dr_ref, gate_hbm, up_hbm, wgu_hbm, wd_hbm, xh_hbm, rstd_ref, lg_ref,
             h_hbm, dg_hbm, du_hbm, dprev_hbm, dlg_ref, dlb_ref, *scr) = refs
            xh_sc = scr.pop()
        else:
            (dr_ref, gate_hbm, up_hbm, wgu_hbm, wd_hbm,
             h_hbm, dg_hbm, du_hbm, dprev_hbm, *scr) = refs
        (df_sc, dx_sc, wg_buf, wu_buf, wd_buf, gi_buf, ui_buf, h_buf, dg_buf, du_buf, i_sem, o_sem,
         dp_sc, t_sem) = scr
        i = pl.program_id(0)
        rows = pl.ds(pl.multiple_of(i * tm, tm), tm)
        dp_out = pltpu.make_async_copy(dp_sc, dprev_hbm.at[rows, :], t_sem.at[0])
        if has_prev:
            xh_in = pltpu.make_async_copy(xh_hbm.at[rows, :], xh_sc, t_sem.at[1])
            xh_in.start()

        def cols(j, base=0):
            return pl.ds(pl.multiple_of(base + j * tn, LANES), tn)

        def i_copies(j, slot, tile=None):
            at = rows if tile is None else pl.ds(pl.multiple_of(tile * tm, tm), tm)
            return (pltpu.make_async_copy(wgu_hbm.at[:, cols(j)], wg_buf.at[slot], i_sem.at[slot]),
                    pltpu.make_async_copy(wgu_hbm.at[:, cols(j, F)], wu_buf.at[slot], i_sem.at[2 + slot]),
                    pltpu.make_async_copy(wd_hbm.at[cols(j), :], wd_buf.at[slot], i_sem.at[4 + slot]),
                    pltpu.make_async_copy(gate_hbm.at[at, cols(j)], gi_buf.at[slot], i_sem.at[6 + slot]),
                    pltpu.make_async_copy(up_hbm.at[at, cols(j)], ui_buf.at[slot], i_sem.at[8 + slot]))

        def o_copies(j, slot):
            return (pltpu.make_async_copy(h_buf.at[slot], h_hbm.at[rows, cols(j)], o_sem.at[slot]),
                    pltpu.make_async_copy(dg_buf.at[slot], dg_hbm.at[rows, cols(j)], o_sem.at[2 + slot]),
                    pltpu.make_async_copy(du_buf.at[slot], du_hbm.at[rows, cols(j)], o_sem.at[4 + slot]))

        def start(copies):
            for cp in copies:
                cp.start()

        def wait(copies):
            for cp in copies:
                cp.wait()

        _when(i == 0, lambda: start(i_copies(0, 0)))
        for rs in _row_chunks(tm):
            d = dr_ref[rs, :]
            df_sc[rs, :] = (0.5 * d).astype(BF16)
            dx_sc[rs, :] = ALPHA * d

        def step(j, slot):
            _when(j + 1 < nj, lambda: start(i_copies(j + 1, 1 - slot)))
            wait(i_copies(j, slot))
            g = gi_buf[slot].astype(F32)
            u = ui_buf[slot].astype(F32)
            dh = _dot_nt(df_sc[...], wd_buf[slot])
            s = _sig(g)
            sil = g * s
            dg = (dh * u * (s * (1.0 + g * (1.0 - s)))).astype(BF16)
            du = (dh * sil).astype(BF16)
            _when(j >= 2, lambda: wait(o_copies(j - 2, slot)))
            h_buf[slot] = (sil * u).astype(BF16)
            dg_buf[slot] = dg
            du_buf[slot] = du
            start(o_copies(j, slot))
            dx_sc[...] += _dot_nt(dg, wg_buf[slot])
            dx_sc[...] += _dot_nt(du, wu_buf[slot])

        _hidden_loop(nj, step)
        _when(i + 1 < nt, lambda: start(i_copies(0, 0, i + 1)))
        for j in range(max(nj - 2, 0), nj):
            wait(o_copies(j, j % 2))

        if has_prev:
            @pl.when(i == 0)
            def _():
                dlg_ref[...] = jnp.zeros_like(dlg_ref)
                dlb_ref[...] = jnp.zeros_like(dlb_ref)
            xh_in.wait()
        _when(i > 0, dp_out.wait)
        for rs in _row_chunks(tm):
            dxin = dx_sc[rs, :]
            if not has_prev:
                dp_sc[rs, :] = dxin
            else:
                x_hat = xh_sc[rs, :]
                dlg_ref[...] += _colsum(dxin * x_hat)
                dlb_ref[...] += _colsum(dxin)
                dp_sc[rs, :] = _ln_bwd(dxin, x_hat, rstd_ref[rs, 0:1], lg_ref[...])
        dp_out.start()
        _when(i == nt - 1, dp_out.wait)

    row = pl.BlockSpec((tm, D), lambda i: (i, 0))
    vec = pl.BlockSpec((1, D), lambda i: (0, 0))
    in_specs = [row, ANY, ANY, ANY, ANY]
    args = [dr, gate, up, wgu, wd]
    out_shape = [jax.ShapeDtypeStruct((T, F), BF16)] * 3 + [jax.ShapeDtypeStruct((T, D), F32)]
    out_specs = [ANY, ANY, ANY, ANY]
    scratch = [pltpu.VMEM((tm, D), BF16), pltpu.VMEM((tm, D), F32),
               pltpu.VMEM((2, D, tn), BF16), pltpu.VMEM((2, D, tn), BF16), pltpu.VMEM((2, tn, D), BF16)]
    scratch += [pltpu.VMEM((2, tm, tn), BF16)] * 5
    scratch += [pltpu.SemaphoreType.DMA((10,)), pltpu.SemaphoreType.DMA((6,)),
                pltpu.VMEM((tm, D), F32), pltpu.SemaphoreType.DMA((2,))]
    if has_prev:
        in_specs += [ANY, pl.BlockSpec((tm, LANES), lambda i: (i, 0)), vec]
        args += list(prev)
        out_shape += [jax.ShapeDtypeStruct((1, D), F32)] * 2
        out_specs += [vec, vec]
        scratch += [pltpu.VMEM((tm, D), F32)]
    return _call(body, name=name, grid=(nt,), in_specs=in_specs, out_specs=out_specs, out_shape=out_shape,
                 scratch_shapes=scratch, sem=("arbitrary",), args=args, carry=carry)


def _ffn_fwd(xh, lg, lb, wgu, wd, *, affine, name, final=None, carry=None):
    T, D = xh.shape
    F = wd.shape[0]
    tm = _tile(T, 512)
    tn = _tile(F, 512, LANES)
    nj = F // tn
    is_final = final is not None

    def body(*refs):
        if is_final:
            (xh_ref, lg_ref, lb_ref, wg_ref, wu_ref, wd_ref, ng_ref, nb_ref, tgt_ref,
             gate_ref, up_ref, xb_sc, dr_ref, loss_ref, dng_ref, dnb_ref, acc_sc) = refs
        else:
            (xh_ref, lg_ref, lb_ref, wg_ref, wu_ref, wd_ref,
             gate_ref, up_ref, xb_sc, xho_ref, rstd_ref, acc_sc) = refs
        i = pl.program_id(0)
        j = pl.program_id(1)

        def xin():
            v = xh_ref[...]
            return v * lg_ref[...] + lb_ref[...] if affine else v

        @pl.when(j == 0)
        def _():
            xb_sc[...] = xin().astype(BF16)
            acc_sc[...] = jnp.zeros_like(acc_sc)

        xb = xb_sc[...]
        g = _dot(xb, wg_ref[...])
        u = _dot(xb, wu_ref[...])
        gate_ref[...] = g.astype(BF16)
        up_ref[...] = u.astype(BF16)
        h = g * _sig(g) * u
        acc_sc[...] += _dot(h.astype(BF16), wd_ref[...])

        @pl.when(j == nj - 1)
        def _():
            if is_final:
                @pl.when(i == 0)
                def _():
                    loss_ref[...] = jnp.zeros_like(loss_ref)
                    dng_ref[...] = jnp.zeros_like(dng_ref)
                    dnb_ref[...] = jnp.zeros_like(dnb_ref)
            for rs in _row_chunks(tm):
                v = xh_ref[rs, :]
                if affine:
                    v = v * lg_ref[...] + lb_ref[...]
                r = ALPHA * v + 0.5 * acc_sc[rs, :]
                xho, rstd = _ln_stats(r)
                if not is_final:
                    xho_ref[rs, :] = xho
                    rstd_ref[rs, :] = jnp.broadcast_to(rstd, (rs.stop - rs.start, LANES))
                else:
                    ng = ng_ref[...]
                    e = xho * ng + nb_ref[...] - tgt_ref[rs, :]
                    part = _colsum(jnp.sum(e * e, axis=1, keepdims=True)) * (0.5 / D)
                    loss_ref[...] += jnp.broadcast_to(part, loss_ref.shape)
                    dy = e * (1.0 / D)
                    dng_ref[...] += _colsum(dy * xho)
                    dnb_ref[...] += _colsum(dy)
                    dr_ref[rs, :] = _ln_bwd(dy, xho, rstd, ng)

    row = pl.BlockSpec((tm, D), lambda i, j: (i, 0))
    vec = pl.BlockSpec((1, D), lambda i, j: (0, 0))
    hid = pl.BlockSpec((tm, tn), lambda i, j: (i, j))
    in_specs = [row, vec, vec,
                pl.BlockSpec((D, tn), lambda i, j: (0, j)),
                pl.BlockSpec((D, tn), lambda i, j: (0, j + nj)),
                pl.BlockSpec((tn, D), lambda i, j: (j, 0))]
    args = [xh, lg, lb, wgu, wgu, wd]
    out_shape = [jax.ShapeDtypeStruct((T, F), BF16), jax.ShapeDtypeStruct((T, F), BF16),
                 jax.ShapeDtypeStruct((T, D), BF16)]
    out_specs = [hid, hid, row]
    if is_final:
        in_specs += [vec, vec, row]
        args += list(final)
        out_shape += [jax.ShapeDtypeStruct((T, D), F32), jax.ShapeDtypeStruct((8, LANES), F32),
                      jax.ShapeDtypeStruct((1, D), F32), jax.ShapeDtypeStruct((1, D), F32)]
        out_specs += [row, pl.BlockSpec((8, LANES), lambda i, j: (0, 0)), vec, vec]
        sem = ("arbitrary", "arbitrary")
    else:
        out_shape += [jax.ShapeDtypeStruct((T, D), F32), jax.ShapeDtypeStruct((T, LANES), F32)]
        out_specs += [row, pl.BlockSpec((tm, LANES), lambda i, j: (i, 0))]
        sem = ("parallel", "arbitrary")
    return _call(body, name=name, grid=(T // tm, nj), in_specs=in_specs, out_specs=out_specs, out_shape=out_shape,
                 scratch_shapes=[pltpu.VMEM((tm, D), F32)], sem=sem, args=args, carry=carry)


def _ffn_bwd(dr, gate, up, wgu, wd, *, name, prev=None, carry=None):
    T, D = dr.shape
    F = wd.shape[0]
    tm = _tile(T, 512)
    tn = _tile(F, 512, LANES)
    nj = F // tn
    has_prev = prev is not None

    def body(*refs):
        if has_prev:
            (dr_ref, gate_ref, up_ref, wd_ref, wg_ref, wu_ref, xh_ref, rstd_ref, lg_ref,
             h_ref, dg_ref, du_ref, dprev_ref, dlg_ref, dlb_ref, df_sc, dx_sc) = refs
        else:
            (dr_ref, gate_ref, up_ref, wd_ref, wg_ref, wu_ref,
             h_ref, dg_ref, du_ref, dprev_ref, df_sc, dx_sc) = refs
        i = pl.program_id(0)
        j = pl.program_id(1)

        @pl.when(j == 0)
        def _():
            d = dr_ref[...]
            df_sc[...] = (0.5 * d).astype(BF16)
            dx_sc[...] = ALPHA * d

        g = gate_ref[...].astype(F32)
        u = up_ref[...].astype(F32)
        dh = _dot_nt(df_sc[...], wd_ref[...])
        s = _sig(g)
        sil = g * s
        h_ref[...] = (sil * u).astype(BF16)
        dg = (dh * u * (s * (1.0 + g * (1.0 - s)))).astype(BF16)
        du = (dh * sil).astype(BF16)
        dg_ref[...] = dg
        du_ref[...] = du
        dx_sc[...] += _dot_nt(dg, wg_ref[...]) + _dot_nt(du, wu_ref[...])

        @pl.when(j == nj - 1)
        def _():
            dxin = dx_sc[...]
            if not has_prev:
                dprev_ref[...] = dxin
            else:
                @pl.when(i == 0)
                def _():
                    dlg_ref[...] = jnp.zeros_like(dlg_ref)
                    dlb_ref[...] = jnp.zeros_like(dlb_ref)
                xh = xh_ref[...]
                dlg_ref[...] += _colsum(dxin * xh)
                dlb_ref[...] += _colsum(dxin)
                dprev_ref[...] = _ln_bwd(dxin, xh, rstd_ref[:, 0:1], lg_ref[...])

    row = pl.BlockSpec((tm, D), lambda i, j: (i, 0))
    vec = pl.BlockSpec((1, D), lambda i, j: (0, 0))
    hid = pl.BlockSpec((tm, tn), lambda i, j: (i, j))
    in_specs = [row, hid, hid,
                pl.BlockSpec((tn, D), lambda i, j: (j, 0)),
                pl.BlockSpec((D, tn), lambda i, j: (0, j)),
                pl.BlockSpec((D, tn), lambda i, j: (0, j + nj))]
    args = [dr, gate, up, wd, wgu, wgu]
    out_shape = [jax.ShapeDtypeStruct((T, F), BF16)] * 3 + [jax.ShapeDtypeStruct((T, D), F32)]
    out_specs = [hid, hid, hid, row]
    if has_prev:
        in_specs += [row, pl.BlockSpec((tm, LANES), lambda i, j: (i, 0)), vec]
        args += list(prev)
        out_shape += [jax.ShapeDtypeStruct((1, D), F32)] * 2
        out_specs += [vec, vec]
        sem = ("arbitrary", "arbitrary")
    else:
        sem = ("parallel", "arbitrary")
    return _call(body, name=name, grid=(T // tm, nj), in_specs=in_specs, out_specs=out_specs, out_shape=out_shape,
                 scratch_shapes=[pltpu.VMEM((tm, D), BF16), pltpu.VMEM((tm, D), F32)], sem=sem, args=args,
                 carry=carry)


def _mm_tn(a, b, *, name, tm_pref, tn_pref, scale=1.0, a_affine=None, into=None, col_off=0, n_total=None,
           carry=None):
    T, M = a.shape
    N = b.shape[1]
    n_total = N if n_total is None else n_total
    tM = _tile(M, tm_pref, LANES)
    tN = _tile(N, tn_pref, LANES)
    tk = _tile(T, 1024)
    nt = T // tk
    assert col_off % tN == 0
    off_blocks = col_off // tN
    has_aff = a_affine is not None
    has_into = into is not None

    def body(*refs):
        refs = list(refs)
        a_ref = refs.pop(0)
        if has_aff:
            lg_ref = refs.pop(0)
            lb_ref = refs.pop(0)
        b_ref = refs.pop(0)
        if has_into:
            refs.pop(0)
        o_ref, acc_sc = refs
        t = pl.program_id(2)

        @pl.when(t == 0)
        def _():
            acc_sc[...] = jnp.zeros_like(acc_sc)

        av = a_ref[...]
        if has_aff:
            av = av * lg_ref[...] + lb_ref[...]
        acc_sc[...] += _dot_tn(av.astype(BF16), b_ref[...].astype(BF16))

        @pl.when(t == nt - 1)
        def _():
            o_ref[...] = (acc_sc[...] * scale).astype(BF16)

    in_specs = [pl.BlockSpec((tk, tM), lambda m, n, t: (t, m))]
    args = [a]
    if has_aff:
        in_specs += [pl.BlockSpec((1, tM), lambda m, n, t: (0, m))] * 2
        args += list(a_affine)
    in_specs.append(pl.BlockSpec((tk, tN), lambda m, n, t: (t, n)))
    args.append(b)
    aliases = {}
    if has_into:
        aliases = {len(args): 0}
        in_specs.append(ANY)
        args.append(into)
    res = _call(body, name=name, grid=(M // tM, N // tN, nt), in_specs=in_specs,
                out_specs=[pl.BlockSpec((tM, tN), lambda m, n, t: (m, n + off_blocks))],
                out_shape=[jax.ShapeDtypeStruct((M, n_total), BF16)],
                scratch_shapes=[pltpu.VMEM((tM, tN), F32)], aliases=aliases,
                sem=("parallel", "parallel", "arbitrary"), args=args, carry=carry)
    return res[0] if carry is None else (res[0][0], res[1])


def _inproj_fwd(xh, lg, lb, w, bias, *, name, carry=None):
    T, D = xh.shape
    N = w.shape[1]
    tm = _tile(T, 1024)
    tn = _tile(N, 1024, LANES)

    def body(xh_ref, lg_ref, lb_ref, w_ref, b_ref, o_ref, xb_ref):
        @pl.when(pl.program_id(1) == 0)
        def _():
            xb_ref[...] = (xh_ref[...] * lg_ref[...] + lb_ref[...]).astype(BF16)
        o_ref[...] = _dot(xb_ref[...], w_ref[...]) + b_ref[...]

    return _call(
        body, name=name, grid=(T // tm, N // tn),
        in_specs=[pl.BlockSpec((tm, D), lambda i, j: (i, 0)),
                  pl.BlockSpec((1, D), lambda i, j: (0, 0)), pl.BlockSpec((1, D), lambda i, j: (0, 0)),
                  pl.BlockSpec((D, tn), lambda i, j: (0, j)), pl.BlockSpec((1, tn), lambda i, j: (0, j))],
        out_specs=[pl.BlockSpec((tm, tn), lambda i, j: (i, j)), pl.BlockSpec((tm, D), lambda i, j: (i, 0))],
        out_shape=[jax.ShapeDtypeStruct((T, N), F32), jax.ShapeDtypeStruct((T, D), BF16)],
        sem=("parallel", "arbitrary"), args=[xh, lg, lb, w, bias], carry=carry)


def _inproj_bwd(dproj, w, dr_next, xh, rstd, lg, *, name, carry=None):
    T, N = dproj.shape
    D = w.shape[0]
    tm = _tile(T, 512)
    tn = _tile(N, 2048, LANES)
    nj = N // tn

    def body(dp_ref, w_ref, drn_ref, xh_ref, rstd_ref, lg_ref, dprev_ref, dlg_ref, dlb_ref, dx_sc):
        i = pl.program_id(0)
        j = pl.program_id(1)

        @pl.when(j == 0)
        def _():
            for rs in _row_chunks(tm):
                dx_sc[rs, :] = ALPHA * drn_ref[rs, :]

        dx_sc[...] += _dot_nt(dp_ref[...], w_ref[...])

        @pl.when(j == nj - 1)
        def _():
            @pl.when(i == 0)
            def _():
                dlg_ref[...] = jnp.zeros_like(dlg_ref)
                dlb_ref[...] = jnp.zeros_like(dlb_ref)
            for rs in _row_chunks(tm):
                dx = dx_sc[rs, :]
                x_hat = xh_ref[rs, :]
                dlg_ref[...] += _colsum(dx * x_hat)
                dlb_ref[...] += _colsum(dx)
                dprev_ref[rs, :] = _ln_bwd(dx, x_hat, rstd_ref[rs, 0:1], lg_ref[...])

    row = pl.BlockSpec((tm, D), lambda i, j: (i, 0))
    vec = pl.BlockSpec((1, D), lambda i, j: (0, 0))
    return _call(
        body, name=name, grid=(T // tm, nj),
        in_specs=[pl.BlockSpec((tm, tn), lambda i, j: (i, j)), pl.BlockSpec((D, tn), lambda i, j: (0, j)),
                  row, row, pl.BlockSpec((tm, LANES), lambda i, j: (i, 0)), vec],
        out_specs=[row, vec, vec],
        out_shape=[jax.ShapeDtypeStruct((T, D), F32), jax.ShapeDtypeStruct((1, D), F32),
                   jax.ShapeDtypeStruct((1, D), F32)],
        scratch_shapes=[pltpu.VMEM((tm, D), F32)], sem=("arbitrary", "arbitrary"),
        args=[dproj, w, dr_next, xh, rstd, lg], carry=carry)


def _sgu_fwd(proj, ln_g, ln_b, w_s, b_sb, *, name):
    T = proj.shape[0]
    A = proj.shape[1] // 8
    hd = A // A_GROUPS
    tm = _tile(T, 256, GMLP_BLOCK)

    def body(u_ref, v_ref, g_ref, b_ref, ws_ref, bs_ref, o_ref):
        gu = _gelu(u_ref[...])
        vh, _ = _ln_stats(_gelu(v_ref[...]))
        vn = (vh * g_ref[...] + b_ref[...]).astype(BF16)
        mask = _chunk_mask(False)
        for h in range(A_GROUPS):
            wm = jnp.where(mask, ws_ref[h], 0.0).astype(BF16)
            cols = slice(h * hd, (h + 1) * hd)
            for n in range(tm // GMLP_BLOCK):
                rows = slice(n * GMLP_BLOCK, (n + 1) * GMLP_BLOCK)
                s = _dot(wm, vn[rows, cols]) + bs_ref[h][:, :hd]
                o_ref[rows, cols] = (gu[rows, cols] * s).astype(BF16)

    vec = pl.BlockSpec((1, A), lambda i: (0, 0))
    full = pl.BlockSpec((A_GROUPS, GMLP_BLOCK, GMLP_BLOCK), lambda i: (0, 0, 0))
    return pl.pallas_call(
        body, name=name, grid=(T // tm,),
        in_specs=[pl.BlockSpec((tm, A), lambda i: (i, 0)), pl.BlockSpec((tm, A), lambda i: (i, 1)),
                  vec, vec, full, full],
        out_specs=pl.BlockSpec((tm, A), lambda i: (i, 0)),
        out_shape=jax.ShapeDtypeStruct((T, A), BF16),
        compiler_params=_params("parallel"))(proj, proj, ln_g, ln_b, w_s, b_sb)


def _sgu_bwd(proj, dsg, ln_g, ln_b, w_s, w_st, b_sb, dproj, *, name):
    T = proj.shape[0]
    A = proj.shape[1] // 8
    hd = A // A_GROUPS
    tm = _tile(T, 256, GMLP_BLOCK)
    nt = T // tm

    def body(u_ref, v_ref, dsg_ref, g_ref, b_ref, ws_ref, wst_ref, bs_ref, _alias,
             dp_ref, dbin_ref, dlg_ref, dlb_ref, dws_ref, dbs_ref, dvn_sc, dgu_sc, dbs_sc):
        i = pl.program_id(0)

        @pl.when(i == 0)
        def _():
            dbin_ref[...] = jnp.zeros_like(dbin_ref)
            dlg_ref[...] = jnp.zeros_like(dlg_ref)
            dlb_ref[...] = jnp.zeros_like(dlb_ref)
            dws_ref[...] = jnp.zeros_like(dws_ref)
            dbs_sc[...] = jnp.zeros_like(dbs_sc)

        u = u_ref[...]
        v = v_ref[...]
        gu = _gelu(u)
        vh, rstd = _ln_stats(_gelu(v))
        gain = g_ref[...]
        vn = (vh * gain + b_ref[...]).astype(BF16)
        dsg_v = dsg_ref[...]
        mask = _chunk_mask(False)
        mask_t = _chunk_mask(True)
        for h in range(A_GROUPS):
            wm = jnp.where(mask, ws_ref[h], 0.0).astype(BF16)
            wmt = jnp.where(mask_t, wst_ref[h], 0.0).astype(BF16)
            cols = slice(h * hd, (h + 1) * hd)
            for n in range(tm // GMLP_BLOCK):
                rows = slice(n * GMLP_BLOCK, (n + 1) * GMLP_BLOCK)
                vb = vn[rows, cols]
                s = _dot(wm, vb) + bs_ref[h][:, :hd]
                d_out = dsg_v[rows, cols]
                dgu_sc[rows, cols] = d_out * s
                ds = d_out * gu[rows, cols]
                ds_b = ds.astype(BF16)
                dws_ref[h] += _dot_nt(ds_b, vb)
                dbs_sc[h] += ds
                dvn_sc[rows, cols] = _dot(wmt, ds_b)
        dvn = dvn_sc[...]
        dlg_ref[...] += _colsum(dvn * vh)
        dlb_ref[...] += _colsum(dvn)
        dv = _ln_bwd(dvn, vh, rstd, gain) * _gelu_grad(v)
        du = dgu_sc[...] * _gelu_grad(u)
        dp_ref[:, 0:A] = du.astype(BF16)
        dp_ref[:, A:2 * A] = dv.astype(BF16)
        dbin_ref[:, 0:A] += _colsum(du)
        dbin_ref[:, A:2 * A] += _colsum(dv)

        @pl.when(i == nt - 1)
        def _():
            for h in range(A_GROUPS):
                dws_ref[h] = jnp.where(mask, dws_ref[h], 0.0)
                dbs_ref[h:h + 1, :] = _colsum(dbs_sc[h].T)

    vec = pl.BlockSpec((1, A), lambda i: (0, 0))
    full = pl.BlockSpec((A_GROUPS, GMLP_BLOCK, GMLP_BLOCK), lambda i: (0, 0, 0))
    tile = pl.BlockSpec((tm, A), lambda i: (i, 0))
    return pl.pallas_call(
        body, name=name, grid=(nt,),
        in_specs=[tile, pl.BlockSpec((tm, A), lambda i: (i, 1)), tile, vec, vec, full, full, full, ANY],
        out_specs=[pl.BlockSpec((tm, 2 * A), lambda i: (i, 0)), pl.BlockSpec((1, 2 * A), lambda i: (0, 0)),
                   vec, vec, full, pl.BlockSpec((A_GROUPS, GMLP_BLOCK), lambda i: (0, 0))],
        out_shape=[jax.ShapeDtypeStruct(dproj.shape, BF16), jax.ShapeDtypeStruct((1, 2 * A), F32),
                   jax.ShapeDtypeStruct((1, A), F32), jax.ShapeDtypeStruct((1, A), F32),
                   jax.ShapeDtypeStruct((A_GROUPS, GMLP_BLOCK, GMLP_BLOCK), F32),
                   jax.ShapeDtypeStruct((A_GROUPS, GMLP_BLOCK), F32)],
        scratch_shapes=[pltpu.VMEM((tm, A), F32), pltpu.VMEM((tm, A), F32),
                        pltpu.VMEM((A_GROUPS, GMLP_BLOCK, hd), F32)],
        input_output_aliases={8: 0},
        compiler_params=_params("arbitrary"))(proj, proj, dsg, ln_g, ln_b, w_s, w_st, b_sb, dproj)


def _conv_tiles(T, B):
    tm = _tile(T, 256, CONV_ROWS)
    lb = min(LANES, B)
    return tm, tm // CONV_HALO, lb


def _fill_phases(src, dst, B, lb):
    rows = dst.shape[1]
    for p in range(1, 8):
        for cb in range(B // lb):
            ls = slice(cb * lb, (cb + 1) * lb)
            dst[p - 1, :, ls] = src[p:p + rows, ls]


def _shifted(src, phases, off, ls):
    m, p = divmod(off, 8)
    if p == 0:
        return src[off:off + CONV_ROWS, ls]
    return phases[p - 1, 8 * m:8 * m + CONV_ROWS, ls]


def _conv_fwd(proj, w_dw, b_dw, ln_g, ln_b, *, name):
    T = proj.shape[0]
    B = proj.shape[1] // 8
    tm, nh, lb = _conv_tiles(T, B)

    def body(ap_ref, gp_ref, a_ref, g_ref, w_ref, bdw_ref, lg_ref, lb_ref, c_ref, cv_ref, z_sc, zp_sc):
        i = pl.program_id(0)
        z_sc[0:CONV_HALO, :] = jnp.where(i > 0, ap_ref[...] * _sig(gp_ref[...]), 0.0)
        z_sc[CONV_HALO:CONV_HALO + tm, :] = a_ref[...] * _sig(g_ref[...])
        _fill_phases(z_sc, zp_sc, B, lb)
        for cb in range(B // lb):
            ls = slice(cb * lb, (cb + 1) * lb)
            for rc in range(tm // CONV_ROWS):
                acc = jnp.zeros((CONV_ROWS, lb), F32)
                for k in range(CONV_WIDTH):
                    off = rc * CONV_ROWS + CONV_HALO - (CONV_WIDTH - 1) + k
                    acc = acc + w_ref[k:k + 1, ls] * _shifted(z_sc, zp_sc, off, ls)
                c_ref[rc * CONV_ROWS:(rc + 1) * CONV_ROWS, ls] = acc + bdw_ref[:, ls]
        xh, _ = _ln_stats(c_ref[...])
        y = xh * lg_ref[...] + lb_ref[...]
        cv_ref[...] = (y * _sig(y)).astype(BF16)

    vec = pl.BlockSpec((1, B), lambda i: (0, 0))
    halo_a = pl.BlockSpec((CONV_HALO, B), lambda i: (jnp.maximum(i * nh - 1, 0), 2))
    halo_g = pl.BlockSpec((CONV_HALO, B), lambda i: (jnp.maximum(i * nh - 1, 0), 3))
    return pl.pallas_call(
        body, name=name, grid=(T // tm,),
        in_specs=[halo_a, halo_g, pl.BlockSpec((tm, B), lambda i: (i, 2)), pl.BlockSpec((tm, B), lambda i: (i, 3)),
                  pl.BlockSpec((CONV_WPAD, B), lambda i: (0, 0)), vec, vec, vec],
        out_specs=[pl.BlockSpec((tm, B), lambda i: (i, 0))] * 2,
        out_shape=[jax.ShapeDtypeStruct((T, B), F32), jax.ShapeDtypeStruct((T, B), BF16)],
        scratch_shapes=[pltpu.VMEM((CONV_HALO + tm, B), F32), pltpu.VMEM((7, CONV_HALO + tm - 8, B), F32)],
        compiler_params=_params("parallel"))(proj, proj, proj, proj, w_dw, b_dw, ln_g, ln_b)


def _conv_bwd_ln(dcv, c, ln_g, ln_b, *, name):
    T, B = c.shape
    tm = _tile(T, 512)

    def body(dcv_ref, c_ref, lg_ref, lb_ref, dc_ref, dlg_ref, dlb_ref, dbdw_ref):
        @pl.when(pl.program_id(0) == 0)
        def _():
            dlg_ref[...] = jnp.zeros_like(dlg_ref)
            dlb_ref[...] = jnp.zeros_like(dlb_ref)
            dbdw_ref[...] = jnp.zeros_like(dbdw_ref)
        gain = lg_ref[...]
        xh, rstd = _ln_stats(c_ref[...])
        y = xh * gain + lb_ref[...]
        s = _sig(y)
        dy = dcv_ref[...] * (s * (1.0 + y * (1.0 - s)))
        dlg_ref[...] += _colsum(dy * xh)
        dlb_ref[...] += _colsum(dy)
        dc = _ln_bwd(dy, xh, rstd, gain)
        dc_ref[...] = dc
        dbdw_ref[...] += _colsum(dc)

    tile = pl.BlockSpec((tm, B), lambda i: (i, 0))
    vec = pl.BlockSpec((1, B), lambda i: (0, 0))
    return pl.pallas_call(
        body, name=name, grid=(T // tm,), in_specs=[tile, tile, vec, vec], out_specs=[tile, vec, vec, vec],
        out_shape=[jax.ShapeDtypeStruct((T, B), F32)] + [jax.ShapeDtypeStruct((1, B), F32)] * 3,
        compiler_params=_params("arbitrary"))(dcv, c, ln_g, ln_b)


def _conv_bwd(proj, dc, w_dw, dproj, *, name, carry=None):
    T = proj.shape[0]
    B = proj.shape[1] // 8
    tm, nh, lb = _conv_tiles(T, B)
    nt = T // tm
    n_halo = T // CONV_HALO

    def body(ap_ref, gp_ref, a_ref, g_ref, dc_ref, dcn_ref, w_ref, _alias,
             dp_ref, dbin_ref, dw_ref, z_sc, dc_sc, dz_sc, dw_sc, zp_sc, dcp_sc):
        i = pl.program_id(0)

        @pl.when(i == 0)
        def _():
            dbin_ref[...] = jnp.zeros_like(dbin_ref)
            dw_sc[...] = jnp.zeros_like(dw_sc)

        a = a_ref[...]
        s = _sig(g_ref[...])
        z_sc[0:CONV_HALO, :] = jnp.where(i > 0, ap_ref[...] * _sig(gp_ref[...]), 0.0)
        z_sc[CONV_HALO:CONV_HALO + tm, :] = a * s
        dc_sc[0:tm, :] = dc_ref[...]
        dc_sc[tm:tm + CONV_HALO, :] = jnp.where(i < nt - 1, dcn_ref[...], 0.0)
        _fill_phases(z_sc, zp_sc, B, lb)
        _fill_phases(dc_sc, dcp_sc, B, lb)
        for cb in range(B // lb):
            ls = slice(cb * lb, (cb + 1) * lb)
            for rc in range(tm // CONV_ROWS):
                r0 = rc * CONV_ROWS
                acc = jnp.zeros((CONV_ROWS, lb), F32)
                for k in range(CONV_WIDTH):
                    acc = acc + w_ref[k:k + 1, ls] * _shifted(dc_sc, dcp_sc, r0 + (CONV_WIDTH - 1) - k, ls)
                dz_sc[r0:r0 + CONV_ROWS, ls] = acc
            for k in range(CONV_WIDTH):
                part = jnp.zeros((8, lb), F32)
                for rc in range(tm // CONV_ROWS):
                    r0 = rc * CONV_ROWS
                    prod = dc_sc[r0:r0 + CONV_ROWS, ls] * _shifted(
                        z_sc, zp_sc, r0 + CONV_HALO - (CONV_WIDTH - 1) + k, ls)
                    part = part + jnp.sum(prod.reshape(CONV_ROWS // 8, 8, lb), axis=0)
                dw_sc[8 * k:8 * k + 8, ls] += part
        dz = dz_sc[...]
        da = dz * s
        dg = dz * a * s * (1.0 - s)
        dp_ref[:, 0:B] = da.astype(BF16)
        dp_ref[:, B:2 * B] = dg.astype(BF16)
        dbin_ref[:, 0:B] += _colsum(da)
        dbin_ref[:, B:2 * B] += _colsum(dg)

        @pl.when(i == nt - 1)
        def _():
            for k in range(CONV_WIDTH):
                dw_ref[k:k + 1, :] = _colsum(dw_sc[8 * k:8 * k + 8, :])

    halo_a = pl.BlockSpec((CONV_HALO, B), lambda i: (jnp.maximum(i * nh - 1, 0), 2))
    halo_g = pl.BlockSpec((CONV_HALO, B), lambda i: (jnp.maximum(i * nh - 1, 0), 3))
    halo_dc = pl.BlockSpec((CONV_HALO, B), lambda i: (jnp.minimum((i + 1) * nh, n_halo - 1), 0))
    return _call(
        body, name=name, grid=(nt,),
        in_specs=[halo_a, halo_g, pl.BlockSpec((tm, B), lambda i: (i, 2)), pl.BlockSpec((tm, B), lambda i: (i, 3)),
                  pl.BlockSpec((tm, B), lambda i: (i, 0)), halo_dc,
                  pl.BlockSpec((CONV_WPAD, B), lambda i: (0, 0)), ANY],
        out_specs=[pl.BlockSpec((tm, 2 * B), lambda i: (i, 1)), pl.BlockSpec((1, 2 * B), lambda i: (0, 0)),
                   pl.BlockSpec((CONV_WIDTH, B), lambda i: (0, 0))],
        out_shape=[jax.ShapeDtypeStruct(dproj.shape, BF16), jax.ShapeDtypeStruct((1, 2 * B), F32),
                   jax.ShapeDtypeStruct((CONV_WIDTH, B), F32)],
        scratch_shapes=[pltpu.VMEM((CONV_HALO + tm, B), F32), pltpu.VMEM((tm + CONV_HALO, B), F32),
                        pltpu.VMEM((tm, B), F32), pltpu.VMEM((8 * CONV_WIDTH, B), F32),
                        pltpu.VMEM((7, CONV_HALO + tm - 8, B), F32), pltpu.VMEM((7, CONV_HALO + tm - 8, B), F32)],
        aliases={7: 0}, sem=("arbitrary",), args=[proj, proj, proj, proj, dc, dc, w_dw, dproj], carry=carry)


def _mix_fwd_gate(sg, cv, proj, wa, wb, *, name):
    T, A = sg.shape
    D = wa.shape[1]
    tm = _tile(T, 256)

    def body(sg_ref, cv_ref, la_ref, lb_ref, wa_ref, wb_ref, ya_ref, yb_ref, m_ref):
        ya = _dot(sg_ref[...], wa_ref[...])
        yb = _dot(cv_ref[...], wb_ref[...])
        ya_ref[...] = ya.astype(BF16)
        yb_ref[...] = yb.astype(BF16)
        m_ref[...] = (_sig(la_ref[...]) * ya + _sig(lb_ref[...]) * yb).astype(BF16)

    act = pl.BlockSpec((tm, A), lambda i: (i, 0))
    wide = pl.BlockSpec((tm, D), lambda i: (i, 0))
    wspec = pl.BlockSpec((A, D), lambda i: (0, 0))
    return pl.pallas_call(
        body, name=name, grid=(T // tm,),
        in_specs=[act, act, pl.BlockSpec((tm, D), lambda i: (i, 2)), pl.BlockSpec((tm, D), lambda i: (i, 3)),
                  wspec, wspec],
        out_specs=[wide] * 3, out_shape=[jax.ShapeDtypeStruct((T, D), BF16)] * 3,
        compiler_params=_params("parallel"))(sg, cv, proj, proj, wa, wb)


def _mix_fwd_out(m, wout, xh, lg, lb, *, name):
    T, D = xh.shape
    tm = _tile(T, 512)

    def body(m_ref, w_ref, xh_ref, lg_ref, lb_ref, xho_ref, rstd_ref):
        r = ALPHA * (xh_ref[...] * lg_ref[...] + lb_ref[...]) + _dot(m_ref[...], w_ref[...])
        xho, rstd = _ln_stats(r)
        xho_ref[...] = xho
        rstd_ref[...] = jnp.broadcast_to(rstd, (tm, LANES))

    row = pl.BlockSpec((tm, D), lambda i: (i, 0))
    vec = pl.BlockSpec((1, D), lambda i: (0, 0))
    return pl.pallas_call(
        body, name=name, grid=(T // tm,),
        in_specs=[row, pl.BlockSpec((D, D), lambda i: (0, 0)), row, vec, vec],
        out_specs=[row, pl.BlockSpec((tm, LANES), lambda i: (i, 0))],
        out_shape=[jax.ShapeDtypeStruct((T, D), F32), jax.ShapeDtypeStruct((T, LANES), F32)],
        compiler_params=_params("parallel"))(m, wout, xh, lg, lb)


def _mix_bwd_gate(dr, wout, proj, ya, yb, *, name, carry=None):
    T, D = dr.shape
    N = proj.shape[1]
    tm = _tile(T, 256)

    def body(dr_ref, w_ref, la_ref, lb_ref, ya_ref, yb_ref, dya_ref, dyb_ref, dp_ref, dbin_ref):
        @pl.when(pl.program_id(0) == 0)
        def _():
            dbin_ref[...] = jnp.zeros_like(dbin_ref)
        dm = _dot_nt(dr_ref[...].astype(BF16), w_ref[...])
        sa = _sig(la_ref[...])
        sb = _sig(lb_ref[...])
        dya_ref[...] = (dm * sa).astype(BF16)
        dyb_ref[...] = (dm * sb).astype(BF16)
        dla = dm * ya_ref[...].astype(F32) * sa * (1.0 - sa)
        dlb = dm * yb_ref[...].astype(F32) * sb * (1.0 - sb)
        dp_ref[:, 0:D] = dla.astype(BF16)
        dp_ref[:, D:2 * D] = dlb.astype(BF16)
        dbin_ref[:, 0:D] += _colsum(dla)
        dbin_ref[:, D:2 * D] += _colsum(dlb)

    row = pl.BlockSpec((tm, D), lambda i: (i, 0))
    return _call(
        body, name=name, grid=(T // tm,),
        in_specs=[row, pl.BlockSpec((D, D), lambda i: (0, 0)), pl.BlockSpec((tm, D), lambda i: (i, 2)),
                  pl.BlockSpec((tm, D), lambda i: (i, 3)), row, row],
        out_specs=[row, row, pl.BlockSpec((tm, 2 * D), lambda i: (i, 1)), pl.BlockSpec((1, 2 * D), lambda i: (0, 0))],
        out_shape=[jax.ShapeDtypeStruct((T, D), BF16), jax.ShapeDtypeStruct((T, D), BF16),
                   jax.ShapeDtypeStruct((T, N), BF16), jax.ShapeDtypeStruct((1, 2 * D), F32)],
        sem=("arbitrary",), args=[dr, wout, proj, proj, ya, yb], carry=carry)


def _mix_bwd_proj(dya, dyb, wa, wb, *, name):
    T, D = dya.shape
    A = wa.shape[0]
    tm = _tile(T, 512)

    def body(dya_ref, dyb_ref, wa_ref, wb_ref, dsg_ref, dcv_ref):
        dsg_ref[...] = _dot_nt(dya_ref[...], wa_ref[...])
        dcv_ref[...] = _dot_nt(dyb_ref[...], wb_ref[...])

    row = pl.BlockSpec((tm, D), lambda i: (i, 0))
    wspec = pl.BlockSpec((A, D), lambda i: (0, 0))
    act = pl.BlockSpec((tm, A), lambda i: (i, 0))
    return pl.pallas_call(
        body, name=name, grid=(T // tm,), in_specs=[row, row, wspec, wspec], out_specs=[act, act],
        out_shape=[jax.ShapeDtypeStruct((T, A), F32)] * 2,
        compiler_params=_params("parallel"))(dya, dyb, wa, wb)


def _mesh_pos():
    return lax.axis_index("x"), lax.axis_index("y"), lax.axis_index("c")


def _shard_view(ref, p, shape, axis):
    r, c = shape
    if axis == 0:
        return ref.at[pl.ds(pl.multiple_of(p * r, 16), r), :]
    return ref.at[:, pl.ds(pl.multiple_of(p * c, LANES), c)]


def _all_gather(shards, axes, progressive=False, busy=0.93):
    n = len(shards)
    shapes = [s.shape for s in shards]
    sizes = [s.size * s.dtype.itemsize for s in shards]
    if progressive:
        done = [sum(sizes[:t + 1]) / sum(sizes) for t in range(n)]
        when = (0.0,) + tuple(min(0.97, 0.04 + busy * d) for d in done) + (1.0,)
    else:
        when = (0.0, 1.0)

    def run(ins, outs, sems, phase):
        send_sems, recv_sems, local_sems = sems
        x, y, c = _mesh_pos()
        me, sibling = (x, y, c), (x, y, 1 - c)
        chips = [(1 - x, y), (x, 1 - y), (1 - x, 1 - y)]

        def view(t, pos):
            px, py, pc = pos
            return _shard_view(outs[t], 4 * px + 2 * py + pc, shapes[t], axes[t])

        def copy(t, k, block, to, src=None):
            return pltpu.make_async_remote_copy(
                src_ref=view(t, block) if src is None else src, dst_ref=view(t, block),
                send_sem=send_sems.at[7 * t + k], recv_sem=recv_sems.at[7 * t + k],
                device_id=to, device_id_type=MESH)

        mine = [pltpu.make_async_copy(ins[t], view(t, me), local_sems.at[t]) for t in range(n)]
        first = []
        for t in range(n):
            first.append(copy(t, 0, me, sibling, src=ins[t]))
            first += [copy(t, 1 + j, me, (*chip, c), src=ins[t]) for j, chip in enumerate(chips)]
        if phase == 0:
            for cp in mine + first:
                cp.start()
            return

        def forward(t):
            for j, chip in enumerate(chips):
                copy(t, 1 + j, (*chip, c), me).wait_recv()
                copy(t, 4 + j, (*chip, c), sibling).start()

        if progressive and phase <= n:
            forward(phase - 1)
            return
        if not progressive:
            for t in range(n):
                forward(t)
        passed = [copy(t, 4 + j, (*chip, c), sibling) for t in range(n) for j, chip in enumerate(chips)]
        for t in range(n):
            copy(t, 0, sibling, me).wait_recv()
            for j, chip in enumerate(chips):
                copy(t, 4 + j, (*chip, 1 - c), me).wait_recv()
        for cp in first + passed:
            cp.wait_send()
        for cp in mine:
            cp.wait()

    out_shape = [jax.ShapeDtypeStruct((N_DEV * s.shape[0], s.shape[1]) if ax == 0
                                      else (s.shape[0], N_DEV * s.shape[1]), s.dtype)
                 for s, ax in zip(shards, axes)]
    return _Comm(shards, out_shape, [pltpu.SemaphoreType.DMA((7 * n,)), pltpu.SemaphoreType.DMA((7 * n,)),
                                     pltpu.SemaphoreType.DMA((n,))], run, when)


def _all_gather_routed(shards, axes):
    n = len(shards)
    shapes = [s.shape for s in shards]
    assert all(s[0] % 32 == 0 for s in shapes)

    def run(ins, outs, sems, phase):
        send_sems, recv_sems, local_sems = sems
        x, y, c = _mesh_pos()
        me, sibling = (x, y, c), (x, y, 1 - c)
        nx, ny, nd = (1 - x, y), (x, 1 - y), (1 - x, 1 - y)

        def block(t, chip, core):
            return _shard_view(outs[t], 4 * chip[0] + 2 * chip[1] + core, shapes[t], axes[t])

        def half(t, chip, core, h):
            hr = shapes[t][0] // 2
            return block(t, chip, core).at[pl.ds(h * hr, hr), :]

        def own_half(t, h):
            hr = shapes[t][0] // 2
            return ins[t].at[pl.ds(h * hr, hr), :]

        def copy(t, k, src, dst, to):
            return pltpu.make_async_remote_copy(src_ref=src, dst_ref=dst, send_sem=send_sems.at[10 * t + k],
                                                recv_sem=recv_sems.at[10 * t + k], device_id=to, device_id_type=MESH)

        def arrived(t, k, dst):
            copy(t, k, dst, dst, me).wait_recv()

        mine = [pltpu.make_async_copy(ins[t], block(t, (x, y), c), local_sems.at[t]) for t in range(n)]
        own = []
        for t in range(n):
            own += [copy(t, 0, ins[t], block(t, (x, y), c), sibling),
                    copy(t, 1, own_half(t, 0), half(t, (x, y), c, 0), (*nx, c)),
                    copy(t, 2, own_half(t, 1), half(t, (x, y), c, 1), (*nx, c)),
                    copy(t, 3, own_half(t, 1), half(t, (x, y), c, 1), (*ny, c)),
                    copy(t, 4, own_half(t, 0), half(t, (x, y), c, 0), (*ny, c))]
        if phase == 0:
            for cp in mine + own:
                cp.start()
            return
        relays = []

        def relay(t, k, view, to):
            cp = copy(t, k, view, view, to)
            cp.start()
            relays.append(cp)

        for t in range(n):
            arrived(t, 1, half(t, nx, c, 0))
            relay(t, 5, half(t, nx, c, 0), (*ny, c))
            arrived(t, 3, half(t, ny, c, 1))
            relay(t, 6, half(t, ny, c, 1), (*nx, c))
            arrived(t, 2, half(t, nx, c, 1))
            relay(t, 7, block(t, nx, c), sibling)
            arrived(t, 4, half(t, ny, c, 0))
            relay(t, 8, block(t, ny, c), sibling)
            arrived(t, 5, half(t, nd, c, 0))
            arrived(t, 6, half(t, nd, c, 1))
            relay(t, 9, block(t, nd, c), sibling)
        for t in range(n):
            arrived(t, 0, block(t, (x, y), 1 - c))
            arrived(t, 7, block(t, nx, 1 - c))
            arrived(t, 8, block(t, ny, 1 - c))
            arrived(t, 9, block(t, nd, 1 - c))
        for cp in own + relays:
            cp.wait_send()
        for cp in mine:
            cp.wait()

    out_shape = [jax.ShapeDtypeStruct((N_DEV * s.shape[0], s.shape[1]) if ax == 0
                                      else (s.shape[0], N_DEV * s.shape[1]), s.dtype)
                 for s, ax in zip(shards, axes)]
    return _Comm(shards, out_shape, [pltpu.SemaphoreType.DMA((10 * n,)), pltpu.SemaphoreType.DMA((10 * n,)),
                                     pltpu.SemaphoreType.DMA((n,))], run)


def _rs_to_sibling(grads, shapes, axes):
    n = len(grads)

    def run(gs, outs, sems, phase):
        send_sems, recv_sems = sems
        x, y, c = _mesh_pos()
        copies = [pltpu.make_async_remote_copy(
            src_ref=_shard_view(gs[t], 2 * k + (1 - c), shapes[t], axes[t]), dst_ref=outs[t].at[k],
            send_sem=send_sems.at[4 * t + k], recv_sem=recv_sems.at[4 * t + k],
            device_id=(x, y, 1 - c), device_id_type=MESH) for t in range(n) for k in range(4)]
        if phase == 0:
            for cp in copies:
                cp.start()
            return
        for cp in copies:
            cp.wait_recv()
        for cp in copies:
            cp.wait_send()

    return _Comm(grads, [jax.ShapeDtypeStruct((4,) + tuple(s), BF16) for s in shapes],
                 [pltpu.SemaphoreType.DMA((4 * n,)), pltpu.SemaphoreType.DMA((4 * n,))], run)


def _rs_pair_sum(g, recv, cidx, shape, axis, *, name):
    r, c = shape
    tr = _tile(r, max(8, (1 << 21) // c), 16)
    nr = r // tr

    def body(c_ref, g_ref, rv_ref, o_ref):
        o_ref[...] = (g_ref[...].astype(F32) + rv_ref[...].astype(F32)).astype(BF16)

    if axis == 1:
        g_spec = pl.BlockSpec((tr, c), lambda k, i, s: (i, 2 * k + s[0]))
    else:
        g_spec = pl.BlockSpec((tr, c), lambda k, i, s: ((2 * k + s[0]) * nr + i, 0))
    blk = pl.BlockSpec((None, tr, c), lambda k, i, s: (k, i, 0))
    return pl.pallas_call(
        body, name=name,
        grid_spec=pltpu.PrefetchScalarGridSpec(num_scalar_prefetch=1, grid=(4, nr), in_specs=[g_spec, blk],
                                               out_specs=blk),
        out_shape=jax.ShapeDtypeStruct((4, r, c), BF16),
        compiler_params=_params("parallel", "parallel"))(cidx, g, recv)


def _rs_to_chips(parts):
    n = len(parts)

    def run(ps, outs, sems, phase):
        send_sems, recv_sems, local_sems = sems
        x, y, c = _mesh_pos()
        my_chip = 2 * x + y
        peers = [(1 - x, y), (x, 1 - y), (1 - x, 1 - y)]
        local = [pltpu.make_async_copy(ps[t].at[my_chip], outs[t].at[my_chip], local_sems.at[t]) for t in range(n)]
        sends = [pltpu.make_async_remote_copy(
            src_ref=ps[t].at[2 * px + py], dst_ref=outs[t].at[my_chip],
            send_sem=send_sems.at[3 * t + j], recv_sem=recv_sems.at[3 * t + j],
            device_id=(px, py, c), device_id_type=MESH) for t in range(n) for j, (px, py) in enumerate(peers)]
        if phase == 0:
            for cp in local + sends:
                cp.start()
            return
        for t in range(n):
            for j, (px, py) in enumerate(peers):
                pltpu.make_async_remote_copy(
                    src_ref=ps[t].at[2 * px + py], dst_ref=outs[t].at[2 * px + py],
                    send_sem=send_sems.at[3 * t + j], recv_sem=recv_sems.at[3 * t + j],
                    device_id=(x, y, c), device_id_type=MESH).wait_recv()
        for cp in sends:
            cp.wait_send()
        for cp in local:
            cp.wait()

    return _Comm(parts, [jax.ShapeDtypeStruct(p.shape, BF16) for p in parts],
                 [pltpu.SemaphoreType.DMA((3 * n,)), pltpu.SemaphoreType.DMA((3 * n,)),
                  pltpu.SemaphoreType.DMA((n,))], run)


def _exchange_small(buf):
    def run(ins, outs, sems, phase):
        (in_ref,), (slots,) = ins, outs
        send_sems, recv_sems, local_sem = sems
        x, y, c = _mesh_pos()
        me = 4 * x + 2 * y + c
        local = pltpu.make_async_copy(in_ref, slots.at[me], local_sem.at[0])
        flips = [(fx, fy, fc) for fx in (0, 1) for fy in (0, 1) for fc in (0, 1)][1:]
        peers = [(1 - x if fx else x, 1 - y if fy else y, 1 - c if fc else c) for fx, fy, fc in flips]
        sends = [pltpu.make_async_remote_copy(src_ref=in_ref, dst_ref=slots.at[me], send_sem=send_sems.at[k],
                                              recv_sem=recv_sems.at[k], device_id=peer, device_id_type=MESH)
                 for k, peer in enumerate(peers)]
        if phase == 0:
            for cp in [local] + sends:
                cp.start()
            return
        for k, (px, py, pc) in enumerate(peers):
            pltpu.make_async_remote_copy(src_ref=in_ref, dst_ref=slots.at[4 * px + 2 * py + pc],
                                         send_sem=send_sems.at[k], recv_sem=recv_sems.at[k],
                                         device_id=(x, y, c), device_id_type=MESH).wait_recv()
        for cp in sends:
            cp.wait_send()
        local.wait()

    return _Comm([buf], [jax.ShapeDtypeStruct((N_DEV,) + buf.shape, F32)],
                 [pltpu.SemaphoreType.DMA((7,)), pltpu.SemaphoreType.DMA((7,)), pltpu.SemaphoreType.DMA((1,))], run)


def _sum_slots(slots, *, name):
    _, R, C = slots.shape
    tr = _tile(R, 512)

    def body(s_ref, o_ref):
        acc = s_ref[0]
        for p in range(1, N_DEV):
            acc = acc + s_ref[p]
        o_ref[...] = acc

    return pl.pallas_call(
        body, name=name, grid=(R // tr,), in_specs=[pl.BlockSpec((N_DEV, tr, C), lambda i: (0, i, 0))],
        out_specs=pl.BlockSpec((tr, C), lambda i: (i, 0)), out_shape=jax.ShapeDtypeStruct((R, C), F32),
        compiler_params=_params("parallel"))(slots)


def _adam_math(g, w, m, v):
    m_new = ADAM_B1 * m + (1.0 - ADAM_B1) * g
    v_new = ADAM_B2 * v + (1.0 - ADAM_B2) * (g * g)
    m_hat = m_new / ADAM_C1
    v_hat = v_new / ADAM_C2
    delta = -ADAM_LR * (m_hat / (jnp.sqrt(v_hat) + ADAM_EPS) + ADAM_WD * w)
    return delta, m_new, v_new


def _adamw_sharded(q, w, m, v, *, name):
    r, c = w.shape
    tr = _tile(r, max(8, (1 << 19) // c), 16)

    def body(q_ref, w_ref, m_ref, v_ref, g_ref, d_ref, mo_ref, vo_ref):
        g = ((q_ref[0].astype(F32) + q_ref[1].astype(F32)) + q_ref[2].astype(F32)) + q_ref[3].astype(F32)
        g_ref[...] = g
        d_ref[...], mo_ref[...], vo_ref[...] = _adam_math(g, w_ref[...], m_ref[...], v_ref[...])

    blk = pl.BlockSpec((tr, c), lambda i: (i, 0))
    return pl.pallas_call(
        body, name=name, grid=(r // tr,),
        in_specs=[pl.BlockSpec((4, tr, c), lambda i: (0, i, 0)), blk, blk, blk], out_specs=[blk] * 4,
        out_shape=[jax.ShapeDtypeStruct((r, c), F32)] * 4,
        compiler_params=_params("parallel"))(q, w, m, v)


def _adamw_plain(g, w, m, v, *, name):
    r, c = w.shape
    tr = _tile(r, 512)

    def body(g_ref, w_ref, m_ref, v_ref, d_ref, mo_ref, vo_ref):
        d_ref[...], mo_ref[...], vo_ref[...] = _adam_math(g_ref[...], w_ref[...], m_ref[...], v_ref[...])

    blk = pl.BlockSpec((tr, c), lambda i: (i, 0))
    return pl.pallas_call(
        body, name=name, grid=(r // tr,), in_specs=[blk] * 4, out_specs=[blk] * 3,
        out_shape=[jax.ShapeDtypeStruct((r, c), F32)] * 3,
        compiler_params=_params("parallel"))(g, w, m, v)


def _pack_rows(arrays):
    return jnp.concatenate([a.reshape(-1, LANES) for a in arrays], axis=0)


def kernel(x, ffn1_w_gu, ffn1_w_down, ln1_g, ln1_b, w_in, b_in, sgu_ln_g, sgu_ln_b, sgu_w_s, sgu_b_s, w_a_proj, conv_w_dw, conv_b_dw, conv_ln_g, conv_ln_b, w_b_proj, w_out, ln2_g, ln2_b, ffn2_w_gu, ffn2_w_down, ln3_g, ln3_b, loss_target, m_ffn1_w_gu, m_ffn1_w_down, m_ln1_g, m_ln1_b, m_w_in, m_b_in, m_sgu_ln_g, m_sgu_ln_b, m_sgu_w_s, m_sgu_b_s, m_w_a_proj, m_conv_w_dw, m_conv_b_dw, m_conv_ln_g, m_conv_ln_b, m_w_b_proj, m_w_out, m_ln2_g, m_ln2_b, m_ffn2_w_gu, m_ffn2_w_down, m_ln3_g, m_ln3_b, v_ffn1_w_gu, v_ffn1_w_down, v_ln1_g, v_ln1_b, v_w_in, v_b_in, v_sgu_ln_g, v_sgu_ln_b, v_sgu_w_s, v_sgu_b_s, v_w_a_proj, v_conv_w_dw, v_conv_b_dw, v_conv_ln_g, v_conv_ln_b, v_w_b_proj, v_w_out, v_ln2_g, v_ln2_b, v_ffn2_w_gu, v_ffn2_w_down, v_ln3_g, v_ln3_b):
    given = dict(locals())
    w = {n: given[n][0] for n in WEIGHTS}
    mom = {n: given["m_" + n][0] for n in WEIGHTS}
    var = {n: given["v_" + n][0] for n in WEIGHTS}
    xt = x[0]
    target = loss_target[0]
    T, D = xt.shape
    A = w['w_a_proj'].shape[0]

    big_names = list(BIG)
    early = ['ffn1_w_gu', 'ffn1_w_down']
    later = ['ffn2_w_gu']
    late = [n for n in big_names if n not in early + later]
    conv_w_pad = jnp.pad(w['conv_w_dw'], ((0, CONV_WPAD - CONV_WIDTH), (0, 0)))
    w_bf = {n: w[n].astype(BF16) for n in big_names}
    full = dict(zip(early, _comm_call(_all_gather_routed([w_bf[n] for n in early], [BIG[n] for n in early]),
                                      name="all_gather_ffn1")))
    gather_late = _all_gather([w_bf[n] for n in late] + [conv_w_pad], [BIG[n] for n in late] + [1],
                              progressive=True, busy=0.65)
    gather_later = _all_gather([w_bf[n] for n in later], [BIG[n] for n in later], progressive=True, busy=0.8)

    def row(v):
        return v.reshape(1, -1)

    ones = jnp.ones((1, D), F32)
    zeros = jnp.zeros((1, D), F32)
    w_s = w['sgu_w_s']
    w_st = jnp.swapaxes(w_s, 1, 2)
    b_sb = jnp.broadcast_to(w['sgu_b_s'][:, :, None], w_s.shape)

    (gate1, up1, xb0, xh1, rstd1), gathered = _ffn_fwd(xt, ones, zeros, full['ffn1_w_gu'], full['ffn1_w_down'],
                                                  affine=False, name="ffn1_fwd", carry=gather_late)
    full.update(zip(late, gathered[:-1]))
    conv_w_full = gathered[-1]
    g1, b1 = row(w['ln1_g']), row(w['ln1_b'])
    (proj, xb1), gathered = _inproj_fwd(xh1, g1, b1, full['w_in'], row(w['b_in']), name="inproj_fwd",
                                        carry=gather_later)
    full.update(zip(later, gathered))
    sg = _sgu_fwd(proj, row(w['sgu_ln_g']), row(w['sgu_ln_b']), w_s, b_sb, name="sgu_fwd")
    conv_out, cv = _conv_fwd(proj, conv_w_full, row(w['conv_b_dw']), row(w['conv_ln_g']), row(w['conv_ln_b']),
                             name="conv_fwd")
    ya, yb, mixed = _mix_fwd_gate(sg, cv, proj, full['w_a_proj'], full['w_b_proj'], name="mix_fwd_gate")
    xh2, rstd2 = _mix_fwd_out(mixed, full['w_out'], xh1, g1, b1, name="mix_fwd_out")
    g2, b2 = row(w['ln2_g']), row(w['ln2_b'])
    gate2, up2, xb2, dr3, loss_part, d_ln3_g, d_ln3_b = _ffn_fwd(
        xh2, g2, b2, full['ffn2_w_gu'], full['ffn2_w_down'], affine=True, name="ffn2_fwd_loss",
        final=(row(w['ln3_g']), row(w['ln3_b']), target))

    F = full['ffn2_w_down'].shape[0]
    h2, dgate2, dup2, dr2, d_ln2_g, d_ln2_b = _ffn_bwd(dr3, gate2, up2, full['ffn2_w_gu'], full['ffn2_w_down'],
                                                      name="ffn2_bwd", prev=(xh2, rstd2, g2))
    G, P, Q = {}, {}, {}
    cidx = lax.axis_index("c").astype(jnp.int32).reshape(1)

    def to_sibling(names):
        return _rs_to_sibling([G[n] for n in names], [w[n].shape for n in names], [BIG[n] for n in names])

    def pair_sum(names, received):
        for n, rv in zip(names, received):
            P[n] = _rs_pair_sum(G[n], rv, cidx, w[n].shape, BIG[n], name="rs_pair_sum_" + n)

    def to_chips(names):
        return _rs_to_chips([P[n] for n in names])

    G['ffn2_w_down'] = _mm_tn(h2, dr3, name="dw_ffn2_down", tm_pref=1408, tn_pref=2048, scale=0.5)
    gu, rv = _mm_tn(xb2, dgate2, name="dw_ffn2_gate", tm_pref=2048, tn_pref=1408,
                    n_total=2 * F, carry=to_sibling(['ffn2_w_down']))
    pair_sum(['ffn2_w_down'], rv)
    G['ffn2_w_gu'], q = _mm_tn(xb2, dup2, name="dw_ffn2_up", tm_pref=2048, tn_pref=1408,
                               into=gu, col_off=F, n_total=2 * F, carry=to_chips(['ffn2_w_down']))
    Q['ffn2_w_down'] = q[0]

    (dya, dyb, dproj, dbin_gate), rv = _mix_bwd_gate(dr2, full['w_out'], proj, ya, yb, name="mix_bwd_gate",
                                                     carry=to_sibling(['ffn2_w_gu']))
    pair_sum(['ffn2_w_gu'], rv)
    dsg, dcv = _mix_bwd_proj(dya, dyb, full['w_a_proj'], full['w_b_proj'], name="mix_bwd_proj")
    dproj, dbin_sgu, d_sgu_ln_g, d_sgu_ln_b, d_w_s, d_b_s = _sgu_bwd(
        proj, dsg, row(w['sgu_ln_g']), row(w['sgu_ln_b']), w_s, w_st, b_sb, dproj, name="sgu_bwd")
    dconv, d_conv_ln_g, d_conv_ln_b, d_conv_b = _conv_bwd_ln(dcv, conv_out, row(w['conv_ln_g']),
                                                            row(w['conv_ln_b']), name="conv_bwd_ln")
    (dproj, dbin_conv, d_conv_w), q = _conv_bwd(proj, dconv, conv_w_full, dproj, name="conv_bwd",
                                                carry=to_chips(['ffn2_w_gu']))
    Q['ffn2_w_gu'] = q[0]

    mid = ['w_out', 'w_a_proj', 'w_b_proj']
    G['w_out'] = _mm_tn(mixed, dr2, name="dw_out", tm_pref=2048, tn_pref=1024)
    G['w_a_proj'] = _mm_tn(sg, dya, name="dw_a_proj", tm_pref=1024, tn_pref=2048)
    G['w_b_proj'] = _mm_tn(cv, dyb, name="dw_b_proj", tm_pref=1024, tn_pref=2048)
    G['w_in'], rv = _mm_tn(xb1, dproj, name="dw_in", tm_pref=2048, tn_pref=1024, carry=to_sibling(mid))
    pair_sum(mid, rv)
    both = _join(to_chips(mid), to_sibling(['w_in']))
    (dr1, d_ln1_g, d_ln1_b), moved = _inproj_bwd(dproj, full['w_in'], dr2, xh1, rstd1, g1, name="inproj_bwd",
                                                 carry=both)
    q, rv = both.split(moved)
    Q.update(zip(mid, q))
    pair_sum(['w_in'], rv)

    h1, dgate1, dup1, grad_x = _ffn_bwd(dr1, gate1, up1, full['ffn1_w_gu'], full['ffn1_w_down'], name="ffn1_bwd")
    G['ffn1_w_down'], q = _mm_tn(h1, dr1, name="dw_ffn1_down", tm_pref=1408, tn_pref=2048, scale=0.5,
                                 carry=to_chips(['w_in']))
    Q['w_in'] = q[0]
    small_g = {'ln1_g': d_ln1_g, 'ln1_b': d_ln1_b,
               'b_in': jnp.concatenate([dbin_sgu, dbin_conv, dbin_gate], axis=1),
               'sgu_ln_g': d_sgu_ln_g, 'sgu_ln_b': d_sgu_ln_b, 'sgu_w_s': d_w_s, 'sgu_b_s': d_b_s,
               'conv_b_dw': d_conv_b, 'conv_ln_g': d_conv_ln_g, 'conv_ln_b': d_conv_ln_b,
               'ln2_g': d_ln2_g, 'ln2_b': d_ln2_b, 'ln3_g': d_ln3_g, 'ln3_b': d_ln3_b}
    packed = _pack_rows([small_g[n] for n in SMALL] + [d_conv_w, loss_part])
    both = _join(to_sibling(['ffn1_w_down']), _exchange_small(packed))
    gu, moved = _mm_tn(xb0, dgate1, name="dw_ffn1_gate", tm_pref=2048, tn_pref=1408, n_total=2 * F, carry=both)
    rv, slots = both.split(moved)
    pair_sum(['ffn1_w_down'], rv)
    reduced = _sum_slots(slots[0], name="sum_small")
    G['ffn1_w_gu'], q = _mm_tn(xb0, dup1, name="dw_ffn1_up", tm_pref=2048, tn_pref=1408, into=gu, col_off=F,
                               n_total=2 * F, carry=to_chips(['ffn1_w_down']))
    Q['ffn1_w_down'] = q[0]
    pair_sum(['ffn1_w_gu'], _comm_call(to_sibling(['ffn1_w_gu']), name="rs_to_sibling_last"))
    Q['ffn1_w_gu'] = _comm_call(to_chips(['ffn1_w_gu']), name="rs_to_chips_last")[0]

    grads, deltas, new_m, new_v = {}, {}, {}, {}
    for n in big_names:
        grads[n], deltas[n], new_m[n], new_v[n] = _adamw_sharded(Q[n], w[n], mom[n], var[n], name="adamw_" + n)

    B = conv_w_full.shape[1]
    n_small_rows = sum(w[n].size for n in SMALL) // LANES
    conv_rows = CONV_WIDTH * B // LANES
    d_small, m_small, v_small = _adamw_plain(
        reduced[:n_small_rows], _pack_rows([w[n] for n in SMALL]), _pack_rows([mom[n] for n in SMALL]),
        _pack_rows([var[n] for n in SMALL]), name="adamw_small")
    off = 0
    for n in SMALL:
        rows = w[n].size // LANES
        grads[n] = reduced[off:off + rows].reshape(w[n].shape)
        deltas[n] = d_small[off:off + rows].reshape(w[n].shape)
        new_m[n] = m_small[off:off + rows].reshape(w[n].shape)
        new_v[n] = v_small[off:off + rows].reshape(w[n].shape)
        off += rows
    conv_g_full = reduced[off:off + conv_rows].reshape(CONV_WIDTH, B)
    bs = w['conv_w_dw'].shape[1]
    my_block = 4 * lax.axis_index("x") + 2 * lax.axis_index("y") + lax.axis_index("c")
    grads['conv_w_dw'] = lax.dynamic_slice(conv_g_full, (0, my_block * bs), (CONV_WIDTH, bs))
    deltas['conv_w_dw'], new_m['conv_w_dw'], new_v['conv_w_dw'] = _adamw_plain(
        grads['conv_w_dw'], w['conv_w_dw'], mom['conv_w_dw'], var['conv_w_dw'], name="adamw_conv_w")
    loss = reduced[off + conv_rows, 0]

    def lead(a):
        return a[None]

    return (loss, grad_x[None], *[lead(grads[n]) for n in WEIGHTS], *[lead(deltas[n]) for n in WEIGHTS],
            *[lead(new_m[n]) for n in WEIGHTS], *[lead(new_v[n]) for n in WEIGHTS])
```

```python
import functools
import math

import jax
import jax.numpy as jnp
from jax import lax
from jax.experimental import pallas as pl
from jax.experimental.pallas import tpu as pltpu

F32 = jnp.float32
BF16 = jnp.bfloat16

ALPHA = 2.0 ** 0.25
LN_EPS = 1e-5
CONV_WIDTH = 31
CONV_HALO = 32
CONV_ROWS = 64
CONV_WPAD = 32
CHUNK = 64
GMLP_BLOCK = 128
A_GROUPS = 8
N_DEV = 8
LANES = 128

ADAM_LR = 0.001
ADAM_B1 = 0.9
ADAM_B2 = 0.999
ADAM_EPS = 1e-08
ADAM_WD = 0.01
ADAM_STEP = 10
ADAM_C1 = 1.0 - ADAM_B1 ** ADAM_STEP
ADAM_C2 = 1.0 - ADAM_B2 ** ADAM_STEP

VMEM_LIMIT_BYTES = 60 * 2 ** 20
MESH = pl.DeviceIdType.MESH
ANY = pl.BlockSpec(memory_space=pl.ANY)

WEIGHTS = ['ffn1_w_gu', 'ffn1_w_down', 'ln1_g', 'ln1_b', 'w_in', 'b_in', 'sgu_ln_g', 'sgu_ln_b', 'sgu_w_s',
           'sgu_b_s', 'w_a_proj', 'conv_w_dw', 'conv_b_dw', 'conv_ln_g', 'conv_ln_b', 'w_b_proj', 'w_out',
           'ln2_g', 'ln2_b', 'ffn2_w_gu', 'ffn2_w_down', 'ln3_g', 'ln3_b']
BIG = {'ffn1_w_gu': 1, 'ffn1_w_down': 0, 'w_in': 1, 'w_a_proj': 1, 'w_b_proj': 1, 'w_out': 0,
       'ffn2_w_gu': 1, 'ffn2_w_down': 0}
SMALL = [n for n in WEIGHTS if n not in BIG and n != 'conv_w_dw']


def _tile(n, pref, mult=8):
    best = None
    for d in range(mult, min(n, pref) + 1, mult):
        if n % d == 0:
            best = d
    return n if best is None else best


def _params(*sem):
    return pltpu.CompilerParams(dimension_semantics=sem, vmem_limit_bytes=VMEM_LIMIT_BYTES)


def _dot(a, b):
    return jnp.dot(a, b, preferred_element_type=F32)


def _dot_nt(a, b):
    return lax.dot_general(a, b, (((1,), (1,)), ((), ())), preferred_element_type=F32)


def _dot_tn(a, b):
    return lax.dot_general(a, b, (((0,), (0,)), ((), ())), preferred_element_type=F32)


def _sig(x):
    return 1.0 / (1.0 + jnp.exp(-x))


_GELU_K = math.sqrt(2.0 / math.pi)
_GELU_C = 0.044715


def _gelu(x):
    t = jnp.tanh(_GELU_K * (x + _GELU_C * x * x * x))
    return 0.5 * x * (1.0 + t)


def _gelu_grad(x):
    x2 = x * x
    t = jnp.tanh(_GELU_K * (x + _GELU_C * x2 * x))
    return 0.5 * (1.0 + t) + 0.5 * x * (1.0 - t * t) * (_GELU_K * (1.0 + 3.0 * _GELU_C * x2))


def _ln_stats(r):
    mu = jnp.mean(r, axis=-1, keepdims=True)
    rc = r - mu
    var = jnp.mean(rc * rc, axis=-1, keepdims=True)
    rstd = lax.rsqrt(var + LN_EPS)
    return rc * rstd, rstd


def _ln_bwd(dy, xh, rstd, g):
    dxh = dy * g
    m1 = jnp.mean(dxh, axis=-1, keepdims=True)
    m2 = jnp.mean(dxh * xh, axis=-1, keepdims=True)
    return rstd * (dxh - m1 - xh * m2)


def _colsum(v):
    return jnp.sum(v, axis=0, keepdims=True)


def _chunk_mask(transposed):
    shift = CHUNK.bit_length() - 1
    r = lax.broadcasted_iota(jnp.int32, (GMLP_BLOCK, GMLP_BLOCK), 0) >> shift
    c = lax.broadcasted_iota(jnp.int32, (GMLP_BLOCK, GMLP_BLOCK), 1) >> shift
    return (r <= c) if transposed else (c <= r)


class _Comm:
    def __init__(self, inputs, out_shape, scratch, run, when=(0.0, 1.0)):
        self.inputs, self.out_shape, self.scratch, self.run = list(inputs), list(out_shape), list(scratch), run
        self.when = tuple(when)
        self.parts = [len(self.out_shape)]

    def split(self, outs):
        res, o = [], 0
        for n in self.parts:
            res.append(list(outs[o:o + n]))
            o += n
        return res


def _join(*comms):
    comms = [c for c in comms if c is not None]
    if not comms:
        return None
    assert all(c.when == (0.0, 1.0) for c in comms)

    def run(ins, outs, sems, phase):
        i = o = s = 0
        for c in comms:
            c.run(ins[i:i + len(c.inputs)], outs[o:o + len(c.out_shape)], sems[s:s + len(c.scratch)], phase)
            i, o, s = i + len(c.inputs), o + len(c.out_shape), s + len(c.scratch)

    joined = _Comm(sum((c.inputs for c in comms), []), sum((c.out_shape for c in comms), []),
                   sum((c.scratch for c in comms), []), run)
    joined.parts = [len(c.out_shape) for c in comms]
    return joined


def _call(body, *, name, grid, in_specs, out_specs, out_shape, args, sem, scratch_shapes=(), aliases=None, carry=None):
    in_specs, out_specs, out_shape = list(in_specs), list(out_specs), list(out_shape)
    scratch_shapes = list(scratch_shapes)
    if carry is None:
        return pl.pallas_call(body, name=name, grid=grid, in_specs=in_specs, out_specs=out_specs,
                              out_shape=out_shape, scratch_shapes=scratch_shapes,
                              input_output_aliases=aliases or {}, compiler_params=_params(*sem))(*args)
    n_in, n_out, n_scr = len(args), len(out_shape), len(scratch_shapes)
    c_in, c_out = len(carry.inputs), len(carry.out_shape)
    n_steps = math.prod(grid)
    at_step = [int(round(f * (n_steps - 1))) for f in carry.when]
    assert at_step[0] == 0 and at_step[-1] == n_steps - 1 and at_step == sorted(at_step)

    def wrapped(*refs):
        ins, c_ins = refs[:n_in], refs[n_in:n_in + c_in]
        o0 = n_in + c_in
        outs, c_outs = refs[o0:o0 + n_out], refs[o0 + n_out:o0 + n_out + c_out]
        s0 = o0 + n_out + c_out
        scr, c_sems = refs[s0:s0 + n_scr], refs[s0 + n_scr:]
        step = 0
        for a, g in enumerate(grid):
            step = step * g + pl.program_id(a)
        pl.when(step == 0)(functools.partial(carry.run, c_ins, c_outs, c_sems, 0))
        body(*ins, *outs, *scr)
        for k in range(1, len(at_step)):
            pl.when(step == at_step[k])(functools.partial(carry.run, c_ins, c_outs, c_sems, k))

    res = pl.pallas_call(
        wrapped, name=name, grid=grid, in_specs=in_specs + [ANY] * c_in, out_specs=out_specs + [ANY] * c_out,
        out_shape=out_shape + carry.out_shape, scratch_shapes=scratch_shapes + carry.scratch,
        input_output_aliases=aliases or {},
        compiler_params=pltpu.CompilerParams(dimension_semantics=("arbitrary",) * len(grid),
                                             vmem_limit_bytes=VMEM_LIMIT_BYTES, has_side_effects=True),
    )(*args, *carry.inputs)
    return list(res[:n_out]), list(res[n_out:])


def _comm_call(comm, *, name):
    n_in, n_out = len(comm.inputs), len(comm.out_shape)

    def body(*refs):
        ins, outs, sems = refs[:n_in], refs[n_in:n_in + n_out], refs[n_in + n_out:]
        for k in range(len(comm.when)):
            comm.run(ins, outs, sems, k)

    return list(pl.pallas_call(
        body, name=name, in_specs=[ANY] * n_in, out_specs=[ANY] * n_out, out_shape=comm.out_shape,
        scratch_shapes=comm.scratch, compiler_params=pltpu.CompilerParams(has_side_effects=True))(*comm.inputs))


def _when(cond, fn):
    if isinstance(cond, bool):
        if cond:
            fn()
    else:
        pl.when(cond)(fn)


def _ffn_tiles(T, F):
    return _tile(T, 512), _tile(F, 512, LANES)


def _row_chunks(tm, rows=128):
    rows = _tile(tm, rows)
    return [slice(r, r + rows) for r in range(0, tm, rows)]


def _hidden_loop(nj, step):
    def pair(jj, c):
        step(2 * jj, 0)
        step(2 * jj + 1, 1)
        return c
    if nj // 2:
        lax.fori_loop(0, nj // 2, pair, 0)
    if nj % 2:
        step(nj - 1, 0)


def _ffn_fwd_looped(xh, lg, lb, wgu, wd, *, affine, name, final=None, carry=None):
    T, D = xh.shape
    F = wd.shape[0]
    tm, tn = _ffn_tiles(T, F)
    nj, nt = F // tn, T // tm
    is_final = final is not None

    def body(*refs):
        if is_final:
            (xh_ref, lg_ref, lb_ref, wgu_hbm, wd_hbm, ng_ref, nb_ref, tgt_hbm,
             gate_hbm, up_hbm, xb_ref, dr_hbm, loss_ref, dng_ref, dnb_ref,
             acc_sc, wg_buf, wu_buf, wd_buf, g_buf, u_buf, w_sem, o_sem, tgt_sc, dr_sc, t_sem) = refs
        else:
            (xh_ref, lg_ref, lb_ref, wgu_hbm, wd_hbm,
             gate_hbm, up_hbm, xb_ref, xho_ref, rstd_ref,
             acc_sc, wg_buf, wu_buf, wd_buf, g_buf, u_buf, w_sem, o_sem) = refs
        i = pl.program_id(0)
        rows = pl.ds(pl.multiple_of(i * tm, tm), tm)

        def cols(j, base=0):
            return pl.ds(pl.multiple_of(base + j * tn, LANES), tn)

        def w_copies(j, slot):
            return (pltpu.make_async_copy(wgu_hbm.at[:, cols(j)], wg_buf.at[slot], w_sem.at[slot]),
                    pltpu.make_async_copy(wgu_hbm.at[:, cols(j, F)], wu_buf.at[slot], w_sem.at[2 + slot]),
                    pltpu.make_async_copy(wd_hbm.at[cols(j), :], wd_buf.at[slot], w_sem.at[4 + slot]))

        def o_copies(j, slot):
            return (pltpu.make_async_copy(g_buf.at[slot], gate_hbm.at[rows, cols(j)], o_sem.at[slot]),
                    pltpu.make_async_copy(u_buf.at[slot], up_hbm.at[rows, cols(j)], o_sem.at[2 + slot]))

        def start(copies):
            for cp in copies:
                cp.start()

        def wait(copies):
            for cp in copies:
                cp.wait()

        def xin(rs):
            v = xh_ref[rs, :]
            return v * lg_ref[...] + lb_ref[...] if affine else v

        _when(i == 0, lambda: start(w_copies(0, 0)))
        if is_final:
            tgt_in = pltpu.make_async_copy(tgt_hbm.at[rows, :], tgt_sc, t_sem.at[0])
            dr_out = pltpu.make_async_copy(dr_sc, dr_hbm.at[rows, :], t_sem.at[1])
            tgt_in.start()
        for rs in _row_chunks(tm):
            xb_ref[rs, :] = xin(rs).astype(BF16)
        acc_sc[...] = jnp.zeros_like(acc_sc)

        def step(j, slot):
            _when(j + 1 < nj, lambda: start(w_copies(j + 1, 1 - slot)))
            wait(w_copies(j, slot))
            xb = xb_ref[...]
            g = _dot(xb, wg_buf[slot])
            u = _dot(xb, wu_buf[slot])
            _when(j >= 2, lambda: wait(o_copies(j - 2, slot)))
            g_buf[slot] = g.astype(BF16)
            u_buf[slot] = u.astype(BF16)
            start(o_copies(j, slot))
            h = g * _sig(g) * u
            acc_sc[...] += _dot(h.astype(BF16), wd_buf[slot])

        _hidden_loop(nj, step)
        _when(i + 1 < nt, lambda: start(w_copies(0, 0)))
        for j in range(max(nj - 2, 0), nj):
            wait(o_copies(j, j % 2))

        if is_final:
            @pl.when(i == 0)
            def _():
                loss_ref[...] = jnp.zeros_like(loss_ref)
                dng_ref[...] = jnp.zeros_like(dng_ref)
                dnb_ref[...] = jnp.zeros_like(dnb_ref)
            tgt_in.wait()
            _when(i > 0, dr_out.wait)
        for rs in _row_chunks(tm):
            r = ALPHA * xin(rs) + 0.5 * acc_sc[rs, :]
            xho, rstd = _ln_stats(r)
            if not is_final:
                xho_ref[rs, :] = xho
                rstd_ref[rs, :] = jnp.broadcast_to(rstd, (rs.stop - rs.start, LANES))
            else:
                ng = ng_ref[...]
                e = xho * ng + nb_ref[...] - tgt_sc[rs, :]
                part = _colsum(jnp.sum(e * e, axis=1, keepdims=True)) * (0.5 / D)
                loss_ref[...] += jnp.broadcast_to(part, loss_ref.shape)
                dy = e * (1.0 / D)
                dng_ref[...] += _colsum(dy * xho)
                dnb_ref[...] += _colsum(dy)
                dr_sc[rs, :] = _ln_bwd(dy, xho, rstd, ng)
        if is_final:
            dr_out.start()
            _when(i == nt - 1, dr_out.wait)

    row = pl.BlockSpec((tm, D), lambda i: (i, 0))
    vec = pl.BlockSpec((1, D), lambda i: (0, 0))
    in_specs = [row, vec, vec, ANY, ANY]
    args = [xh, lg, lb, wgu, wd]
    out_shape = [jax.ShapeDtypeStruct((T, F), BF16), jax.ShapeDtypeStruct((T, F), BF16),
                 jax.ShapeDtypeStruct((T, D), BF16)]
    out_specs = [ANY, ANY, row]
    scratch = [pltpu.VMEM((tm, D), F32),
               pltpu.VMEM((2, D, tn), BF16), pltpu.VMEM((2, D, tn), BF16), pltpu.VMEM((2, tn, D), BF16),
               pltpu.VMEM((2, tm, tn), BF16), pltpu.VMEM((2, tm, tn), BF16),
               pltpu.SemaphoreType.DMA((6,)), pltpu.SemaphoreType.DMA((4,))]
    if is_final:
        in_specs += [vec, vec, ANY]
        args += list(final)
        out_shape += [jax.ShapeDtypeStruct((T, D), F32), jax.ShapeDtypeStruct((8, LANES), F32),
                      jax.ShapeDtypeStruct((1, D), F32), jax.ShapeDtypeStruct((1, D), F32)]
        out_specs += [ANY, pl.BlockSpec((8, LANES), lambda i: (0, 0)), vec, vec]
        scratch += [pltpu.VMEM((tm, D), F32), pltpu.VMEM((tm, D), F32), pltpu.SemaphoreType.DMA((2,))]
    else:
        out_shape += [jax.ShapeDtypeStruct((T, D), F32), jax.ShapeDtypeStruct((T, LANES), F32)]
        out_specs += [row, pl.BlockSpec((tm, LANES), lambda i: (i, 0))]
    return _call(body, name=name, grid=(nt,), in_specs=in_specs, out_specs=out_specs, out_shape=out_shape,
                 scratch_shapes=scratch, sem=("arbitrary",), args=args, carry=carry)


def _ffn_bwd_looped(dr, gate, up, wgu, wd, *, name, prev=None, carry=None):
    T, D = dr.shape
    F = wd.shape[0]
    tm, tn = _ffn_tiles(T, F)
    nj, nt = F // tn, T // tm
    has_prev = prev is not None

    def body(*refs):
        if has_prev:
            (dr_ref, gate_hbm, up_hbm, wgu_hbm, wd_hbm, xh_hbm, rstd_ref, lg_ref,
             h_hbm, dg_hbm, du_hbm, dprev_hbm, dlg_ref, dlb_ref, *scr) = refs
            xh_sc = scr.pop()
        else:
            (dr_ref, gate_hbm, up_hbm, wgu_hbm, wd_hbm,
             h_hbm, dg_hbm, du_hbm, dprev_hbm, *scr) = refs
        (df_sc, dx_sc, wg_buf, wu_buf, wd_buf, gi_buf, ui_buf, h_buf, dg_buf, du_buf, i_sem, o_sem,
         dp_sc, t_sem) = scr
        i = pl.program_id(0)
        rows = pl.ds(pl.multiple_of(i * tm, tm), tm)
        dp_out = pltpu.make_async_copy(dp_sc, dprev_hbm.at[rows, :], t_sem.at[0])
        if has_prev:
            xh_in = pltpu.make_async_copy(xh_hbm.at[rows, :], xh_sc, t_sem.at[1])
            xh_in.start()

        def cols(j, base=0):
            return pl.ds(pl.multiple_of(base + j * tn, LANES), tn)

        def i_copies(j, slot, tile=None):
            at = rows if tile is None else pl.ds(pl.multiple_of(tile * tm, tm), tm)
            return (pltpu.make_async_copy(wgu_hbm.at[:, cols(j)], wg_buf.at[slot], i_sem.at[slot]),
                    pltpu.make_async_copy(wgu_hbm.at[:, cols(j, F)], wu_buf.at[slot], i_sem.at[2 + slot]),
                    pltpu.make_async_copy(wd_hbm.at[cols(j), :], wd_buf.at[slot], i_sem.at[4 + slot]),
                    pltpu.make_async_copy(gate_hbm.at[at, cols(j)], gi_buf.at[slot], i_sem.at[6 + slot]),
                    pltpu.make_async_copy(up_hbm.at[at, cols(j)], ui_buf.at[slot], i_sem.at[8 + slot]))

        def o_copies(j, slot):
            return (pltpu.make_async_copy(h_buf.at[slot], h_hbm.at[rows, cols(j)], o_sem.at[slot]),
                    pltpu.make_async_copy(dg_buf.at[slot], dg_hbm.at[rows, cols(j)], o_sem.at[2 + slot]),
                    pltpu.make_async_copy(du_buf.at[slot], du_hbm.at[rows, cols(j)], o_sem.at[4 + slot]))

        def start(copies):
            for cp in copies:
                cp.start()

        def wait(copies):
            for cp in copies:
                cp.wait()

        _when(i == 0, lambda: start(i_copies(0, 0)))
        for rs in _row_chunks(tm):
            d = dr_ref[rs, :]
            df_sc[rs, :] = (0.5 * d).astype(BF16)
            dx_sc[rs, :] = ALPHA * d

        def step(j, slot):
            _when(j + 1 < nj, lambda: start(i_copies(j + 1, 1 - slot)))
            wait(i_copies(j, slot))
            g = gi_buf[slot].astype(F32)
            u = ui_buf[slot].astype(F32)
            dh = _dot_nt(df_sc[...], wd_buf[slot])
            s = _sig(g)
            sil = g * s
            dg = (dh * u * (s * (1.0 + g * (1.0 - s)))).astype(BF16)
            du = (dh * sil).astype(BF16)
            _when(j >= 2, lambda: wait(o_copies(j - 2, slot)))
            h_buf[slot] = (sil * u).astype(BF16)
            dg_buf[slot] = dg
            du_buf[slot] = du
            start(o_copies(j, slot))
            dx_sc[...] += _dot_nt(dg, wg_buf[slot])
            dx_sc[...] += _dot_nt(du, wu_buf[slot])

        _hidden_loop(nj, step)
        _when(i + 1 < nt, lambda: start(i_copies(0, 0, i + 1)))
        for j in range(max(nj - 2, 0), nj):
            wait(o_copies(j, j % 2))

        if has_prev:
            @pl.when(i == 0)
            def _():
                dlg_ref[...] = jnp.zeros_like(dlg_ref)
                dlb_ref[...] = jnp.zeros_like(dlb_ref)
            xh_in.wait()
        _when(i > 0, dp_out.wait)
        for rs in _row_chunks(tm):
            dxin = dx_sc[rs, :]
            if not has_prev:
                dp_sc[rs, :] = dxin
            else:
                x_hat = xh_sc[rs, :]
                dlg_ref[...] += _colsum(dxin * x_hat)
                dlb_ref[...] += _colsum(dxin)
                dp_sc[rs, :] = _ln_bwd(dxin, x_hat, rstd_ref[rs, 0:1], lg_ref[...])
        dp_out.start()
        _when(i == nt - 1, dp_out.wait)

    row = pl.BlockSpec((tm, D), lambda i: (i, 0))
    vec = pl.BlockSpec((1, D), lambda i: (0, 0))
    in_specs = [row, ANY, ANY, ANY, ANY]
    args = [dr, gate, up, wgu, wd]
    out_shape = [jax.ShapeDtypeStruct((T, F), BF16)] * 3 + [jax.ShapeDtypeStruct((T, D), F32)]
    out_specs = [ANY, ANY, ANY, ANY]
    scratch = [pltpu.VMEM((tm, D), BF16), pltpu.VMEM((tm, D), F32),
               pltpu.VMEM((2, D, tn), BF16), pltpu.VMEM((2, D, tn), BF16), pltpu.VMEM((2, tn, D), BF16)]
    scratch += [pltpu.VMEM((2, tm, tn), BF16)] * 5
    scratch += [pltpu.SemaphoreType.DMA((10,)), pltpu.SemaphoreType.DMA((6,)),
                pltpu.VMEM((tm, D), F32), pltpu.SemaphoreType.DMA((2,))]
    if has_prev:
        in_specs += [ANY, pl.BlockSpec((tm, LANES), lambda i: (i, 0)), vec]
        args += list(prev)
        out_shape += [jax.ShapeDtypeStruct((1, D), F32)] * 2
        out_specs += [vec, vec]
        scratch += [pltpu.VMEM((tm, D), F32)]
    return _call(body, name=name, grid=(nt,), in_specs=in_specs, out_specs=out_specs, out_shape=out_shape,
                 scratch_shapes=scratch, sem=("arbitrary",), args=args, carry=carry)


def _ffn_fwd(xh, lg, lb, wgu, wd, *, affine, name, final=None, carry=None):
    T, D = xh.shape
    F = wd.shape[0]
    tm = _tile(T, 512)
    tn = _tile(F, 512, LANES)
    nj = F // tn
    is_final = final is not None

    def body(*refs):
        if is_final:
            (xh_ref, lg_ref, lb_ref, wg_ref, wu_ref, wd_ref, ng_ref, nb_ref, tgt_ref,
             gate_ref, up_ref, xb_sc, dr_ref, loss_ref, dng_ref, dnb_ref, acc_sc) = refs
        else:
            (xh_ref, lg_ref, lb_ref, wg_ref, wu_ref, wd_ref,
             gate_ref, up_ref, xb_sc, xho_ref, rstd_ref, acc_sc) = refs
        i = pl.program_id(0)
        j = pl.program_id(1)

        def xin():
            v = xh_ref[...]
            return v * lg_ref[...] + lb_ref[...] if affine else v

        @pl.when(j == 0)
        def _():
            xb_sc[...] = xin().astype(BF16)
            acc_sc[...] = jnp.zeros_like(acc_sc)

        xb = xb_sc[...]
        g = _dot(xb, wg_ref[...])
        u = _dot(xb, wu_ref[...])
        gate_ref[...] = g.astype(BF16)
        up_ref[...] = u.astype(BF16)
        h = g * _sig(g) * u
        acc_sc[...] += _dot(h.astype(BF16), wd_ref[...])

        @pl.when(j == nj - 1)
        def _():
            if is_final:
                @pl.when(i == 0)
                def _():
                    loss_ref[...] = jnp.zeros_like(loss_ref)
                    dng_ref[...] = jnp.zeros_like(dng_ref)
                    dnb_ref[...] = jnp.zeros_like(dnb_ref)
            for rs in _row_chunks(tm):
                v = xh_ref[rs, :]
                if affine:
                    v = v * lg_ref[...] + lb_ref[...]
                r = ALPHA * v + 0.5 * acc_sc[rs, :]
                xho, rstd = _ln_stats(r)
                if not is_final:
                    xho_ref[rs, :] = xho
                    rstd_ref[rs, :] = jnp.broadcast_to(rstd, (rs.stop - rs.start, LANES))
                else:
                    ng = ng_ref[...]
                    e = xho * ng + nb_ref[...] - tgt_ref[rs, :]
                    part = _colsum(jnp.sum(e * e, axis=1, keepdims=True)) * (0.5 / D)
                    loss_ref[...] += jnp.broadcast_to(part, loss_ref.shape)
                    dy = e * (1.0 / D)
                    dng_ref[...] += _colsum(dy * xho)
                    dnb_ref[...] += _colsum(dy)
                    dr_ref[rs, :] = _ln_bwd(dy, xho, rstd, ng)

    row = pl.BlockSpec((tm, D), lambda i, j: (i, 0))
    vec = pl.BlockSpec((1, D), lambda i, j: (0, 0))
    hid = pl.BlockSpec((tm, tn), lambda i, j: (i, j))
    in_specs = [row, vec, vec,
                pl.BlockSpec((D, tn), lambda i, j: (0, j)),
                pl.BlockSpec((D, tn), lambda i, j: (0, j + nj)),
                pl.BlockSpec((tn, D), lambda i, j: (j, 0))]
    args = [xh, lg, lb, wgu, wgu, wd]
    out_shape = [jax.ShapeDtypeStruct((T, F), BF16), jax.ShapeDtypeStruct((T, F), BF16),
                 jax.ShapeDtypeStruct((T, D), BF16)]
    out_specs = [hid, hid, row]
    if is_final:
        in_specs += [vec, vec, row]
        args += list(final)
        out_shape += [jax.ShapeDtypeStruct((T, D), F32), jax.ShapeDtypeStruct((8, LANES), F32),
                      jax.ShapeDtypeStruct((1, D), F32), jax.ShapeDtypeStruct((1, D), F32)]
        out_specs += [row, pl.BlockSpec((8, LANES), lambda i, j: (0, 0)), vec, vec]
        sem = ("arbitrary", "arbitrary")
    else:
        out_shape += [jax.ShapeDtypeStruct((T, D), F32), jax.ShapeDtypeStruct((T, LANES), F32)]
        out_specs += [row, pl.BlockSpec((tm, LANES), lambda i, j: (i, 0))]
        sem = ("parallel", "arbitrary")
    return _call(body, name=name, grid=(T // tm, nj), in_specs=in_specs, out_specs=out_specs, out_shape=out_shape,
                 scratch_shapes=[pltpu.VMEM((tm, D), F32)], sem=sem, args=args, carry=carry)


def _ffn_bwd(dr, gate, up, wgu, wd, *, name, prev=None, carry=None):
    T, D = dr.shape
    F = wd.shape[0]
    tm = _tile(T, 512)
    tn = _tile(F, 512, LANES)
    nj = F // tn
    has_prev = prev is not None

    def body(*refs):
        if has_prev:
            (dr_ref, gate_ref, up_ref, wd_ref, wg_ref, wu_ref, xh_ref, rstd_ref, lg_ref,
             h_ref, dg_ref, du_ref, dprev_ref, dlg_ref, dlb_ref, df_sc, dx_sc) = refs
        else:
            (dr_ref, gate_ref, up_ref, wd_ref, wg_ref, wu_ref,
             h_ref, dg_ref, du_ref, dprev_ref, df_sc, dx_sc) = refs
        i = pl.program_id(0)
        j = pl.program_id(1)

        @pl.when(j == 0)
        def _():
            d = dr_ref[...]
            df_sc[...] = (0.5 * d).astype(BF16)
            dx_sc[...] = ALPHA * d

        g = gate_ref[...].astype(F32)
        u = up_ref[...].astype(F32)
        dh = _dot_nt(df_sc[...], wd_ref[...])
        s = _sig(g)
        sil = g * s
        h_ref[...] = (sil * u).astype(BF16)
        dg = (dh * u * (s * (1.0 + g * (1.0 - s)))).astype(BF16)
        du = (dh * sil).astype(BF16)
        dg_ref[...] = dg
        du_ref[...] = du
        dx_sc[...] += _dot_nt(dg, wg_ref[...]) + _dot_nt(du, wu_ref[...])

        @pl.when(j == nj - 1)
        def _():
            dxin = dx_sc[...]
            if not has_prev:
                dprev_ref[...] = dxin
            else:
                @pl.when(i == 0)
                def _():
                    dlg_ref[...] = jnp.zeros_like(dlg_ref)
                    dlb_ref[...] = jnp.zeros_like(dlb_ref)
                xh = xh_ref[...]
                dlg_ref[...] += _colsum(dxin * xh)
                dlb_ref[...] += _colsum(dxin)
                dprev_ref[...] = _ln_bwd(dxin, xh, rstd_ref[:, 0:1], lg_ref[...])

    row = pl.BlockSpec((tm, D), lambda i, j: (i, 0))
    vec = pl.BlockSpec((1, D), lambda i, j: (0, 0))
    hid = pl.BlockSpec((tm, tn), lambda i, j: (i, j))
    in_specs = [row, hid, hid,
                pl.BlockSpec((tn, D), lambda i, j: (j, 0)),
                pl.BlockSpec((D, tn), lambda i, j: (0, j)),
                pl.BlockSpec((D, tn), lambda i, j: (0, j + nj))]
    args = [dr, gate, up, wd, wgu, wgu]
    out_shape = [jax.ShapeDtypeStruct((T, F), BF16)] * 3 + [jax.ShapeDtypeStruct((T, D), F32)]
    out_specs = [hid, hid, hid, row]
    if has_prev:
        in_specs += [row, pl.BlockSpec((tm, LANES), lambda i, j: (i, 0)), vec]
        args += list(prev)
        out_shape += [jax.ShapeDtypeStruct((1, D), F32)] * 2
        out_specs += [vec, vec]
        sem = ("arbitrary", "arbitrary")
    else:
        sem = ("parallel", "arbitrary")
    return _call(body, name=name, grid=(T // tm, nj), in_specs=in_specs, out_specs=out_specs, out_shape=out_shape,
                 scratch_shapes=[pltpu.VMEM((tm, D), BF16), pltpu.VMEM((tm, D), F32)], sem=sem, args=args,
                 carry=carry)


def _mm_tn(a, b, *, name, tm_pref, tn_pref, tk_pref=1024, scale=1.0, a_affine=None, into=None, col_off=0,
           n_total=None, carry=None):
    T, M = a.shape
    N = b.shape[1]
    n_total = N if n_total is None else n_total
    tM = _tile(M, tm_pref, LANES)
    tN = _tile(N, tn_pref, LANES)
    tk = _tile(T, tk_pref)
    nt = T // tk
    assert col_off % tN == 0
    off_blocks = col_off // tN
    has_aff = a_affine is not None
    has_into = into is not None

    def body(*refs):
        refs = list(refs)
        a_ref = refs.pop(0)
        if has_aff:
            lg_ref = refs.pop(0)
            lb_ref = refs.pop(0)
        b_ref = refs.pop(0)
        if has_into:
            refs.pop(0)
        o_ref, acc_sc = refs
        t = pl.program_id(2)

        @pl.when(t == 0)
        def _():
            acc_sc[...] = jnp.zeros_like(acc_sc)

        av = a_ref[...]
        if has_aff:
            av = av * lg_ref[...] + lb_ref[...]
        acc_sc[...] += _dot_tn(av.astype(BF16), b_ref[...].astype(BF16))

        @pl.when(t == nt - 1)
        def _():
            o_ref[...] = (acc_sc[...] * scale).astype(BF16)

    in_specs = [pl.BlockSpec((tk, tM), lambda m, n, t: (t, m))]
    args = [a]
    if has_aff:
        in_specs += [pl.BlockSpec((1, tM), lambda m, n, t: (0, m))] * 2
        args += list(a_affine)
    in_specs.append(pl.BlockSpec((tk, tN), lambda m, n, t: (t, n)))
    args.append(b)
    aliases = {}
    if has_into:
        aliases = {len(args): 0}
        in_specs.append(ANY)
        args.append(into)
    res = _call(body, name=name, grid=(M // tM, N // tN, nt), in_specs=in_specs,
                out_specs=[pl.BlockSpec((tM, tN), lambda m, n, t: (m, n + off_blocks))],
                out_shape=[jax.ShapeDtypeStruct((M, n_total), BF16)],
                scratch_shapes=[pltpu.VMEM((tM, tN), F32)], aliases=aliases,
                sem=("parallel", "parallel", "arbitrary"), args=args, carry=carry)
    return res[0] if carry is None else (res[0][0], res[1])


def _inproj_fwd(xh, lg, lb, w, bias, *, name, carry=None):
    T, D = xh.shape
    N = w.shape[1]
    tm = _tile(T, 1024)
    tn = _tile(N, 1024, LANES)

    def body(xh_ref, lg_ref, lb_ref, w_ref, b_ref, o_ref, xb_ref):
        @pl.when(pl.program_id(1) == 0)
        def _():
            xb_ref[...] = (xh_ref[...] * lg_ref[...] + lb_ref[...]).astype(BF16)
        o_ref[...] = _dot(xb_ref[...], w_ref[...]) + b_ref[...]

    return _call(
        body, name=name, grid=(T // tm, N // tn),
        in_specs=[pl.BlockSpec((tm, D), lambda i, j: (i, 0)),
                  pl.BlockSpec((1, D), lambda i, j: (0, 0)), pl.BlockSpec((1, D), lambda i, j: (0, 0)),
                  pl.BlockSpec((D, tn), lambda i, j: (0, j)), pl.BlockSpec((1, tn), lambda i, j: (0, j))],
        out_specs=[pl.BlockSpec((tm, tn), lambda i, j: (i, j)), pl.BlockSpec((tm, D), lambda i, j: (i, 0))],
        out_shape=[jax.ShapeDtypeStruct((T, N), F32), jax.ShapeDtypeStruct((T, D), BF16)],
        sem=("parallel", "arbitrary"), args=[xh, lg, lb, w, bias], carry=carry)


def _inproj_bwd(dproj, w, dr_next, xh, rstd, lg, *, name, carry=None):
    T, N = dproj.shape
    D = w.shape[0]
    tm = _tile(T, 512)
    tn = _tile(N, 2048, LANES)
    nj = N // tn

    def body(dp_ref, w_ref, drn_ref, xh_ref, rstd_ref, lg_ref, dprev_ref, dlg_ref, dlb_ref, dx_sc):
        i = pl.program_id(0)
        j = pl.program_id(1)

        @pl.when(j == 0)
        def _():
            for rs in _row_chunks(tm):
                dx_sc[rs, :] = ALPHA * drn_ref[rs, :]

        dx_sc[...] += _dot_nt(dp_ref[...], w_ref[...])

        @pl.when(j == nj - 1)
        def _():
            @pl.when(i == 0)
            def _():
                dlg_ref[...] = jnp.zeros_like(dlg_ref)
                dlb_ref[...] = jnp.zeros_like(dlb_ref)
            for rs in _row_chunks(tm):
                dx = dx_sc[rs, :]
                x_hat = xh_ref[rs, :]
                dlg_ref[...] += _colsum(dx * x_hat)
                dlb_ref[...] += _colsum(dx)
                dprev_ref[rs, :] = _ln_bwd(dx, x_hat, rstd_ref[rs, 0:1], lg_ref[...])

    row = pl.BlockSpec((tm, D), lambda i, j: (i, 0))
    vec = pl.BlockSpec((1, D), lambda i, j: (0, 0))
    return _call(
        body, name=name, grid=(T // tm, nj),
        in_specs=[pl.BlockSpec((tm, tn), lambda i, j: (i, j)), pl.BlockSpec((D, tn), lambda i, j: (0, j)),
                  row, row, pl.BlockSpec((tm, LANES), lambda i, j: (i, 0)), vec],
        out_specs=[row, vec, vec],
        out_shape=[jax.ShapeDtypeStruct((T, D), F32), jax.ShapeDtypeStruct((1, D), F32),
                   jax.ShapeDtypeStruct((1, D), F32)],
        scratch_shapes=[pltpu.VMEM((tm, D), F32)], sem=("arbitrary", "arbitrary"),
        args=[dproj, w, dr_next, xh, rstd, lg], carry=carry)


def _sgu_fwd(proj, ln_g, ln_b, w_s, b_sb, *, name):
    T = proj.shape[0]
    A = proj.shape[1] // 8
    hd = A // A_GROUPS
    tm = _tile(T, 512, GMLP_BLOCK)

    def body(u_ref, v_ref, g_ref, b_ref, ws_ref, bs_ref, o_ref):
        gu = _gelu(u_ref[...])
        vh, _ = _ln_stats(_gelu(v_ref[...]))
        vn = (vh * g_ref[...] + b_ref[...]).astype(BF16)
        mask = _chunk_mask(False)
        for h in range(A_GROUPS):
            wm = jnp.where(mask, ws_ref[h], 0.0).astype(BF16)
            cols = slice(h * hd, (h + 1) * hd)
            for n in range(tm // GMLP_BLOCK):
                rows = slice(n * GMLP_BLOCK, (n + 1) * GMLP_BLOCK)
                s = _dot(wm, vn[rows, cols]) + bs_ref[h][:, :hd]
                o_ref[rows, cols] = (gu[rows, cols] * s).astype(BF16)

    vec = pl.BlockSpec((1, A), lambda i: (0, 0))
    full = pl.BlockSpec((A_GROUPS, GMLP_BLOCK, GMLP_BLOCK), lambda i: (0, 0, 0))
    return pl.pallas_call(
        body, name=name, grid=(T // tm,),
        in_specs=[pl.BlockSpec((tm, A), lambda i: (i, 0)), pl.BlockSpec((tm, A), lambda i: (i, 1)),
                  vec, vec, full, full],
        out_specs=pl.BlockSpec((tm, A), lambda i: (i, 0)),
        out_shape=jax.ShapeDtypeStruct((T, A), BF16),
        compiler_params=_params("parallel"))(proj, proj, ln_g, ln_b, w_s, b_sb)


def _sgu_bwd(proj, dsg, ln_g, ln_b, w_s, w_st, b_sb, dproj, *, name):
    T = proj.shape[0]
    A = proj.shape[1] // 8
    hd = A // A_GROUPS
    tm = _tile(T, 512, GMLP_BLOCK)
    nt = T // tm

    def body(u_ref, v_ref, dsg_ref, g_ref, b_ref, ws_ref, wst_ref, bs_ref, _alias,
             dp_ref, dbin_ref, dlg_ref, dlb_ref, dws_ref, dbs_ref, dvn_sc, dgu_sc, dbs_sc):
        i = pl.program_id(0)

        @pl.when(i == 0)
        def _():
            dbin_ref[...] = jnp.zeros_like(dbin_ref)
            dlg_ref[...] = jnp.zeros_like(dlg_ref)
            dlb_ref[...] = jnp.zeros_like(dlb_ref)
            dws_ref[...] = jnp.zeros_like(dws_ref)
            dbs_sc[...] = jnp.zeros_like(dbs_sc)

        u = u_ref[...]
        v = v_ref[...]
        gu = _gelu(u)
        vh, rstd = _ln_stats(_gelu(v))
        gain = g_ref[...]
        vn = (vh * gain + b_ref[...]).astype(BF16)
        dsg_v = dsg_ref[...]
        mask = _chunk_mask(False)
        mask_t = _chunk_mask(True)
        for h in range(A_GROUPS):
            wm = jnp.where(mask, ws_ref[h], 0.0).astype(BF16)
            wmt = jnp.where(mask_t, wst_ref[h], 0.0).astype(BF16)
            cols = slice(h * hd, (h + 1) * hd)
            for n in range(tm // GMLP_BLOCK):
                rows = slice(n * GMLP_BLOCK, (n + 1) * GMLP_BLOCK)
                vb = vn[rows, cols]
                s = _dot(wm, vb) + bs_ref[h][:, :hd]
                d_out = dsg_v[rows, cols]
                dgu_sc[rows, cols] = d_out * s
                ds = d_out * gu[rows, cols]
                ds_b = ds.astype(BF16)
                dws_ref[h] += _dot_nt(ds_b, vb)
                dbs_sc[h] += ds
                dvn_sc[rows, cols] = _dot(wmt, ds_b)
        dvn = dvn_sc[...]
        dlg_ref[...] += _colsum(dvn * vh)
        dlb_ref[...] += _colsum(dvn)
        dv = _ln_bwd(dvn, vh, rstd, gain) * _gelu_grad(v)
        du = dgu_sc[...] * _gelu_grad(u)
        dp_ref[:, 0:A] = du.astype(BF16)
        dp_ref[:, A:2 * A] = dv.astype(BF16)
        dbin_ref[:, 0:A] += _colsum(du)
        dbin_ref[:, A:2 * A] += _colsum(dv)

        @pl.when(i == nt - 1)
        def _():
            for h in range(A_GROUPS):
                dws_ref[h] = jnp.where(mask, dws_ref[h], 0.0)
                dbs_ref[h:h + 1, :] = _colsum(dbs_sc[h].T)

    vec = pl.BlockSpec((1, A), lambda i: (0, 0))
    full = pl.BlockSpec((A_GROUPS, GMLP_BLOCK, GMLP_BLOCK), lambda i: (0, 0, 0))
    tile = pl.BlockSpec((tm, A), lambda i: (i, 0))
    return pl.pallas_call(
        body, name=name, grid=(nt,),
        in_specs=[tile, pl.BlockSpec((tm, A), lambda i: (i, 1)), tile, vec, vec, full, full, full, ANY],
        out_specs=[pl.BlockSpec((tm, 2 * A), lambda i: (i, 0)), pl.BlockSpec((1, 2 * A), lambda i: (0, 0)),
                   vec, vec, full, pl.BlockSpec((A_GROUPS, GMLP_BLOCK), lambda i: (0, 0))],
        out_shape=[jax.ShapeDtypeStruct(dproj.shape, BF16), jax.ShapeDtypeStruct((1, 2 * A), F32),
                   jax.ShapeDtypeStruct((1, A), F32), jax.ShapeDtypeStruct((1, A), F32),
                   jax.ShapeDtypeStruct((A_GROUPS, GMLP_BLOCK, GMLP_BLOCK), F32),
                   jax.ShapeDtypeStruct((A_GROUPS, GMLP_BLOCK), F32)],
        scratch_shapes=[pltpu.VMEM((tm, A), F32), pltpu.VMEM((tm, A), F32),
                        pltpu.VMEM((A_GROUPS, GMLP_BLOCK, hd), F32)],
        input_output_aliases={8: 0},
        compiler_params=_params("arbitrary"))(proj, proj, dsg, ln_g, ln_b, w_s, w_st, b_sb, dproj)


def _conv_tiles(T, B):
    tm = _tile(T, 256, CONV_ROWS)
    lb = min(LANES, B)
    return tm, tm // CONV_HALO, lb


def _fill_phases(src, dst, B, lb):
    rows = dst.shape[1]
    for p in range(1, 8):
        for cb in range(B // lb):
            ls = slice(cb * lb, (cb + 1) * lb)
            dst[p - 1, :, ls] = src[p:p + rows, ls]


def _tap_sums(w_ref, src, phases, ls, tm, offset_of_tap):
    lb = ls.stop - ls.start
    n_rc = tm // CONV_ROWS
    accs = [jnp.zeros((CONV_ROWS // 8, 8, lb), F32) for _ in range(n_rc)]
    for k in range(CONV_WIDTH):
        wk = jnp.broadcast_to(w_ref[k:k + 1, ls], (8, lb))[None]
        for rc in range(n_rc):
            win = _shifted(src, phases, rc * CONV_ROWS + offset_of_tap(k), ls)
            accs[rc] = accs[rc] + wk * win.reshape(CONV_ROWS // 8, 8, lb)
    return [a.reshape(CONV_ROWS, lb) for a in accs]


def _shifted(src, phases, off, ls):
    m, p = divmod(off, 8)
    if p == 0:
        return src[off:off + CONV_ROWS, ls]
    return phases[p - 1, 8 * m:8 * m + CONV_ROWS, ls]


def _conv_fwd(proj, w_dw, b_dw, ln_g, ln_b, *, name):
    T = proj.shape[0]
    B = proj.shape[1] // 8
    tm, nh, lb = _conv_tiles(T, B)

    def body(ap_ref, gp_ref, a_ref, g_ref, w_ref, bdw_ref, lg_ref, lb_ref, c_ref, cv_ref, z_sc, zp_sc):
        i = pl.program_id(0)
        z_sc[0:CONV_HALO, :] = jnp.where(i > 0, ap_ref[...] * _sig(gp_ref[...]), 0.0)
        z_sc[CONV_HALO:CONV_HALO + tm, :] = a_ref[...] * _sig(g_ref[...])
        _fill_phases(z_sc, zp_sc, B, lb)
        for cb in range(B // lb):
            ls = slice(cb * lb, (cb + 1) * lb)
            accs = _tap_sums(w_ref, z_sc, zp_sc, ls, tm, lambda k: CONV_HALO - (CONV_WIDTH - 1) + k)
            for rc, acc in enumerate(accs):
                c_ref[rc * CONV_ROWS:(rc + 1) * CONV_ROWS, ls] = acc + bdw_ref[:, ls]
        xh, _ = _ln_stats(c_ref[...])
        y = xh * lg_ref[...] + lb_ref[...]
        cv_ref[...] = (y * _sig(y)).astype(BF16)

    vec = pl.BlockSpec((1, B), lambda i: (0, 0))
    halo_a = pl.BlockSpec((CONV_HALO, B), lambda i: (jnp.maximum(i * nh - 1, 0), 2))
    halo_g = pl.BlockSpec((CONV_HALO, B), lambda i: (jnp.maximum(i * nh - 1, 0), 3))
    return pl.pallas_call(
        body, name=name, grid=(T // tm,),
        in_specs=[halo_a, halo_g, pl.BlockSpec((tm, B), lambda i: (i, 2)), pl.BlockSpec((tm, B), lambda i: (i, 3)),
                  pl.BlockSpec((CONV_WPAD, B), lambda i: (0, 0)), vec, vec, vec],
        out_specs=[pl.BlockSpec((tm, B), lambda i: (i, 0))] * 2,
        out_shape=[jax.ShapeDtypeStruct((T, B), F32), jax.ShapeDtypeStruct((T, B), BF16)],
        scratch_shapes=[pltpu.VMEM((CONV_HALO + tm, B), F32), pltpu.VMEM((7, CONV_HALO + tm - 8, B), F32)],
        compiler_params=_params("parallel"))(proj, proj, proj, proj, w_dw, b_dw, ln_g, ln_b)


def _conv_bwd_ln(dcv, c, ln_g, ln_b, *, name):
    T, B = c.shape
    tm = _tile(T, 512)

    def body(dcv_ref, c_ref, lg_ref, lb_ref, dc_ref, dlg_ref, dlb_ref, dbdw_ref):
        @pl.when(pl.program_id(0) == 0)
        def _():
            dlg_ref[...] = jnp.zeros_like(dlg_ref)
            dlb_ref[...] = jnp.zeros_like(dlb_ref)
            dbdw_ref[...] = jnp.zeros_like(dbdw_ref)
        gain = lg_ref[...]
        xh, rstd = _ln_stats(c_ref[...])
        y = xh * gain + lb_ref[...]
        s = _sig(y)
        dy = dcv_ref[...] * (s * (1.0 + y * (1.0 - s)))
        dlg_ref[...] += _colsum(dy * xh)
        dlb_ref[...] += _colsum(dy)
        dc = _ln_bwd(dy, xh, rstd, gain)
        dc_ref[...] = dc
        dbdw_ref[...] += _colsum(dc)

    tile = pl.BlockSpec((tm, B), lambda i: (i, 0))
    vec = pl.BlockSpec((1, B), lambda i: (0, 0))
    return pl.pallas_call(
        body, name=name, grid=(T // tm,), in_specs=[tile, tile, vec, vec], out_specs=[tile, vec, vec, vec],
        out_shape=[jax.ShapeDtypeStruct((T, B), F32)] + [jax.ShapeDtypeStruct((1, B), F32)] * 3,
        compiler_params=_params("arbitrary"))(dcv, c, ln_g, ln_b)


def _conv_bwd(proj, dc, w_dw, dproj, *, name, carry=None):
    T = proj.shape[0]
    B = proj.shape[1] // 8
    tm, nh, lb = _conv_tiles(T, B)
    nt = T // tm
    n_halo = T // CONV_HALO

    def body(ap_ref, gp_ref, a_ref, g_ref, dc_ref, dcn_ref, w_ref, _alias,
             dp_ref, dbin_ref, dw_ref, z_sc, dc_sc, dz_sc, dw_sc, zp_sc, dcp_sc):
        i = pl.program_id(0)

        @pl.when(i == 0)
        def _():
            dbin_ref[...] = jnp.zeros_like(dbin_ref)
            dw_sc[...] = jnp.zeros_like(dw_sc)

        a = a_ref[...]
        s = _sig(g_ref[...])
        z_sc[0:CONV_HALO, :] = jnp.where(i > 0, ap_ref[...] * _sig(gp_ref[...]), 0.0)
        z_sc[CONV_HALO:CONV_HALO + tm, :] = a * s
        dc_sc[0:tm, :] = dc_ref[...]
        dc_sc[tm:tm + CONV_HALO, :] = jnp.where(i < nt - 1, dcn_ref[...], 0.0)
        _fill_phases(z_sc, zp_sc, B, lb)
        _fill_phases(dc_sc, dcp_sc, B, lb)
        for cb in range(B // lb):
            ls = slice(cb * lb, (cb + 1) * lb)
            accs = _tap_sums(w_ref, dc_sc, dcp_sc, ls, tm, lambda k: (CONV_WIDTH - 1) - k)
            for rc, acc in enumerate(accs):
                dz_sc[rc * CONV_ROWS:(rc + 1) * CONV_ROWS, ls] = acc
            for k in range(CONV_WIDTH):
                part = jnp.zeros((8, lb), F32)
                for rc in range(tm // CONV_ROWS):
                    r0 = rc * CONV_ROWS
                    prod = dc_sc[r0:r0 + CONV_ROWS, ls] * _shifted(
                        z_sc, zp_sc, r0 + CONV_HALO - (CONV_WIDTH - 1) + k, ls)
                    part = part + jnp.sum(prod.reshape(CONV_ROWS // 8, 8, lb), axis=0)
                dw_sc[8 * k:8 * k + 8, ls] += part
        dz = dz_sc[...]
        da = dz * s
        dg = dz * a * s * (1.0 - s)
        dp_ref[:, 0:B] = da.astype(BF16)
        dp_ref[:, B:2 * B] = dg.astype(BF16)
        dbin_ref[:, 0:B] += _colsum(da)
        dbin_ref[:, B:2 * B] += _colsum(dg)

        @pl.when(i == nt - 1)
        def _():
            for k in range(CONV_WIDTH):
                dw_ref[k:k + 1, :] = _colsum(dw_sc[8 * k:8 * k + 8, :])

    halo_a = pl.BlockSpec((CONV_HALO, B), lambda i: (jnp.maximum(i * nh - 1, 0), 2))
    halo_g = pl.BlockSpec((CONV_HALO, B), lambda i: (jnp.maximum(i * nh - 1, 0), 3))
    halo_dc = pl.BlockSpec((CONV_HALO, B), lambda i: (jnp.minimum((i + 1) * nh, n_halo - 1), 0))
    return _call(
        body, name=name, grid=(nt,),
        in_specs=[halo_a, halo_g, pl.BlockSpec((tm, B), lambda i: (i, 2)), pl.BlockSpec((tm, B), lambda i: (i, 3)),
                  pl.BlockSpec((tm, B), lambda i: (i, 0)), halo_dc,
                  pl.BlockSpec((CONV_WPAD, B), lambda i: (0, 0)), ANY],
        out_specs=[pl.BlockSpec((tm, 2 * B), lambda i: (i, 1)), pl.BlockSpec((1, 2 * B), lambda i: (0, 0)),
                   pl.BlockSpec((CONV_WIDTH, B), lambda i: (0, 0))],
        out_shape=[jax.ShapeDtypeStruct(dproj.shape, BF16), jax.ShapeDtypeStruct((1, 2 * B), F32),
                   jax.ShapeDtypeStruct((CONV_WIDTH, B), F32)],
        scratch_shapes=[pltpu.VMEM((CONV_HALO + tm, B), F32), pltpu.VMEM((tm + CONV_HALO, B), F32),
                        pltpu.VMEM((tm, B), F32), pltpu.VMEM((8 * CONV_WIDTH, B), F32),
                        pltpu.VMEM((7, CONV_HALO + tm - 8, B), F32), pltpu.VMEM((7, CONV_HALO + tm - 8, B), F32)],
        aliases={7: 0}, sem=("arbitrary",), args=[proj, proj, proj, proj, dc, dc, w_dw, dproj], carry=carry)


def _mix_fwd_gate(sg, cv, proj, wa, wb, *, name):
    T, A = sg.shape
    D = wa.shape[1]
    tm = _tile(T, 256)

    def body(sg_ref, cv_ref, la_ref, lb_ref, wa_ref, wb_ref, ya_ref, yb_ref, m_ref):
        ya = _dot(sg_ref[...], wa_ref[...])
        yb = _dot(cv_ref[...], wb_ref[...])
        ya_ref[...] = ya.astype(BF16)
        yb_ref[...] = yb.astype(BF16)
        m_ref[...] = (_sig(la_ref[...]) * ya + _sig(lb_ref[...]) * yb).astype(BF16)

    act = pl.BlockSpec((tm, A), lambda i: (i, 0))
    wide = pl.BlockSpec((tm, D), lambda i: (i, 0))
    wspec = pl.BlockSpec((A, D), lambda i: (0, 0))
    return pl.pallas_call(
        body, name=name, grid=(T // tm,),
        in_specs=[act, act, pl.BlockSpec((tm, D), lambda i: (i, 2)), pl.BlockSpec((tm, D), lambda i: (i, 3)),
                  wspec, wspec],
        out_specs=[wide] * 3, out_shape=[jax.ShapeDtypeStruct((T, D), BF16)] * 3,
        compiler_params=_params("parallel"))(sg, cv, proj, proj, wa, wb)


def _mix_fwd_out(m, wout, xh, lg, lb, *, name):
    T, D = xh.shape
    tm = _tile(T, 512)

    def body(m_ref, w_ref, xh_ref, lg_ref, lb_ref, xho_ref, rstd_ref):
        r = ALPHA * (xh_ref[...] * lg_ref[...] + lb_ref[...]) + _dot(m_ref[...], w_ref[...])
        xho, rstd = _ln_stats(r)
        xho_ref[...] = xho
        rstd_ref[...] = jnp.broadcast_to(rstd, (tm, LANES))

    row = pl.BlockSpec((tm, D), lambda i: (i, 0))
    vec = pl.BlockSpec((1, D), lambda i: (0, 0))
    return pl.pallas_call(
        body, name=name, grid=(T // tm,),
        in_specs=[row, pl.BlockSpec((D, D), lambda i: (0, 0)), row, vec, vec],
        out_specs=[row, pl.BlockSpec((tm, LANES), lambda i: (i, 0))],
        out_shape=[jax.ShapeDtypeStruct((T, D), F32), jax.ShapeDtypeStruct((T, LANES), F32)],
        compiler_params=_params("parallel"))(m, wout, xh, lg, lb)


def _mix_bwd_gate(dr, wout, proj, ya, yb, *, name, carry=None):
    T, D = dr.shape
    N = proj.shape[1]
    tm = _tile(T, 256)

    def body(dr_ref, w_ref, la_ref, lb_ref, ya_ref, yb_ref, dya_ref, dyb_ref, dp_ref, dbin_ref):
        @pl.when(pl.program_id(0) == 0)
        def _():
            dbin_ref[...] = jnp.zeros_like(dbin_ref)
        dm = _dot_nt(dr_ref[...].astype(BF16), w_ref[...])
        sa = _sig(la_ref[...])
        sb = _sig(lb_ref[...])
        dya_ref[...] = (dm * sa).astype(BF16)
        dyb_ref[...] = (dm * sb).astype(BF16)
        dla = dm * ya_ref[...].astype(F32) * sa * (1.0 - sa)
        dlb = dm * yb_ref[...].astype(F32) * sb * (1.0 - sb)
        dp_ref[:, 0:D] = dla.astype(BF16)
        dp_ref[:, D:2 * D] = dlb.astype(BF16)
        dbin_ref[:, 0:D] += _colsum(dla)
        dbin_ref[:, D:2 * D] += _colsum(dlb)

    row = pl.BlockSpec((tm, D), lambda i: (i, 0))
    return _call(
        body, name=name, grid=(T // tm,),
        in_specs=[row, pl.BlockSpec((D, D), lambda i: (0, 0)), pl.BlockSpec((tm, D), lambda i: (i, 2)),
                  pl.BlockSpec((tm, D), lambda i: (i, 3)), row, row],
        out_specs=[row, row, pl.BlockSpec((tm, 2 * D), lambda i: (i, 1)), pl.BlockSpec((1, 2 * D), lambda i: (0, 0))],
        out_shape=[jax.ShapeDtypeStruct((T, D), BF16), jax.ShapeDtypeStruct((T, D), BF16),
                   jax.ShapeDtypeStruct((T, N), BF16), jax.ShapeDtypeStruct((1, 2 * D), F32)],
        sem=("arbitrary",), args=[dr, wout, proj, proj, ya, yb], carry=carry)


def _mix_bwd_proj(dya, dyb, wa, wb, *, name):
    T, D = dya.shape
    A = wa.shape[0]
    tm = _tile(T, 512)

    def body(dya_ref, dyb_ref, wa_ref, wb_ref, dsg_ref, dcv_ref):
        dsg_ref[...] = _dot_nt(dya_ref[...], wa_ref[...])
        dcv_ref[...] = _dot_nt(dyb_ref[...], wb_ref[...])

    row = pl.BlockSpec((tm, D), lambda i: (i, 0))
    wspec = pl.BlockSpec((A, D), lambda i: (0, 0))
    act = pl.BlockSpec((tm, A), lambda i: (i, 0))
    return pl.pallas_call(
        body, name=name, grid=(T // tm,), in_specs=[row, row, wspec, wspec], out_specs=[act, act],
        out_shape=[jax.ShapeDtypeStruct((T, A), F32)] * 2,
        compiler_params=_params("parallel"))(dya, dyb, wa, wb)


def _mesh_pos():
    return lax.axis_index("x"), lax.axis_index("y"), lax.axis_index("c")


def _shard_view(ref, p, shape, axis):
    r, c = shape
    if axis == 0:
        return ref.at[pl.ds(pl.multiple_of(p * r, 16), r), :]
    return ref.at[:, pl.ds(pl.multiple_of(p * c, LANES), c)]


def _all_gather(shards, axes, progressive=False, busy=0.93):
    n = len(shards)
    shapes = [s.shape for s in shards]
    sizes = [s.size * s.dtype.itemsize for s in shards]
    if progressive:
        done = [sum(sizes[:t + 1]) / sum(sizes) for t in range(n)]
        when = (0.0,) + tuple(min(0.97, 0.04 + busy * d) for d in done) + (1.0,)
    else:
        when = (0.0, 1.0)

    def run(ins, outs, sems, phase):
        send_sems, recv_sems, local_sems = sems
        x, y, c = _mesh_pos()
        me, sibling = (x, y, c), (x, y, 1 - c)
        chips = [(1 - x, y), (x, 1 - y), (1 - x, 1 - y)]

        def view(t, pos):
            px, py, pc = pos
            return _shard_view(outs[t], 4 * px + 2 * py + pc, shapes[t], axes[t])

        def copy(t, k, block, to, src=None):
            return pltpu.make_async_remote_copy(
                src_ref=view(t, block) if src is None else src, dst_ref=view(t, block),
                send_sem=send_sems.at[7 * t + k], recv_sem=recv_sems.at[7 * t + k],
                device_id=to, device_id_type=MESH)

        mine = [pltpu.make_async_copy(ins[t], view(t, me), local_sems.at[t]) for t in range(n)]
        first = []
        for t in range(n):
            first.append(copy(t, 0, me, sibling, src=ins[t]))
            first += [copy(t, 1 + j, me, (*chip, c), src=ins[t]) for j, chip in enumerate(chips)]
        if phase == 0:
            for cp in mine + first:
                cp.start()
            return

        def forward(t):
            for j, chip in enumerate(chips):
                copy(t, 1 + j, (*chip, c), me).wait_recv()
                copy(t, 4 + j, (*chip, c), sibling).start()

        if progressive and phase <= n:
            forward(phase - 1)
            return
        if not progressive:
            for t in range(n):
                forward(t)
        passed = [copy(t, 4 + j, (*chip, c), sibling) for t in range(n) for j, chip in enumerate(chips)]
        for t in range(n):
            copy(t, 0, sibling, me).wait_recv()
            for j, chip in enumerate(chips):
                copy(t, 4 + j, (*chip, 1 - c), me).wait_recv()
        for cp in first + passed:
            cp.wait_send()
        for cp in mine:
            cp.wait()

    out_shape = [jax.ShapeDtypeStruct((N_DEV * s.shape[0], s.shape[1]) if ax == 0
                                      else (s.shape[0], N_DEV * s.shape[1]), s.dtype)
                 for s, ax in zip(shards, axes)]
    return _Comm(shards, out_shape, [pltpu.SemaphoreType.DMA((7 * n,)), pltpu.SemaphoreType.DMA((7 * n,)),
                                     pltpu.SemaphoreType.DMA((n,))], run, when)


def _all_gather_routed(shards, axes):
    n = len(shards)
    shapes = [s.shape for s in shards]
    assert all(s[0] % 32 == 0 for s in shapes)

    def run(ins, outs, sems, phase):
        send_sems, recv_sems, local_sems = sems
        x, y, c = _mesh_pos()
        me, sibling = (x, y, c), (x, y, 1 - c)
        nx, ny, nd = (1 - x, y), (x, 1 - y), (1 - x, 1 - y)

        def block(t, chip, core):
            return _shard_view(outs[t], 4 * chip[0] + 2 * chip[1] + core, shapes[t], axes[t])

        def half(t, chip, core, h):
            hr = shapes[t][0] // 2
            return block(t, chip, core).at[pl.ds(h * hr, hr), :]

        def own_half(t, h):
            hr = shapes[t][0] // 2
            return ins[t].at[pl.ds(h * hr, hr), :]

        def copy(t, k, src, dst, to):
            return pltpu.make_async_remote_copy(src_ref=src, dst_ref=dst, send_sem=send_sems.at[10 * t + k],
                                                recv_sem=recv_sems.at[10 * t + k], device_id=to, device_id_type=MESH)

        def arrived(t, k, dst):
            copy(t, k, dst, dst, me).wait_recv()

        mine = [pltpu.make_async_copy(ins[t], block(t, (x, y), c), local_sems.at[t]) for t in range(n)]
        own = []
        for t in range(n):
            own += [copy(t, 0, ins[t], block(t, (x, y), c), sibling),
                    copy(t, 1, own_half(t, 0), half(t, (x, y), c, 0), (*nx, c)),
                    copy(t, 2, own_half(t, 1), half(t, (x, y), c, 1), (*nx, c)),
                    copy(t, 3, own_half(t, 1), half(t, (x, y), c, 1), (*ny, c)),
                    copy(t, 4, own_half(t, 0), half(t, (x, y), c, 0), (*ny, c))]
        if phase == 0:
            for cp in mine + own:
                cp.start()
            return
        relays = []

        def relay(t, k, view, to):
            cp = copy(t, k, view, view, to)
            cp.start()
            relays.append(cp)

        for t in range(n):
            arrived(t, 1, half(t, nx, c, 0))
            relay(t, 5, half(t, nx, c, 0), (*ny, c))
            arrived(t, 3, half(t, ny, c, 1))
            relay(t, 6, half(t, ny, c, 1), (*nx, c))
            arrived(t, 2, half(t, nx, c, 1))
            relay(t, 7, block(t, nx, c), sibling)
            arrived(t, 4, half(t, ny, c, 0))
            relay(t, 8, block(t, ny, c), sibling)
            arrived(t, 5, half(t, nd, c, 0))
            arrived(t, 6, half(t, nd, c, 1))
            relay(t, 9, block(t, nd, c), sibling)
        for t in range(n):
            arrived(t, 0, block(t, (x, y), 1 - c))
            arrived(t, 7, block(t, nx, 1 - c))
            arrived(t, 8, block(t, ny, 1 - c))
            arrived(t, 9, block(t, nd, 1 - c))
        for cp in own + relays:
            cp.wait_send()
        for cp in mine:
            cp.wait()

    out_shape = [jax.ShapeDtypeStruct((N_DEV * s.shape[0], s.shape[1]) if ax == 0
                                      else (s.shape[0], N_DEV * s.shape[1]), s.dtype)
                 for s, ax in zip(shards, axes)]
    return _Comm(shards, out_shape, [pltpu.SemaphoreType.DMA((10 * n,)), pltpu.SemaphoreType.DMA((10 * n,)),
                                     pltpu.SemaphoreType.DMA((n,))], run)


def _rs_to_sibling(grads, shapes, axes):
    n = len(grads)

    def run(gs, outs, sems, phase):
        send_sems, recv_sems = sems
        x, y, c = _mesh_pos()
        copies = [pltpu.make_async_remote_copy(
            src_ref=_shard_view(gs[t], 2 * k + (1 - c), shapes[t], axes[t]), dst_ref=outs[t].at[k],
            send_sem=send_sems.at[4 * t + k], recv_sem=recv_sems.at[4 * t + k],
            device_id=(x, y, 1 - c), device_id_type=MESH) for t in range(n) for k in range(4)]
        if phase == 0:
            for cp in copies:
                cp.start()
            return
        for cp in copies:
            cp.wait_recv()
        for cp in copies:
            cp.wait_send()

    return _Comm(grads, [jax.ShapeDtypeStruct((4,) + tuple(s), BF16) for s in shapes],
                 [pltpu.SemaphoreType.DMA((4 * n,)), pltpu.SemaphoreType.DMA((4 * n,))], run)


def _rs_pair_sum(g, recv, cidx, shape, axis, *, name):
    r, c = shape
    tr = _tile(r, max(8, (1 << 21) // c), 16)
    nr = r // tr

    def body(c_ref, g_ref, rv_ref, o_ref):
        o_ref[...] = (g_ref[...].astype(F32) + rv_ref[...].astype(F32)).astype(BF16)

    if axis == 1:
        g_spec = pl.BlockSpec((tr, c), lambda k, i, s: (i, 2 * k + s[0]))
    else:
        g_spec = pl.BlockSpec((tr, c), lambda k, i, s: ((2 * k + s[0]) * nr + i, 0))
    blk = pl.BlockSpec((None, tr, c), lambda k, i, s: (k, i, 0))
    return pl.pallas_call(
        body, name=name,
        grid_spec=pltpu.PrefetchScalarGridSpec(num_scalar_prefetch=1, grid=(4, nr), in_specs=[g_spec, blk],
                                               out_specs=blk),
        out_shape=jax.ShapeDtypeStruct((4, r, c), BF16),
        compiler_params=_params("parallel", "parallel"))(cidx, g, recv)


def _rs_to_chips(parts):
    n = len(parts)

    def run(ps, outs, sems, phase):
        send_sems, recv_sems, local_sems = sems
        x, y, c = _mesh_pos()
        my_chip = 2 * x + y
        peers = [(1 - x, y), (x, 1 - y), (1 - x, 1 - y)]
        local = [pltpu.make_async_copy(ps[t].at[my_chip], outs[t].at[my_chip], local_sems.at[t]) for t in range(n)]
        sends = [pltpu.make_async_remote_copy(
            src_ref=ps[t].at[2 * px + py], dst_ref=outs[t].at[my_chip],
            send_sem=send_sems.at[3 * t + j], recv_sem=recv_sems.at[3 * t + j],
            device_id=(px, py, c), device_id_type=MESH) for t in range(n) for j, (px, py) in enumerate(peers)]
        if phase == 0:
            for cp in local + sends:
                cp.start()
            return
        for t in range(n):
            for j, (px, py) in enumerate(peers):
                pltpu.make_async_remote_copy(
                    src_ref=ps[t].at[2 * px + py], dst_ref=outs[t].at[2 * px + py],
                    send_sem=send_sems.at[3 * t + j], recv_sem=recv_sems.at[3 * t + j],
                    device_id=(x, y, c), device_id_type=MESH).wait_recv()
        for cp in sends:
            cp.wait_send()
        for cp in local:
            cp.wait()

    return _Comm(parts, [jax.ShapeDtypeStruct(p.shape, BF16) for p in parts],
                 [pltpu.SemaphoreType.DMA((3 * n,)), pltpu.SemaphoreType.DMA((3 * n,)),
                  pltpu.SemaphoreType.DMA((n,))], run)


def _exchange_small(buf):
    def run(ins, outs, sems, phase):
        (in_ref,), (slots,) = ins, outs
        send_sems, recv_sems, local_sem = sems
        x, y, c = _mesh_pos()
        me = 4 * x + 2 * y + c
        local = pltpu.make_async_copy(in_ref, slots.at[me], local_sem.at[0])
        flips = [(fx, fy, fc) for fx in (0, 1) for fy in (0, 1) for fc in (0, 1)][1:]
        peers = [(1 - x if fx else x, 1 - y if fy else y, 1 - c if fc else c) for fx, fy, fc in flips]
        sends = [pltpu.make_async_remote_copy(src_ref=in_ref, dst_ref=slots.at[me], send_sem=send_sems.at[k],
                                              recv_sem=recv_sems.at[k], device_id=peer, device_id_type=MESH)
                 for k, peer in enumerate(peers)]
        if phase == 0:
            for cp in [local] + sends:
                cp.start()
            return
        for k, (px, py, pc) in enumerate(peers):
            pltpu.make_async_remote_copy(src_ref=in_ref, dst_ref=slots.at[4 * px + 2 * py + pc],
                                         send_sem=send_sems.at[k], recv_sem=recv_sems.at[k],
                                         device_id=(x, y, c), device_id_type=MESH).wait_recv()
        for cp in sends:
            cp.wait_send()
        local.wait()

    return _Comm([buf], [jax.ShapeDtypeStruct((N_DEV,) + buf.shape, F32)],
                 [pltpu.SemaphoreType.DMA((7,)), pltpu.SemaphoreType.DMA((7,)), pltpu.SemaphoreType.DMA((1,))], run)


def _sum_slots(slots, *, name):
    _, R, C = slots.shape
    tr = _tile(R, 512)

    def body(s_ref, o_ref):
        acc = s_ref[0]
        for p in range(1, N_DEV):
            acc = acc + s_ref[p]
        o_ref[...] = acc

    return pl.pallas_call(
        body, name=name, grid=(R // tr,), in_specs=[pl.BlockSpec((N_DEV, tr, C), lambda i: (0, i, 0))],
        out_specs=pl.BlockSpec((tr, C), lambda i: (i, 0)), out_shape=jax.ShapeDtypeStruct((R, C), F32),
        compiler_params=_params("parallel"))(slots)


def _adam_math(g, w, m, v):
    m_new = ADAM_B1 * m + (1.0 - ADAM_B1) * g
    v_new = ADAM_B2 * v + (1.0 - ADAM_B2) * (g * g)
    m_hat = m_new / ADAM_C1
    v_hat = v_new / ADAM_C2
    delta = -ADAM_LR * (m_hat / (jnp.sqrt(v_hat) + ADAM_EPS) + ADAM_WD * w)
    return delta, m_new, v_new


def _adamw_sharded(q, w, m, v, *, name):
    r, c = w.shape
    tr = _tile(r, max(8, (1 << 19) // c), 16)

    def body(q_ref, w_ref, m_ref, v_ref, g_ref, d_ref, mo_ref, vo_ref):
        g = ((q_ref[0].astype(F32) + q_ref[1].astype(F32)) + q_ref[2].astype(F32)) + q_ref[3].astype(F32)
        g_ref[...] = g
        d_ref[...], mo_ref[...], vo_ref[...] = _adam_math(g, w_ref[...], m_ref[...], v_ref[...])

    blk = pl.BlockSpec((tr, c), lambda i: (i, 0))
    return pl.pallas_call(
        body, name=name, grid=(r // tr,),
        in_specs=[pl.BlockSpec((4, tr, c), lambda i: (0, i, 0)), blk, blk, blk], out_specs=[blk] * 4,
        out_shape=[jax.ShapeDtypeStruct((r, c), F32)] * 4,
        compiler_params=_params("parallel"))(q, w, m, v)


def _adamw_plain(g, w, m, v, *, name):
    r, c = w.shape
    tr = _tile(r, 512)

    def body(g_ref, w_ref, m_ref, v_ref, d_ref, mo_ref, vo_ref):
        d_ref[...], mo_ref[...], vo_ref[...] = _adam_math(g_ref[...], w_ref[...], m_ref[...], v_ref[...])

    blk = pl.BlockSpec((tr, c), lambda i: (i, 0))
    return pl.pallas_call(
        body, name=name, grid=(r // tr,), in_specs=[blk] * 4, out_specs=[blk] * 3,
        out_shape=[jax.ShapeDtypeStruct((r, c), F32)] * 3,
        compiler_params=_params("parallel"))(g, w, m, v)


def _pack_rows(arrays):
    return jnp.concatenate([a.reshape(-1, LANES) for a in arrays], axis=0)


def kernel(x, ffn1_w_gu, ffn1_w_down, ln1_g, ln1_b, w_in, b_in, sgu_ln_g, sgu_ln_b, sgu_w_s, sgu_b_s, w_a_proj, conv_w_dw, conv_b_dw, conv_ln_g, conv_ln_b, w_b_proj, w_out, ln2_g, ln2_b, ffn2_w_gu, ffn2_w_down, ln3_g, ln3_b, loss_target, m_ffn1_w_gu, m_ffn1_w_down, m_ln1_g, m_ln1_b, m_w_in, m_b_in, m_sgu_ln_g, m_sgu_ln_b, m_sgu_w_s, m_sgu_b_s, m_w_a_proj, m_conv_w_dw, m_conv_b_dw, m_conv_ln_g, m_conv_ln_b, m_w_b_proj, m_w_out, m_ln2_g, m_ln2_b, m_ffn2_w_gu, m_ffn2_w_down, m_ln3_g, m_ln3_b, v_ffn1_w_gu, v_ffn1_w_down, v_ln1_g, v_ln1_b, v_w_in, v_b_in, v_sgu_ln_g, v_sgu_ln_b, v_sgu_w_s, v_sgu_b_s, v_w_a_proj, v_conv_w_dw, v_conv_b_dw, v_conv_ln_g, v_conv_ln_b, v_w_b_proj, v_w_out, v_ln2_g, v_ln2_b, v_ffn2_w_gu, v_ffn2_w_down, v_ln3_g, v_ln3_b):
    given = dict(locals())
    w = {n: given[n][0] for n in WEIGHTS}
    mom = {n: given["m_" + n][0] for n in WEIGHTS}
    var = {n: given["v_" + n][0] for n in WEIGHTS}
    xt = x[0]
    target = loss_target[0]
    T, D = xt.shape
    A = w['w_a_proj'].shape[0]

    big_names = list(BIG)
    early = ['ffn1_w_gu', 'ffn1_w_down']
    later = ['ffn2_w_gu']
    late = [n for n in big_names if n not in early + later]
    conv_w_pad = jnp.pad(w['conv_w_dw'], ((0, CONV_WPAD - CONV_WIDTH), (0, 0)))
    w_bf = {n: w[n].astype(BF16) for n in big_names}
    full = dict(zip(early, _comm_call(_all_gather_routed([w_bf[n] for n in early], [BIG[n] for n in early]),
                                      name="all_gather_ffn1")))
    gather_late = _all_gather([w_bf[n] for n in late] + [conv_w_pad], [BIG[n] for n in late] + [1],
                              progressive=True, busy=0.65)
    gather_later = _all_gather([w_bf[n] for n in later], [BIG[n] for n in later], progressive=True, busy=0.8)

    def row(v):
        return v.reshape(1, -1)

    ones = jnp.ones((1, D), F32)
    zeros = jnp.zeros((1, D), F32)
    w_s = w['sgu_w_s']
    w_st = jnp.swapaxes(w_s, 1, 2)
    b_sb = jnp.broadcast_to(w['sgu_b_s'][:, :, None], w_s.shape)

    (gate1, up1, xb0, xh1, rstd1), gathered = _ffn_fwd(xt, ones, zeros, full['ffn1_w_gu'], full['ffn1_w_down'],
                                                  affine=False, name="ffn1_fwd", carry=gather_late)
    full.update(zip(late, gathered[:-1]))
    conv_w_full = gathered[-1]
    g1, b1 = row(w['ln1_g']), row(w['ln1_b'])
    (proj, xb1), gathered = _inproj_fwd(xh1, g1, b1, full['w_in'], row(w['b_in']), name="inproj_fwd",
                                        carry=gather_later)
    full.update(zip(later, gathered))
    sg = _sgu_fwd(proj, row(w['sgu_ln_g']), row(w['sgu_ln_b']), w_s, b_sb, name="sgu_fwd")
    conv_out, cv = _conv_fwd(proj, conv_w_full, row(w['conv_b_dw']), row(w['conv_ln_g']), row(w['conv_ln_b']),
                             name="conv_fwd")
    ya, yb, mixed = _mix_fwd_gate(sg, cv, proj, full['w_a_proj'], full['w_b_proj'], name="mix_fwd_gate")
    xh2, rstd2 = _mix_fwd_out(mixed, full['w_out'], xh1, g1, b1, name="mix_fwd_out")
    g2, b2 = row(w['ln2_g']), row(w['ln2_b'])
    gate2, up2, xb2, dr3, loss_part, d_ln3_g, d_ln3_b = _ffn_fwd(
        xh2, g2, b2, full['ffn2_w_gu'], full['ffn2_w_down'], affine=True, name="ffn2_fwd_loss",
        final=(row(w['ln3_g']), row(w['ln3_b']), target))

    F = full['ffn2_w_down'].shape[0]
    h2, dgate2, dup2, dr2, d_ln2_g, d_ln2_b = _ffn_bwd(dr3, gate2, up2, full['ffn2_w_gu'], full['ffn2_w_down'],
                                                      name="ffn2_bwd", prev=(xh2, rstd2, g2))
    G, P, Q = {}, {}, {}
    cidx = lax.axis_index("c").astype(jnp.int32).reshape(1)

    def to_sibling(names):
        return _rs_to_sibling([G[n] for n in names], [w[n].shape for n in names], [BIG[n] for n in names])

    def pair_sum(names, received):
        for n, rv in zip(names, received):
            P[n] = _rs_pair_sum(G[n], rv, cidx, w[n].shape, BIG[n], name="rs_pair_sum_" + n)

    def to_chips(names):
        return _rs_to_chips([P[n] for n in names])

    G['ffn2_w_down'] = _mm_tn(h2, dr3, name="dw_ffn2_down", tm_pref=1408, tn_pref=2048, scale=0.5)
    gu, rv = _mm_tn(xb2, dgate2, name="dw_ffn2_gate", tm_pref=2048, tn_pref=1408,
                    n_total=2 * F, carry=to_sibling(['ffn2_w_down']))
    pair_sum(['ffn2_w_down'], rv)
    G['ffn2_w_gu'], q = _mm_tn(xb2, dup2, name="dw_ffn2_up", tm_pref=2048, tn_pref=1408,
                               into=gu, col_off=F, n_total=2 * F, carry=to_chips(['ffn2_w_down']))
    Q['ffn2_w_down'] = q[0]

    (dya, dyb, dproj, dbin_gate), rv = _mix_bwd_gate(dr2, full['w_out'], proj, ya, yb, name="mix_bwd_gate",
                                                     carry=to_sibling(['ffn2_w_gu']))
    pair_sum(['ffn2_w_gu'], rv)
    dsg, dcv = _mix_bwd_proj(dya, dyb, full['w_a_proj'], full['w_b_proj'], name="mix_bwd_proj")
    dproj, dbin_sgu, d_sgu_ln_g, d_sgu_ln_b, d_w_s, d_b_s = _sgu_bwd(
        proj, dsg, row(w['sgu_ln_g']), row(w['sgu_ln_b']), w_s, w_st, b_sb, dproj, name="sgu_bwd")
    dconv, d_conv_ln_g, d_conv_ln_b, d_conv_b = _conv_bwd_ln(dcv, conv_out, row(w['conv_ln_g']),
                                                            row(w['conv_ln_b']), name="conv_bwd_ln")
    (dproj, dbin_conv, d_conv_w), q = _conv_bwd(proj, dconv, conv_w_full, dproj, name="conv_bwd",
                                                carry=to_chips(['ffn2_w_gu']))
    Q['ffn2_w_gu'] = q[0]

    mid = ['w_out', 'w_a_proj', 'w_b_proj']
    G['w_out'] = _mm_tn(mixed, dr2, name="dw_out", tm_pref=2048, tn_pref=1024)
    G['w_a_proj'] = _mm_tn(sg, dya, name="dw_a_proj", tm_pref=1024, tn_pref=2048)
    G['w_b_proj'] = _mm_tn(cv, dyb, name="dw_b_proj", tm_pref=1024, tn_pref=2048)
    G['w_in'], rv = _mm_tn(xb1, dproj, name="dw_in", tm_pref=2048, tn_pref=1024, tk_pref=2048,
                           carry=to_sibling(mid))
    pair_sum(mid, rv)
    both = _join(to_chips(mid), to_sibling(['w_in']))
    (dr1, d_ln1_g, d_ln1_b), moved = _inproj_bwd(dproj, full['w_in'], dr2, xh1, rstd1, g1, name="inproj_bwd",
                                                 carry=both)
    q, rv = both.split(moved)
    Q.update(zip(mid, q))
    pair_sum(['w_in'], rv)

    h1, dgate1, dup1, grad_x = _ffn_bwd(dr1, gate1, up1, full['ffn1_w_gu'], full['ffn1_w_down'], name="ffn1_bwd")
    G['ffn1_w_down'], q = _mm_tn(h1, dr1, name="dw_ffn1_down", tm_pref=1408, tn_pref=2048, scale=0.5,
                                 carry=to_chips(['w_in']))
    Q['w_in'] = q[0]
    small_g = {'ln1_g': d_ln1_g, 'ln1_b': d_ln1_b,
               'b_in': jnp.concatenate([dbin_sgu, dbin_conv, dbin_gate], axis=1),
               'sgu_ln_g': d_sgu_ln_g, 'sgu_ln_b': d_sgu_ln_b, 'sgu_w_s': d_w_s, 'sgu_b_s': d_b_s,
               'conv_b_dw': d_conv_b, 'conv_ln_g': d_conv_ln_g, 'conv_ln_b': d_conv_ln_b,
               'ln2_g': d_ln2_g, 'ln2_b': d_ln2_b, 'ln3_g': d_ln3_g, 'ln3_b': d_ln3_b}
    packed = _pack_rows([small_g[n] for n in SMALL] + [d_conv_w, loss_part])
    both = _join(to_sibling(['ffn1_w_down']), _exchange_small(packed))
    gu, moved = _mm_tn(xb0, dgate1, name="dw_ffn1_gate", tm_pref=2048, tn_pref=1408, n_total=2 * F, carry=both)
    rv, slots = both.split(moved)
    pair_sum(['ffn1_w_down'], rv)
    reduced = _sum_slots(slots[0], name="sum_small")
    G['ffn1_w_gu'], q = _mm_tn(xb0, dup1, name="dw_ffn1_up", tm_pref=2048, tn_pref=1408, into=gu, col_off=F,
                               n_total=2 * F, carry=to_chips(['ffn1_w_down']))
    Q['ffn1_w_down'] = q[0]
    pair_sum(['ffn1_w_gu'], _comm_call(to_sibling(['ffn1_w_gu']), name="rs_to_sibling_last"))
    Q['ffn1_w_gu'] = _comm_call(to_chips(['ffn1_w_gu']), name="rs_to_chips_last")[0]

    grads, deltas, new_m, new_v = {}, {}, {}, {}
    for n in big_names:
        grads[n], deltas[n], new_m[n], new_v[n] = _adamw_sharded(Q[n], w[n], mom[n], var[n], name="adamw_" + n)

    B = conv_w_full.shape[1]
    n_small_rows = sum(w[n].size for n in SMALL) // LANES
    conv_rows = CONV_WIDTH * B // LANES
    d_small, m_small, v_small = _adamw_plain(
        reduced[:n_small_rows], _pack_rows([w[n] for n in SMALL]), _pack_rows([mom[n] for n in SMALL]),
        _pack_rows([var[n] for n in SMALL]), name="adamw_small")
    off = 0
    for n in SMALL:
        rows = w[n].size // LANES
        grads[n] = reduced[off:off + rows].reshape(w[n].shape)
        deltas[n] = d_small[off:off + rows].reshape(w[n].shape)
        new_m[n] = m_small[off:off + rows].reshape(w[n].shape)
        new_v[n] = v_small[off:off + rows].reshape(w[n].shape)
        off += rows
    conv_g_full = reduced[off:off + conv_rows].reshape(CONV_WIDTH, B)
    bs = w['conv_w_dw'].shape[1]
    my_block = 4 * lax.axis_index("x") + 2 * lax.axis_index("y") + lax.axis_index("c")
    grads['conv_w_dw'] = lax.dynamic_slice(conv_g_full, (0, my_block * bs), (CONV_WIDTH, bs))
    deltas['conv_w_dw'], new_m['conv_w_dw'], new_v['conv_w_dw'] = _adamw_plain(
        grads['conv_w_dw'], w['conv_w_dw'], mom['conv_w_dw'], var['conv_w_dw'], name="adamw_conv_w")
    loss = reduced[off + conv_rows, 0]

    def lead(a):
        return a[None]

    return (loss, grad_x[None], *[lead(grads[n]) for n in WEIGHTS], *[lead(deltas[n]) for n in WEIGHTS],
            *[lead(new_m[n]) for n in WEIGHTS], *[lead(new_v[n]) for n in WEIGHTS])
```

```python
import functools
import math

import jax
import jax.numpy as jnp
from jax import lax
from jax.experimental import pallas as pl
from jax.experimental.pallas import tpu as pltpu

F32 = jnp.float32
BF16 = jnp.bfloat16

ALPHA = 2.0 ** 0.25
LN_EPS = 1e-5
CONV_WIDTH = 31
CONV_HALO = 32
CONV_ROWS = 64
CONV_WPAD = 32
CHUNK = 64
GMLP_BLOCK = 128
A_GROUPS = 8
N_DEV = 8
LANES = 128

ADAM_LR = 0.001
ADAM_B1 = 0.9
ADAM_B2 = 0.999
ADAM_EPS = 1e-08
ADAM_WD = 0.01
ADAM_STEP = 10
ADAM_C1 = 1.0 - ADAM_B1 ** ADAM_STEP
ADAM_C2 = 1.0 - ADAM_B2 ** ADAM_STEP

VMEM_LIMIT_BYTES = 60 * 2 ** 20
MESH = pl.DeviceIdType.MESH
ANY = pl.BlockSpec(memory_space=pl.ANY)

WEIGHTS = ['ffn1_w_gu', 'ffn1_w_down', 'ln1_g', 'ln1_b', 'w_in', 'b_in', 'sgu_ln_g', 'sgu_ln_b', 'sgu_w_s',
           'sgu_b_s', 'w_a_proj', 'conv_w_dw', 'conv_b_dw', 'conv_ln_g', 'conv_ln_b', 'w_b_proj', 'w_out',
           'ln2_g', 'ln2_b', 'ffn2_w_gu', 'ffn2_w_down', 'ln3_g', 'ln3_b']
BIG = {'ffn1_w_gu': 1, 'ffn1_w_down': 0, 'w_in': 1, 'w_a_proj': 1, 'w_b_proj': 1, 'w_out': 0,
       'ffn2_w_gu': 1, 'ffn2_w_down': 0}
SMALL = [n for n in WEIGHTS if n not in BIG and n != 'conv_w_dw']


def _tile(n, pref, mult=8):
    best = None
    for d in range(mult, min(n, pref) + 1, mult):
        if n % d == 0:
            best = d
    return n if best is None else best


def _params(*sem):
    return pltpu.CompilerParams(dimension_semantics=sem, vmem_limit_bytes=VMEM_LIMIT_BYTES)


def _dot(a, b):
    return jnp.dot(a, b, preferred_element_type=F32)


def _dot_nt(a, b):
    return lax.dot_general(a, b, (((1,), (1,)), ((), ())), preferred_element_type=F32)


def _dot_tn(a, b):
    return lax.dot_general(a, b, (((0,), (0,)), ((), ())), preferred_element_type=F32)


def _sig(x):
    return 1.0 / (1.0 + jnp.exp(-x))


_GELU_K = math.sqrt(2.0 / math.pi)
_GELU_C = 0.044715


def _gelu(x):
    t = jnp.tanh(_GELU_K * (x + _GELU_C * x * x * x))
    return 0.5 * x * (1.0 + t)


def _gelu_grad(x):
    x2 = x * x
    t = jnp.tanh(_GELU_K * (x + _GELU_C * x2 * x))
    return 0.5 * (1.0 + t) + 0.5 * x * (1.0 - t * t) * (_GELU_K * (1.0 + 3.0 * _GELU_C * x2))


def _ln_stats(r):
    mu = jnp.mean(r, axis=-1, keepdims=True)
    rc = r - mu
    var = jnp.mean(rc * rc, axis=-1, keepdims=True)
    rstd = lax.rsqrt(var + LN_EPS)
    return rc * rstd, rstd


def _ln_bwd(dy, xh, rstd, g):
    dxh = dy * g
    m1 = jnp.mean(dxh, axis=-1, keepdims=True)
    m2 = jnp.mean(dxh * xh, axis=-1, keepdims=True)
    return rstd * (dxh - m1 - xh * m2)


def _colsum(v):
    return jnp.sum(v, axis=0, keepdims=True)


def _chunk_mask(transposed):
    shift = CHUNK.bit_length() - 1
    r = lax.broadcasted_iota(jnp.int32, (GMLP_BLOCK, GMLP_BLOCK), 0) >> shift
    c = lax.broadcasted_iota(jnp.int32, (GMLP_BLOCK, GMLP_BLOCK), 1) >> shift
    return (r <= c) if transposed else (c <= r)


class _Comm:
    def __init__(self, inputs, out_shape, scratch, run, when=(0.0, 1.0)):
        self.inputs, self.out_shape, self.scratch, self.run = list(inputs), list(out_shape), list(scratch), run
        self.when = tuple(when)
        self.parts = [len(self.out_shape)]

    def split(self, outs):
        res, o = [], 0
        for n in self.parts:
            res.append(list(outs[o:o + n]))
            o += n
        return res


def _join(*comms):
    comms = [c for c in comms if c is not None]
    if not comms:
        return None
    assert all(c.when == (0.0, 1.0) for c in comms)

    def run(ins, outs, sems, phase):
        i = o = s = 0
        for c in comms:
            c.run(ins[i:i + len(c.inputs)], outs[o:o + len(c.out_shape)], sems[s:s + len(c.scratch)], phase)
            i, o, s = i + len(c.inputs), o + len(c.out_shape), s + len(c.scratch)

    joined = _Comm(sum((c.inputs for c in comms), []), sum((c.out_shape for c in comms), []),
                   sum((c.scratch for c in comms), []), run)
    joined.parts = [len(c.out_shape) for c in comms]
    return joined


def _call(body, *, name, grid, in_specs, out_specs, out_shape, args, sem, scratch_shapes=(), aliases=None, carry=None):
    in_specs, out_specs, out_shape = list(in_specs), list(out_specs), list(out_shape)
    scratch_shapes = list(scratch_shapes)
    if carry is None:
        return pl.pallas_call(body, name=name, grid=grid, in_specs=in_specs, out_specs=out_specs,
                              out_shape=out_shape, scratch_shapes=scratch_shapes,
                              input_output_aliases=aliases or {}, compiler_params=_params(*sem))(*args)
    n_in, n_out, n_scr = len(args), len(out_shape), len(scratch_shapes)
    c_in, c_out = len(carry.inputs), len(carry.out_shape)
    n_steps = math.prod(grid)
    at_step = [int(round(f * (n_steps - 1))) for f in carry.when]
    assert at_step[0] == 0 and at_step[-1] == n_steps - 1 and at_step == sorted(at_step)

    def wrapped(*refs):
        ins, c_ins = refs[:n_in], refs[n_in:n_in + c_in]
        o0 = n_in + c_in
        outs, c_outs = refs[o0:o0 + n_out], refs[o0 + n_out:o0 + n_out + c_out]
        s0 = o0 + n_out + c_out
        scr, c_sems = refs[s0:s0 + n_scr], refs[s0 + n_scr:]
        step = 0
        for a, g in enumerate(grid):
            step = step * g + pl.program_id(a)
        pl.when(step == 0)(functools.partial(carry.run, c_ins, c_outs, c_sems, 0))
        body(*ins, *outs, *scr)
        for k in range(1, len(at_step)):
            pl.when(step == at_step[k])(functools.partial(carry.run, c_ins, c_outs, c_sems, k))

    res = pl.pallas_call(
        wrapped, name=name, grid=grid, in_specs=in_specs + [ANY] * c_in, out_specs=out_specs + [ANY] * c_out,
        out_shape=out_shape + carry.out_shape, scratch_shapes=scratch_shapes + carry.scratch,
        input_output_aliases=aliases or {},
        compiler_params=pltpu.CompilerParams(dimension_semantics=("arbitrary",) * len(grid),
                                             vmem_limit_bytes=VMEM_LIMIT_BYTES, has_side_effects=True),
    )(*args, *carry.inputs)
    return list(res[:n_out]), list(res[n_out:])


def _comm_call(comm, *, name):
    n_in, n_out = len(comm.inputs), len(comm.out_shape)

    def body(*refs):
        ins, outs, sems = refs[:n_in], refs[n_in:n_in + n_out], refs[n_in + n_out:]
        for k in range(len(comm.when)):
            comm.run(ins, outs, sems, k)

    return list(pl.pallas_call(
        body, name=name, in_specs=[ANY] * n_in, out_specs=[ANY] * n_out, out_shape=comm.out_shape,
        scratch_shapes=comm.scratch, compiler_params=pltpu.CompilerParams(has_side_effects=True))(*comm.inputs))


def _row_chunks(tm, rows=128):
    rows = _tile(tm, rows)
    return [slice(r, r + rows) for r in range(0, tm, rows)]


def _ffn_fwd(xh, lg, lb, wgu, wd, *, affine, name, final=None, carry=None):
    T, D = xh.shape
    F = wd.shape[0]
    tm = _tile(T, 512)
    tn = _tile(F, 512, LANES)
    nj = F // tn
    is_final = final is not None
    n_x = 3 if affine else 1

    def body(*refs):
        if is_final:
            (wg_ref, wu_ref, wd_ref, ng_ref, nb_ref, tgt_ref,
             gate_ref, up_ref, xb_sc, dr_ref, loss_ref, dng_ref, dnb_ref, acc_sc) = refs[n_x:]
        else:
            (wg_ref, wu_ref, wd_ref,
             gate_ref, up_ref, xb_sc, xho_ref, rstd_ref, acc_sc) = refs[n_x:]
        xh_ref = refs[0]
        lg_ref, lb_ref = refs[1:n_x] if affine else (None, None)
        i = pl.program_id(0)
        j = pl.program_id(1)

        def xin():
            v = xh_ref[...]
            return v * lg_ref[...] + lb_ref[...] if affine else v

        @pl.when(j == 0)
        def _():
            xb_sc[...] = xin().astype(BF16)
            acc_sc[...] = jnp.zeros_like(acc_sc)

        xb = xb_sc[...]
        g = _dot(xb, wg_ref[...])
        u = _dot(xb, wu_ref[...])
        gate_ref[...] = g.astype(BF16)
        up_ref[...] = u.astype(BF16)
        h = g * _sig(g) * u
        acc_sc[...] += _dot(h.astype(BF16), wd_ref[...])

        @pl.when(j == nj - 1)
        def _():
            if is_final:
                @pl.when(i == 0)
                def _():
                    loss_ref[...] = jnp.zeros_like(loss_ref)
                    dng_ref[...] = jnp.zeros_like(dng_ref)
                    dnb_ref[...] = jnp.zeros_like(dnb_ref)
            for rs in _row_chunks(tm):
                v = xh_ref[rs, :]
                if affine:
                    v = v * lg_ref[...] + lb_ref[...]
                r = ALPHA * v + 0.5 * acc_sc[rs, :]
                xho, rstd = _ln_stats(r)
                if not is_final:
                    xho_ref[rs, :] = xho
                    rstd_ref[rs, :] = jnp.broadcast_to(rstd, (rs.stop - rs.start, LANES))
                else:
                    ng = ng_ref[...]
                    e = xho * ng + nb_ref[...] - tgt_ref[rs, :]
                    part = _colsum(jnp.sum(e * e, axis=1, keepdims=True)) * (0.5 / D)
                    loss_ref[...] += jnp.broadcast_to(part, loss_ref.shape)
                    dy = e * (1.0 / D)
                    dng_ref[...] += _colsum(dy * xho)
                    dnb_ref[...] += _colsum(dy)
                    dr_ref[rs, :] = _ln_bwd(dy, xho, rstd, ng)

    row = pl.BlockSpec((tm, D), lambda i, j: (i, 0))
    vec = pl.BlockSpec((1, D), lambda i, j: (0, 0))
    hid = pl.BlockSpec((tm, tn), lambda i, j: (i, j))
    in_specs = [row, vec, vec][:n_x] + [
        pl.BlockSpec((D, tn), lambda i, j: (0, j)),
        pl.BlockSpec((D, tn), lambda i, j: (0, j + nj)),
        pl.BlockSpec((tn, D), lambda i, j: (j, 0))]
    args = [xh, lg, lb][:n_x] + [wgu, wgu, wd]
    out_shape = [jax.ShapeDtypeStruct((T, F), BF16), jax.ShapeDtypeStruct((T, F), BF16),
                 jax.ShapeDtypeStruct((T, D), BF16)]
    out_specs = [hid, hid, row]
    if is_final:
        in_specs += [vec, vec, row]
        args += list(final)
        out_shape += [jax.ShapeDtypeStruct((T, D), F32), jax.ShapeDtypeStruct((8, LANES), F32),
                      jax.ShapeDtypeStruct((1, D), F32), jax.ShapeDtypeStruct((1, D), F32)]
        out_specs += [row, pl.BlockSpec((8, LANES), lambda i, j: (0, 0)), vec, vec]
        sem = ("arbitrary", "arbitrary")
    else:
        out_shape += [jax.ShapeDtypeStruct((T, D), F32), jax.ShapeDtypeStruct((T, LANES), F32)]
        out_specs += [row, pl.BlockSpec((tm, LANES), lambda i, j: (i, 0))]
        sem = ("parallel", "arbitrary")
    return _call(body, name=name, grid=(T // tm, nj), in_specs=in_specs, out_specs=out_specs, out_shape=out_shape,
                 scratch_shapes=[pltpu.VMEM((tm, D), F32)], sem=sem, args=args, carry=carry)


def _ffn_bwd(dr, gate, up, wgu, wd, *, name, prev=None, carry=None):
    T, D = dr.shape
    F = wd.shape[0]
    tm = _tile(T, 512)
    tn = _tile(F, 512, LANES)
    nj = F // tn
    has_prev = prev is not None

    def body(*refs):
        if has_prev:
            (dr_ref, gate_ref, up_ref, wd_ref, wg_ref, wu_ref, xh_ref, rstd_ref, lg_ref,
             h_ref, dg_ref, du_ref, dprev_ref, dlg_ref, dlb_ref, df_sc, dx_sc) = refs
        else:
            (dr_ref, gate_ref, up_ref, wd_ref, wg_ref, wu_ref,
             h_ref, dg_ref, du_ref, dprev_ref, df_sc, dx_sc) = refs
        i = pl.program_id(0)
        j = pl.program_id(1)

        @pl.when(j == 0)
        def _():
            d = dr_ref[...]
            df_sc[...] = (0.5 * d).astype(BF16)
            dx_sc[...] = ALPHA * d

        g = gate_ref[...].astype(F32)
        u = up_ref[...].astype(F32)
        dh = _dot_nt(df_sc[...], wd_ref[...])
        s = _sig(g)
        sil = g * s
        h_ref[...] = (sil * u).astype(BF16)
        dg = (dh * u * (s * (1.0 + g * (1.0 - s)))).astype(BF16)
        du = (dh * sil).astype(BF16)
        dg_ref[...] = dg
        du_ref[...] = du
        dx_sc[...] += _dot_nt(dg, wg_ref[...]) + _dot_nt(du, wu_ref[...])

        @pl.when(j == nj - 1)
        def _():
            dxin = dx_sc[...]
            if not has_prev:
                dprev_ref[...] = dxin
            else:
                @pl.when(i == 0)
                def _():
                    dlg_ref[...] = jnp.zeros_like(dlg_ref)
                    dlb_ref[...] = jnp.zeros_like(dlb_ref)
                xh = xh_ref[...]
                dlg_ref[...] += _colsum(dxin * xh)
                dlb_ref[...] += _colsum(dxin)
                dprev_ref[...] = _ln_bwd(dxin, xh, rstd_ref[:, 0:1], lg_ref[...])

    row = pl.BlockSpec((tm, D), lambda i, j: (i, 0))
    vec = pl.BlockSpec((1, D), lambda i, j: (0, 0))
    hid = pl.BlockSpec((tm, tn), lambda i, j: (i, j))
    in_specs = [row, hid, hid,
                pl.BlockSpec((tn, D), lambda i, j: (j, 0)),
                pl.BlockSpec((D, tn), lambda i, j: (0, j)),
                pl.BlockSpec((D, tn), lambda i, j: (0, j + nj))]
    args = [dr, gate, up, wd, wgu, wgu]
    out_shape = [jax.ShapeDtypeStruct((T, F), BF16)] * 3 + [jax.ShapeDtypeStruct((T, D), F32)]
    out_specs = [hid, hid, hid, row]
    if has_prev:
        in_specs += [row, pl.BlockSpec((tm, LANES), lambda i, j: (i, 0)), vec]
        args += list(prev)
        out_shape += [jax.ShapeDtypeStruct((1, D), F32)] * 2
        out_specs += [vec, vec]
        sem = ("arbitrary", "arbitrary")
    else:
        sem = ("parallel", "arbitrary")
    return _call(body, name=name, grid=(T // tm, nj), in_specs=in_specs, out_specs=out_specs, out_shape=out_shape,
                 scratch_shapes=[pltpu.VMEM((tm, D), BF16), pltpu.VMEM((tm, D), F32)], sem=sem, args=args,
                 carry=carry)


def _mm_tn(a, b, *, name, tm_pref, tn_pref, tk_pref=1024, scale=1.0, a_affine=None, into=None, col_off=0,
           n_total=None, carry=None):
    T, M = a.shape
    N = b.shape[1]
    n_total = N if n_total is None else n_total
    tM = _tile(M, tm_pref, LANES)
    tN = _tile(N, tn_pref, LANES)
    tk = _tile(T, tk_pref)
    nt = T // tk
    assert col_off % tN == 0
    off_blocks = col_off // tN
    has_aff = a_affine is not None
    has_into = into is not None

    def body(*refs):
        refs = list(refs)
        a_ref = refs.pop(0)
        if has_aff:
            lg_ref = refs.pop(0)
            lb_ref = refs.pop(0)
        b_ref = refs.pop(0)
        if has_into:
            refs.pop(0)
        o_ref, acc_sc = refs
        t = pl.program_id(2)

        @pl.when(t == 0)
        def _():
            acc_sc[...] = jnp.zeros_like(acc_sc)

        av = a_ref[...]
        if has_aff:
            av = av * lg_ref[...] + lb_ref[...]
        acc_sc[...] += _dot_tn(av.astype(BF16), b_ref[...].astype(BF16))

        @pl.when(t == nt - 1)
        def _():
            o_ref[...] = (acc_sc[...] * scale).astype(BF16)

    in_specs = [pl.BlockSpec((tk, tM), lambda m, n, t: (t, m))]
    args = [a]
    if has_aff:
        in_specs += [pl.BlockSpec((1, tM), lambda m, n, t: (0, m))] * 2
        args += list(a_affine)
    in_specs.append(pl.BlockSpec((tk, tN), lambda m, n, t: (t, n)))
    args.append(b)
    aliases = {}
    if has_into:
        aliases = {len(args): 0}
        in_specs.append(ANY)
        args.append(into)
    res = _call(body, name=name, grid=(M // tM, N // tN, nt), in_specs=in_specs,
                out_specs=[pl.BlockSpec((tM, tN), lambda m, n, t: (m, n + off_blocks))],
                out_shape=[jax.ShapeDtypeStruct((M, n_total), BF16)],
                scratch_shapes=[pltpu.VMEM((tM, tN), F32)], aliases=aliases,
                sem=("parallel", "parallel", "arbitrary"), args=args, carry=carry)
    return res[0] if carry is None else (res[0][0], res[1])


def _inproj_fwd(xh, lg, lb, w, bias, *, name, carry=None):
    T, D = xh.shape
    N = w.shape[1]
    tm = _tile(T, 1024)
    tn = _tile(N, 1024, LANES)

    def body(xh_ref, lg_ref, lb_ref, w_ref, b_ref, o_ref, xb_ref):
        @pl.when(pl.program_id(1) == 0)
        def _():
            xb_ref[...] = (xh_ref[...] * lg_ref[...] + lb_ref[...]).astype(BF16)
        o_ref[...] = _dot(xb_ref[...], w_ref[...]) + b_ref[...]

    return _call(
        body, name=name, grid=(T // tm, N // tn),
        in_specs=[pl.BlockSpec((tm, D), lambda i, j: (i, 0)),
                  pl.BlockSpec((1, D), lambda i, j: (0, 0)), pl.BlockSpec((1, D), lambda i, j: (0, 0)),
                  pl.BlockSpec((D, tn), lambda i, j: (0, j)), pl.BlockSpec((1, tn), lambda i, j: (0, j))],
        out_specs=[pl.BlockSpec((tm, tn), lambda i, j: (i, j)), pl.BlockSpec((tm, D), lambda i, j: (i, 0))],
        out_shape=[jax.ShapeDtypeStruct((T, N), F32), jax.ShapeDtypeStruct((T, D), BF16)],
        sem=("parallel", "arbitrary"), args=[xh, lg, lb, w, bias], carry=carry)


def _inproj_bwd(dproj, w, dr_next, xh, rstd, lg, *, name, carry=None):
    T, N = dproj.shape
    D = w.shape[0]
    tm = _tile(T, 512)
    tn = _tile(N, 2048, LANES)
    nj = N // tn

    def body(dp_ref, w_ref, drn_ref, xh_ref, rstd_ref, lg_ref, dprev_ref, dlg_ref, dlb_ref, dx_sc):
        i = pl.program_id(0)
        j = pl.program_id(1)

        @pl.when(j == 0)
        def _():
            for rs in _row_chunks(tm):
                dx_sc[rs, :] = ALPHA * drn_ref[rs, :]

        dx_sc[...] += _dot_nt(dp_ref[...], w_ref[...])

        @pl.when(j == nj - 1)
        def _():
            @pl.when(i == 0)
            def _():
                dlg_ref[...] = jnp.zeros_like(dlg_ref)
                dlb_ref[...] = jnp.zeros_like(dlb_ref)
            for rs in _row_chunks(tm):
                dx = dx_sc[rs, :]
                x_hat = xh_ref[rs, :]
                dlg_ref[...] += _colsum(dx * x_hat)
                dlb_ref[...] += _colsum(dx)
                dprev_ref[rs, :] = _ln_bwd(dx, x_hat, rstd_ref[rs, 0:1], lg_ref[...])

    row = pl.BlockSpec((tm, D), lambda i, j: (i, 0))
    vec = pl.BlockSpec((1, D), lambda i, j: (0, 0))
    return _call(
        body, name=name, grid=(T // tm, nj),
        in_specs=[pl.BlockSpec((tm, tn), lambda i, j: (i, j)), pl.BlockSpec((D, tn), lambda i, j: (0, j)),
                  row, row, pl.BlockSpec((tm, LANES), lambda i, j: (i, 0)), vec],
        out_specs=[row, vec, vec],
        out_shape=[jax.ShapeDtypeStruct((T, D), F32), jax.ShapeDtypeStruct((1, D), F32),
                   jax.ShapeDtypeStruct((1, D), F32)],
        scratch_shapes=[pltpu.VMEM((tm, D), F32)], sem=("arbitrary", "arbitrary"),
        args=[dproj, w, dr_next, xh, rstd, lg], carry=carry)


def _sgu_fwd(proj, ln_g, ln_b, w_s, b_sb, *, name):
    T = proj.shape[0]
    A = proj.shape[1] // 8
    hd = A // A_GROUPS
    tm = _tile(T, 512, GMLP_BLOCK)

    def body(u_ref, v_ref, g_ref, b_ref, ws_ref, bs_ref, o_ref):
        gu = _gelu(u_ref[...])
        vh, _ = _ln_stats(_gelu(v_ref[...]))
        vn = (vh * g_ref[...] + b_ref[...]).astype(BF16)
        mask = _chunk_mask(False)
        for h in range(A_GROUPS):
            wm = jnp.where(mask, ws_ref[h], 0.0).astype(BF16)
            cols = slice(h * hd, (h + 1) * hd)
            for n in range(tm // GMLP_BLOCK):
                rows = slice(n * GMLP_BLOCK, (n + 1) * GMLP_BLOCK)
                s = _dot(wm, vn[rows, cols]) + bs_ref[h][:, :hd]
                o_ref[rows, cols] = (gu[rows, cols] * s).astype(BF16)

    vec = pl.BlockSpec((1, A), lambda i: (0, 0))
    full = pl.BlockSpec((A_GROUPS, GMLP_BLOCK, GMLP_BLOCK), lambda i: (0, 0, 0))
    return pl.pallas_call(
        body, name=name, grid=(T // tm,),
        in_specs=[pl.BlockSpec((tm, A), lambda i: (i, 0)), pl.BlockSpec((tm, A), lambda i: (i, 1)),
                  vec, vec, full, full],
        out_specs=pl.BlockSpec((tm, A), lambda i: (i, 0)),
        out_shape=jax.ShapeDtypeStruct((T, A), BF16),
        compiler_params=_params("parallel"))(proj, proj, ln_g, ln_b, w_s, b_sb)


def _sgu_bwd(proj, dsg, ln_g, ln_b, w_s, w_st, b_sb, dproj, *, name):
    T = proj.shape[0]
    A = proj.shape[1] // 8
    hd = A // A_GROUPS
    tm = _tile(T, 512, GMLP_BLOCK)
    nt = T // tm

    def body(u_ref, v_ref, dsg_ref, g_ref, b_ref, ws_ref, wst_ref, bs_ref, _alias,
             dp_ref, dbin_ref, dlg_ref, dlb_ref, dws_ref, dbs_ref, dvn_sc, dgu_sc, dbs_sc):
        i = pl.program_id(0)

        @pl.when(i == 0)
        def _():
            dbin_ref[...] = jnp.zeros_like(dbin_ref)
            dlg_ref[...] = jnp.zeros_like(dlg_ref)
            dlb_ref[...] = jnp.zeros_like(dlb_ref)
            dws_ref[...] = jnp.zeros_like(dws_ref)
            dbs_sc[...] = jnp.zeros_like(dbs_sc)

        u = u_ref[...]
        v = v_ref[...]
        gu = _gelu(u)
        vh, rstd = _ln_stats(_gelu(v))
        gain = g_ref[...]
        vn = (vh * gain + b_ref[...]).astype(BF16)
        dsg_v = dsg_ref[...]
        mask = _chunk_mask(False)
        mask_t = _chunk_mask(True)
        for h in range(A_GROUPS):
            wm = jnp.where(mask, ws_ref[h], 0.0).astype(BF16)
            wmt = jnp.where(mask_t, wst_ref[h], 0.0).astype(BF16)
            cols = slice(h * hd, (h + 1) * hd)
            for n in range(tm // GMLP_BLOCK):
                rows = slice(n * GMLP_BLOCK, (n + 1) * GMLP_BLOCK)
                vb = vn[rows, cols]
                s = _dot(wm, vb) + bs_ref[h][:, :hd]
                d_out = dsg_v[rows, cols]
                dgu_sc[rows, cols] = d_out * s
                ds = d_out * gu[rows, cols]
                ds_b = ds.astype(BF16)
                dws_ref[h] += _dot_nt(ds_b, vb)
                dbs_sc[h] += ds
                dvn_sc[rows, cols] = _dot(wmt, ds_b)
        dvn = dvn_sc[...]
        dlg_ref[...] += _colsum(dvn * vh)
        dlb_ref[...] += _colsum(dvn)
        dv = _ln_bwd(dvn, vh, rstd, gain) * _gelu_grad(v)
        du = dgu_sc[...] * _gelu_grad(u)
        dp_ref[:, 0:A] = du.astype(BF16)
        dp_ref[:, A:2 * A] = dv.astype(BF16)
        dbin_ref[:, 0:A] += _colsum(du)
        dbin_ref[:, A:2 * A] += _colsum(dv)

        @pl.when(i == nt - 1)
        def _():
            for h in range(A_GROUPS):
                dws_ref[h] = jnp.where(mask, dws_ref[h], 0.0)
                dbs_ref[h:h + 1, :] = _colsum(dbs_sc[h].T)

    vec = pl.BlockSpec((1, A), lambda i: (0, 0))
    full = pl.BlockSpec((A_GROUPS, GMLP_BLOCK, GMLP_BLOCK), lambda i: (0, 0, 0))
    tile = pl.BlockSpec((tm, A), lambda i: (i, 0))
    return pl.pallas_call(
        body, name=name, grid=(nt,),
        in_specs=[tile, pl.BlockSpec((tm, A), lambda i: (i, 1)), tile, vec, vec, full, full, full, ANY],
        out_specs=[pl.BlockSpec((tm, 2 * A), lambda i: (i, 0)), pl.BlockSpec((1, 2 * A), lambda i: (0, 0)),
                   vec, vec, full, pl.BlockSpec((A_GROUPS, GMLP_BLOCK), lambda i: (0, 0))],
        out_shape=[jax.ShapeDtypeStruct(dproj.shape, BF16), jax.ShapeDtypeStruct((1, 2 * A), F32),
                   jax.ShapeDtypeStruct((1, A), F32), jax.ShapeDtypeStruct((1, A), F32),
                   jax.ShapeDtypeStruct((A_GROUPS, GMLP_BLOCK, GMLP_BLOCK), F32),
                   jax.ShapeDtypeStruct((A_GROUPS, GMLP_BLOCK), F32)],
        scratch_shapes=[pltpu.VMEM((tm, A), F32), pltpu.VMEM((tm, A), F32),
                        pltpu.VMEM((A_GROUPS, GMLP_BLOCK, hd), F32)],
        input_output_aliases={8: 0},
        compiler_params=_params("arbitrary"))(proj, proj, dsg, ln_g, ln_b, w_s, w_st, b_sb, dproj)


def _conv_tiles(T, B):
    tm = _tile(T, 256, CONV_ROWS)
    lb = min(LANES, B)
    return tm, tm // CONV_HALO, lb


def _fill_phases(src, dst, B, lb):
    rows = dst.shape[1]
    for p in range(1, 8):
        for cb in range(B // lb):
            ls = slice(cb * lb, (cb + 1) * lb)
            dst[p - 1, :, ls] = src[p:p + rows, ls]


def _tap_sums(w_ref, src, phases, ls, tm, offset_of_tap):
    lb = ls.stop - ls.start
    n_rc = tm // CONV_ROWS
    accs = [jnp.zeros((CONV_ROWS // 8, 8, lb), F32) for _ in range(n_rc)]
    for k in range(CONV_WIDTH):
        wk = jnp.broadcast_to(w_ref[k:k + 1, ls], (8, lb))[None]
        for rc in range(n_rc):
            win = _shifted(src, phases, rc * CONV_ROWS + offset_of_tap(k), ls)
            accs[rc] = accs[rc] + wk * win.reshape(CONV_ROWS // 8, 8, lb)
    return [a.reshape(CONV_ROWS, lb) for a in accs]


def _shifted(src, phases, off, ls):
    m, p = divmod(off, 8)
    if p == 0:
        return src[off:off + CONV_ROWS, ls]
    return phases[p - 1, 8 * m:8 * m + CONV_ROWS, ls]


def _conv_fwd(proj, w_dw, b_dw, ln_g, ln_b, *, name):
    T = proj.shape[0]
    B = proj.shape[1] // 8
    tm, nh, lb = _conv_tiles(T, B)

    def body(ap_ref, gp_ref, a_ref, g_ref, w_ref, bdw_ref, lg_ref, lb_ref, c_ref, cv_ref, z_sc, zp_sc):
        i = pl.program_id(0)
        z_sc[0:CONV_HALO, :] = jnp.where(i > 0, ap_ref[...] * _sig(gp_ref[...]), 0.0)
        z_sc[CONV_HALO:CONV_HALO + tm, :] = a_ref[...] * _sig(g_ref[...])
        _fill_phases(z_sc, zp_sc, B, lb)
        for cb in range(B // lb):
            ls = slice(cb * lb, (cb + 1) * lb)
            accs = _tap_sums(w_ref, z_sc, zp_sc, ls, tm, lambda k: CONV_HALO - (CONV_WIDTH - 1) + k)
            for rc, acc in enumerate(accs):
                c_ref[rc * CONV_ROWS:(rc + 1) * CONV_ROWS, ls] = acc + bdw_ref[:, ls]
        xh, _ = _ln_stats(c_ref[...])
        y = xh * lg_ref[...] + lb_ref[...]
        cv_ref[...] = (y * _sig(y)).astype(BF16)

    vec = pl.BlockSpec((1, B), lambda i: (0, 0))
    halo_a = pl.BlockSpec((CONV_HALO, B), lambda i: (jnp.maximum(i * nh - 1, 0), 2))
    halo_g = pl.BlockSpec((CONV_HALO, B), lambda i: (jnp.maximum(i * nh - 1, 0), 3))
    return pl.pallas_call(
        body, name=name, grid=(T // tm,),
        in_specs=[halo_a, halo_g, pl.BlockSpec((tm, B), lambda i: (i, 2)), pl.BlockSpec((tm, B), lambda i: (i, 3)),
                  pl.BlockSpec((CONV_WPAD, B), lambda i: (0, 0)), vec, vec, vec],
        out_specs=[pl.BlockSpec((tm, B), lambda i: (i, 0))] * 2,
        out_shape=[jax.ShapeDtypeStruct((T, B), F32), jax.ShapeDtypeStruct((T, B), BF16)],
        scratch_shapes=[pltpu.VMEM((CONV_HALO + tm, B), F32), pltpu.VMEM((7, CONV_HALO + tm - 8, B), F32)],
        compiler_params=_params("parallel"))(proj, proj, proj, proj, w_dw, b_dw, ln_g, ln_b)


def _conv_bwd_ln(dcv, c, ln_g, ln_b, *, name):
    T, B = c.shape
    tm = _tile(T, 512)

    def body(dcv_ref, c_ref, lg_ref, lb_ref, dc_ref, dlg_ref, dlb_ref, dbdw_ref):
        @pl.when(pl.program_id(0) == 0)
        def _():
            dlg_ref[...] = jnp.zeros_like(dlg_ref)
            dlb_ref[...] = jnp.zeros_like(dlb_ref)
            dbdw_ref[...] = jnp.zeros_like(dbdw_ref)
        gain = lg_ref[...]
        xh, rstd = _ln_stats(c_ref[...])
        y = xh * gain + lb_ref[...]
        s = _sig(y)
        dy = dcv_ref[...] * (s * (1.0 + y * (1.0 - s)))
        dlg_ref[...] += _colsum(dy * xh)
        dlb_ref[...] += _colsum(dy)
        dc = _ln_bwd(dy, xh, rstd, gain)
        dc_ref[...] = dc
        dbdw_ref[...] += _colsum(dc)

    tile = pl.BlockSpec((tm, B), lambda i: (i, 0))
    vec = pl.BlockSpec((1, B), lambda i: (0, 0))
    return pl.pallas_call(
        body, name=name, grid=(T // tm,), in_specs=[tile, tile, vec, vec], out_specs=[tile, vec, vec, vec],
        out_shape=[jax.ShapeDtypeStruct((T, B), F32)] + [jax.ShapeDtypeStruct((1, B), F32)] * 3,
        compiler_params=_params("arbitrary"))(dcv, c, ln_g, ln_b)


def _conv_bwd(proj, dc, w_dw, dproj, *, name, carry=None):
    T = proj.shape[0]
    B = proj.shape[1] // 8
    tm, nh, lb = _conv_tiles(T, B)
    nt = T // tm
    n_halo = T // CONV_HALO

    def body(ap_ref, gp_ref, a_ref, g_ref, dc_ref, dcn_ref, w_ref, _alias,
             dp_ref, dbin_ref, dw_ref, z_sc, dc_sc, dz_sc, dw_sc, zp_sc, dcp_sc):
        i = pl.program_id(0)

        @pl.when(i == 0)
        def _():
            dbin_ref[...] = jnp.zeros_like(dbin_ref)
            dw_sc[...] = jnp.zeros_like(dw_sc)

        a = a_ref[...]
        s = _sig(g_ref[...])
        z_sc[0:CONV_HALO, :] = jnp.where(i > 0, ap_ref[...] * _sig(gp_ref[...]), 0.0)
        z_sc[CONV_HALO:CONV_HALO + tm, :] = a * s
        dc_sc[0:tm, :] = dc_ref[...]
        dc_sc[tm:tm + CONV_HALO, :] = jnp.where(i < nt - 1, dcn_ref[...], 0.0)
        _fill_phases(z_sc, zp_sc, B, lb)
        _fill_phases(dc_sc, dcp_sc, B, lb)
        for cb in range(B // lb):
            ls = slice(cb * lb, (cb + 1) * lb)
            for rc in range(tm // CONV_ROWS):
                r0 = rc * CONV_ROWS
                acc = jnp.zeros((CONV_ROWS, lb), F32)
                for k in range(CONV_WIDTH):
                    acc = acc + w_ref[k:k + 1, ls] * _shifted(dc_sc, dcp_sc, r0 + (CONV_WIDTH - 1) - k, ls)
                dz_sc[r0:r0 + CONV_ROWS, ls] = acc
            for k in range(CONV_WIDTH):
                part = jnp.zeros((8, lb), F32)
                for rc in range(tm // CONV_ROWS):
                    r0 = rc * CONV_ROWS
                    prod = dc_sc[r0:r0 + CONV_ROWS, ls] * _shifted(
                        z_sc, zp_sc, r0 + CONV_HALO - (CONV_WIDTH - 1) + k, ls)
                    part = part + jnp.sum(prod.reshape(CONV_ROWS // 8, 8, lb), axis=0)
                dw_sc[8 * k:8 * k + 8, ls] += part
        dz = dz_sc[...]
        da = dz * s
        dg = dz * a * s * (1.0 - s)
        dp_ref[:, 0:B] = da.astype(BF16)
        dp_ref[:, B:2 * B] = dg.astype(BF16)
        dbin_ref[:, 0:B] += _colsum(da)
        dbin_ref[:, B:2 * B] += _colsum(dg)

        @pl.when(i == nt - 1)
        def _():
            for k in range(CONV_WIDTH):
                dw_ref[k:k + 1, :] = _colsum(dw_sc[8 * k:8 * k + 8, :])

    halo_a = pl.BlockSpec((CONV_HALO, B), lambda i: (jnp.maximum(i * nh - 1, 0), 2))
    halo_g = pl.BlockSpec((CONV_HALO, B), lambda i: (jnp.maximum(i * nh - 1, 0), 3))
    halo_dc = pl.BlockSpec((CONV_HALO, B), lambda i: (jnp.minimum((i + 1) * nh, n_halo - 1), 0))
    return _call(
        body, name=name, grid=(nt,),
        in_specs=[halo_a, halo_g, pl.BlockSpec((tm, B), lambda i: (i, 2)), pl.BlockSpec((tm, B), lambda i: (i, 3)),
                  pl.BlockSpec((tm, B), lambda i: (i, 0)), halo_dc,
                  pl.BlockSpec((CONV_WPAD, B), lambda i: (0, 0)), ANY],
        out_specs=[pl.BlockSpec((tm, 2 * B), lambda i: (i, 1)), pl.BlockSpec((1, 2 * B), lambda i: (0, 0)),
                   pl.BlockSpec((CONV_WIDTH, B), lambda i: (0, 0))],
        out_shape=[jax.ShapeDtypeStruct(dproj.shape, BF16), jax.ShapeDtypeStruct((1, 2 * B), F32),
                   jax.ShapeDtypeStruct((CONV_WIDTH, B), F32)],
        scratch_shapes=[pltpu.VMEM((CONV_HALO + tm, B), F32), pltpu.VMEM((tm + CONV_HALO, B), F32),
                        pltpu.VMEM((tm, B), F32), pltpu.VMEM((8 * CONV_WIDTH, B), F32),
                        pltpu.VMEM((7, CONV_HALO + tm - 8, B), F32), pltpu.VMEM((7, CONV_HALO + tm - 8, B), F32)],
        aliases={7: 0}, sem=("arbitrary",), args=[proj, proj, proj, proj, dc, dc, w_dw, dproj], carry=carry)


def _mix_fwd_gate(sg, cv, proj, wa, wb, *, name):
    T, A = sg.shape
    D = wa.shape[1]
    tm = _tile(T, 256)

    def body(sg_ref, cv_ref, la_ref, lb_ref, wa_ref, wb_ref, ya_ref, yb_ref, m_ref):
        ya = _dot(sg_ref[...], wa_ref[...])
        yb = _dot(cv_ref[...], wb_ref[...])
        ya_ref[...] = ya.astype(BF16)
        yb_ref[...] = yb.astype(BF16)
        m_ref[...] = (_sig(la_ref[...]) * ya + _sig(lb_ref[...]) * yb).astype(BF16)

    act = pl.BlockSpec((tm, A), lambda i: (i, 0))
    wide = pl.BlockSpec((tm, D), lambda i: (i, 0))
    wspec = pl.BlockSpec((A, D), lambda i: (0, 0))
    return pl.pallas_call(
        body, name=name, grid=(T // tm,),
        in_specs=[act, act, pl.BlockSpec((tm, D), lambda i: (i, 2)), pl.BlockSpec((tm, D), lambda i: (i, 3)),
                  wspec, wspec],
        out_specs=[wide] * 3, out_shape=[jax.ShapeDtypeStruct((T, D), BF16)] * 3,
        compiler_params=_params("parallel"))(sg, cv, proj, proj, wa, wb)


def _mix_fwd_out(m, wout, xh, lg, lb, *, name):
    T, D = xh.shape
    tm = _tile(T, 512)

    def body(m_ref, w_ref, xh_ref, lg_ref, lb_ref, xho_ref, rstd_ref):
        r = ALPHA * (xh_ref[...] * lg_ref[...] + lb_ref[...]) + _dot(m_ref[...], w_ref[...])
        xho, rstd = _ln_stats(r)
        xho_ref[...] = xho
        rstd_ref[...] = jnp.broadcast_to(rstd, (tm, LANES))

    row = pl.BlockSpec((tm, D), lambda i: (i, 0))
    vec = pl.BlockSpec((1, D), lambda i: (0, 0))
    return pl.pallas_call(
        body, name=name, grid=(T // tm,),
        in_specs=[row, pl.BlockSpec((D, D), lambda i: (0, 0)), row, vec, vec],
        out_specs=[row, pl.BlockSpec((tm, LANES), lambda i: (i, 0))],
        out_shape=[jax.ShapeDtypeStruct((T, D), F32), jax.ShapeDtypeStruct((T, LANES), F32)],
        compiler_params=_params("parallel"))(m, wout, xh, lg, lb)


def _mix_bwd_gate(dr, wout, proj, ya, yb, *, name, carry=None):
    T, D = dr.shape
    N = proj.shape[1]
    tm = _tile(T, 256)

    def body(dr_ref, w_ref, la_ref, lb_ref, ya_ref, yb_ref, dya_ref, dyb_ref, dp_ref, dbin_ref):
        @pl.when(pl.program_id(0) == 0)
        def _():
            dbin_ref[...] = jnp.zeros_like(dbin_ref)
        dm = _dot_nt(dr_ref[...].astype(BF16), w_ref[...])
        sa = _sig(la_ref[...])
        sb = _sig(lb_ref[...])
        dya_ref[...] = (dm * sa).astype(BF16)
        dyb_ref[...] = (dm * sb).astype(BF16)
        dla = dm * ya_ref[...].astype(F32) * sa * (1.0 - sa)
        dlb = dm * yb_ref[...].astype(F32) * sb * (1.0 - sb)
        dp_ref[:, 0:D] = dla.astype(BF16)
        dp_ref[:, D:2 * D] = dlb.astype(BF16)
        dbin_ref[:, 0:D] += _colsum(dla)
        dbin_ref[:, D:2 * D] += _colsum(dlb)

    row = pl.BlockSpec((tm, D), lambda i: (i, 0))
    return _call(
        body, name=name, grid=(T // tm,),
        in_specs=[row, pl.BlockSpec((D, D), lambda i: (0, 0)), pl.BlockSpec((tm, D), lambda i: (i, 2)),
                  pl.BlockSpec((tm, D), lambda i: (i, 3)), row, row],
        out_specs=[row, row, pl.BlockSpec((tm, 2 * D), lambda i: (i, 1)), pl.BlockSpec((1, 2 * D), lambda i: (0, 0))],
        out_shape=[jax.ShapeDtypeStruct((T, D), BF16), jax.ShapeDtypeStruct((T, D), BF16),
                   jax.ShapeDtypeStruct((T, N), BF16), jax.ShapeDtypeStruct((1, 2 * D), F32)],
        sem=("arbitrary",), args=[dr, wout, proj, proj, ya, yb], carry=carry)


def _mix_bwd_proj(dya, dyb, wa, wb, *, name):
    T, D = dya.shape
    A = wa.shape[0]
    tm = _tile(T, 512)

    def body(dya_ref, dyb_ref, wa_ref, wb_ref, dsg_ref, dcv_ref):
        dsg_ref[...] = _dot_nt(dya_ref[...], wa_ref[...])
        dcv_ref[...] = _dot_nt(dyb_ref[...], wb_ref[...])

    row = pl.BlockSpec((tm, D), lambda i: (i, 0))
    wspec = pl.BlockSpec((A, D), lambda i: (0, 0))
    act = pl.BlockSpec((tm, A), lambda i: (i, 0))
    return pl.pallas_call(
        body, name=name, grid=(T // tm,), in_specs=[row, row, wspec, wspec], out_specs=[act, act],
        out_shape=[jax.ShapeDtypeStruct((T, A), F32)] * 2,
        compiler_params=_params("parallel"))(dya, dyb, wa, wb)


def _mesh_pos():
    return lax.axis_index("x"), lax.axis_index("y"), lax.axis_index("c")


def _shard_view(ref, p, shape, axis):
    r, c = shape
    if axis == 0:
        return ref.at[pl.ds(pl.multiple_of(p * r, 16), r), :]
    return ref.at[:, pl.ds(pl.multiple_of(p * c, LANES), c)]


def _all_gather(shards, axes, progressive=False, busy=0.93):
    n = len(shards)
    shapes = [s.shape for s in shards]
    sizes = [s.size * s.dtype.itemsize for s in shards]
    if progressive:
        done = [sum(sizes[:t + 1]) / sum(sizes) for t in range(n)]
        when = (0.0,) + tuple(min(0.97, 0.04 + busy * d) for d in done) + (1.0,)
    else:
        when = (0.0, 1.0)

    def run(ins, outs, sems, phase):
        send_sems, recv_sems, local_sems = sems
        x, y, c = _mesh_pos()
        me, sibling = (x, y, c), (x, y, 1 - c)
        chips = [(1 - x, y), (x, 1 - y), (1 - x, 1 - y)]

        def view(t, pos):
            px, py, pc = pos
            return _shard_view(outs[t], 4 * px + 2 * py + pc, shapes[t], axes[t])

        def copy(t, k, block, to, src=None):
            return pltpu.make_async_remote_copy(
                src_ref=view(t, block) if src is None else src, dst_ref=view(t, block),
                send_sem=send_sems.at[7 * t + k], recv_sem=recv_sems.at[7 * t + k],
                device_id=to, device_id_type=MESH)

        mine = [pltpu.make_async_copy(ins[t], view(t, me), local_sems.at[t]) for t in range(n)]
        first = []
        for t in range(n):
            first.append(copy(t, 0, me, sibling, src=ins[t]))
            first += [copy(t, 1 + j, me, (*chip, c), src=ins[t]) for j, chip in enumerate(chips)]
        if phase == 0:
            for cp in mine + first:
                cp.start()
            return

        def forward(t):
            for j, chip in enumerate(chips):
                copy(t, 1 + j, (*chip, c), me).wait_recv()
                copy(t, 4 + j, (*chip, c), sibling).start()

        if progressive and phase <= n:
            forward(phase - 1)
            return
        if not progressive:
            for t in range(n):
                forward(t)
        passed = [copy(t, 4 + j, (*chip, c), sibling) for t in range(n) for j, chip in enumerate(chips)]
        for t in range(n):
            copy(t, 0, sibling, me).wait_recv()
            for j, chip in enumerate(chips):
                copy(t, 4 + j, (*chip, 1 - c), me).wait_recv()
        for cp in first + passed:
            cp.wait_send()
        for cp in mine:
            cp.wait()

    out_shape = [jax.ShapeDtypeStruct((N_DEV * s.shape[0], s.shape[1]) if ax == 0
                                      else (s.shape[0], N_DEV * s.shape[1]), s.dtype)
                 for s, ax in zip(shards, axes)]
    return _Comm(shards, out_shape, [pltpu.SemaphoreType.DMA((7 * n,)), pltpu.SemaphoreType.DMA((7 * n,)),
                                     pltpu.SemaphoreType.DMA((n,))], run, when)


def _all_gather_routed(shards, axes):
    n = len(shards)
    shapes = [s.shape for s in shards]
    assert all(s[0] % 32 == 0 for s in shapes)

    def run(ins, outs, sems, phase):
        send_sems, recv_sems, local_sems = sems
        x, y, c = _mesh_pos()
        me, sibling = (x, y, c), (x, y, 1 - c)
        nx, ny, nd = (1 - x, y), (x, 1 - y), (1 - x, 1 - y)

        def block(t, chip, core):
            return _shard_view(outs[t], 4 * chip[0] + 2 * chip[1] + core, shapes[t], axes[t])

        def half(t, chip, core, h):
            hr = shapes[t][0] // 2
            return block(t, chip, core).at[pl.ds(h * hr, hr), :]

        def own_half(t, h):
            hr = shapes[t][0] // 2
            return ins[t].at[pl.ds(h * hr, hr), :]

        def copy(t, k, src, dst, to):
            return pltpu.make_async_remote_copy(src_ref=src, dst_ref=dst, send_sem=send_sems.at[10 * t + k],
                                                recv_sem=recv_sems.at[10 * t + k], device_id=to, device_id_type=MESH)

        def arrived(t, k, dst):
            copy(t, k, dst, dst, me).wait_recv()

        mine = [pltpu.make_async_copy(ins[t], block(t, (x, y), c), local_sems.at[t]) for t in range(n)]
        own = []
        for t in range(n):
            own += [copy(t, 0, ins[t], block(t, (x, y), c), sibling),
                    copy(t, 1, own_half(t, 0), half(t, (x, y), c, 0), (*nx, c)),
                    copy(t, 2, own_half(t, 1), half(t, (x, y), c, 1), (*nx, c)),
                    copy(t, 3, own_half(t, 1), half(t, (x, y), c, 1), (*ny, c)),
                    copy(t, 4, own_half(t, 0), half(t, (x, y), c, 0), (*ny, c))]
        if phase == 0:
            for cp in mine + own:
                cp.start()
            return
        relays = []

        def relay(t, k, view, to):
            cp = copy(t, k, view, view, to)
            cp.start()
            relays.append(cp)

        for t in range(n):
            arrived(t, 1, half(t, nx, c, 0))
            relay(t, 5, half(t, nx, c, 0), (*ny, c))
            arrived(t, 3, half(t, ny, c, 1))
            relay(t, 6, half(t, ny, c, 1), (*nx, c))
            arrived(t, 2, half(t, nx, c, 1))
            relay(t, 7, block(t, nx, c), sibling)
            arrived(t, 4, half(t, ny, c, 0))
            relay(t, 8, block(t, ny, c), sibling)
            arrived(t, 5, half(t, nd, c, 0))
            arrived(t, 6, half(t, nd, c, 1))
            relay(t, 9, block(t, nd, c), sibling)
        for t in range(n):
            arrived(t, 0, block(t, (x, y), 1 - c))
            arrived(t, 7, block(t, nx, 1 - c))
            arrived(t, 8, block(t, ny, 1 - c))
            arrived(t, 9, block(t, nd, 1 - c))
        for cp in own + relays:
            cp.wait_send()
        for cp in mine:
            cp.wait()

    out_shape = [jax.ShapeDtypeStruct((N_DEV * s.shape[0], s.shape[1]) if ax == 0
                                      else (s.shape[0], N_DEV * s.shape[1]), s.dtype)
                 for s, ax in zip(shards, axes)]
    return _Comm(shards, out_shape, [pltpu.SemaphoreType.DMA((10 * n,)), pltpu.SemaphoreType.DMA((10 * n,)),
                                     pltpu.SemaphoreType.DMA((n,))], run)


def _rs_to_sibling(grads, shapes, axes):
    n = len(grads)

    def run(gs, outs, sems, phase):
        send_sems, recv_sems = sems
        x, y, c = _mesh_pos()
        copies = [pltpu.make_async_remote_copy(
            src_ref=_shard_view(gs[t], 2 * k + (1 - c), shapes[t], axes[t]), dst_ref=outs[t].at[k],
            send_sem=send_sems.at[4 * t + k], recv_sem=recv_sems.at[4 * t + k],
            device_id=(x, y, 1 - c), device_id_type=MESH) for t in range(n) for k in range(4)]
        if phase == 0:
            for cp in copies:
                cp.start()
            return
        for cp in copies:
            cp.wait_recv()
        for cp in copies:
            cp.wait_send()

    return _Comm(grads, [jax.ShapeDtypeStruct((4,) + tuple(s), BF16) for s in shapes],
                 [pltpu.SemaphoreType.DMA((4 * n,)), pltpu.SemaphoreType.DMA((4 * n,))], run)


def _rs_pair_sum(g, recv, cidx, shape, axis, *, name):
    r, c = shape
    tr = _tile(r, max(8, (1 << 21) // c), 16)
    nr = r // tr

    def body(c_ref, g_ref, rv_ref, o_ref):
        o_ref[...] = (g_ref[...].astype(F32) + rv_ref[...].astype(F32)).astype(BF16)

    if axis == 1:
        g_spec = pl.BlockSpec((tr, c), lambda k, i, s: (i, 2 * k + s[0]))
    else:
        g_spec = pl.BlockSpec((tr, c), lambda k, i, s: ((2 * k + s[0]) * nr + i, 0))
    blk = pl.BlockSpec((None, tr, c), lambda k, i, s: (k, i, 0))
    return pl.pallas_call(
        body, name=name,
        grid_spec=pltpu.PrefetchScalarGridSpec(num_scalar_prefetch=1, grid=(4, nr), in_specs=[g_spec, blk],
                                               out_specs=blk),
        out_shape=jax.ShapeDtypeStruct((4, r, c), BF16),
        compiler_params=_params("parallel", "parallel"))(cidx, g, recv)


def _rs_to_chips(parts):
    n = len(parts)

    def run(ps, outs, sems, phase):
        send_sems, recv_sems, local_sems = sems
        x, y, c = _mesh_pos()
        my_chip = 2 * x + y
        peers = [(1 - x, y), (x, 1 - y), (1 - x, 1 - y)]
        local = [pltpu.make_async_copy(ps[t].at[my_chip], outs[t].at[my_chip], local_sems.at[t]) for t in range(n)]
        sends = [pltpu.make_async_remote_copy(
            src_ref=ps[t].at[2 * px + py], dst_ref=outs[t].at[my_chip],
            send_sem=send_sems.at[3 * t + j], recv_sem=recv_sems.at[3 * t + j],
            device_id=(px, py, c), device_id_type=MESH) for t in range(n) for j, (px, py) in enumerate(peers)]
        if phase == 0:
            for cp in local + sends:
                cp.start()
            return
        for t in range(n):
            for j, (px, py) in enumerate(peers):
                pltpu.make_async_remote_copy(
                    src_ref=ps[t].at[2 * px + py], dst_ref=outs[t].at[2 * px + py],
                    send_sem=send_sems.at[3 * t + j], recv_sem=recv_sems.at[3 * t + j],
                    device_id=(x, y, c), device_id_type=MESH).wait_recv()
        for cp in sends:
            cp.wait_send()
        for cp in local:
            cp.wait()

    return _Comm(parts, [jax.ShapeDtypeStruct(p.shape, BF16) for p in parts],
                 [pltpu.SemaphoreType.DMA((3 * n,)), pltpu.SemaphoreType.DMA((3 * n,)),
                  pltpu.SemaphoreType.DMA((n,))], run)


def _exchange_small(buf):
    def run(ins, outs, sems, phase):
        (in_ref,), (slots,) = ins, outs
        send_sems, recv_sems, local_sem = sems
        x, y, c = _mesh_pos()
        me = 4 * x + 2 * y + c
        local = pltpu.make_async_copy(in_ref, slots.at[me], local_sem.at[0])
        flips = [(fx, fy, fc) for fx in (0, 1) for fy in (0, 1) for fc in (0, 1)][1:]
        peers = [(1 - x if fx else x, 1 - y if fy else y, 1 - c if fc else c) for fx, fy, fc in flips]
        sends = [pltpu.make_async_remote_copy(src_ref=in_ref, dst_ref=slots.at[me], send_sem=send_sems.at[k],
                                              recv_sem=recv_sems.at[k], device_id=peer, device_id_type=MESH)
                 for k, peer in enumerate(peers)]
        if phase == 0:
            for cp in [local] + sends:
                cp.start()
            return
        for k, (px, py, pc) in enumerate(peers):
            pltpu.make_async_remote_copy(src_ref=in_ref, dst_ref=slots.at[4 * px + 2 * py + pc],
                                         send_sem=send_sems.at[k], recv_sem=recv_sems.at[k],
                                         device_id=(x, y, c), device_id_type=MESH).wait_recv()
        for cp in sends:
            cp.wait_send()
        local.wait()

    return _Comm([buf], [jax.ShapeDtypeStruct((N_DEV,) + buf.shape, F32)],
                 [pltpu.SemaphoreType.DMA((7,)), pltpu.SemaphoreType.DMA((7,)), pltpu.SemaphoreType.DMA((1,))], run)


def _sum_slots(slots, *, name):
    _, R, C = slots.shape
    tr = _tile(R, 512)

    def body(s_ref, o_ref):
        acc = s_ref[0]
        for p in range(1, N_DEV):
            acc = acc + s_ref[p]
        o_ref[...] = acc

    return pl.pallas_call(
        body, name=name, grid=(R // tr,), in_specs=[pl.BlockSpec((N_DEV, tr, C), lambda i: (0, i, 0))],
        out_specs=pl.BlockSpec((tr, C), lambda i: (i, 0)), out_shape=jax.ShapeDtypeStruct((R, C), F32),
        compiler_params=_params("parallel"))(slots)


def _adam_math(g, w, m, v):
    m_new = ADAM_B1 * m + (1.0 - ADAM_B1) * g
    v_new = ADAM_B2 * v + (1.0 - ADAM_B2) * (g * g)
    m_hat = m_new / ADAM_C1
    v_hat = v_new / ADAM_C2
    delta = -ADAM_LR * (m_hat / (jnp.sqrt(v_hat) + ADAM_EPS) + ADAM_WD * w)
    return delta, m_new, v_new


def _adamw_sharded(q, w, m, v, *, name):
    r, c = w.shape
    tr = _tile(r, max(8, (1 << 19) // c), 16)

    def body(q_ref, w_ref, m_ref, v_ref, g_ref, d_ref, mo_ref, vo_ref):
        g = ((q_ref[0].astype(F32) + q_ref[1].astype(F32)) + q_ref[2].astype(F32)) + q_ref[3].astype(F32)
        g_ref[...] = g
        d_ref[...], mo_ref[...], vo_ref[...] = _adam_math(g, w_ref[...], m_ref[...], v_ref[...])

    blk = pl.BlockSpec((tr, c), lambda i: (i, 0))
    return pl.pallas_call(
        body, name=name, grid=(r // tr,),
        in_specs=[pl.BlockSpec((4, tr, c), lambda i: (0, i, 0)), blk, blk, blk], out_specs=[blk] * 4,
        out_shape=[jax.ShapeDtypeStruct((r, c), F32)] * 4,
        compiler_params=_params("parallel"))(q, w, m, v)


def _adamw_plain(g, w, m, v, *, name):
    r, c = w.shape
    tr = _tile(r, 512)

    def body(g_ref, w_ref, m_ref, v_ref, d_ref, mo_ref, vo_ref):
        d_ref[...], mo_ref[...], vo_ref[...] = _adam_math(g_ref[...], w_ref[...], m_ref[...], v_ref[...])

    blk = pl.BlockSpec((tr, c), lambda i: (i, 0))
    return pl.pallas_call(
        body, name=name, grid=(r // tr,), in_specs=[blk] * 4, out_specs=[blk] * 3,
        out_shape=[jax.ShapeDtypeStruct((r, c), F32)] * 3,
        compiler_params=_params("parallel"))(g, w, m, v)


def _pack_rows(arrays):
    return jnp.concatenate([a.reshape(-1, LANES) for a in arrays], axis=0)


def kernel(x, ffn1_w_gu, ffn1_w_down, ln1_g, ln1_b, w_in, b_in, sgu_ln_g, sgu_ln_b, sgu_w_s, sgu_b_s, w_a_proj, conv_w_dw, conv_b_dw, conv_ln_g, conv_ln_b, w_b_proj, w_out, ln2_g, ln2_b, ffn2_w_gu, ffn2_w_down, ln3_g, ln3_b, loss_target, m_ffn1_w_gu, m_ffn1_w_down, m_ln1_g, m_ln1_b, m_w_in, m_b_in, m_sgu_ln_g, m_sgu_ln_b, m_sgu_w_s, m_sgu_b_s, m_w_a_proj, m_conv_w_dw, m_conv_b_dw, m_conv_ln_g, m_conv_ln_b, m_w_b_proj, m_w_out, m_ln2_g, m_ln2_b, m_ffn2_w_gu, m_ffn2_w_down, m_ln3_g, m_ln3_b, v_ffn1_w_gu, v_ffn1_w_down, v_ln1_g, v_ln1_b, v_w_in, v_b_in, v_sgu_ln_g, v_sgu_ln_b, v_sgu_w_s, v_sgu_b_s, v_w_a_proj, v_conv_w_dw, v_conv_b_dw, v_conv_ln_g, v_conv_ln_b, v_w_b_proj, v_w_out, v_ln2_g, v_ln2_b, v_ffn2_w_gu, v_ffn2_w_down, v_ln3_g, v_ln3_b):
    given = dict(locals())
    w = {n: given[n][0] for n in WEIGHTS}
    mom = {n: given["m_" + n][0] for n in WEIGHTS}
    var = {n: given["v_" + n][0] for n in WEIGHTS}
    xt = x[0]
    target = loss_target[0]
    T, D = xt.shape
    A = w['w_a_proj'].shape[0]

    big_names = list(BIG)
    early = ['ffn1_w_gu', 'ffn1_w_down']
    later = ['ffn2_w_gu']
    late = [n for n in big_names if n not in early + later]
    conv_w_pad = jnp.pad(w['conv_w_dw'], ((0, CONV_WPAD - CONV_WIDTH), (0, 0)))
    w_bf = {n: w[n].astype(BF16) for n in big_names}
    full = dict(zip(early, _comm_call(_all_gather_routed([w_bf[n] for n in early], [BIG[n] for n in early]),
                                      name="all_gather_ffn1")))
    gather_late = _all_gather([w_bf[n] for n in late] + [conv_w_pad], [BIG[n] for n in late] + [1],
                              progressive=True, busy=0.65)
    gather_later = _all_gather([w_bf[n] for n in later], [BIG[n] for n in later], progressive=True, busy=0.8)

    def row(v):
        return v.reshape(1, -1)

    ones = jnp.ones((1, D), F32)
    zeros = jnp.zeros((1, D), F32)
    w_s = w['sgu_w_s']
    w_st = jnp.swapaxes(w_s, 1, 2)
    b_sb = jnp.broadcast_to(w['sgu_b_s'][:, :, None], w_s.shape)

    (gate1, up1, xb0, xh1, rstd1), gathered = _ffn_fwd(xt, ones, zeros, full['ffn1_w_gu'], full['ffn1_w_down'],
                                                  affine=False, name="ffn1_fwd", carry=gather_late)
    full.update(zip(late, gathered[:-1]))
    conv_w_full = gathered[-1]
    g1, b1 = row(w['ln1_g']), row(w['ln1_b'])
    (proj, xb1), gathered = _inproj_fwd(xh1, g1, b1, full['w_in'], row(w['b_in']), name="inproj_fwd",
                                        carry=gather_later)
    full.update(zip(later, gathered))
    sg = _sgu_fwd(proj, row(w['sgu_ln_g']), row(w['sgu_ln_b']), w_s, b_sb, name="sgu_fwd")
    conv_out, cv = _conv_fwd(proj, conv_w_full, row(w['conv_b_dw']), row(w['conv_ln_g']), row(w['conv_ln_b']),
                             name="conv_fwd")
    ya, yb, mixed = _mix_fwd_gate(sg, cv, proj, full['w_a_proj'], full['w_b_proj'], name="mix_fwd_gate")
    xh2, rstd2 = _mix_fwd_out(mixed, full['w_out'], xh1, g1, b1, name="mix_fwd_out")
    g2, b2 = row(w['ln2_g']), row(w['ln2_b'])
    gate2, up2, xb2, dr3, loss_part, d_ln3_g, d_ln3_b = _ffn_fwd(
        xh2, g2, b2, full['ffn2_w_gu'], full['ffn2_w_down'], affine=True, name="ffn2_fwd_loss",
        final=(row(w['ln3_g']), row(w['ln3_b']), target))

    F = full['ffn2_w_down'].shape[0]
    h2, dgate2, dup2, dr2, d_ln2_g, d_ln2_b = _ffn_bwd(dr3, gate2, up2, full['ffn2_w_gu'], full['ffn2_w_down'],
                                                      name="ffn2_bwd", prev=(xh2, rstd2, g2))
    G, P, Q = {}, {}, {}
    cidx = lax.axis_index("c").astype(jnp.int32).reshape(1)

    def to_sibling(names):
        return _rs_to_sibling([G[n] for n in names], [w[n].shape for n in names], [BIG[n] for n in names])

    def pair_sum(names, received):
        for n, rv in zip(names, received):
            P[n] = _rs_pair_sum(G[n], rv, cidx, w[n].shape, BIG[n], name="rs_pair_sum_" + n)

    def to_chips(names):
        return _rs_to_chips([P[n] for n in names])

    G['ffn2_w_down'] = _mm_tn(h2, dr3, name="dw_ffn2_down", tm_pref=1408, tn_pref=2048, scale=0.5)
    gu, rv = _mm_tn(xb2, dgate2, name="dw_ffn2_gate", tm_pref=2048, tn_pref=1408,
                    n_total=2 * F, carry=to_sibling(['ffn2_w_down']))
    pair_sum(['ffn2_w_down'], rv)
    G['ffn2_w_gu'], q = _mm_tn(xb2, dup2, name="dw_ffn2_up", tm_pref=2048, tn_pref=1408,
                               into=gu, col_off=F, n_total=2 * F, carry=to_chips(['ffn2_w_down']))
    Q['ffn2_w_down'] = q[0]

    (dya, dyb, dproj, dbin_gate), rv = _mix_bwd_gate(dr2, full['w_out'], proj, ya, yb, name="mix_bwd_gate",
                                                     carry=to_sibling(['ffn2_w_gu']))
    pair_sum(['ffn2_w_gu'], rv)
    dsg, dcv = _mix_bwd_proj(dya, dyb, full['w_a_proj'], full['w_b_proj'], name="mix_bwd_proj")
    dproj, dbin_sgu, d_sgu_ln_g, d_sgu_ln_b, d_w_s, d_b_s = _sgu_bwd(
        proj, dsg, row(w['sgu_ln_g']), row(w['sgu_ln_b']), w_s, w_st, b_sb, dproj, name="sgu_bwd")
    dconv, d_conv_ln_g, d_conv_ln_b, d_conv_b = _conv_bwd_ln(dcv, conv_out, row(w['conv_ln_g']),
                                                            row(w['conv_ln_b']), name="conv_bwd_ln")
    (dproj, dbin_conv, d_conv_w), q = _conv_bwd(proj, dconv, conv_w_full, dproj, name="conv_bwd",
                                                carry=to_chips(['ffn2_w_gu']))
    Q['ffn2_w_gu'] = q[0]

    mid = ['w_out', 'w_a_proj', 'w_b_proj']
    G['w_out'] = _mm_tn(mixed, dr2, name="dw_out", tm_pref=2048, tn_pref=1024)
    G['w_a_proj'] = _mm_tn(sg, dya, name="dw_a_proj", tm_pref=1024, tn_pref=2048)
    G['w_b_proj'] = _mm_tn(cv, dyb, name="dw_b_proj", tm_pref=1024, tn_pref=2048)
    G['w_in'], rv = _mm_tn(xb1, dproj, name="dw_in", tm_pref=2048, tn_pref=1024, tk_pref=2048,
                           carry=to_sibling(mid))
    pair_sum(mid, rv)
    both = _join(to_chips(mid), to_sibling(['w_in']))
    (dr1, d_ln1_g, d_ln1_b), moved = _inproj_bwd(dproj, full['w_in'], dr2, xh1, rstd1, g1, name="inproj_bwd",
                                                 carry=both)
    q, rv = both.split(moved)
    Q.update(zip(mid, q))
    pair_sum(['w_in'], rv)

    h1, dgate1, dup1, grad_x = _ffn_bwd(dr1, gate1, up1, full['ffn1_w_gu'], full['ffn1_w_down'], name="ffn1_bwd")
    G['ffn1_w_down'], q = _mm_tn(h1, dr1, name="dw_ffn1_down", tm_pref=1408, tn_pref=2048, scale=0.5,
                                 carry=to_chips(['w_in']))
    Q['w_in'] = q[0]
    small_g = {'ln1_g': d_ln1_g, 'ln1_b': d_ln1_b,
               'b_in': jnp.concatenate([dbin_sgu, dbin_conv, dbin_gate], axis=1),
               'sgu_ln_g': d_sgu_ln_g, 'sgu_ln_b': d_sgu_ln_b, 'sgu_w_s': d_w_s, 'sgu_b_s': d_b_s,
               'conv_b_dw': d_conv_b, 'conv_ln_g': d_conv_ln_g, 'conv_ln_b': d_conv_ln_b,
               'ln2_g': d_ln2_g, 'ln2_b': d_ln2_b, 'ln3_g': d_ln3_g, 'ln3_b': d_ln3_b}
    packed = _pack_rows([small_g[n] for n in SMALL] + [d_conv_w, loss_part])
    both = _join(to_sibling(['ffn1_w_down']), _exchange_small(packed))
    gu, moved = _mm_tn(xb0, dgate1, name="dw_ffn1_gate", tm_pref=2048, tn_pref=1408, n_total=2 * F, carry=both)
    rv, slots = both.split(moved)
    pair_sum(['ffn1_w_down'], rv)
    reduced = _sum_slots(slots[0], name="sum_small")
    G['ffn1_w_gu'], q = _mm_tn(xb0, dup1, name="dw_ffn1_up", tm_pref=2048, tn_pref=1408, into=gu, col_off=F,
                               n_total=2 * F, carry=to_chips(['ffn1_w_down']))
    Q['ffn1_w_down'] = q[0]
    pair_sum(['ffn1_w_gu'], _comm_call(to_sibling(['ffn1_w_gu']), name="rs_to_sibling_last"))
    Q['ffn1_w_gu'] = _comm_call(to_chips(['ffn1_w_gu']), name="rs_to_chips_last")[0]

    grads, deltas, new_m, new_v = {}, {}, {}, {}
    for n in big_names:
        grads[n], deltas[n], new_m[n], new_v[n] = _adamw_sharded(Q[n], w[n], mom[n], var[n], name="adamw_" + n)

    B = conv_w_full.shape[1]
    n_small_rows = sum(w[n].size for n in SMALL) // LANES
    conv_rows = CONV_WIDTH * B // LANES
    d_small, m_small, v_small = _adamw_plain(
        reduced[:n_small_rows], _pack_rows([w[n] for n in SMALL]), _pack_rows([mom[n] for n in SMALL]),
        _pack_rows([var[n] for n in SMALL]), name="adamw_small")
    off = 0
    for n in SMALL:
        rows = w[n].size // LANES
        grads[n] = reduced[off:off + rows].reshape(w[n].shape)
        deltas[n] = d_small[off:off + rows].reshape(w[n].shape)
        new_m[n] = m_small[off:off + rows].reshape(w[n].shape)
        new_v[n] = v_small[off:off + rows].reshape(w[n].shape)
        off += rows
    conv_g_full = reduced[off:off + conv_rows].reshape(CONV_WIDTH, B)
    bs = w['conv_w_dw'].shape[1]
    my_block = 4 * lax.axis_index("x") + 2 * lax.axis_index("y") + lax.axis_index("c")
    grads['conv_w_dw'] = lax.dynamic_slice(conv_g_full, (0, my_block * bs), (CONV_WIDTH, bs))
    deltas['conv_w_dw'], new_m['conv_w_dw'], new_v['conv_w_dw'] = _adamw_plain(
        grads['conv_w_dw'], w['conv_w_dw'], mom['conv_w_dw'], var['conv_w_dw'], name="adamw_conv_w")
    loss = reduced[off + conv_rows, 0]

    def lead(a):
        return a[None]

    return (loss, grad_x[None], *[lead(grads[n]) for n in WEIGHTS], *[lead(deltas[n]) for n in WEIGHTS],
            *[lead(new_m[n]) for n in WEIGHTS], *[lead(new_v[n]) for n in WEIGHTS])
```

```python
import functools
import math

import jax
import jax.numpy as jnp
from jax import lax
from jax.experimental import pallas as pl
from jax.experimental.pallas import tpu as pltpu

F32 = jnp.float32
BF16 = jnp.bfloat16

ALPHA = 2.0 ** 0.25
LN_EPS = 1e-5
CONV_WIDTH = 31
CONV_HALO = 32
CONV_ROWS = 64
CONV_WPAD = 32
CHUNK = 64
GMLP_BLOCK = 128
A_GROUPS = 8
N_DEV = 8
LANES = 128

ADAM_LR = 0.001
ADAM_B1 = 0.9
ADAM_B2 = 0.999
ADAM_EPS = 1e-08
ADAM_WD = 0.01
ADAM_STEP = 10
ADAM_C1 = 1.0 - ADAM_B1 ** ADAM_STEP
ADAM_C2 = 1.0 - ADAM_B2 ** ADAM_STEP

VMEM_LIMIT_BYTES = 60 * 2 ** 20
MESH = pl.DeviceIdType.MESH
ANY = pl.BlockSpec(memory_space=pl.ANY)

WEIGHTS = ['ffn1_w_gu', 'ffn1_w_down', 'ln1_g', 'ln1_b', 'w_in', 'b_in', 'sgu_ln_g', 'sgu_ln_b', 'sgu_w_s',
           'sgu_b_s', 'w_a_proj', 'conv_w_dw', 'conv_b_dw', 'conv_ln_g', 'conv_ln_b', 'w_b_proj', 'w_out',
           'ln2_g', 'ln2_b', 'ffn2_w_gu', 'ffn2_w_down', 'ln3_g', 'ln3_b']
BIG = {'ffn1_w_gu': 1, 'ffn1_w_down': 0, 'w_in': 1, 'w_a_proj': 1, 'w_b_proj': 1, 'w_out': 0,
       'ffn2_w_gu': 1, 'ffn2_w_down': 0}
SMALL = [n for n in WEIGHTS if n not in BIG and n != 'conv_w_dw']


def _tile(n, pref, mult=8):
    best = None
    for d in range(mult, min(n, pref) + 1, mult):
        if n % d == 0:
            best = d
    return n if best is None else best


def _params(*sem):
    return pltpu.CompilerParams(dimension_semantics=sem, vmem_limit_bytes=VMEM_LIMIT_BYTES)


def _dot(a, b):
    return jnp.dot(a, b, preferred_element_type=F32)


def _dot_nt(a, b):
    return lax.dot_general(a, b, (((1,), (1,)), ((), ())), preferred_element_type=F32)


def _dot_tn(a, b):
    return lax.dot_general(a, b, (((0,), (0,)), ((), ())), preferred_element_type=F32)


def _sig(x):
    return 1.0 / (1.0 + jnp.exp(-x))


_GELU_K = math.sqrt(2.0 / math.pi)
_GELU_C = 0.044715


def _gelu(x):
    t = jnp.tanh(_GELU_K * (x + _GELU_C * x * x * x))
    return 0.5 * x * (1.0 + t)


def _gelu_grad(x):
    x2 = x * x
    t = jnp.tanh(_GELU_K * (x + _GELU_C * x2 * x))
    return 0.5 * (1.0 + t) + 0.5 * x * (1.0 - t * t) * (_GELU_K * (1.0 + 3.0 * _GELU_C * x2))


def _ln_stats(r):
    mu = jnp.mean(r, axis=-1, keepdims=True)
    rc = r - mu
    var = jnp.mean(rc * rc, axis=-1, keepdims=True)
    rstd = lax.rsqrt(var + LN_EPS)
    return rc * rstd, rstd


def _ln_bwd(dy, xh, rstd, g):
    dxh = dy * g
    m1 = jnp.mean(dxh, axis=-1, keepdims=True)
    m2 = jnp.mean(dxh * xh, axis=-1, keepdims=True)
    return rstd * (dxh - m1 - xh * m2)


def _colsum(v):
    return jnp.sum(v, axis=0, keepdims=True)


def _chunk_mask(transposed):
    shift = CHUNK.bit_length() - 1
    r = lax.broadcasted_iota(jnp.int32, (GMLP_BLOCK, GMLP_BLOCK), 0) >> shift
    c = lax.broadcasted_iota(jnp.int32, (GMLP_BLOCK, GMLP_BLOCK), 1) >> shift
    return (r <= c) if transposed else (c <= r)


class _Comm:
    def __init__(self, inputs, out_shape, scratch, run, when=(0.0, 1.0)):
        self.inputs, self.out_shape, self.scratch, self.run = list(inputs), list(out_shape), list(scratch), run
        self.when = tuple(when)
        self.parts = [len(self.out_shape)]

    def split(self, outs):
        res, o = [], 0
        for n in self.parts:
            res.append(list(outs[o:o + n]))
            o += n
        return res


def _join(*comms):
    comms = [c for c in comms if c is not None]
    if not comms:
        return None
    assert all(c.when == (0.0, 1.0) for c in comms)

    def run(ins, outs, sems, phase):
        i = o = s = 0
        for c in comms:
            c.run(ins[i:i + len(c.inputs)], outs[o:o + len(c.out_shape)], sems[s:s + len(c.scratch)], phase)
            i, o, s = i + len(c.inputs), o + len(c.out_shape), s + len(c.scratch)

    joined = _Comm(sum((c.inputs for c in comms), []), sum((c.out_shape for c in comms), []),
                   sum((c.scratch for c in comms), []), run)
    joined.parts = [len(c.out_shape) for c in comms]
    return joined


def _call(body, *, name, grid, in_specs, out_specs, out_shape, args, sem, scratch_shapes=(), aliases=None, carry=None):
    in_specs, out_specs, out_shape = list(in_specs), list(out_specs), list(out_shape)
    scratch_shapes = list(scratch_shapes)
    if carry is None:
        return pl.pallas_call(body, name=name, grid=grid, in_specs=in_specs, out_specs=out_specs,
                              out_shape=out_shape, scratch_shapes=scratch_shapes,
                              input_output_aliases=aliases or {}, compiler_params=_params(*sem))(*args)
    n_in, n_out, n_scr = len(args), len(out_shape), len(scratch_shapes)
    c_in, c_out = len(carry.inputs), len(carry.out_shape)
    n_steps = math.prod(grid)
    at_step = [int(round(f * (n_steps - 1))) for f in carry.when]
    assert at_step[0] == 0 and at_step[-1] == n_steps - 1 and at_step == sorted(at_step)

    def wrapped(*refs):
        ins, c_ins = refs[:n_in], refs[n_in:n_in + c_in]
        o0 = n_in + c_in
        outs, c_outs = refs[o0:o0 + n_out], refs[o0 + n_out:o0 + n_out + c_out]
        s0 = o0 + n_out + c_out
        scr, c_sems = refs[s0:s0 + n_scr], refs[s0 + n_scr:]
        step = 0
        for a, g in enumerate(grid):
            step = step * g + pl.program_id(a)
        pl.when(step == 0)(functools.partial(carry.run, c_ins, c_outs, c_sems, 0))
        body(*ins, *outs, *scr)
        for k in range(1, len(at_step)):
            pl.when(step == at_step[k])(functools.partial(carry.run, c_ins, c_outs, c_sems, k))

    res = pl.pallas_call(
        wrapped, name=name, grid=grid, in_specs=in_specs + [ANY] * c_in, out_specs=out_specs + [ANY] * c_out,
        out_shape=out_shape + carry.out_shape, scratch_shapes=scratch_shapes + carry.scratch,
        input_output_aliases=aliases or {},
        compiler_params=pltpu.CompilerParams(dimension_semantics=tuple(sem),
                                             vmem_limit_bytes=VMEM_LIMIT_BYTES, has_side_effects=True),
    )(*args, *carry.inputs)
    return list(res[:n_out]), list(res[n_out:])


def _comm_call(comm, *, name):
    n_in, n_out = len(comm.inputs), len(comm.out_shape)

    def body(*refs):
        ins, outs, sems = refs[:n_in], refs[n_in:n_in + n_out], refs[n_in + n_out:]
        for k in range(len(comm.when)):
            comm.run(ins, outs, sems, k)

    return list(pl.pallas_call(
        body, name=name, in_specs=[ANY] * n_in, out_specs=[ANY] * n_out, out_shape=comm.out_shape,
        scratch_shapes=comm.scratch, compiler_params=pltpu.CompilerParams(has_side_effects=True))(*comm.inputs))


def _row_chunks(tm, rows=128):
    rows = _tile(tm, rows)
    return [slice(r, r + rows) for r in range(0, tm, rows)]


def _ffn_fwd(xh, lg, lb, wgu, wd, *, affine, name, final=None, carry=None):
    T, D = xh.shape
    F = wd.shape[0]
    tm = _tile(T, 512)
    tn = _tile(F, 512, LANES)
    nj = F // tn
    is_final = final is not None
    n_x = 3 if affine else 1

    def body(*refs):
        if is_final:
            (wg_ref, wu_ref, wd_ref, ng_ref, nb_ref, tgt_ref,
             gate_ref, up_ref, xb_sc, dr_ref, loss_ref, dng_ref, dnb_ref, acc_sc) = refs[n_x:]
        else:
            (wg_ref, wu_ref, wd_ref,
             gate_ref, up_ref, xb_sc, xho_ref, rstd_ref, acc_sc) = refs[n_x:]
        xh_ref = refs[0]
        lg_ref, lb_ref = refs[1:n_x] if affine else (None, None)
        i = pl.program_id(0)
        j = pl.program_id(1)

        def xin():
            v = xh_ref[...]
            return v * lg_ref[...] + lb_ref[...] if affine else v

        @pl.when(j == 0)
        def _():
            xb_sc[...] = xin().astype(BF16)
            acc_sc[...] = jnp.zeros_like(acc_sc)

        xb = xb_sc[...]
        g = _dot(xb, wg_ref[...])
        u = _dot(xb, wu_ref[...])
        gate_ref[...] = g.astype(BF16)
        up_ref[...] = u.astype(BF16)
        h = g * _sig(g) * u
        acc_sc[...] += _dot(h.astype(BF16), wd_ref[...])

        @pl.when(j == nj - 1)
        def _():
            if is_final:
                @pl.when(i == 0)
                def _():
                    loss_ref[...] = jnp.zeros_like(loss_ref)
                    dng_ref[...] = jnp.zeros_like(dng_ref)
                    dnb_ref[...] = jnp.zeros_like(dnb_ref)
            for rs in _row_chunks(tm):
                v = xh_ref[rs, :]
                if affine:
                    v = v * lg_ref[...] + lb_ref[...]
                r = ALPHA * v + 0.5 * acc_sc[rs, :]
                xho, rstd = _ln_stats(r)
                if not is_final:
                    xho_ref[rs, :] = xho
                    rstd_ref[rs, :] = jnp.broadcast_to(rstd, (rs.stop - rs.start, LANES))
                else:
                    ng = ng_ref[...]
                    e = xho * ng + nb_ref[...] - tgt_ref[rs, :]
                    part = _colsum(jnp.sum(e * e, axis=1, keepdims=True)) * (0.5 / D)
                    loss_ref[...] += jnp.broadcast_to(part, loss_ref.shape)
                    dy = e * (1.0 / D)
                    dng_ref[...] += _colsum(dy * xho)
                    dnb_ref[...] += _colsum(dy)
                    dr_ref[rs, :] = _ln_bwd(dy, xho, rstd, ng)

    row = pl.BlockSpec((tm, D), lambda i, j: (i, 0))
    vec = pl.BlockSpec((1, D), lambda i, j: (0, 0))
    hid = pl.BlockSpec((tm, tn), lambda i, j: (i, j))
    in_specs = [row, vec, vec][:n_x] + [
        pl.BlockSpec((D, tn), lambda i, j: (0, j)),
        pl.BlockSpec((D, tn), lambda i, j: (0, j + nj)),
        pl.BlockSpec((tn, D), lambda i, j: (j, 0))]
    args = [xh, lg, lb][:n_x] + [wgu, wgu, wd]
    out_shape = [jax.ShapeDtypeStruct((T, F), BF16), jax.ShapeDtypeStruct((T, F), BF16),
                 jax.ShapeDtypeStruct((T, D), BF16)]
    out_specs = [hid, hid, row]
    if is_final:
        in_specs += [vec, vec, row]
        args += list(final)
        out_shape += [jax.ShapeDtypeStruct((T, D), F32), jax.ShapeDtypeStruct((8, LANES), F32),
                      jax.ShapeDtypeStruct((1, D), F32), jax.ShapeDtypeStruct((1, D), F32)]
        out_specs += [row, pl.BlockSpec((8, LANES), lambda i, j: (0, 0)), vec, vec]
        sem = ("arbitrary", "arbitrary")
    else:
        out_shape += [jax.ShapeDtypeStruct((T, D), F32), jax.ShapeDtypeStruct((T, LANES), F32)]
        out_specs += [row, pl.BlockSpec((tm, LANES), lambda i, j: (i, 0))]
        sem = ("parallel", "arbitrary")
    return _call(body, name=name, grid=(T // tm, nj), in_specs=in_specs, out_specs=out_specs, out_shape=out_shape,
                 scratch_shapes=[pltpu.VMEM((tm, D), F32)], sem=sem, args=args, carry=carry)


def _ffn_bwd(dr, gate, up, wgu, wd, *, name, prev=None, carry=None):
    T, D = dr.shape
    F = wd.shape[0]
    tm = _tile(T, 512)
    tn = _tile(F, 512, LANES)
    nj = F // tn
    has_prev = prev is not None

    def body(*refs):
        if has_prev:
            (dr_ref, gate_ref, up_ref, wd_ref, wg_ref, wu_ref, xh_ref, rstd_ref, lg_ref,
             h_ref, dg_ref, du_ref, dprev_ref, dlg_ref, dlb_ref, df_sc, dx_sc) = refs
        else:
            (dr_ref, gate_ref, up_ref, wd_ref, wg_ref, wu_ref,
             h_ref, dg_ref, du_ref, dprev_ref, df_sc, dx_sc) = refs
        i = pl.program_id(0)
        j = pl.program_id(1)

        @pl.when(j == 0)
        def _():
            d = dr_ref[...]
            df_sc[...] = (0.5 * d).astype(BF16)
            dx_sc[...] = ALPHA * d

        g = gate_ref[...].astype(F32)
        u = up_ref[...].astype(F32)
        dh = _dot_nt(df_sc[...], wd_ref[...])
        s = _sig(g)
        sil = g * s
        h_ref[...] = (sil * u).astype(BF16)
        dg = (dh * u * (s * (1.0 + g * (1.0 - s)))).astype(BF16)
        du = (dh * sil).astype(BF16)
        dg_ref[...] = dg
        du_ref[...] = du
        dx_sc[...] += _dot_nt(dg, wg_ref[...]) + _dot_nt(du, wu_ref[...])

        @pl.when(j == nj - 1)
        def _():
            dxin = dx_sc[...]
            if not has_prev:
                dprev_ref[...] = dxin
            else:
                @pl.when(i == 0)
                def _():
                    dlg_ref[...] = jnp.zeros_like(dlg_ref)
                    dlb_ref[...] = jnp.zeros_like(dlb_ref)
                xh = xh_ref[...]
                dlg_ref[...] += _colsum(dxin * xh)
                dlb_ref[...] += _colsum(dxin)
                dprev_ref[...] = _ln_bwd(dxin, xh, rstd_ref[:, 0:1], lg_ref[...])

    row = pl.BlockSpec((tm, D), lambda i, j: (i, 0))
    vec = pl.BlockSpec((1, D), lambda i, j: (0, 0))
    hid = pl.BlockSpec((tm, tn), lambda i, j: (i, j))
    in_specs = [row, hid, hid,
                pl.BlockSpec((tn, D), lambda i, j: (j, 0)),
                pl.BlockSpec((D, tn), lambda i, j: (0, j)),
                pl.BlockSpec((D, tn), lambda i, j: (0, j + nj))]
    args = [dr, gate, up, wd, wgu, wgu]
    out_shape = [jax.ShapeDtypeStruct((T, F), BF16)] * 3 + [jax.ShapeDtypeStruct((T, D), F32)]
    out_specs = [hid, hid, hid, row]
    if has_prev:
        in_specs += [row, pl.BlockSpec((tm, LANES), lambda i, j: (i, 0)), vec]
        args += list(prev)
        out_shape += [jax.ShapeDtypeStruct((1, D), F32)] * 2
        out_specs += [vec, vec]
        sem = ("arbitrary", "arbitrary")
    else:
        sem = ("parallel", "arbitrary")
    return _call(body, name=name, grid=(T // tm, nj), in_specs=in_specs, out_specs=out_specs, out_shape=out_shape,
                 scratch_shapes=[pltpu.VMEM((tm, D), BF16), pltpu.VMEM((tm, D), F32)], sem=sem, args=args,
                 carry=carry)


def _mm_tn(a, b, *, name, tm_pref, tn_pref, tk_pref=1024, scale=1.0, a_affine=None, into=None, col_off=0,
           n_total=None, carry=None):
    T, M = a.shape
    N = b.shape[1]
    n_total = N if n_total is None else n_total
    tM = _tile(M, tm_pref, LANES)
    tN = _tile(N, tn_pref, LANES)
    tk = _tile(T, tk_pref)
    nt = T // tk
    assert col_off % tN == 0
    off_blocks = col_off // tN
    has_aff = a_affine is not None
    has_into = into is not None

    def body(*refs):
        refs = list(refs)
        a_ref = refs.pop(0)
        if has_aff:
            lg_ref = refs.pop(0)
            lb_ref = refs.pop(0)
        b_ref = refs.pop(0)
        if has_into:
            refs.pop(0)
        o_ref, acc_sc = refs
        t = pl.program_id(2)

        @pl.when(t == 0)
        def _():
            acc_sc[...] = jnp.zeros_like(acc_sc)

        av = a_ref[...]
        if has_aff:
            av = av * lg_ref[...] + lb_ref[...]
        acc_sc[...] += _dot_tn(av.astype(BF16), b_ref[...].astype(BF16))

        @pl.when(t == nt - 1)
        def _():
            o_ref[...] = (acc_sc[...] * scale).astype(BF16)

    in_specs = [pl.BlockSpec((tk, tM), lambda m, n, t: (t, m))]
    args = [a]
    if has_aff:
        in_specs += [pl.BlockSpec((1, tM), lambda m, n, t: (0, m))] * 2
        args += list(a_affine)
    in_specs.append(pl.BlockSpec((tk, tN), lambda m, n, t: (t, n)))
    args.append(b)
    aliases = {}
    if has_into:
        aliases = {len(args): 0}
        in_specs.append(ANY)
        args.append(into)
    res = _call(body, name=name, grid=(M // tM, N // tN, nt), in_specs=in_specs,
                out_specs=[pl.BlockSpec((tM, tN), lambda m, n, t: (m, n + off_blocks))],
                out_shape=[jax.ShapeDtypeStruct((M, n_total), BF16)],
                scratch_shapes=[pltpu.VMEM((tM, tN), F32)], aliases=aliases,
                sem=("parallel", "parallel", "arbitrary"), args=args, carry=carry)
    return res[0] if carry is None else (res[0][0], res[1])


def _inproj_fwd(xh, lg, lb, w, bias, *, name, carry=None):
    T, D = xh.shape
    N = w.shape[1]
    tm = _tile(T, 1024)
    tn = _tile(N, 1024, LANES)

    def body(xh_ref, lg_ref, lb_ref, w_ref, b_ref, o_ref, xb_ref):
        @pl.when(pl.program_id(1) == 0)
        def _():
            xb_ref[...] = (xh_ref[...] * lg_ref[...] + lb_ref[...]).astype(BF16)
        o_ref[...] = _dot(xb_ref[...], w_ref[...]) + b_ref[...]

    return _call(
        body, name=name, grid=(T // tm, N // tn),
        in_specs=[pl.BlockSpec((tm, D), lambda i, j: (i, 0)),
                  pl.BlockSpec((1, D), lambda i, j: (0, 0)), pl.BlockSpec((1, D), lambda i, j: (0, 0)),
                  pl.BlockSpec((D, tn), lambda i, j: (0, j)), pl.BlockSpec((1, tn), lambda i, j: (0, j))],
        out_specs=[pl.BlockSpec((tm, tn), lambda i, j: (i, j)), pl.BlockSpec((tm, D), lambda i, j: (i, 0))],
        out_shape=[jax.ShapeDtypeStruct((T, N), F32), jax.ShapeDtypeStruct((T, D), BF16)],
        sem=("parallel", "arbitrary"), args=[xh, lg, lb, w, bias], carry=carry)


def _inproj_bwd(dproj, w, dr_next, xh, rstd, lg, *, name, carry=None):
    T, N = dproj.shape
    D = w.shape[0]
    tm = _tile(T, 512)
    tn = _tile(N, 2048, LANES)
    nj = N // tn

    def body(dp_ref, w_ref, drn_ref, xh_ref, rstd_ref, lg_ref, dprev_ref, dlg_ref, dlb_ref, dx_sc):
        i = pl.program_id(0)
        j = pl.program_id(1)

        @pl.when(j == 0)
        def _():
            for rs in _row_chunks(tm):
                dx_sc[rs, :] = ALPHA * drn_ref[rs, :]

        dx_sc[...] += _dot_nt(dp_ref[...], w_ref[...])

        @pl.when(j == nj - 1)
        def _():
            @pl.when(i == 0)
            def _():
                dlg_ref[...] = jnp.zeros_like(dlg_ref)
                dlb_ref[...] = jnp.zeros_like(dlb_ref)
            for rs in _row_chunks(tm):
                dx = dx_sc[rs, :]
                x_hat = xh_ref[rs, :]
                dlg_ref[...] += _colsum(dx * x_hat)
                dlb_ref[...] += _colsum(dx)
                dprev_ref[rs, :] = _ln_bwd(dx, x_hat, rstd_ref[rs, 0:1], lg_ref[...])

    row = pl.BlockSpec((tm, D), lambda i, j: (i, 0))
    vec = pl.BlockSpec((1, D), lambda i, j: (0, 0))
    return _call(
        body, name=name, grid=(T // tm, nj),
        in_specs=[pl.BlockSpec((tm, tn), lambda i, j: (i, j)), pl.BlockSpec((D, tn), lambda i, j: (0, j)),
                  row, row, pl.BlockSpec((tm, LANES), lambda i, j: (i, 0)), vec],
        out_specs=[row, vec, vec],
        out_shape=[jax.ShapeDtypeStruct((T, D), F32), jax.ShapeDtypeStruct((1, D), F32),
                   jax.ShapeDtypeStruct((1, D), F32)],
        scratch_shapes=[pltpu.VMEM((tm, D), F32)], sem=("arbitrary", "arbitrary"),
        args=[dproj, w, dr_next, xh, rstd, lg], carry=carry)


def _sgu_fwd(proj, ln_g, ln_b, w_s, b_sb, *, name):
    T = proj.shape[0]
    A = proj.shape[1] // 8
    hd = A // A_GROUPS
    tm = _tile(T, 512, GMLP_BLOCK)

    def body(u_ref, v_ref, g_ref, b_ref, ws_ref, bs_ref, o_ref):
        gu = _gelu(u_ref[...])
        vh, _ = _ln_stats(_gelu(v_ref[...]))
        vn = (vh * g_ref[...] + b_ref[...]).astype(BF16)
        mask = _chunk_mask(False)
        for h in range(A_GROUPS):
            wm = jnp.where(mask, ws_ref[h], 0.0).astype(BF16)
            cols = slice(h * hd, (h + 1) * hd)
            for n in range(tm // GMLP_BLOCK):
                rows = slice(n * GMLP_BLOCK, (n + 1) * GMLP_BLOCK)
                s = _dot(wm, vn[rows, cols]) + bs_ref[h][:, :hd]
                o_ref[rows, cols] = (gu[rows, cols] * s).astype(BF16)

    vec = pl.BlockSpec((1, A), lambda i: (0, 0))
    full = pl.BlockSpec((A_GROUPS, GMLP_BLOCK, GMLP_BLOCK), lambda i: (0, 0, 0))
    return pl.pallas_call(
        body, name=name, grid=(T // tm,),
        in_specs=[pl.BlockSpec((tm, A), lambda i: (i, 0)), pl.BlockSpec((tm, A), lambda i: (i, 1)),
                  vec, vec, full, full],
        out_specs=pl.BlockSpec((tm, A), lambda i: (i, 0)),
        out_shape=jax.ShapeDtypeStruct((T, A), BF16),
        compiler_params=_params("parallel"))(proj, proj, ln_g, ln_b, w_s, b_sb)


def _sgu_bwd(proj, dsg, ln_g, ln_b, w_s, w_st, b_sb, dproj, *, name):
    T = proj.shape[0]
    A = proj.shape[1] // 8
    hd = A // A_GROUPS
    tm = _tile(T, 512, GMLP_BLOCK)
    nt = T // tm

    def body(u_ref, v_ref, dsg_ref, g_ref, b_ref, ws_ref, wst_ref, bs_ref, _alias,
             dp_ref, dbin_ref, dlg_ref, dlb_ref, dws_ref, dbs_ref, dvn_sc, dgu_sc, dbs_sc):
        i = pl.program_id(0)

        @pl.when(i == 0)
        def _():
            dbin_ref[...] = jnp.zeros_like(dbin_ref)
            dlg_ref[...] = jnp.zeros_like(dlg_ref)
            dlb_ref[...] = jnp.zeros_like(dlb_ref)
            dws_ref[...] = jnp.zeros_like(dws_ref)
            dbs_sc[...] = jnp.zeros_like(dbs_sc)

        u = u_ref[...]
        v = v_ref[...]
        gu = _gelu(u)
        vh, rstd = _ln_stats(_gelu(v))
        gain = g_ref[...]
        vn = (vh * gain + b_ref[...]).astype(BF16)
        dsg_v = dsg_ref[...]
        mask = _chunk_mask(False)
        mask_t = _chunk_mask(True)
        for h in range(A_GROUPS):
            wm = jnp.where(mask, ws_ref[h], 0.0).astype(BF16)
            wmt = jnp.where(mask_t, wst_ref[h], 0.0).astype(BF16)
            cols = slice(h * hd, (h + 1) * hd)
            for n in range(tm // GMLP_BLOCK):
                rows = slice(n * GMLP_BLOCK, (n + 1) * GMLP_BLOCK)
                vb = vn[rows, cols]
                s = _dot(wm, vb) + bs_ref[h][:, :hd]
                d_out = dsg_v[rows, cols]
                dgu_sc[rows, cols] = d_out * s
                ds = d_out * gu[rows, cols]
                ds_b = ds.astype(BF16)
                dws_ref[h] += _dot_nt(ds_b, vb)
                dbs_sc[h] += ds
                dvn_sc[rows, cols] = _dot(wmt, ds_b)
        dvn = dvn_sc[...]
        dlg_ref[...] += _colsum(dvn * vh)
        dlb_ref[...] += _colsum(dvn)
        dv = _ln_bwd(dvn, vh, rstd, gain) * _gelu_grad(v)
        du = dgu_sc[...] * _gelu_grad(u)
        dp_ref[:, 0:A] = du.astype(BF16)
        dp_ref[:, A:2 * A] = dv.astype(BF16)
        dbin_ref[:, 0:A] += _colsum(du)
        dbin_ref[:, A:2 * A] += _colsum(dv)

        @pl.when(i == nt - 1)
        def _():
            for h in range(A_GROUPS):
                dws_ref[h] = jnp.where(mask, dws_ref[h], 0.0)
                dbs_ref[h:h + 1, :] = _colsum(dbs_sc[h].T)

    vec = pl.BlockSpec((1, A), lambda i: (0, 0))
    full = pl.BlockSpec((A_GROUPS, GMLP_BLOCK, GMLP_BLOCK), lambda i: (0, 0, 0))
    tile = pl.BlockSpec((tm, A), lambda i: (i, 0))
    return pl.pallas_call(
        body, name=name, grid=(nt,),
        in_specs=[tile, pl.BlockSpec((tm, A), lambda i: (i, 1)), tile, vec, vec, full, full, full, ANY],
        out_specs=[pl.BlockSpec((tm, 2 * A), lambda i: (i, 0)), pl.BlockSpec((1, 2 * A), lambda i: (0, 0)),
                   vec, vec, full, pl.BlockSpec((A_GROUPS, GMLP_BLOCK), lambda i: (0, 0))],
        out_shape=[jax.ShapeDtypeStruct(dproj.shape, BF16), jax.ShapeDtypeStruct((1, 2 * A), F32),
                   jax.ShapeDtypeStruct((1, A), F32), jax.ShapeDtypeStruct((1, A), F32),
                   jax.ShapeDtypeStruct((A_GROUPS, GMLP_BLOCK, GMLP_BLOCK), F32),
                   jax.ShapeDtypeStruct((A_GROUPS, GMLP_BLOCK), F32)],
        scratch_shapes=[pltpu.VMEM((tm, A), F32), pltpu.VMEM((tm, A), F32),
                        pltpu.VMEM((A_GROUPS, GMLP_BLOCK, hd), F32)],
        input_output_aliases={8: 0},
        compiler_params=_params("arbitrary"))(proj, proj, dsg, ln_g, ln_b, w_s, w_st, b_sb, dproj)


def _conv_tiles(T, B):
    tm = _tile(T, 256, CONV_ROWS)
    lb = min(LANES, B)
    return tm, tm // CONV_HALO, lb


def _fill_phases(src, dst, B, lb):
    rows = dst.shape[1]
    for p in range(1, 8):
        for cb in range(B // lb):
            ls = slice(cb * lb, (cb + 1) * lb)
            dst[p - 1, :, ls] = src[p:p + rows, ls]


def _tap_sums(w_ref, src, phases, ls, tm, offset_of_tap):
    lb = ls.stop - ls.start
    n_rc = tm // CONV_ROWS
    accs = [jnp.zeros((CONV_ROWS // 8, 8, lb), F32) for _ in range(n_rc)]
    for k in range(CONV_WIDTH):
        wk = jnp.broadcast_to(w_ref[k:k + 1, ls], (8, lb))[None]
        for rc in range(n_rc):
            win = _shifted(src, phases, rc * CONV_ROWS + offset_of_tap(k), ls)
            accs[rc] = accs[rc] + wk * win.reshape(CONV_ROWS // 8, 8, lb)
    return [a.reshape(CONV_ROWS, lb) for a in accs]


def _shifted(src, phases, off, ls):
    m, p = divmod(off, 8)
    if p == 0:
        return src[off:off + CONV_ROWS, ls]
    return phases[p - 1, 8 * m:8 * m + CONV_ROWS, ls]


def _conv_fwd(proj, w_dw, b_dw, ln_g, ln_b, *, name):
    T = proj.shape[0]
    B = proj.shape[1] // 8
    tm, nh, lb = _conv_tiles(T, B)

    def body(ap_ref, gp_ref, a_ref, g_ref, w_ref, bdw_ref, lg_ref, lb_ref, c_ref, cv_ref, z_sc, zp_sc):
        i = pl.program_id(0)
        z_sc[0:CONV_HALO, :] = jnp.where(i > 0, ap_ref[...] * _sig(gp_ref[...]), 0.0)
        z_sc[CONV_HALO:CONV_HALO + tm, :] = a_ref[...] * _sig(g_ref[...])
        _fill_phases(z_sc, zp_sc, B, lb)
        for cb in range(B // lb):
            ls = slice(cb * lb, (cb + 1) * lb)
            accs = _tap_sums(w_ref, z_sc, zp_sc, ls, tm, lambda k: CONV_HALO - (CONV_WIDTH - 1) + k)
            for rc, acc in enumerate(accs):
                c_ref[rc * CONV_ROWS:(rc + 1) * CONV_ROWS, ls] = acc + bdw_ref[:, ls]
        xh, _ = _ln_stats(c_ref[...])
        y = xh * lg_ref[...] + lb_ref[...]
        cv_ref[...] = (y * _sig(y)).astype(BF16)

    vec = pl.BlockSpec((1, B), lambda i: (0, 0))
    halo_a = pl.BlockSpec((CONV_HALO, B), lambda i: (jnp.maximum(i * nh - 1, 0), 2))
    halo_g = pl.BlockSpec((CONV_HALO, B), lambda i: (jnp.maximum(i * nh - 1, 0), 3))
    return pl.pallas_call(
        body, name=name, grid=(T // tm,),
        in_specs=[halo_a, halo_g, pl.BlockSpec((tm, B), lambda i: (i, 2)), pl.BlockSpec((tm, B), lambda i: (i, 3)),
                  pl.BlockSpec((CONV_WPAD, B), lambda i: (0, 0)), vec, vec, vec],
        out_specs=[pl.BlockSpec((tm, B), lambda i: (i, 0))] * 2,
        out_shape=[jax.ShapeDtypeStruct((T, B), F32), jax.ShapeDtypeStruct((T, B), BF16)],
        scratch_shapes=[pltpu.VMEM((CONV_HALO + tm, B), F32), pltpu.VMEM((7, CONV_HALO + tm - 8, B), F32)],
        compiler_params=_params("parallel"))(proj, proj, proj, proj, w_dw, b_dw, ln_g, ln_b)


def _conv_bwd_ln(dcv, c, ln_g, ln_b, *, name):
    T, B = c.shape
    tm = _tile(T, 512)

    def body(dcv_ref, c_ref, lg_ref, lb_ref, dc_ref, dlg_ref, dlb_ref, dbdw_ref):
        @pl.when(pl.program_id(0) == 0)
        def _():
            dlg_ref[...] = jnp.zeros_like(dlg_ref)
            dlb_ref[...] = jnp.zeros_like(dlb_ref)
            dbdw_ref[...] = jnp.zeros_like(dbdw_ref)
        gain = lg_ref[...]
        xh, rstd = _ln_stats(c_ref[...])
        y = xh * gain + lb_ref[...]
        s = _sig(y)
        dy = dcv_ref[...] * (s * (1.0 + y * (1.0 - s)))
        dlg_ref[...] += _colsum(dy * xh)
        dlb_ref[...] += _colsum(dy)
        dc = _ln_bwd(dy, xh, rstd, gain)
        dc_ref[...] = dc
        dbdw_ref[...] += _colsum(dc)

    tile = pl.BlockSpec((tm, B), lambda i: (i, 0))
    vec = pl.BlockSpec((1, B), lambda i: (0, 0))
    return pl.pallas_call(
        body, name=name, grid=(T // tm,), in_specs=[tile, tile, vec, vec], out_specs=[tile, vec, vec, vec],
        out_shape=[jax.ShapeDtypeStruct((T, B), F32)] + [jax.ShapeDtypeStruct((1, B), F32)] * 3,
        compiler_params=_params("arbitrary"))(dcv, c, ln_g, ln_b)


def _conv_bwd(proj, dc, w_dw, dproj, *, name, carry=None):
    T = proj.shape[0]
    B = proj.shape[1] // 8
    tm, nh, lb = _conv_tiles(T, B)
    nt = T // tm
    n_halo = T // CONV_HALO

    def body(ap_ref, gp_ref, a_ref, g_ref, dc_ref, dcn_ref, w_ref, _alias,
             dp_ref, dbin_ref, dw_ref, z_sc, dc_sc, dz_sc, dw_sc, zp_sc, dcp_sc):
        i = pl.program_id(0)

        @pl.when(i == 0)
        def _():
            dbin_ref[...] = jnp.zeros_like(dbin_ref)
            dw_sc[...] = jnp.zeros_like(dw_sc)

        a = a_ref[...]
        s = _sig(g_ref[...])
        z_sc[0:CONV_HALO, :] = jnp.where(i > 0, ap_ref[...] * _sig(gp_ref[...]), 0.0)
        z_sc[CONV_HALO:CONV_HALO + tm, :] = a * s
        dc_sc[0:tm, :] = dc_ref[...]
        dc_sc[tm:tm + CONV_HALO, :] = jnp.where(i < nt - 1, dcn_ref[...], 0.0)
        _fill_phases(z_sc, zp_sc, B, lb)
        _fill_phases(dc_sc, dcp_sc, B, lb)
        for cb in range(B // lb):
            ls = slice(cb * lb, (cb + 1) * lb)
            for rc in range(tm // CONV_ROWS):
                r0 = rc * CONV_ROWS
                acc = jnp.zeros((CONV_ROWS, lb), F32)
                for k in range(CONV_WIDTH):
                    acc = acc + w_ref[k:k + 1, ls] * _shifted(dc_sc, dcp_sc, r0 + (CONV_WIDTH - 1) - k, ls)
                dz_sc[r0:r0 + CONV_ROWS, ls] = acc
            for k in range(CONV_WIDTH):
                part = jnp.zeros((8, lb), F32)
                for rc in range(tm // CONV_ROWS):
                    r0 = rc * CONV_ROWS
                    prod = dc_sc[r0:r0 + CONV_ROWS, ls] * _shifted(
                        z_sc, zp_sc, r0 + CONV_HALO - (CONV_WIDTH - 1) + k, ls)
                    part = part + jnp.sum(prod.reshape(CONV_ROWS // 8, 8, lb), axis=0)
                dw_sc[8 * k:8 * k + 8, ls] += part
        dz = dz_sc[...]
        da = dz * s
        dg = dz * a * s * (1.0 - s)
        dp_ref[:, 0:B] = da.astype(BF16)
        dp_ref[:, B:2 * B] = dg.astype(BF16)
        dbin_ref[:, 0:B] += _colsum(da)
        dbin_ref[:, B:2 * B] += _colsum(dg)

        @pl.when(i == nt - 1)
        def _():
            for k in range(CONV_WIDTH):
                dw_ref[k:k + 1, :] = _colsum(dw_sc[8 * k:8 * k + 8, :])

    halo_a = pl.BlockSpec((CONV_HALO, B), lambda i: (jnp.maximum(i * nh - 1, 0), 2))
    halo_g = pl.BlockSpec((CONV_HALO, B), lambda i: (jnp.maximum(i * nh - 1, 0), 3))
    halo_dc = pl.BlockSpec((CONV_HALO, B), lambda i: (jnp.minimum((i + 1) * nh, n_halo - 1), 0))
    return _call(
        body, name=name, grid=(nt,),
        in_specs=[halo_a, halo_g, pl.BlockSpec((tm, B), lambda i: (i, 2)), pl.BlockSpec((tm, B), lambda i: (i, 3)),
                  pl.BlockSpec((tm, B), lambda i: (i, 0)), halo_dc,
                  pl.BlockSpec((CONV_WPAD, B), lambda i: (0, 0)), ANY],
        out_specs=[pl.BlockSpec((tm, 2 * B), lambda i: (i, 1)), pl.BlockSpec((1, 2 * B), lambda i: (0, 0)),
                   pl.BlockSpec((CONV_WIDTH, B), lambda i: (0, 0))],
        out_shape=[jax.ShapeDtypeStruct(dproj.shape, BF16), jax.ShapeDtypeStruct((1, 2 * B), F32),
                   jax.ShapeDtypeStruct((CONV_WIDTH, B), F32)],
        scratch_shapes=[pltpu.VMEM((CONV_HALO + tm, B), F32), pltpu.VMEM((tm + CONV_HALO, B), F32),
                        pltpu.VMEM((tm, B), F32), pltpu.VMEM((8 * CONV_WIDTH, B), F32),
                        pltpu.VMEM((7, CONV_HALO + tm - 8, B), F32), pltpu.VMEM((7, CONV_HALO + tm - 8, B), F32)],
        aliases={7: 0}, sem=("arbitrary",), args=[proj, proj, proj, proj, dc, dc, w_dw, dproj], carry=carry)


def _mix_fwd_gate(sg, cv, proj, wa, wb, *, name):
    T, A = sg.shape
    D = wa.shape[1]
    tm = _tile(T, 256)

    def body(sg_ref, cv_ref, la_ref, lb_ref, wa_ref, wb_ref, ya_ref, yb_ref, m_ref):
        ya = _dot(sg_ref[...], wa_ref[...])
        yb = _dot(cv_ref[...], wb_ref[...])
        ya_ref[...] = ya.astype(BF16)
        yb_ref[...] = yb.astype(BF16)
        m_ref[...] = (_sig(la_ref[...]) * ya + _sig(lb_ref[...]) * yb).astype(BF16)

    act = pl.BlockSpec((tm, A), lambda i: (i, 0))
    wide = pl.BlockSpec((tm, D), lambda i: (i, 0))
    wspec = pl.BlockSpec((A, D), lambda i: (0, 0))
    return pl.pallas_call(
        body, name=name, grid=(T // tm,),
        in_specs=[act, act, pl.BlockSpec((tm, D), lambda i: (i, 2)), pl.BlockSpec((tm, D), lambda i: (i, 3)),
                  wspec, wspec],
        out_specs=[wide] * 3, out_shape=[jax.ShapeDtypeStruct((T, D), BF16)] * 3,
        compiler_params=_params("parallel"))(sg, cv, proj, proj, wa, wb)


def _mix_fwd_out(m, wout, xh, lg, lb, *, name):
    T, D = xh.shape
    tm = _tile(T, 512)

    def body(m_ref, w_ref, xh_ref, lg_ref, lb_ref, xho_ref, rstd_ref):
        r = ALPHA * (xh_ref[...] * lg_ref[...] + lb_ref[...]) + _dot(m_ref[...], w_ref[...])
        xho, rstd = _ln_stats(r)
        xho_ref[...] = xho
        rstd_ref[...] = jnp.broadcast_to(rstd, (tm, LANES))

    row = pl.BlockSpec((tm, D), lambda i: (i, 0))
    vec = pl.BlockSpec((1, D), lambda i: (0, 0))
    return pl.pallas_call(
        body, name=name, grid=(T // tm,),
        in_specs=[row, pl.BlockSpec((D, D), lambda i: (0, 0)), row, vec, vec],
        out_specs=[row, pl.BlockSpec((tm, LANES), lambda i: (i, 0))],
        out_shape=[jax.ShapeDtypeStruct((T, D), F32), jax.ShapeDtypeStruct((T, LANES), F32)],
        compiler_params=_params("parallel"))(m, wout, xh, lg, lb)


def _mix_bwd_gate(dr, wout, proj, ya, yb, *, name, carry=None):
    T, D = dr.shape
    N = proj.shape[1]
    tm = _tile(T, 256)

    def body(dr_ref, w_ref, la_ref, lb_ref, ya_ref, yb_ref, dya_ref, dyb_ref, dp_ref, dbin_ref):
        @pl.when(pl.program_id(0) == 0)
        def _():
            dbin_ref[...] = jnp.zeros_like(dbin_ref)
        dm = _dot_nt(dr_ref[...].astype(BF16), w_ref[...])
        sa = _sig(la_ref[...])
        sb = _sig(lb_ref[...])
        dya_ref[...] = (dm * sa).astype(BF16)
        dyb_ref[...] = (dm * sb).astype(BF16)
        dla = dm * ya_ref[...].astype(F32) * sa * (1.0 - sa)
        dlb = dm * yb_ref[...].astype(F32) * sb * (1.0 - sb)
        dp_ref[:, 0:D] = dla.astype(BF16)
        dp_ref[:, D:2 * D] = dlb.astype(BF16)
        dbin_ref[:, 0:D] += _colsum(dla)
        dbin_ref[:, D:2 * D] += _colsum(dlb)

    row = pl.BlockSpec((tm, D), lambda i: (i, 0))
    return _call(
        body, name=name, grid=(T // tm,),
        in_specs=[row, pl.BlockSpec((D, D), lambda i: (0, 0)), pl.BlockSpec((tm, D), lambda i: (i, 2)),
                  pl.BlockSpec((tm, D), lambda i: (i, 3)), row, row],
        out_specs=[row, row, pl.BlockSpec((tm, 2 * D), lambda i: (i, 1)), pl.BlockSpec((1, 2 * D), lambda i: (0, 0))],
        out_shape=[jax.ShapeDtypeStruct((T, D), BF16), jax.ShapeDtypeStruct((T, D), BF16),
                   jax.ShapeDtypeStruct((T, N), BF16), jax.ShapeDtypeStruct((1, 2 * D), F32)],
        sem=("arbitrary",), args=[dr, wout, proj, proj, ya, yb], carry=carry)


def _mix_bwd_proj(dya, dyb, wa, wb, *, name):
    T, D = dya.shape
    A = wa.shape[0]
    tm = _tile(T, 512)

    def body(dya_ref, dyb_ref, wa_ref, wb_ref, dsg_ref, dcv_ref):
        dsg_ref[...] = _dot_nt(dya_ref[...], wa_ref[...])
        dcv_ref[...] = _dot_nt(dyb_ref[...], wb_ref[...])

    row = pl.BlockSpec((tm, D), lambda i: (i, 0))
    wspec = pl.BlockSpec((A, D), lambda i: (0, 0))
    act = pl.BlockSpec((tm, A), lambda i: (i, 0))
    return pl.pallas_call(
        body, name=name, grid=(T // tm,), in_specs=[row, row, wspec, wspec], out_specs=[act, act],
        out_shape=[jax.ShapeDtypeStruct((T, A), F32)] * 2,
        compiler_params=_params("parallel"))(dya, dyb, wa, wb)


def _mesh_pos():
    return lax.axis_index("x"), lax.axis_index("y"), lax.axis_index("c")


def _shard_view(ref, p, shape, axis):
    r, c = shape
    if axis == 0:
        return ref.at[pl.ds(pl.multiple_of(p * r, 16), r), :]
    return ref.at[:, pl.ds(pl.multiple_of(p * c, LANES), c)]


def _all_gather(shards, axes, progressive=False, busy=0.93):
    n = len(shards)
    shapes = [s.shape for s in shards]
    sizes = [s.size * s.dtype.itemsize for s in shards]
    if progressive:
        done = [sum(sizes[:t + 1]) / sum(sizes) for t in range(n)]
        when = (0.0,) + tuple(min(0.97, 0.04 + busy * d) for d in done) + (1.0,)
    else:
        when = (0.0, 1.0)

    def run(ins, outs, sems, phase):
        send_sems, recv_sems, local_sems = sems
        x, y, c = _mesh_pos()
        me, sibling = (x, y, c), (x, y, 1 - c)
        chips = [(1 - x, y), (x, 1 - y), (1 - x, 1 - y)]

        def view(t, pos):
            px, py, pc = pos
            return _shard_view(outs[t], 4 * px + 2 * py + pc, shapes[t], axes[t])

        def copy(t, k, block, to, src=None):
            return pltpu.make_async_remote_copy(
                src_ref=view(t, block) if src is None else src, dst_ref=view(t, block),
                send_sem=send_sems.at[7 * t + k], recv_sem=recv_sems.at[7 * t + k],
                device_id=to, device_id_type=MESH)

        mine = [pltpu.make_async_copy(ins[t], view(t, me), local_sems.at[t]) for t in range(n)]
        first = []
        for t in range(n):
            first.append(copy(t, 0, me, sibling, src=ins[t]))
            first += [copy(t, 1 + j, me, (*chip, c), src=ins[t]) for j, chip in enumerate(chips)]
        if phase == 0:
            for cp in mine + first:
                cp.start()
            return

        def forward(t):
            for j, chip in enumerate(chips):
                copy(t, 1 + j, (*chip, c), me).wait_recv()
                copy(t, 4 + j, (*chip, c), sibling).start()

        if progressive and phase <= n:
            forward(phase - 1)
            return
        if not progressive:
            for t in range(n):
                forward(t)
        passed = [copy(t, 4 + j, (*chip, c), sibling) for t in range(n) for j, chip in enumerate(chips)]
        for t in range(n):
            copy(t, 0, sibling, me).wait_recv()
            for j, chip in enumerate(chips):
                copy(t, 4 + j, (*chip, 1 - c), me).wait_recv()
        for cp in first + passed:
            cp.wait_send()
        for cp in mine:
            cp.wait()

    out_shape = [jax.ShapeDtypeStruct((N_DEV * s.shape[0], s.shape[1]) if ax == 0
                                      else (s.shape[0], N_DEV * s.shape[1]), s.dtype)
                 for s, ax in zip(shards, axes)]
    return _Comm(shards, out_shape, [pltpu.SemaphoreType.DMA((7 * n,)), pltpu.SemaphoreType.DMA((7 * n,)),
                                     pltpu.SemaphoreType.DMA((n,))], run, when)


def _all_gather_routed(shards, axes):
    n = len(shards)
    shapes = [s.shape for s in shards]
    assert all(s[0] % 32 == 0 for s in shapes)

    def run(ins, outs, sems, phase):
        send_sems, recv_sems, local_sems = sems
        x, y, c = _mesh_pos()
        me, sibling = (x, y, c), (x, y, 1 - c)
        nx, ny, nd = (1 - x, y), (x, 1 - y), (1 - x, 1 - y)

        def block(t, chip, core):
            return _shard_view(outs[t], 4 * chip[0] + 2 * chip[1] + core, shapes[t], axes[t])

        def half(t, chip, core, h):
            hr = shapes[t][0] // 2
            return block(t, chip, core).at[pl.ds(h * hr, hr), :]

        def own_half(t, h):
            hr = shapes[t][0] // 2
            return ins[t].at[pl.ds(h * hr, hr), :]

        def copy(t, k, src, dst, to):
            return pltpu.make_async_remote_copy(src_ref=src, dst_ref=dst, send_sem=send_sems.at[10 * t + k],
                                                recv_sem=recv_sems.at[10 * t + k], device_id=to, device_id_type=MESH)

        def arrived(t, k, dst):
            copy(t, k, dst, dst, me).wait_recv()

        mine = [pltpu.make_async_copy(ins[t], block(t, (x, y), c), local_sems.at[t]) for t in range(n)]
        own = []
        for t in range(n):
            own += [copy(t, 0, ins[t], block(t, (x, y), c), sibling),
                    copy(t, 1, own_half(t, 0), half(t, (x, y), c, 0), (*nx, c)),
                    copy(t, 2, own_half(t, 1), half(t, (x, y), c, 1), (*nx, c)),
                    copy(t, 3, own_half(t, 1), half(t, (x, y), c, 1), (*ny, c)),
                    copy(t, 4, own_half(t, 0), half(t, (x, y), c, 0), (*ny, c))]
        if phase == 0:
            for cp in mine + own:
                cp.start()
            return
        relays = []

        def relay(t, k, view, to):
            cp = copy(t, k, view, view, to)
            cp.start()
            relays.append(cp)

        for t in range(n):
            arrived(t, 1, half(t, nx, c, 0))
            relay(t, 5, half(t, nx, c, 0), (*ny, c))
            arrived(t, 3, half(t, ny, c, 1))
            relay(t, 6, half(t, ny, c, 1), (*nx, c))
            arrived(t, 2, half(t, nx, c, 1))
            relay(t, 7, block(t, nx, c), sibling)
            arrived(t, 4, half(t, ny, c, 0))
            relay(t, 8, block(t, ny, c), sibling)
            arrived(t, 5, half(t, nd, c, 0))
            arrived(t, 6, half(t, nd, c, 1))
            relay(t, 9, block(t, nd, c), sibling)
        for t in range(n):
            arrived(t, 0, block(t, (x, y), 1 - c))
            arrived(t, 7, block(t, nx, 1 - c))
            arrived(t, 8, block(t, ny, 1 - c))
            arrived(t, 9, block(t, nd, 1 - c))
        for cp in own + relays:
            cp.wait_send()
        for cp in mine:
            cp.wait()

    out_shape = [jax.ShapeDtypeStruct((N_DEV * s.shape[0], s.shape[1]) if ax == 0
                                      else (s.shape[0], N_DEV * s.shape[1]), s.dtype)
                 for s, ax in zip(shards, axes)]
    return _Comm(shards, out_shape, [pltpu.SemaphoreType.DMA((10 * n,)), pltpu.SemaphoreType.DMA((10 * n,)),
                                     pltpu.SemaphoreType.DMA((n,))], run)


def _rs_to_sibling(grads, shapes, axes):
    n = len(grads)

    def run(gs, outs, sems, phase):
        send_sems, recv_sems = sems
        x, y, c = _mesh_pos()
        copies = [pltpu.make_async_remote_copy(
            src_ref=_shard_view(gs[t], 2 * k + (1 - c), shapes[t], axes[t]), dst_ref=outs[t].at[k],
            send_sem=send_sems.at[4 * t + k], recv_sem=recv_sems.at[4 * t + k],
            device_id=(x, y, 1 - c), device_id_type=MESH) for t in range(n) for k in range(4)]
        if phase == 0:
            for cp in copies:
                cp.start()
            return
        for cp in copies:
            cp.wait_recv()
        for cp in copies:
            cp.wait_send()

    return _Comm(grads, [jax.ShapeDtypeStruct((4,) + tuple(s), BF16) for s in shapes],
                 [pltpu.SemaphoreType.DMA((4 * n,)), pltpu.SemaphoreType.DMA((4 * n,))], run)


def _rs_pair_sum(g, recv, cidx, shape, axis, *, name):
    r, c = shape
    tr = _tile(r, max(8, (1 << 21) // c), 16)
    nr = r // tr

    def body(c_ref, g_ref, rv_ref, o_ref):
        o_ref[...] = (g_ref[...].astype(F32) + rv_ref[...].astype(F32)).astype(BF16)

    if axis == 1:
        g_spec = pl.BlockSpec((tr, c), lambda k, i, s: (i, 2 * k + s[0]))
    else:
        g_spec = pl.BlockSpec((tr, c), lambda k, i, s: ((2 * k + s[0]) * nr + i, 0))
    blk = pl.BlockSpec((None, tr, c), lambda k, i, s: (k, i, 0))
    return pl.pallas_call(
        body, name=name,
        grid_spec=pltpu.PrefetchScalarGridSpec(num_scalar_prefetch=1, grid=(4, nr), in_specs=[g_spec, blk],
                                               out_specs=blk),
        out_shape=jax.ShapeDtypeStruct((4, r, c), BF16),
        compiler_params=_params("parallel", "parallel"))(cidx, g, recv)


def _rs_to_chips(parts):
    n = len(parts)

    def run(ps, outs, sems, phase):
        send_sems, recv_sems, local_sems = sems
        x, y, c = _mesh_pos()
        my_chip = 2 * x + y
        peers = [(1 - x, y), (x, 1 - y), (1 - x, 1 - y)]
        local = [pltpu.make_async_copy(ps[t].at[my_chip], outs[t].at[my_chip], local_sems.at[t]) for t in range(n)]
        sends = [pltpu.make_async_remote_copy(
            src_ref=ps[t].at[2 * px + py], dst_ref=outs[t].at[my_chip],
            send_sem=send_sems.at[3 * t + j], recv_sem=recv_sems.at[3 * t + j],
            device_id=(px, py, c), device_id_type=MESH) for t in range(n) for j, (px, py) in enumerate(peers)]
        if phase == 0:
            for cp in local + sends:
                cp.start()
            return
        for t in range(n):
            for j, (px, py) in enumerate(peers):
                pltpu.make_async_remote_copy(
                    src_ref=ps[t].at[2 * px + py], dst_ref=outs[t].at[2 * px + py],
                    send_sem=send_sems.at[3 * t + j], recv_sem=recv_sems.at[3 * t + j],
                    device_id=(x, y, c), device_id_type=MESH).wait_recv()
        for cp in sends:
            cp.wait_send()
        for cp in local:
            cp.wait()

    return _Comm(parts, [jax.ShapeDtypeStruct(p.shape, BF16) for p in parts],
                 [pltpu.SemaphoreType.DMA((3 * n,)), pltpu.SemaphoreType.DMA((3 * n,)),
                  pltpu.SemaphoreType.DMA((n,))], run)


def _exchange_small(buf):
    def run(ins, outs, sems, phase):
        (in_ref,), (slots,) = ins, outs
        send_sems, recv_sems, local_sem = sems
        x, y, c = _mesh_pos()
        me = 4 * x + 2 * y + c
        local = pltpu.make_async_copy(in_ref, slots.at[me], local_sem.at[0])
        flips = [(fx, fy, fc) for fx in (0, 1) for fy in (0, 1) for fc in (0, 1)][1:]
        peers = [(1 - x if fx else x, 1 - y if fy else y, 1 - c if fc else c) for fx, fy, fc in flips]
        sends = [pltpu.make_async_remote_copy(src_ref=in_ref, dst_ref=slots.at[me], send_sem=send_sems.at[k],
                                              recv_sem=recv_sems.at[k], device_id=peer, device_id_type=MESH)
                 for k, peer in enumerate(peers)]
        if phase == 0:
            for cp in [local] + sends:
                cp.start()
            return
        for k, (px, py, pc) in enumerate(peers):
            pltpu.make_async_remote_copy(src_ref=in_ref, dst_ref=slots.at[4 * px + 2 * py + pc],
                                         send_sem=send_sems.at[k], recv_sem=recv_sems.at[k],
                                         device_id=(x, y, c), device_id_type=MESH).wait_recv()
        for cp in sends:
            cp.wait_send()
        local.wait()

    return _Comm([buf], [jax.ShapeDtypeStruct((N_DEV,) + buf.shape, F32)],
                 [pltpu.SemaphoreType.DMA((7,)), pltpu.SemaphoreType.DMA((7,)), pltpu.SemaphoreType.DMA((1,))], run)


def _sum_slots(slots, *, name):
    _, R, C = slots.shape
    tr = _tile(R, 512)

    def body(s_ref, o_ref):
        acc = s_ref[0]
        for p in range(1, N_DEV):
            acc = acc + s_ref[p]
        o_ref[...] = acc

    return pl.pallas_call(
        body, name=name, grid=(R // tr,), in_specs=[pl.BlockSpec((N_DEV, tr, C), lambda i: (0, i, 0))],
        out_specs=pl.BlockSpec((tr, C), lambda i: (i, 0)), out_shape=jax.ShapeDtypeStruct((R, C), F32),
        compiler_params=_params("parallel"))(slots)


def _adam_math(g, w, m, v):
    m_new = ADAM_B1 * m + (1.0 - ADAM_B1) * g
    v_new = ADAM_B2 * v + (1.0 - ADAM_B2) * (g * g)
    m_hat = m_new / ADAM_C1
    v_hat = v_new / ADAM_C2
    delta = -ADAM_LR * (m_hat / (jnp.sqrt(v_hat) + ADAM_EPS) + ADAM_WD * w)
    return delta, m_new, v_new


def _adamw_sharded(q, w, m, v, *, name):
    r, c = w.shape
    tr = _tile(r, max(8, (1 << 19) // c), 16)

    def body(q_ref, w_ref, m_ref, v_ref, g_ref, d_ref, mo_ref, vo_ref):
        g = ((q_ref[0].astype(F32) + q_ref[1].astype(F32)) + q_ref[2].astype(F32)) + q_ref[3].astype(F32)
        g_ref[...] = g
        d_ref[...], mo_ref[...], vo_ref[...] = _adam_math(g, w_ref[...], m_ref[...], v_ref[...])

    blk = pl.BlockSpec((tr, c), lambda i: (i, 0))
    return pl.pallas_call(
        body, name=name, grid=(r // tr,),
        in_specs=[pl.BlockSpec((4, tr, c), lambda i: (0, i, 0)), blk, blk, blk], out_specs=[blk] * 4,
        out_shape=[jax.ShapeDtypeStruct((r, c), F32)] * 4,
        compiler_params=_params("parallel"))(q, w, m, v)


def _adamw_plain(g, w, m, v, *, name):
    r, c = w.shape
    tr = _tile(r, 512)

    def body(g_ref, w_ref, m_ref, v_ref, d_ref, mo_ref, vo_ref):
        d_ref[...], mo_ref[...], vo_ref[...] = _adam_math(g_ref[...], w_ref[...], m_ref[...], v_ref[...])

    blk = pl.BlockSpec((tr, c), lambda i: (i, 0))
    return pl.pallas_call(
        body, name=name, grid=(r // tr,), in_specs=[blk] * 4, out_specs=[blk] * 3,
        out_shape=[jax.ShapeDtypeStruct((r, c), F32)] * 3,
        compiler_params=_params("parallel"))(g, w, m, v)


def _pack_rows(arrays):
    return jnp.concatenate([a.reshape(-1, LANES) for a in arrays], axis=0)


def kernel(x, ffn1_w_gu, ffn1_w_down, ln1_g, ln1_b, w_in, b_in, sgu_ln_g, sgu_ln_b, sgu_w_s, sgu_b_s, w_a_proj, conv_w_dw, conv_b_dw, conv_ln_g, conv_ln_b, w_b_proj, w_out, ln2_g, ln2_b, ffn2_w_gu, ffn2_w_down, ln3_g, ln3_b, loss_target, m_ffn1_w_gu, m_ffn1_w_down, m_ln1_g, m_ln1_b, m_w_in, m_b_in, m_sgu_ln_g, m_sgu_ln_b, m_sgu_w_s, m_sgu_b_s, m_w_a_proj, m_conv_w_dw, m_conv_b_dw, m_conv_ln_g, m_conv_ln_b, m_w_b_proj, m_w_out, m_ln2_g, m_ln2_b, m_ffn2_w_gu, m_ffn2_w_down, m_ln3_g, m_ln3_b, v_ffn1_w_gu, v_ffn1_w_down, v_ln1_g, v_ln1_b, v_w_in, v_b_in, v_sgu_ln_g, v_sgu_ln_b, v_sgu_w_s, v_sgu_b_s, v_w_a_proj, v_conv_w_dw, v_conv_b_dw, v_conv_ln_g, v_conv_ln_b, v_w_b_proj, v_w_out, v_ln2_g, v_ln2_b, v_ffn2_w_gu, v_ffn2_w_down, v_ln3_g, v_ln3_b):
    given = dict(locals())
    w = {n: given[n][0] for n in WEIGHTS}
    mom = {n: given["m_" + n][0] for n in WEIGHTS}
    var = {n: given["v_" + n][0] for n in WEIGHTS}
    xt = x[0]
    target = loss_target[0]
    T, D = xt.shape
    A = w['w_a_proj'].shape[0]

    big_names = list(BIG)
    early = ['ffn1_w_gu', 'ffn1_w_down']
    later = ['ffn2_w_gu']
    late = [n for n in big_names if n not in early + later]
    conv_w_pad = jnp.pad(w['conv_w_dw'], ((0, CONV_WPAD - CONV_WIDTH), (0, 0)))
    w_bf = {n: w[n].astype(BF16) for n in big_names}
    full = dict(zip(early, _comm_call(_all_gather_routed([w_bf[n] for n in early], [BIG[n] for n in early]),
                                      name="all_gather_ffn1")))
    gather_late = _all_gather([w_bf[n] for n in late] + [conv_w_pad], [BIG[n] for n in late] + [1],
                              progressive=True, busy=0.65)
    gather_later = _all_gather([w_bf[n] for n in later], [BIG[n] for n in later], progressive=True, busy=0.8)

    def row(v):
        return v.reshape(1, -1)

    ones = jnp.ones((1, D), F32)
    zeros = jnp.zeros((1, D), F32)
    w_s = w['sgu_w_s']
    w_st = jnp.swapaxes(w_s, 1, 2)
    b_sb = jnp.broadcast_to(w['sgu_b_s'][:, :, None], w_s.shape)

    (gate1, up1, xb0, xh1, rstd1), gathered = _ffn_fwd(xt, ones, zeros, full['ffn1_w_gu'], full['ffn1_w_down'],
                                                  affine=False, name="ffn1_fwd", carry=gather_late)
    full.update(zip(late, gathered[:-1]))
    conv_w_full = gathered[-1]
    g1, b1 = row(w['ln1_g']), row(w['ln1_b'])
    (proj, xb1), gathered = _inproj_fwd(xh1, g1, b1, full['w_in'], row(w['b_in']), name="inproj_fwd",
                                        carry=gather_later)
    full.update(zip(later, gathered))
    sg = _sgu_fwd(proj, row(w['sgu_ln_g']), row(w['sgu_ln_b']), w_s, b_sb, name="sgu_fwd")
    conv_out, cv = _conv_fwd(proj, conv_w_full, row(w['conv_b_dw']), row(w['conv_ln_g']), row(w['conv_ln_b']),
                             name="conv_fwd")
    ya, yb, mixed = _mix_fwd_gate(sg, cv, proj, full['w_a_proj'], full['w_b_proj'], name="mix_fwd_gate")
    xh2, rstd2 = _mix_fwd_out(mixed, full['w_out'], xh1, g1, b1, name="mix_fwd_out")
    g2, b2 = row(w['ln2_g']), row(w['ln2_b'])
    gate2, up2, xb2, dr3, loss_part, d_ln3_g, d_ln3_b = _ffn_fwd(
        xh2, g2, b2, full['ffn2_w_gu'], full['ffn2_w_down'], affine=True, name="ffn2_fwd_loss",
        final=(row(w['ln3_g']), row(w['ln3_b']), target))

    F = full['ffn2_w_down'].shape[0]
    h2, dgate2, dup2, dr2, d_ln2_g, d_ln2_b = _ffn_bwd(dr3, gate2, up2, full['ffn2_w_gu'], full['ffn2_w_down'],
                                                      name="ffn2_bwd", prev=(xh2, rstd2, g2))
    G, P, Q = {}, {}, {}
    cidx = lax.axis_index("c").astype(jnp.int32).reshape(1)

    def to_sibling(names):
        return _rs_to_sibling([G[n] for n in names], [w[n].shape for n in names], [BIG[n] for n in names])

    def pair_sum(names, received):
        for n, rv in zip(names, received):
            P[n] = _rs_pair_sum(G[n], rv, cidx, w[n].shape, BIG[n], name="rs_pair_sum_" + n)

    def to_chips(names):
        return _rs_to_chips([P[n] for n in names])

    G['ffn2_w_down'] = _mm_tn(h2, dr3, name="dw_ffn2_down", tm_pref=1408, tn_pref=2048, scale=0.5)
    gu, rv = _mm_tn(xb2, dgate2, name="dw_ffn2_gate", tm_pref=2048, tn_pref=1408,
                    n_total=2 * F, carry=to_sibling(['ffn2_w_down']))
    pair_sum(['ffn2_w_down'], rv)
    G['ffn2_w_gu'], q = _mm_tn(xb2, dup2, name="dw_ffn2_up", tm_pref=2048, tn_pref=1408,
                               into=gu, col_off=F, n_total=2 * F, carry=to_chips(['ffn2_w_down']))
    Q['ffn2_w_down'] = q[0]

    (dya, dyb, dproj, dbin_gate), rv = _mix_bwd_gate(dr2, full['w_out'], proj, ya, yb, name="mix_bwd_gate",
                                                     carry=to_sibling(['ffn2_w_gu']))
    pair_sum(['ffn2_w_gu'], rv)
    dsg, dcv = _mix_bwd_proj(dya, dyb, full['w_a_proj'], full['w_b_proj'], name="mix_bwd_proj")
    dproj, dbin_sgu, d_sgu_ln_g, d_sgu_ln_b, d_w_s, d_b_s = _sgu_bwd(
        proj, dsg, row(w['sgu_ln_g']), row(w['sgu_ln_b']), w_s, w_st, b_sb, dproj, name="sgu_bwd")
    dconv, d_conv_ln_g, d_conv_ln_b, d_conv_b = _conv_bwd_ln(dcv, conv_out, row(w['conv_ln_g']),
                                                            row(w['conv_ln_b']), name="conv_bwd_ln")
    (dproj, dbin_conv, d_conv_w), q = _conv_bwd(proj, dconv, conv_w_full, dproj, name="conv_bwd",
                                                carry=to_chips(['ffn2_w_gu']))
    Q['ffn2_w_gu'] = q[0]

    mid = ['w_out', 'w_a_proj', 'w_b_proj']
    G['w_out'] = _mm_tn(mixed, dr2, name="dw_out", tm_pref=2048, tn_pref=1024)
    G['w_a_proj'] = _mm_tn(sg, dya, name="dw_a_proj", tm_pref=1024, tn_pref=2048)
    G['w_b_proj'] = _mm_tn(cv, dyb, name="dw_b_proj", tm_pref=1024, tn_pref=2048)
    G['w_in'], rv = _mm_tn(xb1, dproj, name="dw_in", tm_pref=2048, tn_pref=1024, tk_pref=2048,
                           carry=to_sibling(mid))
    pair_sum(mid, rv)
    both = _join(to_chips(mid), to_sibling(['w_in']))
    (dr1, d_ln1_g, d_ln1_b), moved = _inproj_bwd(dproj, full['w_in'], dr2, xh1, rstd1, g1, name="inproj_bwd",
                                                 carry=both)
    q, rv = both.split(moved)
    Q.update(zip(mid, q))
    pair_sum(['w_in'], rv)

    h1, dgate1, dup1, grad_x = _ffn_bwd(dr1, gate1, up1, full['ffn1_w_gu'], full['ffn1_w_down'], name="ffn1_bwd")
    G['ffn1_w_down'], q = _mm_tn(h1, dr1, name="dw_ffn1_down", tm_pref=1408, tn_pref=2048, scale=0.5,
                                 carry=to_chips(['w_in']))
    Q['w_in'] = q[0]
    small_g = {'ln1_g': d_ln1_g, 'ln1_b': d_ln1_b,
               'b_in': jnp.concatenate([dbin_sgu, dbin_conv, dbin_gate], axis=1),
               'sgu_ln_g': d_sgu_ln_g, 'sgu_ln_b': d_sgu_ln_b, 'sgu_w_s': d_w_s, 'sgu_b_s': d_b_s,
               'conv_b_dw': d_conv_b, 'conv_ln_g': d_conv_ln_g, 'conv_ln_b': d_conv_ln_b,
               'ln2_g': d_ln2_g, 'ln2_b': d_ln2_b, 'ln3_g': d_ln3_g, 'ln3_b': d_ln3_b}
    packed = _pack_rows([small_g[n] for n in SMALL] + [d_conv_w, loss_part])
    both = _join(to_sibling(['ffn1_w_down']), _exchange_small(packed))
    gu, moved = _mm_tn(xb0, dgate1, name="dw_ffn1_gate", tm_pref=2048, tn_pref=1408, n_total=2 * F, carry=both)
    rv, slots = both.split(moved)
    pair_sum(['ffn1_w_down'], rv)
    reduced = _sum_slots(slots[0], name="sum_small")
    G['ffn1_w_gu'], q = _mm_tn(xb0, dup1, name="dw_ffn1_up", tm_pref=2048, tn_pref=1408, into=gu, col_off=F,
                               n_total=2 * F, carry=to_chips(['ffn1_w_down']))
    Q['ffn1_w_down'] = q[0]
    pair_sum(['ffn1_w_gu'], _comm_call(to_sibling(['ffn1_w_gu']), name="rs_to_sibling_last"))
    Q['ffn1_w_gu'] = _comm_call(to_chips(['ffn1_w_gu']), name="rs_to_chips_last")[0]

    grads, deltas, new_m, new_v = {}, {}, {}, {}
    for n in big_names:
        grads[n], deltas[n], new_m[n], new_v[n] = _adamw_sharded(Q[n], w[n], mom[n], var[n], name="adamw_" + n)

    B = conv_w_full.shape[1]
    n_small_rows = sum(w[n].size for n in SMALL) // LANES
    conv_rows = CONV_WIDTH * B // LANES
    d_small, m_small, v_small = _adamw_plain(
        reduced[:n_small_rows], _pack_rows([w[n] for n in SMALL]), _pack_rows([mom[n] for n in SMALL]),
        _pack_rows([var[n] for n in SMALL]), name="adamw_small")
    off = 0
    for n in SMALL:
        rows = w[n].size // LANES
        grads[n] = reduced[off:off + rows].reshape(w[n].shape)
        deltas[n] = d_small[off:off + rows].reshape(w[n].shape)
        new_m[n] = m_small[off:off + rows].reshape(w[n].shape)
        new_v[n] = v_small[off:off + rows].reshape(w[n].shape)
        off += rows
    conv_g_full = reduced[off:off + conv_rows].reshape(CONV_WIDTH, B)
    bs = w['conv_w_dw'].shape[1]
    my_block = 4 * lax.axis_index("x") + 2 * lax.axis_index("y") + lax.axis_index("c")
    grads['conv_w_dw'] = lax.dynamic_slice(conv_g_full, (0, my_block * bs), (CONV_WIDTH, bs))
    deltas['conv_w_dw'], new_m['conv_w_dw'], new_v['conv_w_dw'] = _adamw_plain(
        grads['conv_w_dw'], w['conv_w_dw'], mom['conv_w_dw'], var['conv_w_dw'], name="adamw_conv_w")
    loss = reduced[off + conv_rows, 0]

    def lead(a):
        return a[None]

    return (loss, grad_x[None], *[lead(grads[n]) for n in WEIGHTS], *[lead(deltas[n]) for n in WEIGHTS],
            *[lead(new_m[n]) for n in WEIGHTS], *[lead(new_v[n]) for n in WEIGHTS])
```

```python
import functools
import math

import jax
import jax.numpy as jnp
from jax import lax
from jax.experimental import pallas as pl
from jax.experimental.pallas import tpu as pltpu
from jax.experimental.pallas import tpu_sc as plsc

F32 = jnp.float32
BF16 = jnp.bfloat16

ALPHA = 2.0 ** 0.25
LN_EPS = 1e-5
CONV_WIDTH = 31
CONV_HALO = 32
CONV_ROWS = 64
CONV_WPAD = 32
CHUNK = 64
GMLP_BLOCK = 128
A_GROUPS = 8
N_DEV = 8
LANES = 128

ADAM_LR = 0.001
ADAM_B1 = 0.9
ADAM_B2 = 0.999
ADAM_EPS = 1e-08
ADAM_WD = 0.01
ADAM_STEP = 10
ADAM_C1 = 1.0 - ADAM_B1 ** ADAM_STEP
ADAM_C2 = 1.0 - ADAM_B2 ** ADAM_STEP

VMEM_LIMIT_BYTES = 60 * 2 ** 20
MESH = pl.DeviceIdType.MESH
ANY = pl.BlockSpec(memory_space=pl.ANY)

WEIGHTS = ['ffn1_w_gu', 'ffn1_w_down', 'ln1_g', 'ln1_b', 'w_in', 'b_in', 'sgu_ln_g', 'sgu_ln_b', 'sgu_w_s',
           'sgu_b_s', 'w_a_proj', 'conv_w_dw', 'conv_b_dw', 'conv_ln_g', 'conv_ln_b', 'w_b_proj', 'w_out',
           'ln2_g', 'ln2_b', 'ffn2_w_gu', 'ffn2_w_down', 'ln3_g', 'ln3_b']
BIG = {'ffn1_w_gu': 1, 'ffn1_w_down': 0, 'w_in': 1, 'w_a_proj': 1, 'w_b_proj': 1, 'w_out': 0,
       'ffn2_w_gu': 1, 'ffn2_w_down': 0}
SMALL = [n for n in WEIGHTS if n not in BIG and n != 'conv_w_dw']


def _tile(n, pref, mult=8):
    best = None
    for d in range(mult, min(n, pref) + 1, mult):
        if n % d == 0:
            best = d
    return n if best is None else best


def _params(*sem):
    return pltpu.CompilerParams(dimension_semantics=sem, vmem_limit_bytes=VMEM_LIMIT_BYTES)


def _dot(a, b):
    return jnp.dot(a, b, preferred_element_type=F32)


def _dot_nt(a, b):
    return lax.dot_general(a, b, (((1,), (1,)), ((), ())), preferred_element_type=F32)


def _dot_tn(a, b):
    return lax.dot_general(a, b, (((0,), (0,)), ((), ())), preferred_element_type=F32)


def _sig(x):
    return 1.0 / (1.0 + jnp.exp(-x))


_GELU_K = math.sqrt(2.0 / math.pi)
_GELU_C = 0.044715


def _gelu(x):
    t = jnp.tanh(_GELU_K * (x + _GELU_C * x * x * x))
    return 0.5 * x * (1.0 + t)


def _gelu_grad(x):
    x2 = x * x
    t = jnp.tanh(_GELU_K * (x + _GELU_C * x2 * x))
    return 0.5 * (1.0 + t) + 0.5 * x * (1.0 - t * t) * (_GELU_K * (1.0 + 3.0 * _GELU_C * x2))


def _ln_stats(r):
    mu = jnp.mean(r, axis=-1, keepdims=True)
    rc = r - mu
    var = jnp.mean(rc * rc, axis=-1, keepdims=True)
    rstd = lax.rsqrt(var + LN_EPS)
    return rc * rstd, rstd


def _ln_bwd(dy, xh, rstd, g):
    dxh = dy * g
    m1 = jnp.mean(dxh, axis=-1, keepdims=True)
    m2 = jnp.mean(dxh * xh, axis=-1, keepdims=True)
    return rstd * (dxh - m1 - xh * m2)


def _colsum(v):
    return jnp.sum(v, axis=0, keepdims=True)


def _chunk_mask(transposed):
    shift = CHUNK.bit_length() - 1
    r = lax.broadcasted_iota(jnp.int32, (GMLP_BLOCK, GMLP_BLOCK), 0) >> shift
    c = lax.broadcasted_iota(jnp.int32, (GMLP_BLOCK, GMLP_BLOCK), 1) >> shift
    return (r <= c) if transposed else (c <= r)


class _Comm:
    def __init__(self, inputs, out_shape, scratch, run, when=(0.0, 1.0)):
        self.inputs, self.out_shape, self.scratch, self.run = list(inputs), list(out_shape), list(scratch), run
        self.when = tuple(when)
        self.parts = [len(self.out_shape)]

    def split(self, outs):
        res, o = [], 0
        for n in self.parts:
            res.append(list(outs[o:o + n]))
            o += n
        return res


def _join(*comms):
    comms = [c for c in comms if c is not None]
    if not comms:
        return None
    assert all(c.when == (0.0, 1.0) for c in comms)

    def run(ins, outs, sems, phase):
        i = o = s = 0
        for c in comms:
            c.run(ins[i:i + len(c.inputs)], outs[o:o + len(c.out_shape)], sems[s:s + len(c.scratch)], phase)
            i, o, s = i + len(c.inputs), o + len(c.out_shape), s + len(c.scratch)

    joined = _Comm(sum((c.inputs for c in comms), []), sum((c.out_shape for c in comms), []),
                   sum((c.scratch for c in comms), []), run)
    joined.parts = [len(c.out_shape) for c in comms]
    return joined


def _call(body, *, name, grid, in_specs, out_specs, out_shape, args, sem, scratch_shapes=(), aliases=None, carry=None):
    in_specs, out_specs, out_shape = list(in_specs), list(out_specs), list(out_shape)
    scratch_shapes = list(scratch_shapes)
    if carry is None:
        return pl.pallas_call(body, name=name, grid=grid, in_specs=in_specs, out_specs=out_specs,
                              out_shape=out_shape, scratch_shapes=scratch_shapes,
                              input_output_aliases=aliases or {}, compiler_params=_params(*sem))(*args)
    n_in, n_out, n_scr = len(args), len(out_shape), len(scratch_shapes)
    c_in, c_out = len(carry.inputs), len(carry.out_shape)
    n_steps = math.prod(grid)
    at_step = [int(round(f * (n_steps - 1))) for f in carry.when]
    assert at_step[0] == 0 and at_step[-1] == n_steps - 1 and at_step == sorted(at_step)

    def wrapped(*refs):
        ins, c_ins = refs[:n_in], refs[n_in:n_in + c_in]
        o0 = n_in + c_in
        outs, c_outs = refs[o0:o0 + n_out], refs[o0 + n_out:o0 + n_out + c_out]
        s0 = o0 + n_out + c_out
        scr, c_sems = refs[s0:s0 + n_scr], refs[s0 + n_scr:]
        step = 0
        for a, g in enumerate(grid):
            step = step * g + pl.program_id(a)
        pl.when(step == 0)(functools.partial(carry.run, c_ins, c_outs, c_sems, 0))
        body(*ins, *outs, *scr)
        for k in range(1, len(at_step)):
            pl.when(step == at_step[k])(functools.partial(carry.run, c_ins, c_outs, c_sems, k))

    res = pl.pallas_call(
        wrapped, name=name, grid=grid, in_specs=in_specs + [ANY] * c_in, out_specs=out_specs + [ANY] * c_out,
        out_shape=out_shape + carry.out_shape, scratch_shapes=scratch_shapes + carry.scratch,
        input_output_aliases=aliases or {},
        compiler_params=pltpu.CompilerParams(dimension_semantics=tuple(sem),
                                             vmem_limit_bytes=VMEM_LIMIT_BYTES, has_side_effects=True),
    )(*args, *carry.inputs)
    return list(res[:n_out]), list(res[n_out:])


def _comm_call(comm, *, name):
    n_in, n_out = len(comm.inputs), len(comm.out_shape)

    def body(*refs):
        ins, outs, sems = refs[:n_in], refs[n_in:n_in + n_out], refs[n_in + n_out:]
        for k in range(len(comm.when)):
            comm.run(ins, outs, sems, k)

    return list(pl.pallas_call(
        body, name=name, in_specs=[ANY] * n_in, out_specs=[ANY] * n_out, out_shape=comm.out_shape,
        scratch_shapes=comm.scratch, compiler_params=pltpu.CompilerParams(has_side_effects=True))(*comm.inputs))


def _row_chunks(tm, rows=128):
    rows = _tile(tm, rows)
    return [slice(r, r + rows) for r in range(0, tm, rows)]


def _ffn_fwd(xh, lg, lb, wgu, wd, *, affine, name, final=None, carry=None):
    T, D = xh.shape
    F = wd.shape[0]
    tm = _tile(T, 512)
    tn = _tile(F, 512, LANES)
    nj = F // tn
    is_final = final is not None
    n_x = 3 if affine else 1

    def body(*refs):
        if is_final:
            (wg_ref, wu_ref, wd_ref, ng_ref, nb_ref, tgt_ref,
             gate_ref, up_ref, xb_sc, dr_ref, loss_ref, dng_ref, dnb_ref, acc_sc) = refs[n_x:]
        else:
            (wg_ref, wu_ref, wd_ref,
             gate_ref, up_ref, xb_sc, xho_ref, rstd_ref, acc_sc) = refs[n_x:]
        xh_ref = refs[0]
        lg_ref, lb_ref = refs[1:n_x] if affine else (None, None)
        i = pl.program_id(0)
        j = pl.program_id(1)

        def xin():
            v = xh_ref[...]
            return v * lg_ref[...] + lb_ref[...] if affine else v

        @pl.when(j == 0)
        def _():
            xb_sc[...] = xin().astype(BF16)
            acc_sc[...] = jnp.zeros_like(acc_sc)

        xb = xb_sc[...]
        g = _dot(xb, wg_ref[...])
        u = _dot(xb, wu_ref[...])
        gate_ref[...] = g.astype(BF16)
        up_ref[...] = u.astype(BF16)
        h = g * _sig(g) * u
        acc_sc[...] += _dot(h.astype(BF16), wd_ref[...])

        @pl.when(j == nj - 1)
        def _():
            if is_final:
                @pl.when(i == 0)
                def _():
                    loss_ref[...] = jnp.zeros_like(loss_ref)
                    dng_ref[...] = jnp.zeros_like(dng_ref)
                    dnb_ref[...] = jnp.zeros_like(dnb_ref)
            for rs in _row_chunks(tm):
                v = xh_ref[rs, :]
                if affine:
                    v = v * lg_ref[...] + lb_ref[...]
                r = ALPHA * v + 0.5 * acc_sc[rs, :]
                xho, rstd = _ln_stats(r)
                if not is_final:
                    xho_ref[rs, :] = xho
                    rstd_ref[rs, :] = jnp.broadcast_to(rstd, (rs.stop - rs.start, LANES))
                else:
                    ng = ng_ref[...]
                    e = xho * ng + nb_ref[...] - tgt_ref[rs, :]
                    part = _colsum(jnp.sum(e * e, axis=1, keepdims=True)) * (0.5 / D)
                    loss_ref[...] += jnp.broadcast_to(part, loss_ref.shape)
                    dy = e * (1.0 / D)
                    dng_ref[...] += _colsum(dy * xho)
                    dnb_ref[...] += _colsum(dy)
                    dr_ref[rs, :] = _ln_bwd(dy, xho, rstd, ng)

    row = pl.BlockSpec((tm, D), lambda i, j: (i, 0))
    vec = pl.BlockSpec((1, D), lambda i, j: (0, 0))
    hid = pl.BlockSpec((tm, tn), lambda i, j: (i, j))
    in_specs = [row, vec, vec][:n_x] + [
        pl.BlockSpec((D, tn), lambda i, j: (0, j)),
        pl.BlockSpec((D, tn), lambda i, j: (0, j + nj)),
        pl.BlockSpec((tn, D), lambda i, j: (j, 0))]
    args = [xh, lg, lb][:n_x] + [wgu, wgu, wd]
    out_shape = [jax.ShapeDtypeStruct((T, F), BF16), jax.ShapeDtypeStruct((T, F), BF16),
                 jax.ShapeDtypeStruct((T, D), BF16)]
    out_specs = [hid, hid, row]
    if is_final:
        in_specs += [vec, vec, row]
        args += list(final)
        out_shape += [jax.ShapeDtypeStruct((T, D), F32), jax.ShapeDtypeStruct((8, LANES), F32),
                      jax.ShapeDtypeStruct((1, D), F32), jax.ShapeDtypeStruct((1, D), F32)]
        out_specs += [row, pl.BlockSpec((8, LANES), lambda i, j: (0, 0)), vec, vec]
        sem = ("arbitrary", "arbitrary")
    else:
        out_shape += [jax.ShapeDtypeStruct((T, D), F32), jax.ShapeDtypeStruct((T, LANES), F32)]
        out_specs += [row, pl.BlockSpec((tm, LANES), lambda i, j: (i, 0))]
        sem = ("parallel", "arbitrary")
    return _call(body, name=name, grid=(T // tm, nj), in_specs=in_specs, out_specs=out_specs, out_shape=out_shape,
                 scratch_shapes=[pltpu.VMEM((tm, D), F32)], sem=sem, args=args, carry=carry)


def _ffn_bwd(dr, gate, up, wgu, wd, *, name, prev=None, carry=None):
    T, D = dr.shape
    F = wd.shape[0]
    tm = _tile(T, 512)
    tn = _tile(F, 512, LANES)
    nj = F // tn
    has_prev = prev is not None

    def body(*refs):
        if has_prev:
            (dr_ref, gate_ref, up_ref, wd_ref, wg_ref, wu_ref, xh_ref, rstd_ref, lg_ref,
             h_ref, dg_ref, du_ref, dprev_ref, dlg_ref, dlb_ref, df_sc, dx_sc) = refs
        else:
            (dr_ref, gate_ref, up_ref, wd_ref, wg_ref, wu_ref,
             h_ref, dg_ref, du_ref, dprev_ref, df_sc, dx_sc) = refs
        i = pl.program_id(0)
        j = pl.program_id(1)

        @pl.when(j == 0)
        def _():
            d = dr_ref[...]
            df_sc[...] = (0.5 * d).astype(BF16)
            dx_sc[...] = ALPHA * d

        g = gate_ref[...].astype(F32)
        u = up_ref[...].astype(F32)
        dh = _dot_nt(df_sc[...], wd_ref[...])
        s = _sig(g)
        sil = g * s
        h_ref[...] = (sil * u).astype(BF16)
        dg = (dh * u * (s * (1.0 + g * (1.0 - s)))).astype(BF16)
        du = (dh * sil).astype(BF16)
        dg_ref[...] = dg
        du_ref[...] = du
        dx_sc[...] += _dot_nt(dg, wg_ref[...]) + _dot_nt(du, wu_ref[...])

        @pl.when(j == nj - 1)
        def _():
            dxin = dx_sc[...]
            if not has_prev:
                dprev_ref[...] = dxin
            else:
                @pl.when(i == 0)
                def _():
                    dlg_ref[...] = jnp.zeros_like(dlg_ref)
                    dlb_ref[...] = jnp.zeros_like(dlb_ref)
                xh = xh_ref[...]
                dlg_ref[...] += _colsum(dxin * xh)
                dlb_ref[...] += _colsum(dxin)
                dprev_ref[...] = _ln_bwd(dxin, xh, rstd_ref[:, 0:1], lg_ref[...])

    row = pl.BlockSpec((tm, D), lambda i, j: (i, 0))
    vec = pl.BlockSpec((1, D), lambda i, j: (0, 0))
    hid = pl.BlockSpec((tm, tn), lambda i, j: (i, j))
    in_specs = [row, hid, hid,
                pl.BlockSpec((tn, D), lambda i, j: (j, 0)),
                pl.BlockSpec((D, tn), lambda i, j: (0, j)),
                pl.BlockSpec((D, tn), lambda i, j: (0, j + nj))]
    args = [dr, gate, up, wd, wgu, wgu]
    out_shape = [jax.ShapeDtypeStruct((T, F), BF16)] * 3 + [jax.ShapeDtypeStruct((T, D), F32)]
    out_specs = [hid, hid, hid, row]
    if has_prev:
        in_specs += [row, pl.BlockSpec((tm, LANES), lambda i, j: (i, 0)), vec]
        args += list(prev)
        out_shape += [jax.ShapeDtypeStruct((1, D), F32)] * 2
        out_specs += [vec, vec]
        sem = ("arbitrary", "arbitrary")
    else:
        sem = ("parallel", "arbitrary")
    return _call(body, name=name, grid=(T // tm, nj), in_specs=in_specs, out_specs=out_specs, out_shape=out_shape,
                 scratch_shapes=[pltpu.VMEM((tm, D), BF16), pltpu.VMEM((tm, D), F32)], sem=sem, args=args,
                 carry=carry)


def _mm_tn(a, b, *, name, tm_pref, tn_pref, tk_pref=1024, scale=1.0, a_affine=None, into=None, col_off=0,
           n_total=None, carry=None):
    T, M = a.shape
    N = b.shape[1]
    n_total = N if n_total is None else n_total
    tM = _tile(M, tm_pref, LANES)
    tN = _tile(N, tn_pref, LANES)
    tk = _tile(T, tk_pref)
    nt = T // tk
    assert col_off % tN == 0
    off_blocks = col_off // tN
    has_aff = a_affine is not None
    has_into = into is not None

    def body(*refs):
        refs = list(refs)
        a_ref = refs.pop(0)
        if has_aff:
            lg_ref = refs.pop(0)
            lb_ref = refs.pop(0)
        b_ref = refs.pop(0)
        if has_into:
            refs.pop(0)
        o_ref, acc_sc = refs
        t = pl.program_id(2)

        @pl.when(t == 0)
        def _():
            acc_sc[...] = jnp.zeros_like(acc_sc)

        av = a_ref[...]
        if has_aff:
            av = av * lg_ref[...] + lb_ref[...]
        acc_sc[...] += _dot_tn(av.astype(BF16), b_ref[...].astype(BF16))

        @pl.when(t == nt - 1)
        def _():
            o_ref[...] = (acc_sc[...] * scale).astype(BF16)

    in_specs = [pl.BlockSpec((tk, tM), lambda m, n, t: (t, m))]
    args = [a]
    if has_aff:
        in_specs += [pl.BlockSpec((1, tM), lambda m, n, t: (0, m))] * 2
        args += list(a_affine)
    in_specs.append(pl.BlockSpec((tk, tN), lambda m, n, t: (t, n)))
    args.append(b)
    aliases = {}
    if has_into:
        aliases = {len(args): 0}
        in_specs.append(ANY)
        args.append(into)
    res = _call(body, name=name, grid=(M // tM, N // tN, nt), in_specs=in_specs,
                out_specs=[pl.BlockSpec((tM, tN), lambda m, n, t: (m, n + off_blocks))],
                out_shape=[jax.ShapeDtypeStruct((M, n_total), BF16)],
                scratch_shapes=[pltpu.VMEM((tM, tN), F32)], aliases=aliases,
                sem=("parallel", "parallel", "arbitrary"), args=args, carry=carry)
    return res[0] if carry is None else (res[0][0], res[1])


def _inproj_fwd(xh, lg, lb, w, bias, *, name, carry=None):
    T, D = xh.shape
    N = w.shape[1]
    tm = _tile(T, 1024)
    tn = _tile(N, 1024, LANES)

    def body(xh_ref, lg_ref, lb_ref, w_ref, b_ref, o_ref, xb_ref):
        @pl.when(pl.program_id(1) == 0)
        def _():
            xb_ref[...] = (xh_ref[...] * lg_ref[...] + lb_ref[...]).astype(BF16)
        o_ref[...] = _dot(xb_ref[...], w_ref[...]) + b_ref[...]

    return _call(
        body, name=name, grid=(T // tm, N // tn),
        in_specs=[pl.BlockSpec((tm, D), lambda i, j: (i, 0)),
                  pl.BlockSpec((1, D), lambda i, j: (0, 0)), pl.BlockSpec((1, D), lambda i, j: (0, 0)),
                  pl.BlockSpec((D, tn), lambda i, j: (0, j)), pl.BlockSpec((1, tn), lambda i, j: (0, j))],
        out_specs=[pl.BlockSpec((tm, tn), lambda i, j: (i, j)), pl.BlockSpec((tm, D), lambda i, j: (i, 0))],
        out_shape=[jax.ShapeDtypeStruct((T, N), F32), jax.ShapeDtypeStruct((T, D), BF16)],
        sem=("parallel", "arbitrary"), args=[xh, lg, lb, w, bias], carry=carry)


def _inproj_bwd(dproj, w, dr_next, xh, rstd, lg, *, name, carry=None):
    T, N = dproj.shape
    D = w.shape[0]
    tm = _tile(T, 512)
    tn = _tile(N, 2048, LANES)
    nj = N // tn

    def body(dp_ref, w_ref, drn_ref, xh_ref, rstd_ref, lg_ref, dprev_ref, dlg_ref, dlb_ref, dx_sc):
        i = pl.program_id(0)
        j = pl.program_id(1)

        @pl.when(j == 0)
        def _():
            for rs in _row_chunks(tm):
                dx_sc[rs, :] = ALPHA * drn_ref[rs, :]

        dx_sc[...] += _dot_nt(dp_ref[...], w_ref[...])

        @pl.when(j == nj - 1)
        def _():
            @pl.when(i == 0)
            def _():
                dlg_ref[...] = jnp.zeros_like(dlg_ref)
                dlb_ref[...] = jnp.zeros_like(dlb_ref)
            for rs in _row_chunks(tm):
                dx = dx_sc[rs, :]
                x_hat = xh_ref[rs, :]
                dlg_ref[...] += _colsum(dx * x_hat)
                dlb_ref[...] += _colsum(dx)
                dprev_ref[rs, :] = _ln_bwd(dx, x_hat, rstd_ref[rs, 0:1], lg_ref[...])

    row = pl.BlockSpec((tm, D), lambda i, j: (i, 0))
    vec = pl.BlockSpec((1, D), lambda i, j: (0, 0))
    return _call(
        body, name=name, grid=(T // tm, nj),
        in_specs=[pl.BlockSpec((tm, tn), lambda i, j: (i, j)), pl.BlockSpec((D, tn), lambda i, j: (0, j)),
                  row, row, pl.BlockSpec((tm, LANES), lambda i, j: (i, 0)), vec],
        out_specs=[row, vec, vec],
        out_shape=[jax.ShapeDtypeStruct((T, D), F32), jax.ShapeDtypeStruct((1, D), F32),
                   jax.ShapeDtypeStruct((1, D), F32)],
        scratch_shapes=[pltpu.VMEM((tm, D), F32)], sem=("arbitrary", "arbitrary"),
        args=[dproj, w, dr_next, xh, rstd, lg], carry=carry)


def _sgu_fwd(proj, ln_g, ln_b, w_s, b_sb, *, name):
    T = proj.shape[0]
    A = proj.shape[1] // 8
    hd = A // A_GROUPS
    tm = _tile(T, 512, GMLP_BLOCK)

    def body(u_ref, v_ref, g_ref, b_ref, ws_ref, bs_ref, o_ref):
        gu = _gelu(u_ref[...])
        vh, _ = _ln_stats(_gelu(v_ref[...]))
        vn = (vh * g_ref[...] + b_ref[...]).astype(BF16)
        mask = _chunk_mask(False)
        for h in range(A_GROUPS):
            wm = jnp.where(mask, ws_ref[h], 0.0).astype(BF16)
            cols = slice(h * hd, (h + 1) * hd)
            for n in range(tm // GMLP_BLOCK):
                rows = slice(n * GMLP_BLOCK, (n + 1) * GMLP_BLOCK)
                s = _dot(wm, vn[rows, cols]) + bs_ref[h][:, :hd]
                o_ref[rows, cols] = (gu[rows, cols] * s).astype(BF16)

    vec = pl.BlockSpec((1, A), lambda i: (0, 0))
    full = pl.BlockSpec((A_GROUPS, GMLP_BLOCK, GMLP_BLOCK), lambda i: (0, 0, 0))
    return pl.pallas_call(
        body, name=name, grid=(T // tm,),
        in_specs=[pl.BlockSpec((tm, A), lambda i: (i, 0)), pl.BlockSpec((tm, A), lambda i: (i, 1)),
                  vec, vec, full, full],
        out_specs=pl.BlockSpec((tm, A), lambda i: (i, 0)),
        out_shape=jax.ShapeDtypeStruct((T, A), BF16),
        compiler_params=_params("parallel"))(proj, proj, ln_g, ln_b, w_s, b_sb)


def _sgu_bwd(proj, dsg, ln_g, ln_b, w_s, w_st, b_sb, dproj, *, name):
    T = proj.shape[0]
    A = proj.shape[1] // 8
    hd = A // A_GROUPS
    tm = _tile(T, 512, GMLP_BLOCK)
    nt = T // tm

    def body(u_ref, v_ref, dsg_ref, g_ref, b_ref, ws_ref, wst_ref, bs_ref, _alias,
             dp_ref, dbin_ref, dlg_ref, dlb_ref, dws_ref, dbs_ref, dvn_sc, dgu_sc, dbs_sc):
        i = pl.program_id(0)

        @pl.when(i == 0)
        def _():
            dbin_ref[...] = jnp.zeros_like(dbin_ref)
            dlg_ref[...] = jnp.zeros_like(dlg_ref)
            dlb_ref[...] = jnp.zeros_like(dlb_ref)
            dws_ref[...] = jnp.zeros_like(dws_ref)
            dbs_sc[...] = jnp.zeros_like(dbs_sc)

        u = u_ref[...]
        v = v_ref[...]
        gu = _gelu(u)
        vh, rstd = _ln_stats(_gelu(v))
        gain = g_ref[...]
        vn = (vh * gain + b_ref[...]).astype(BF16)
        dsg_v = dsg_ref[...]
        mask = _chunk_mask(False)
        mask_t = _chunk_mask(True)
        for h in range(A_GROUPS):
            wm = jnp.where(mask, ws_ref[h], 0.0).astype(BF16)
            wmt = jnp.where(mask_t, wst_ref[h], 0.0).astype(BF16)
            cols = slice(h * hd, (h + 1) * hd)
            for n in range(tm // GMLP_BLOCK):
                rows = slice(n * GMLP_BLOCK, (n + 1) * GMLP_BLOCK)
                vb = vn[rows, cols]
                s = _dot(wm, vb) + bs_ref[h][:, :hd]
                d_out = dsg_v[rows, cols]
                dgu_sc[rows, cols] = d_out * s
                ds = d_out * gu[rows, cols]
                ds_b = ds.astype(BF16)
                dws_ref[h] += _dot_nt(ds_b, vb)
                dbs_sc[h] += ds
                dvn_sc[rows, cols] = _dot(wmt, ds_b)
        dvn = dvn_sc[...]
        dlg_ref[...] += _colsum(dvn * vh)
        dlb_ref[...] += _colsum(dvn)
        dv = _ln_bwd(dvn, vh, rstd, gain) * _gelu_grad(v)
        du = dgu_sc[...] * _gelu_grad(u)
        dp_ref[:, 0:A] = du.astype(BF16)
        dp_ref[:, A:2 * A] = dv.astype(BF16)
        dbin_ref[:, 0:A] += _colsum(du)
        dbin_ref[:, A:2 * A] += _colsum(dv)

        @pl.when(i == nt - 1)
        def _():
            for h in range(A_GROUPS):
                dws_ref[h] = jnp.where(mask, dws_ref[h], 0.0)
                dbs_ref[h:h + 1, :] = _colsum(dbs_sc[h].T)

    vec = pl.BlockSpec((1, A), lambda i: (0, 0))
    full = pl.BlockSpec((A_GROUPS, GMLP_BLOCK, GMLP_BLOCK), lambda i: (0, 0, 0))
    tile = pl.BlockSpec((tm, A), lambda i: (i, 0))
    return pl.pallas_call(
        body, name=name, grid=(nt,),
        in_specs=[tile, pl.BlockSpec((tm, A), lambda i: (i, 1)), tile, vec, vec, full, full, full, ANY],
        out_specs=[pl.BlockSpec((tm, 2 * A), lambda i: (i, 0)), pl.BlockSpec((1, 2 * A), lambda i: (0, 0)),
                   vec, vec, full, pl.BlockSpec((A_GROUPS, GMLP_BLOCK), lambda i: (0, 0))],
        out_shape=[jax.ShapeDtypeStruct(dproj.shape, BF16), jax.ShapeDtypeStruct((1, 2 * A), F32),
                   jax.ShapeDtypeStruct((1, A), F32), jax.ShapeDtypeStruct((1, A), F32),
                   jax.ShapeDtypeStruct((A_GROUPS, GMLP_BLOCK, GMLP_BLOCK), F32),
                   jax.ShapeDtypeStruct((A_GROUPS, GMLP_BLOCK), F32)],
        scratch_shapes=[pltpu.VMEM((tm, A), F32), pltpu.VMEM((tm, A), F32),
                        pltpu.VMEM((A_GROUPS, GMLP_BLOCK, hd), F32)],
        input_output_aliases={8: 0},
        compiler_params=_params("arbitrary"))(proj, proj, dsg, ln_g, ln_b, w_s, w_st, b_sb, dproj)


def _conv_tiles(T, B):
    tm = _tile(T, 256, CONV_ROWS)
    lb = min(LANES, B)
    return tm, tm // CONV_HALO, lb


def _fill_phases(src, dst, B, lb):
    rows = dst.shape[1]
    for p in range(1, 8):
        for cb in range(B // lb):
            ls = slice(cb * lb, (cb + 1) * lb)
            dst[p - 1, :, ls] = src[p:p + rows, ls]


def _tap_sums(w_ref, src, phases, ls, tm, offset_of_tap):
    lb = ls.stop - ls.start
    n_rc = tm // CONV_ROWS
    accs = [jnp.zeros((CONV_ROWS // 8, 8, lb), F32) for _ in range(n_rc)]
    for k in range(CONV_WIDTH):
        wk = jnp.broadcast_to(w_ref[k:k + 1, ls], (8, lb))[None]
        for rc in range(n_rc):
            win = _shifted(src, phases, rc * CONV_ROWS + offset_of_tap(k), ls)
            accs[rc] = accs[rc] + wk * win.reshape(CONV_ROWS // 8, 8, lb)
    return [a.reshape(CONV_ROWS, lb) for a in accs]


def _shifted(src, phases, off, ls):
    m, p = divmod(off, 8)
    if p == 0:
        return src[off:off + CONV_ROWS, ls]
    return phases[p - 1, 8 * m:8 * m + CONV_ROWS, ls]


def _conv_fwd(proj, w_dw, b_dw, ln_g, ln_b, *, name):
    T = proj.shape[0]
    B = proj.shape[1] // 8
    tm, nh, lb = _conv_tiles(T, B)

    def body(ap_ref, gp_ref, a_ref, g_ref, w_ref, bdw_ref, lg_ref, lb_ref, c_ref, cv_ref, z_sc, zp_sc):
        i = pl.program_id(0)
        z_sc[0:CONV_HALO, :] = jnp.where(i > 0, ap_ref[...] * _sig(gp_ref[...]), 0.0)
        z_sc[CONV_HALO:CONV_HALO + tm, :] = a_ref[...] * _sig(g_ref[...])
        _fill_phases(z_sc, zp_sc, B, lb)
        for cb in range(B // lb):
            ls = slice(cb * lb, (cb + 1) * lb)
            accs = _tap_sums(w_ref, z_sc, zp_sc, ls, tm, lambda k: CONV_HALO - (CONV_WIDTH - 1) + k)
            for rc, acc in enumerate(accs):
                c_ref[rc * CONV_ROWS:(rc + 1) * CONV_ROWS, ls] = acc + bdw_ref[:, ls]
        xh, _ = _ln_stats(c_ref[...])
        y = xh * lg_ref[...] + lb_ref[...]
        cv_ref[...] = (y * _sig(y)).astype(BF16)

    vec = pl.BlockSpec((1, B), lambda i: (0, 0))
    halo_a = pl.BlockSpec((CONV_HALO, B), lambda i: (jnp.maximum(i * nh - 1, 0), 2))
    halo_g = pl.BlockSpec((CONV_HALO, B), lambda i: (jnp.maximum(i * nh - 1, 0), 3))
    return pl.pallas_call(
        body, name=name, grid=(T // tm,),
        in_specs=[halo_a, halo_g, pl.BlockSpec((tm, B), lambda i: (i, 2)), pl.BlockSpec((tm, B), lambda i: (i, 3)),
                  pl.BlockSpec((CONV_WPAD, B), lambda i: (0, 0)), vec, vec, vec],
        out_specs=[pl.BlockSpec((tm, B), lambda i: (i, 0))] * 2,
        out_shape=[jax.ShapeDtypeStruct((T, B), F32), jax.ShapeDtypeStruct((T, B), BF16)],
        scratch_shapes=[pltpu.VMEM((CONV_HALO + tm, B), F32), pltpu.VMEM((7, CONV_HALO + tm - 8, B), F32)],
        compiler_params=_params("parallel"))(proj, proj, proj, proj, w_dw, b_dw, ln_g, ln_b)


def _conv_bwd_ln(dcv, c, ln_g, ln_b, *, name):
    T, B = c.shape
    tm = _tile(T, 512)

    def body(dcv_ref, c_ref, lg_ref, lb_ref, dc_ref, dlg_ref, dlb_ref, dbdw_ref):
        @pl.when(pl.program_id(0) == 0)
        def _():
            dlg_ref[...] = jnp.zeros_like(dlg_ref)
            dlb_ref[...] = jnp.zeros_like(dlb_ref)
            dbdw_ref[...] = jnp.zeros_like(dbdw_ref)
        gain = lg_ref[...]
        xh, rstd = _ln_stats(c_ref[...])
        y = xh * gain + lb_ref[...]
        s = _sig(y)
        dy = dcv_ref[...] * (s * (1.0 + y * (1.0 - s)))
        dlg_ref[...] += _colsum(dy * xh)
        dlb_ref[...] += _colsum(dy)
        dc = _ln_bwd(dy, xh, rstd, gain)
        dc_ref[...] = dc
        dbdw_ref[...] += _colsum(dc)

    tile = pl.BlockSpec((tm, B), lambda i: (i, 0))
    vec = pl.BlockSpec((1, B), lambda i: (0, 0))
    return pl.pallas_call(
        body, name=name, grid=(T // tm,), in_specs=[tile, tile, vec, vec], out_specs=[tile, vec, vec, vec],
        out_shape=[jax.ShapeDtypeStruct((T, B), F32)] + [jax.ShapeDtypeStruct((1, B), F32)] * 3,
        compiler_params=_params("arbitrary"))(dcv, c, ln_g, ln_b)


def _conv_bwd(proj, dc, w_dw, dproj, *, name, carry=None):
    T = proj.shape[0]
    B = proj.shape[1] // 8
    tm, nh, lb = _conv_tiles(T, B)
    nt = T // tm
    n_halo = T // CONV_HALO

    def body(ap_ref, gp_ref, a_ref, g_ref, dc_ref, dcn_ref, w_ref, _alias,
             dp_ref, dbin_ref, dw_ref, z_sc, dc_sc, dz_sc, dw_sc, zp_sc, dcp_sc):
        i = pl.program_id(0)

        @pl.when(i == 0)
        def _():
            dbin_ref[...] = jnp.zeros_like(dbin_ref)
            dw_sc[...] = jnp.zeros_like(dw_sc)

        a = a_ref[...]
        s = _sig(g_ref[...])
        z_sc[0:CONV_HALO, :] = jnp.where(i > 0, ap_ref[...] * _sig(gp_ref[...]), 0.0)
        z_sc[CONV_HALO:CONV_HALO + tm, :] = a * s
        dc_sc[0:tm, :] = dc_ref[...]
        dc_sc[tm:tm + CONV_HALO, :] = jnp.where(i < nt - 1, dcn_ref[...], 0.0)
        _fill_phases(z_sc, zp_sc, B, lb)
        _fill_phases(dc_sc, dcp_sc, B, lb)
        for cb in range(B // lb):
            ls = slice(cb * lb, (cb + 1) * lb)
            for rc in range(tm // CONV_ROWS):
                r0 = rc * CONV_ROWS
                acc = jnp.zeros((CONV_ROWS, lb), F32)
                for k in range(CONV_WIDTH):
                    acc = acc + w_ref[k:k + 1, ls] * _shifted(dc_sc, dcp_sc, r0 + (CONV_WIDTH - 1) - k, ls)
                dz_sc[r0:r0 + CONV_ROWS, ls] = acc
            for k in range(CONV_WIDTH):
                part = jnp.zeros((8, lb), F32)
                for rc in range(tm // CONV_ROWS):
                    r0 = rc * CONV_ROWS
                    prod = dc_sc[r0:r0 + CONV_ROWS, ls] * _shifted(
                        z_sc, zp_sc, r0 + CONV_HALO - (CONV_WIDTH - 1) + k, ls)
                    part = part + jnp.sum(prod.reshape(CONV_ROWS // 8, 8, lb), axis=0)
                dw_sc[8 * k:8 * k + 8, ls] += part
        dz = dz_sc[...]
        da = dz * s
        dg = dz * a * s * (1.0 - s)
        dp_ref[:, 0:B] = da.astype(BF16)
        dp_ref[:, B:2 * B] = dg.astype(BF16)
        dbin_ref[:, 0:B] += _colsum(da)
        dbin_ref[:, B:2 * B] += _colsum(dg)

        @pl.when(i == nt - 1)
        def _():
            for k in range(CONV_WIDTH):
                dw_ref[k:k + 1, :] = _colsum(dw_sc[8 * k:8 * k + 8, :])

    halo_a = pl.BlockSpec((CONV_HALO, B), lambda i: (jnp.maximum(i * nh - 1, 0), 2))
    halo_g = pl.BlockSpec((CONV_HALO, B), lambda i: (jnp.maximum(i * nh - 1, 0), 3))
    halo_dc = pl.BlockSpec((CONV_HALO, B), lambda i: (jnp.minimum((i + 1) * nh, n_halo - 1), 0))
    return _call(
        body, name=name, grid=(nt,),
        in_specs=[halo_a, halo_g, pl.BlockSpec((tm, B), lambda i: (i, 2)), pl.BlockSpec((tm, B), lambda i: (i, 3)),
                  pl.BlockSpec((tm, B), lambda i: (i, 0)), halo_dc,
                  pl.BlockSpec((CONV_WPAD, B), lambda i: (0, 0)), ANY],
        out_specs=[pl.BlockSpec((tm, 2 * B), lambda i: (i, 1)), pl.BlockSpec((1, 2 * B), lambda i: (0, 0)),
                   pl.BlockSpec((CONV_WIDTH, B), lambda i: (0, 0))],
        out_shape=[jax.ShapeDtypeStruct(dproj.shape, BF16), jax.ShapeDtypeStruct((1, 2 * B), F32),
                   jax.ShapeDtypeStruct((CONV_WIDTH, B), F32)],
        scratch_shapes=[pltpu.VMEM((CONV_HALO + tm, B), F32), pltpu.VMEM((tm + CONV_HALO, B), F32),
                        pltpu.VMEM((tm, B), F32), pltpu.VMEM((8 * CONV_WIDTH, B), F32),
                        pltpu.VMEM((7, CONV_HALO + tm - 8, B), F32), pltpu.VMEM((7, CONV_HALO + tm - 8, B), F32)],
        aliases={7: 0}, sem=("arbitrary",), args=[proj, proj, proj, proj, dc, dc, w_dw, dproj], carry=carry)


def _mix_fwd_gate(sg, cv, proj, wa, wb, *, name):
    T, A = sg.shape
    D = wa.shape[1]
    tm = _tile(T, 256)

    def body(sg_ref, cv_ref, la_ref, lb_ref, wa_ref, wb_ref, ya_ref, yb_ref, m_ref):
        ya = _dot(sg_ref[...], wa_ref[...])
        yb = _dot(cv_ref[...], wb_ref[...])
        ya_ref[...] = ya.astype(BF16)
        yb_ref[...] = yb.astype(BF16)
        m_ref[...] = (_sig(la_ref[...]) * ya + _sig(lb_ref[...]) * yb).astype(BF16)

    act = pl.BlockSpec((tm, A), lambda i: (i, 0))
    wide = pl.BlockSpec((tm, D), lambda i: (i, 0))
    wspec = pl.BlockSpec((A, D), lambda i: (0, 0))
    return pl.pallas_call(
        body, name=name, grid=(T // tm,),
        in_specs=[act, act, pl.BlockSpec((tm, D), lambda i: (i, 2)), pl.BlockSpec((tm, D), lambda i: (i, 3)),
                  wspec, wspec],
        out_specs=[wide] * 3, out_shape=[jax.ShapeDtypeStruct((T, D), BF16)] * 3,
        compiler_params=_params("parallel"))(sg, cv, proj, proj, wa, wb)


def _mix_fwd_out(m, wout, xh, lg, lb, *, name):
    T, D = xh.shape
    tm = _tile(T, 512)

    def body(m_ref, w_ref, xh_ref, lg_ref, lb_ref, xho_ref, rstd_ref):
        r = ALPHA * (xh_ref[...] * lg_ref[...] + lb_ref[...]) + _dot(m_ref[...], w_ref[...])
        xho, rstd = _ln_stats(r)
        xho_ref[...] = xho
        rstd_ref[...] = jnp.broadcast_to(rstd, (tm, LANES))

    row = pl.BlockSpec((tm, D), lambda i: (i, 0))
    vec = pl.BlockSpec((1, D), lambda i: (0, 0))
    return pl.pallas_call(
        body, name=name, grid=(T // tm,),
        in_specs=[row, pl.BlockSpec((D, D), lambda i: (0, 0)), row, vec, vec],
        out_specs=[row, pl.BlockSpec((tm, LANES), lambda i: (i, 0))],
        out_shape=[jax.ShapeDtypeStruct((T, D), F32), jax.ShapeDtypeStruct((T, LANES), F32)],
        compiler_params=_params("parallel"))(m, wout, xh, lg, lb)


def _mix_bwd_gate(dr, wout, proj, ya, yb, *, name, carry=None):
    T, D = dr.shape
    N = proj.shape[1]
    tm = _tile(T, 256)

    def body(dr_ref, w_ref, la_ref, lb_ref, ya_ref, yb_ref, dya_ref, dyb_ref, dp_ref, dbin_ref):
        @pl.when(pl.program_id(0) == 0)
        def _():
            dbin_ref[...] = jnp.zeros_like(dbin_ref)
        dm = _dot_nt(dr_ref[...].astype(BF16), w_ref[...])
        sa = _sig(la_ref[...])
        sb = _sig(lb_ref[...])
        dya_ref[...] = (dm * sa).astype(BF16)
        dyb_ref[...] = (dm * sb).astype(BF16)
        dla = dm * ya_ref[...].astype(F32) * sa * (1.0 - sa)
        dlb = dm * yb_ref[...].astype(F32) * sb * (1.0 - sb)
        dp_ref[:, 0:D] = dla.astype(BF16)
        dp_ref[:, D:2 * D] = dlb.astype(BF16)
        dbin_ref[:, 0:D] += _colsum(dla)
        dbin_ref[:, D:2 * D] += _colsum(dlb)

    row = pl.BlockSpec((tm, D), lambda i: (i, 0))
    return _call(
        body, name=name, grid=(T // tm,),
        in_specs=[row, pl.BlockSpec((D, D), lambda i: (0, 0)), pl.BlockSpec((tm, D), lambda i: (i, 2)),
                  pl.BlockSpec((tm, D), lambda i: (i, 3)), row, row],
        out_specs=[row, row, pl.BlockSpec((tm, 2 * D), lambda i: (i, 1)), pl.BlockSpec((1, 2 * D), lambda i: (0, 0))],
        out_shape=[jax.ShapeDtypeStruct((T, D), BF16), jax.ShapeDtypeStruct((T, D), BF16),
                   jax.ShapeDtypeStruct((T, N), BF16), jax.ShapeDtypeStruct((1, 2 * D), F32)],
        sem=("arbitrary",), args=[dr, wout, proj, proj, ya, yb], carry=carry)


def _mix_bwd_proj(dya, dyb, wa, wb, *, name):
    T, D = dya.shape
    A = wa.shape[0]
    tm = _tile(T, 512)

    def body(dya_ref, dyb_ref, wa_ref, wb_ref, dsg_ref, dcv_ref):
        dsg_ref[...] = _dot_nt(dya_ref[...], wa_ref[...])
        dcv_ref[...] = _dot_nt(dyb_ref[...], wb_ref[...])

    row = pl.BlockSpec((tm, D), lambda i: (i, 0))
    wspec = pl.BlockSpec((A, D), lambda i: (0, 0))
    act = pl.BlockSpec((tm, A), lambda i: (i, 0))
    return pl.pallas_call(
        body, name=name, grid=(T // tm,), in_specs=[row, row, wspec, wspec], out_specs=[act, act],
        out_shape=[jax.ShapeDtypeStruct((T, A), F32)] * 2,
        compiler_params=_params("parallel"))(dya, dyb, wa, wb)


def _mesh_pos():
    return lax.axis_index("x"), lax.axis_index("y"), lax.axis_index("c")


def _shard_view(ref, p, shape, axis):
    r, c = shape
    if axis == 0:
        return ref.at[pl.ds(pl.multiple_of(p * r, 16), r), :]
    return ref.at[:, pl.ds(pl.multiple_of(p * c, LANES), c)]


def _all_gather(shards, axes, progressive=False, busy=0.93):
    n = len(shards)
    shapes = [s.shape for s in shards]
    sizes = [s.size * s.dtype.itemsize for s in shards]
    if progressive:
        done = [sum(sizes[:t + 1]) / sum(sizes) for t in range(n)]
        when = (0.0,) + tuple(min(0.97, 0.04 + busy * d) for d in done) + (1.0,)
    else:
        when = (0.0, 1.0)

    def run(ins, outs, sems, phase):
        send_sems, recv_sems, local_sems = sems
        x, y, c = _mesh_pos()
        me, sibling = (x, y, c), (x, y, 1 - c)
        chips = [(1 - x, y), (x, 1 - y), (1 - x, 1 - y)]

        def view(t, pos):
            px, py, pc = pos
            return _shard_view(outs[t], 4 * px + 2 * py + pc, shapes[t], axes[t])

        def copy(t, k, block, to, src=None):
            return pltpu.make_async_remote_copy(
                src_ref=view(t, block) if src is None else src, dst_ref=view(t, block),
                send_sem=send_sems.at[7 * t + k], recv_sem=recv_sems.at[7 * t + k],
                device_id=to, device_id_type=MESH)

        mine = [pltpu.make_async_copy(ins[t], view(t, me), local_sems.at[t]) for t in range(n)]
        first = []
        for t in range(n):
            first.append(copy(t, 0, me, sibling, src=ins[t]))
            first += [copy(t, 1 + j, me, (*chip, c), src=ins[t]) for j, chip in enumerate(chips)]
        if phase == 0:
            for cp in mine + first:
                cp.start()
            return

        def forward(t):
            for j, chip in enumerate(chips):
                copy(t, 1 + j, (*chip, c), me).wait_recv()
                copy(t, 4 + j, (*chip, c), sibling).start()

        if progressive and phase <= n:
            forward(phase - 1)
            return
        if not progressive:
            for t in range(n):
                forward(t)
        passed = [copy(t, 4 + j, (*chip, c), sibling) for t in range(n) for j, chip in enumerate(chips)]
        for t in range(n):
            copy(t, 0, sibling, me).wait_recv()
            for j, chip in enumerate(chips):
                copy(t, 4 + j, (*chip, 1 - c), me).wait_recv()
        for cp in first + passed:
            cp.wait_send()
        for cp in mine:
            cp.wait()

    out_shape = [jax.ShapeDtypeStruct((N_DEV * s.shape[0], s.shape[1]) if ax == 0
                                      else (s.shape[0], N_DEV * s.shape[1]), s.dtype)
                 for s, ax in zip(shards, axes)]
    return _Comm(shards, out_shape, [pltpu.SemaphoreType.DMA((7 * n,)), pltpu.SemaphoreType.DMA((7 * n,)),
                                     pltpu.SemaphoreType.DMA((n,))], run, when)


def _all_gather_routed(shards, axes):
    n = len(shards)
    shapes = [s.shape for s in shards]
    assert all(s[0] % 32 == 0 for s in shapes)

    def run(ins, outs, sems, phase):
        send_sems, recv_sems, local_sems = sems
        x, y, c = _mesh_pos()
        me, sibling = (x, y, c), (x, y, 1 - c)
        nx, ny, nd = (1 - x, y), (x, 1 - y), (1 - x, 1 - y)

        def block(t, chip, core):
            return _shard_view(outs[t], 4 * chip[0] + 2 * chip[1] + core, shapes[t], axes[t])

        def half(t, chip, core, h):
            hr = shapes[t][0] // 2
            return block(t, chip, core).at[pl.ds(h * hr, hr), :]

        def own_half(t, h):
            hr = shapes[t][0] // 2
            return ins[t].at[pl.ds(h * hr, hr), :]

        def copy(t, k, src, dst, to):
            return pltpu.make_async_remote_copy(src_ref=src, dst_ref=dst, send_sem=send_sems.at[10 * t + k],
                                                recv_sem=recv_sems.at[10 * t + k], device_id=to, device_id_type=MESH)

        def arrived(t, k, dst):
            copy(t, k, dst, dst, me).wait_recv()

        mine = [pltpu.make_async_copy(ins[t], block(t, (x, y), c), local_sems.at[t]) for t in range(n)]
        own = []
        for t in range(n):
            own += [copy(t, 0, ins[t], block(t, (x, y), c), sibling),
                    copy(t, 1, own_half(t, 0), half(t, (x, y), c, 0), (*nx, c)),
                    copy(t, 2, own_half(t, 1), half(t, (x, y), c, 1), (*nx, c)),
                    copy(t, 3, own_half(t, 1), half(t, (x, y), c, 1), (*ny, c)),
                    copy(t, 4, own_half(t, 0), half(t, (x, y), c, 0), (*ny, c))]
        if phase == 0:
            for cp in mine + own:
                cp.start()
            return
        relays = []

        def relay(t, k, view, to):
            cp = copy(t, k, view, view, to)
            cp.start()
            relays.append(cp)

        for t in range(n):
            arrived(t, 1, half(t, nx, c, 0))
            relay(t, 5, half(t, nx, c, 0), (*ny, c))
            arrived(t, 3, half(t, ny, c, 1))
            relay(t, 6, half(t, ny, c, 1), (*nx, c))
            arrived(t, 2, half(t, nx, c, 1))
            relay(t, 7, block(t, nx, c), sibling)
            arrived(t, 4, half(t, ny, c, 0))
            relay(t, 8, block(t, ny, c), sibling)
            arrived(t, 5, half(t, nd, c, 0))
            arrived(t, 6, half(t, nd, c, 1))
            relay(t, 9, block(t, nd, c), sibling)
        for t in range(n):
            arrived(t, 0, block(t, (x, y), 1 - c))
            arrived(t, 7, block(t, nx, 1 - c))
            arrived(t, 8, block(t, ny, 1 - c))
            arrived(t, 9, block(t, nd, 1 - c))
        for cp in own + relays:
            cp.wait_send()
        for cp in mine:
            cp.wait()

    out_shape = [jax.ShapeDtypeStruct((N_DEV * s.shape[0], s.shape[1]) if ax == 0
                                      else (s.shape[0], N_DEV * s.shape[1]), s.dtype)
                 for s, ax in zip(shards, axes)]
    return _Comm(shards, out_shape, [pltpu.SemaphoreType.DMA((10 * n,)), pltpu.SemaphoreType.DMA((10 * n,)),
                                     pltpu.SemaphoreType.DMA((n,))], run)


def _rs_to_sibling(grads, shapes, axes):
    n = len(grads)

    def run(gs, outs, sems, phase):
        send_sems, recv_sems = sems
        x, y, c = _mesh_pos()
        copies = [pltpu.make_async_remote_copy(
            src_ref=_shard_view(gs[t], 2 * k + (1 - c), shapes[t], axes[t]), dst_ref=outs[t].at[k],
            send_sem=send_sems.at[4 * t + k], recv_sem=recv_sems.at[4 * t + k],
            device_id=(x, y, 1 - c), device_id_type=MESH) for t in range(n) for k in range(4)]
        if phase == 0:
            for cp in copies:
                cp.start()
            return
        for cp in copies:
            cp.wait_recv()
        for cp in copies:
            cp.wait_send()

    return _Comm(grads, [jax.ShapeDtypeStruct((4,) + tuple(s), BF16) for s in shapes],
                 [pltpu.SemaphoreType.DMA((4 * n,)), pltpu.SemaphoreType.DMA((4 * n,))], run)


def _rs_pair_sum(g, recv, cidx, shape, axis, *, name):
    r, c = shape
    tr = _tile(r, max(8, (1 << 21) // c), 16)
    nr = r // tr

    def body(c_ref, g_ref, rv_ref, o_ref):
        o_ref[...] = (g_ref[...].astype(F32) + rv_ref[...].astype(F32)).astype(BF16)

    if axis == 1:
        g_spec = pl.BlockSpec((tr, c), lambda k, i, s: (i, 2 * k + s[0]))
    else:
        g_spec = pl.BlockSpec((tr, c), lambda k, i, s: ((2 * k + s[0]) * nr + i, 0))
    blk = pl.BlockSpec((None, tr, c), lambda k, i, s: (k, i, 0))
    return pl.pallas_call(
        body, name=name,
        grid_spec=pltpu.PrefetchScalarGridSpec(num_scalar_prefetch=1, grid=(4, nr), in_specs=[g_spec, blk],
                                               out_specs=blk),
        out_shape=jax.ShapeDtypeStruct((4, r, c), BF16),
        compiler_params=_params("parallel", "parallel"))(cidx, g, recv)


def _rs_to_chips(parts):
    n = len(parts)

    def run(ps, outs, sems, phase):
        send_sems, recv_sems, local_sems = sems
        x, y, c = _mesh_pos()
        my_chip = 2 * x + y
        peers = [(1 - x, y), (x, 1 - y), (1 - x, 1 - y)]
        local = [pltpu.make_async_copy(ps[t].at[my_chip], outs[t].at[my_chip], local_sems.at[t]) for t in range(n)]
        sends = [pltpu.make_async_remote_copy(
            src_ref=ps[t].at[2 * px + py], dst_ref=outs[t].at[my_chip],
            send_sem=send_sems.at[3 * t + j], recv_sem=recv_sems.at[3 * t + j],
            device_id=(px, py, c), device_id_type=MESH) for t in range(n) for j, (px, py) in enumerate(peers)]
        if phase == 0:
            for cp in local + sends:
                cp.start()
            return
        for t in range(n):
            for j, (px, py) in enumerate(peers):
                pltpu.make_async_remote_copy(
                    src_ref=ps[t].at[2 * px + py], dst_ref=outs[t].at[2 * px + py],
                    send_sem=send_sems.at[3 * t + j], recv_sem=recv_sems.at[3 * t + j],
                    device_id=(x, y, c), device_id_type=MESH).wait_recv()
        for cp in sends:
            cp.wait_send()
        for cp in local:
            cp.wait()

    return _Comm(parts, [jax.ShapeDtypeStruct(p.shape, BF16) for p in parts],
                 [pltpu.SemaphoreType.DMA((3 * n,)), pltpu.SemaphoreType.DMA((3 * n,)),
                  pltpu.SemaphoreType.DMA((n,))], run)


def _sequencer_to_chips(part, *, name):
    p_ref = jax.new_ref(part, memory_space=pltpu.MemorySpace.HBM)
    q_ref = jax.empty_ref(jax.ShapeDtypeStruct(part.shape, part.dtype), memory_space=pltpu.MemorySpace.HBM)
    dma = pltpu.SemaphoreType.DMA

    @pl.kernel(mesh=plsc.ScalarSubcoreMesh(axis_name="seq", num_cores=1), name=name,
               scratch_types=(dma,) * 7, compiler_params=pltpu.CompilerParams(collective_id=7))
    def launch(s0, s1, s2, r0, r1, r2, local_sem):
        x, y, c = _mesh_pos()
        my_chip = 2 * x + y
        peers = [(1 - x, y), (x, 1 - y), (1 - x, 1 - y)]
        barrier = pltpu.get_barrier_semaphore()
        for px, py in peers:
            pl.semaphore_signal(barrier, inc=1, device_id=(px, py, c), device_id_type=MESH)
        pl.semaphore_wait(barrier, len(peers))
        local = pltpu.make_async_copy(p_ref.at[my_chip], q_ref.at[my_chip], local_sem)
        local.start()
        sends = [pltpu.make_async_remote_copy(
            src_ref=p_ref.at[2 * px + py], dst_ref=q_ref.at[my_chip], send_sem=s, recv_sem=r,
            device_id=(px, py, c), device_id_type=MESH) for (px, py), s, r in zip(peers, (s0, s1, s2), (r0, r1, r2))]
        for cp in sends:
            cp.start()
        for (px, py), s, r in zip(peers, (s0, s1, s2), (r0, r1, r2)):
            pltpu.make_async_remote_copy(
                src_ref=p_ref.at[2 * px + py], dst_ref=q_ref.at[2 * px + py], send_sem=s, recv_sem=r,
                device_id=(x, y, c), device_id_type=MESH).wait_recv()
        for cp in sends:
            cp.wait_send()
        local.wait()

    launch()
    return q_ref[...]


def _exchange_small(buf):
    def run(ins, outs, sems, phase):
        (in_ref,), (slots,) = ins, outs
        send_sems, recv_sems, local_sem = sems
        x, y, c = _mesh_pos()
        me = 4 * x + 2 * y + c
        local = pltpu.make_async_copy(in_ref, slots.at[me], local_sem.at[0])
        flips = [(fx, fy, fc) for fx in (0, 1) for fy in (0, 1) for fc in (0, 1)][1:]
        peers = [(1 - x if fx else x, 1 - y if fy else y, 1 - c if fc else c) for fx, fy, fc in flips]
        sends = [pltpu.make_async_remote_copy(src_ref=in_ref, dst_ref=slots.at[me], send_sem=send_sems.at[k],
                                              recv_sem=recv_sems.at[k], device_id=peer, device_id_type=MESH)
                 for k, peer in enumerate(peers)]
        if phase == 0:
            for cp in [local] + sends:
                cp.start()
            return
        for k, (px, py, pc) in enumerate(peers):
            pltpu.make_async_remote_copy(src_ref=in_ref, dst_ref=slots.at[4 * px + 2 * py + pc],
                                         send_sem=send_sems.at[k], recv_sem=recv_sems.at[k],
                                         device_id=(x, y, c), device_id_type=MESH).wait_recv()
        for cp in sends:
            cp.wait_send()
        local.wait()

    return _Comm([buf], [jax.ShapeDtypeStruct((N_DEV,) + buf.shape, F32)],
                 [pltpu.SemaphoreType.DMA((7,)), pltpu.SemaphoreType.DMA((7,)), pltpu.SemaphoreType.DMA((1,))], run)


def _sum_slots(slots, *, name):
    _, R, C = slots.shape
    tr = _tile(R, 512)

    def body(s_ref, o_ref):
        acc = s_ref[0]
        for p in range(1, N_DEV):
            acc = acc + s_ref[p]
        o_ref[...] = acc

    return pl.pallas_call(
        body, name=name, grid=(R // tr,), in_specs=[pl.BlockSpec((N_DEV, tr, C), lambda i: (0, i, 0))],
        out_specs=pl.BlockSpec((tr, C), lambda i: (i, 0)), out_shape=jax.ShapeDtypeStruct((R, C), F32),
        compiler_params=_params("parallel"))(slots)


def _adam_math(g, w, m, v):
    m_new = ADAM_B1 * m + (1.0 - ADAM_B1) * g
    v_new = ADAM_B2 * v + (1.0 - ADAM_B2) * (g * g)
    m_hat = m_new / ADAM_C1
    v_hat = v_new / ADAM_C2
    delta = -ADAM_LR * (m_hat / (jnp.sqrt(v_hat) + ADAM_EPS) + ADAM_WD * w)
    return delta, m_new, v_new


def _adamw_sharded(q, w, m, v, *, name):
    r, c = w.shape
    tr = _tile(r, max(8, (1 << 19) // c), 16)

    def body(q_ref, w_ref, m_ref, v_ref, g_ref, d_ref, mo_ref, vo_ref):
        g = ((q_ref[0].astype(F32) + q_ref[1].astype(F32)) + q_ref[2].astype(F32)) + q_ref[3].astype(F32)
        g_ref[...] = g
        d_ref[...], mo_ref[...], vo_ref[...] = _adam_math(g, w_ref[...], m_ref[...], v_ref[...])

    blk = pl.BlockSpec((tr, c), lambda i: (i, 0))
    return pl.pallas_call(
        body, name=name, grid=(r // tr,),
        in_specs=[pl.BlockSpec((4, tr, c), lambda i: (0, i, 0)), blk, blk, blk], out_specs=[blk] * 4,
        out_shape=[jax.ShapeDtypeStruct((r, c), F32)] * 4,
        compiler_params=_params("parallel"))(q, w, m, v)


def _adamw_plain(g, w, m, v, *, name):
    r, c = w.shape
    tr = _tile(r, 512)

    def body(g_ref, w_ref, m_ref, v_ref, d_ref, mo_ref, vo_ref):
        d_ref[...], mo_ref[...], vo_ref[...] = _adam_math(g_ref[...], w_ref[...], m_ref[...], v_ref[...])

    blk = pl.BlockSpec((tr, c), lambda i: (i, 0))
    return pl.pallas_call(
        body, name=name, grid=(r // tr,), in_specs=[blk] * 4, out_specs=[blk] * 3,
        out_shape=[jax.ShapeDtypeStruct((r, c), F32)] * 3,
        compiler_params=_params("parallel"))(g, w, m, v)


def _pack_rows(arrays):
    return jnp.concatenate([a.reshape(-1, LANES) for a in arrays], axis=0)


def kernel(x, ffn1_w_gu, ffn1_w_down, ln1_g, ln1_b, w_in, b_in, sgu_ln_g, sgu_ln_b, sgu_w_s, sgu_b_s, w_a_proj, conv_w_dw, conv_b_dw, conv_ln_g, conv_ln_b, w_b_proj, w_out, ln2_g, ln2_b, ffn2_w_gu, ffn2_w_down, ln3_g, ln3_b, loss_target, m_ffn1_w_gu, m_ffn1_w_down, m_ln1_g, m_ln1_b, m_w_in, m_b_in, m_sgu_ln_g, m_sgu_ln_b, m_sgu_w_s, m_sgu_b_s, m_w_a_proj, m_conv_w_dw, m_conv_b_dw, m_conv_ln_g, m_conv_ln_b, m_w_b_proj, m_w_out, m_ln2_g, m_ln2_b, m_ffn2_w_gu, m_ffn2_w_down, m_ln3_g, m_ln3_b, v_ffn1_w_gu, v_ffn1_w_down, v_ln1_g, v_ln1_b, v_w_in, v_b_in, v_sgu_ln_g, v_sgu_ln_b, v_sgu_w_s, v_sgu_b_s, v_w_a_proj, v_conv_w_dw, v_conv_b_dw, v_conv_ln_g, v_conv_ln_b, v_w_b_proj, v_w_out, v_ln2_g, v_ln2_b, v_ffn2_w_gu, v_ffn2_w_down, v_ln3_g, v_ln3_b):
    given = dict(locals())
    w = {n: given[n][0] for n in WEIGHTS}
    mom = {n: given["m_" + n][0] for n in WEIGHTS}
    var = {n: given["v_" + n][0] for n in WEIGHTS}
    xt = x[0]
    target = loss_target[0]
    T, D = xt.shape
    A = w['w_a_proj'].shape[0]

    big_names = list(BIG)
    early = ['ffn1_w_gu', 'ffn1_w_down']
    later = ['ffn2_w_gu']
    late = [n for n in big_names if n not in early + later]
    conv_w_pad = jnp.pad(w['conv_w_dw'], ((0, CONV_WPAD - CONV_WIDTH), (0, 0)))
    w_bf = {n: w[n].astype(BF16) for n in big_names}
    full = dict(zip(early, _comm_call(_all_gather_routed([w_bf[n] for n in early], [BIG[n] for n in early]),
                                      name="all_gather_ffn1")))
    gather_late = _all_gather([w_bf[n] for n in late] + [conv_w_pad], [BIG[n] for n in late] + [1],
                              progressive=True, busy=0.65)
    gather_later = _all_gather([w_bf[n] for n in later], [BIG[n] for n in later], progressive=True, busy=0.8)

    def row(v):
        return v.reshape(1, -1)

    ones = jnp.ones((1, D), F32)
    zeros = jnp.zeros((1, D), F32)
    w_s = w['sgu_w_s']
    w_st = jnp.swapaxes(w_s, 1, 2)
    b_sb = jnp.broadcast_to(w['sgu_b_s'][:, :, None], w_s.shape)

    (gate1, up1, xb0, xh1, rstd1), gathered = _ffn_fwd(xt, ones, zeros, full['ffn1_w_gu'], full['ffn1_w_down'],
                                                  affine=False, name="ffn1_fwd", carry=gather_late)
    full.update(zip(late, gathered[:-1]))
    conv_w_full = gathered[-1]
    g1, b1 = row(w['ln1_g']), row(w['ln1_b'])
    (proj, xb1), gathered = _inproj_fwd(xh1, g1, b1, full['w_in'], row(w['b_in']), name="inproj_fwd",
                                        carry=gather_later)
    full.update(zip(later, gathered))
    sg = _sgu_fwd(proj, row(w['sgu_ln_g']), row(w['sgu_ln_b']), w_s, b_sb, name="sgu_fwd")
    conv_out, cv = _conv_fwd(proj, conv_w_full, row(w['conv_b_dw']), row(w['conv_ln_g']), row(w['conv_ln_b']),
                             name="conv_fwd")
    ya, yb, mixed = _mix_fwd_gate(sg, cv, proj, full['w_a_proj'], full['w_b_proj'], name="mix_fwd_gate")
    xh2, rstd2 = _mix_fwd_out(mixed, full['w_out'], xh1, g1, b1, name="mix_fwd_out")
    g2, b2 = row(w['ln2_g']), row(w['ln2_b'])
    gate2, up2, xb2, dr3, loss_part, d_ln3_g, d_ln3_b = _ffn_fwd(
        xh2, g2, b2, full['ffn2_w_gu'], full['ffn2_w_down'], affine=True, name="ffn2_fwd_loss",
        final=(row(w['ln3_g']), row(w['ln3_b']), target))

    F = full['ffn2_w_down'].shape[0]
    h2, dgate2, dup2, dr2, d_ln2_g, d_ln2_b = _ffn_bwd(dr3, gate2, up2, full['ffn2_w_gu'], full['ffn2_w_down'],
                                                      name="ffn2_bwd", prev=(xh2, rstd2, g2))
    G, P, Q = {}, {}, {}
    cidx = lax.axis_index("c").astype(jnp.int32).reshape(1)

    def to_sibling(names):
        return _rs_to_sibling([G[n] for n in names], [w[n].shape for n in names], [BIG[n] for n in names])

    def pair_sum(names, received):
        for n, rv in zip(names, received):
            P[n] = _rs_pair_sum(G[n], rv, cidx, w[n].shape, BIG[n], name="rs_pair_sum_" + n)

    def to_chips(names):
        return _rs_to_chips([P[n] for n in names])

    G['ffn2_w_down'] = _mm_tn(h2, dr3, name="dw_ffn2_down", tm_pref=1408, tn_pref=2048, scale=0.5)
    gu, rv = _mm_tn(xb2, dgate2, name="dw_ffn2_gate", tm_pref=2048, tn_pref=1408,
                    n_total=2 * F, carry=to_sibling(['ffn2_w_down']))
    pair_sum(['ffn2_w_down'], rv)
    G['ffn2_w_gu'], q = _mm_tn(xb2, dup2, name="dw_ffn2_up", tm_pref=2048, tn_pref=1408,
                               into=gu, col_off=F, n_total=2 * F, carry=to_chips(['ffn2_w_down']))
    Q['ffn2_w_down'] = q[0]

    (dya, dyb, dproj, dbin_gate), rv = _mix_bwd_gate(dr2, full['w_out'], proj, ya, yb, name="mix_bwd_gate",
                                                     carry=to_sibling(['ffn2_w_gu']))
    pair_sum(['ffn2_w_gu'], rv)
    dsg, dcv = _mix_bwd_proj(dya, dyb, full['w_a_proj'], full['w_b_proj'], name="mix_bwd_proj")
    dproj, dbin_sgu, d_sgu_ln_g, d_sgu_ln_b, d_w_s, d_b_s = _sgu_bwd(
        proj, dsg, row(w['sgu_ln_g']), row(w['sgu_ln_b']), w_s, w_st, b_sb, dproj, name="sgu_bwd")
    dconv, d_conv_ln_g, d_conv_ln_b, d_conv_b = _conv_bwd_ln(dcv, conv_out, row(w['conv_ln_g']),
                                                            row(w['conv_ln_b']), name="conv_bwd_ln")
    (dproj, dbin_conv, d_conv_w), q = _conv_bwd(proj, dconv, conv_w_full, dproj, name="conv_bwd",
                                                carry=to_chips(['ffn2_w_gu']))
    Q['ffn2_w_gu'] = q[0]

    mid = ['w_out', 'w_a_proj', 'w_b_proj']
    G['w_out'] = _mm_tn(mixed, dr2, name="dw_out", tm_pref=2048, tn_pref=1024)
    G['w_a_proj'] = _mm_tn(sg, dya, name="dw_a_proj", tm_pref=1024, tn_pref=2048)
    G['w_b_proj'] = _mm_tn(cv, dyb, name="dw_b_proj", tm_pref=1024, tn_pref=2048)
    G['w_in'], rv = _mm_tn(xb1, dproj, name="dw_in", tm_pref=2048, tn_pref=1024, tk_pref=2048,
                           carry=to_sibling(mid))
    pair_sum(mid, rv)
    both = _join(to_chips(mid), to_sibling(['w_in']))
    (dr1, d_ln1_g, d_ln1_b), moved = _inproj_bwd(dproj, full['w_in'], dr2, xh1, rstd1, g1, name="inproj_bwd",
                                                 carry=both)
    q, rv = both.split(moved)
    Q.update(zip(mid, q))
    pair_sum(['w_in'], rv)

    h1, dgate1, dup1, grad_x = _ffn_bwd(dr1, gate1, up1, full['ffn1_w_gu'], full['ffn1_w_down'], name="ffn1_bwd")
    G['ffn1_w_down'], q = _mm_tn(h1, dr1, name="dw_ffn1_down", tm_pref=1408, tn_pref=2048, scale=0.5,
                                 carry=to_chips(['w_in']))
    Q['w_in'] = q[0]
    small_g = {'ln1_g': d_ln1_g, 'ln1_b': d_ln1_b,
               'b_in': jnp.concatenate([dbin_sgu, dbin_conv, dbin_gate], axis=1),
               'sgu_ln_g': d_sgu_ln_g, 'sgu_ln_b': d_sgu_ln_b, 'sgu_w_s': d_w_s, 'sgu_b_s': d_b_s,
               'conv_b_dw': d_conv_b, 'conv_ln_g': d_conv_ln_g, 'conv_ln_b': d_conv_ln_b,
               'ln2_g': d_ln2_g, 'ln2_b': d_ln2_b, 'ln3_g': d_ln3_g, 'ln3_b': d_ln3_b}
    packed = _pack_rows([small_g[n] for n in SMALL] + [d_conv_w, loss_part])
    both = _join(to_sibling(['ffn1_w_down']), _exchange_small(packed))
    gu, moved = _mm_tn(xb0, dgate1, name="dw_ffn1_gate", tm_pref=2048, tn_pref=1408, n_total=2 * F, carry=both)
    rv, slots = both.split(moved)
    pair_sum(['ffn1_w_down'], rv)
    reduced = _sum_slots(slots[0], name="sum_small")
    G['ffn1_w_gu'], q = _mm_tn(xb0, dup1, name="dw_ffn1_up", tm_pref=2048, tn_pref=1408, into=gu, col_off=F,
                               n_total=2 * F, carry=to_chips(['ffn1_w_down']))
    Q['ffn1_w_down'] = q[0]
    pair_sum(['ffn1_w_gu'], _comm_call(to_sibling(['ffn1_w_gu']), name="rs_to_sibling_last"))
    Q['ffn1_w_gu'] = _sequencer_to_chips(P['ffn1_w_gu'], name="rs_to_chips_last")

    grads, deltas, new_m, new_v = {}, {}, {}, {}
    for n in sorted(big_names, key=lambda n: n == 'ffn1_w_gu'):
        grads[n], deltas[n], new_m[n], new_v[n] = _adamw_sharded(Q[n], w[n], mom[n], var[n], name="adamw_" + n)

    B = conv_w_full.shape[1]
    n_small_rows = sum(w[n].size for n in SMALL) // LANES
    conv_rows = CONV_WIDTH * B // LANES
    d_small, m_small, v_small = _adamw_plain(
        reduced[:n_small_rows], _pack_rows([w[n] for n in SMALL]), _pack_rows([mom[n] for n in SMALL]),
        _pack_rows([var[n] for n in SMALL]), name="adamw_small")
    off = 0
    for n in SMALL:
        rows = w[n].size // LANES
        grads[n] = reduced[off:off + rows].reshape(w[n].shape)
        deltas[n] = d_small[off:off + rows].reshape(w[n].shape)
        new_m[n] = m_small[off:off + rows].reshape(w[n].shape)
        new_v[n] = v_small[off:off + rows].reshape(w[n].shape)
        off += rows
    conv_g_full = reduced[off:off + conv_rows].reshape(CONV_WIDTH, B)
    bs = w['conv_w_dw'].shape[1]
    my_block = 4 * lax.axis_index("x") + 2 * lax.axis_index("y") + lax.axis_index("c")
    grads['conv_w_dw'] = lax.dynamic_slice(conv_g_full, (0, my_block * bs), (CONV_WIDTH, bs))
    deltas['conv_w_dw'], new_m['conv_w_dw'], new_v['conv_w_dw'] = _adamw_plain(
        grads['conv_w_dw'], w['conv_w_dw'], mom['conv_w_dw'], var['conv_w_dw'], name="adamw_conv_w")
    loss = reduced[off + conv_rows, 0]

    def lead(a):
        return a[None]

    return (loss, grad_x[None], *[lead(grads[n]) for n in WEIGHTS], *[lead(deltas[n]) for n in WEIGHTS],
            *[lead(new_m[n]) for n in WEIGHTS], *[lead(new_v[n]) for n in WEIGHTS])
```
